```python
import math
import jax, jax.numpy as jnp
from jax import lax
import numpy as np

D_MODEL = 1024
BATCH = 4
SEQ = 4096
DEPTH = 2
DEC_BATCH = 32
DEC_SEQ = 8
PAST_LEN = 16384
PAGE_SIZE = 128

GROUP_W = D_MODEL // 4
SSD_HEADDIM = 64
SSD_HEADS = GROUP_W // SSD_HEADDIM
SSD_NGROUPS = 2
SSD_STATE = 128
SSD_CONV = 4
SSD_CHUNK = 128
SSD_CONV_CH = GROUP_W + 2 * SSD_NGROUPS * SSD_STATE
NSA_HEAD_DIM = 64
NSA_HEADS = GROUP_W // NSA_HEAD_DIM
NSA_KV = 2
NSA_REP = NSA_HEADS // NSA_KV
NSA_CMP_BLOCK = 32
NSA_SLC_BLOCK = 64
NSA_TOPN = 16
NSA_WINDOW = 512
NSA_QBLOCK = 64
WIN_QBLOCK = 128
FORCE_SCORE = 1e4
DIFF_HEADS = 4
DIFF_VDIM = GROUP_W // DIFF_HEADS
DIFF_QKDIM = DIFF_VDIM // 2
ATTN_QBLOCK = 128
LRU_W = GROUP_W
LRU_BLOCKS = 4
LRU_BDIM = LRU_W // LRU_BLOCKS
LRU_CONV = 4
LRU_C = 8.0
D_FF = 4 * D_MODEL
ROPE_THETA = 500000.0
ROPE_FRAC = 4
ALPHA = (2 * DEPTH) ** 0.25
BETA = (8 * DEPTH) ** -0.25
EPS = 1e-5
IN_SIZES = (GROUP_W, GROUP_W, SSD_NGROUPS * SSD_STATE, SSD_NGROUPS * SSD_STATE, SSD_HEADS,
            NSA_HEADS * NSA_HEAD_DIM, NSA_KV * NSA_HEAD_DIM, NSA_KV * NSA_HEAD_DIM, NSA_KV * NSA_HEAD_DIM,
            NSA_KV * NSA_HEAD_DIM, NSA_KV * NSA_HEAD_DIM, NSA_KV * NSA_HEAD_DIM, 3 * NSA_HEADS,
            DIFF_HEADS * 2 * DIFF_QKDIM, DIFF_HEADS * 2 * DIFF_QKDIM, DIFF_HEADS * DIFF_VDIM,
            LRU_W, LRU_W)
D_IN = sum(IN_SIZES)

kernel_name = 'hybrid_ssd_nsa_diff_rglru_step'


def layer_norm(x, g, b):
    xf = x.astype(jnp.float32)
    mu = jnp.mean(xf, -1, keepdims=True)
    var = jnp.mean(jnp.square(xf - mu), -1, keepdims=True)
    return (xf - mu) * lax.rsqrt(var + EPS) * g.astype(jnp.float32) + b.astype(jnp.float32)


def rms_norm(x, g):
    xf = x.astype(jnp.float32)
    return xf * lax.rsqrt(jnp.mean(jnp.square(xf), -1, keepdims=True) + EPS) * g.astype(jnp.float32)


def masked_softmax(s, mask):
    s = jnp.where(mask, s.astype(jnp.float32), -jnp.inf)
    m = jnp.max(s, -1, keepdims=True)
    m = jnp.where(jnp.isfinite(m), m, 0.0)
    e = jnp.exp(s - m)
    return e / jnp.maximum(jnp.sum(e, -1, keepdims=True), 1e-30)


def apply_rope(x, pos):
    dh = x.shape[-1]
    rd = dh // ROPE_FRAC
    half = rd // 2
    inv = ROPE_THETA ** (-jnp.arange(half, dtype=jnp.float32) / half)
    ang = pos.astype(jnp.float32)[:, None] * inv[None, :]
    cos = jnp.cos(ang)[:, None, :]
    sin = jnp.sin(ang)[:, None, :]
    x1 = x[..., :half].astype(jnp.float32)
    x2 = x[..., half:rd].astype(jnp.float32)
    rot = jnp.concatenate([x1 * cos - x2 * sin, x1 * sin + x2 * cos], -1).astype(x.dtype)
    return jnp.concatenate([rot, x[..., rd:]], -1)


def causal_conv(x, w, b, prev):
    k = w.shape[0]
    L = x.shape[1]
    xp = jnp.concatenate([prev.astype(x.dtype), x], axis=1)
    y = b + sum(xp[:, j:j + L] * w[j] for j in range(k))
    return y, xp[:, xp.shape[1] - (k - 1):]


def sweep_queries(fn, block, qpos, *qarrs):
    lq = qpos.shape[0]
    qb = min(block, lq)
    nb = -(-lq // qb)
    pad = nb * qb - lq
    if pad:
        qpos = jnp.concatenate([qpos, qpos[-1] + 1 + jnp.arange(pad, dtype=qpos.dtype)])

    def to_blocks(a):
        a = jnp.pad(a, [(0, 0), (0, pad)] + [(0, 0)] * (a.ndim - 2))
        return jnp.moveaxis(a.reshape(a.shape[0], nb, qb, *a.shape[2:]), 1, 0)

    out = lax.map(lambda args: fn(*args), (qpos.reshape(nb, qb),) + tuple(to_blocks(a) for a in qarrs))
    out = jnp.moveaxis(out, 0, 1)
    return out.reshape(out.shape[0], nb * qb, *out.shape[3:])[:, :lq]


def ssd_scan(x, dt, a_neg, bm, cm, h0):
    f32 = jnp.float32
    b, L, H, P = x.shape
    rep = H // bm.shape[2]
    q = min(SSD_CHUNK, L)
    nc = -(-L // q)
    pad = nc * q - L

    def chunks(a):
        a = jnp.pad(a.astype(f32), [(0, 0), (0, pad)] + [(0, 0)] * (a.ndim - 2))
        return a.reshape(b, nc, q, *a.shape[2:])

    xc, dtc = chunks(x), chunks(dt)
    bc = chunks(jnp.repeat(bm, rep, axis=2))
    cc = chunks(jnp.repeat(cm, rep, axis=2))
    acum = jnp.cumsum(dtc * a_neg, axis=2)
    causal = jnp.tril(jnp.ones((q, q), bool))[None, None, :, :, None]
    seg = acum[:, :, :, None, :] - acum[:, :, None, :, :]
    decay = jnp.exp(jnp.where(causal, seg, -jnp.inf))
    cb = jnp.einsum('bcihn,bcjhn->bcijh', cc, bc)
    y_intra = jnp.einsum('bcijh,bcjh,bcjhp->bcihp', cb * decay, dtc, xc)
    w_last = jnp.exp(acum[:, :, -1:, :] - acum) * dtc
    s_chunk = jnp.einsum('bcjh,bcjhn,bcjhp->bchpn', w_last, bc, xc)
    d_chunk = jnp.exp(acum[:, :, -1, :])

    def step(h, inp):
        s_c, d_c = inp
        return d_c[:, :, None, None] * h + s_c, h

    h_end, h_start = lax.scan(step, h0.astype(f32), (jnp.moveaxis(s_chunk, 1, 0), jnp.moveaxis(d_chunk, 1, 0)))
    h_start = jnp.moveaxis(h_start, 0, 1)
    y_inter = jnp.einsum('bcihn,bchpn->bcihp', cc * jnp.exp(acum)[..., None], h_start)
    y = (y_intra + y_inter).reshape(b, nc * q, H, P)[:, :L]
    return y, h_end


def rglru(xr, gate, conv_w, conv_b, w_a, b_a, w_x, b_x, lam, conv_prev, h0):
    f32 = jnp.float32
    b, L, W = xr.shape
    xc, conv_new = causal_conv(xr, conv_w, conv_b, conv_prev)
    xb = xc.reshape(b, L, LRU_BLOCKS, LRU_BDIM)
    r = jax.nn.sigmoid((jnp.einsum('blnd,nde->blne', xb, w_a).reshape(b, L, W) + b_a).astype(f32))
    i = jax.nn.sigmoid((jnp.einsum('blnd,nde->blne', xb, w_x).reshape(b, L, W) + b_x).astype(f32))
    log_a = -LRU_C * r * jax.nn.softplus(-lam.astype(f32))
    a = jnp.exp(log_a)
    u = jnp.sqrt(-jnp.expm1(2.0 * log_a)) * (i * xc.astype(f32))
    a_cum, hs = lax.associative_scan(lambda l, rr: (l[0] * rr[0], rr[0] * l[1] + rr[1]), (a, u), axis=1)
    h = hs + a_cum * h0.astype(f32)[:, None, :]
    y = h * jax.nn.gelu(gate.astype(f32))
    return y, conv_new, h[:, -1]


def nsa_compressed(q, k_all, v_all, w_ck, w_cv, qpos):
    b, lq, hq, dh = q.shape
    t = k_all.shape[1]
    tp = -(-t // NSA_SLC_BLOCK) * NSA_SLC_BLOCK
    nc = tp // NSA_CMP_BLOCK
    pad = ((0, 0), (0, tp - t), (0, 0), (0, 0))
    kb = jnp.pad(k_all, pad).reshape(b, nc, NSA_CMP_BLOCK, NSA_KV, dh)
    vb = jnp.pad(v_all, pad).reshape(b, nc, NSA_CMP_BLOCK, NSA_KV, dh)
    k_cmp = jnp.einsum('bnjgd,gj->bngd', kb, w_ck)
    v_cmp = jnp.einsum('bnjgd,gj->bngd', vb, w_cv)
    qg = q.reshape(b, lq, NSA_KV, NSA_REP, dh)
    s = jnp.einsum('bqgrd,bngd->bgrqn', qg, k_cmp).astype(jnp.float32) * dh ** -0.5
    blk_end = (jnp.arange(nc) + 1) * NSA_CMP_BLOCK - 1
    pr = masked_softmax(s, blk_end[None, :] <= qpos[:, None])
    o = jnp.einsum('bgrqn,bngd->bqgrd', pr.astype(v_cmp.dtype), v_cmp).reshape(b, lq, hq, dh)
    ratio = NSA_SLC_BLOCK // NSA_CMP_BLOCK
    imp = pr.sum(2).reshape(b, NSA_KV, lq, nc // ratio, ratio).sum(-1)
    return o, imp


def nsa_select(q, k_all, v_all, imp, qpos):
    b, lq, hq, dh = q.shape
    ns = imp.shape[-1]
    t = k_all.shape[1]
    n_sel = min(NSA_TOPN, ns)
    blk = jnp.arange(ns)[None, :]
    cur = (qpos // NSA_SLC_BLOCK)[:, None]
    forced = (blk == 0) | (blk == cur) | (blk == cur - 1)
    score = jnp.where(forced, FORCE_SCORE, jnp.where(blk > cur, -1.0, imp))
    _, idx = lax.top_k(score, n_sel)
    pad = ((0, 0), (0, ns * NSA_SLC_BLOCK - t), (0, 0), (0, 0))
    kb = jnp.pad(k_all, pad).reshape(b, ns, NSA_SLC_BLOCK, NSA_KV, dh).transpose(0, 3, 1, 2, 4)
    vb = jnp.pad(v_all, pad).reshape(b, ns, NSA_SLC_BLOCK, NSA_KV, dh).transpose(0, 3, 1, 2, 4)
    pick = jax.vmap(jax.vmap(lambda blocks, ids: blocks[ids]))

    def attend(qpos_b, q_b, idx_b):
        nq = q_b.shape[1]
        ids = jnp.swapaxes(idx_b, 1, 2)
        flat = ids.reshape(b, NSA_KV, nq * n_sel)
        kg = pick(kb, flat).reshape(b, NSA_KV, nq, n_sel * NSA_SLC_BLOCK, dh)
        vg = pick(vb, flat).reshape(b, NSA_KV, nq, n_sel * NSA_SLC_BLOCK, dh)
        kpos = (ids[..., None] * NSA_SLC_BLOCK + jnp.arange(NSA_SLC_BLOCK)).reshape(b, NSA_KV, nq, n_sel * NSA_SLC_BLOCK)
        mask = (kpos <= qpos_b[None, None, :, None])[:, :, None]
        qg = q_b.reshape(b, nq, NSA_KV, NSA_REP, dh)
        s = jnp.einsum('bqgrd,bgqkd->bgrqk', qg, kg).astype(jnp.float32) * dh ** -0.5
        pr = masked_softmax(s, mask)
        o = jnp.einsum('bgrqk,bgqkd->bqgrd', pr.astype(vg.dtype), vg)
        return o.reshape(b, nq, hq, dh)

    return sweep_queries(attend, NSA_QBLOCK, qpos, q, jnp.swapaxes(idx, 1, 2))


def window_banded(q, k, v, qpos):
    b, L, hq, dh = q.shape
    wb = min(WIN_QBLOCK, L)
    nb = L // wb
    nprev = -(-NSA_WINDOW // wb)

    def bands(a):
        ap = jnp.pad(a, ((0, 0), (nprev * wb, 0), (0, 0), (0, 0))).reshape(b, nb + nprev, wb, *a.shape[2:])
        return jnp.concatenate([ap[:, j:j + nb] for j in range(nprev + 1)], axis=2)

    kb, vb = bands(k), bands(v)
    kpos = (jnp.arange(nb)[:, None] - nprev) * wb + jnp.arange((nprev + 1) * wb)[None, :]
    qp = qpos.reshape(nb, wb)
    kp = kpos[:, None, :]
    mask = (kp >= 0) & (kp <= qp[:, :, None]) & (kp > qp[:, :, None] - NSA_WINDOW)
    qg = q.reshape(b, nb, wb, NSA_KV, NSA_REP, dh)
    s = jnp.einsum('bnqgrd,bnkgd->bngrqk', qg, kb).astype(jnp.float32) * dh ** -0.5
    pr = masked_softmax(s, mask[None, :, None, None])
    o = jnp.einsum('bngrqk,bnkgd->bnqgrd', pr.astype(vb.dtype), vb)
    return o.reshape(b, L, hq, dh)


def window_dense(q, k, v, qpos, k0):
    b, L, hq, dh = q.shape
    kpos = k0 + jnp.arange(k.shape[1])
    mask = (kpos[None, :] <= qpos[:, None]) & (kpos[None, :] > qpos[:, None] - NSA_WINDOW)
    qg = q.reshape(b, L, NSA_KV, NSA_REP, dh)
    s = jnp.einsum('bqgrd,bkgd->bgrqk', qg, k).astype(jnp.float32) * dh ** -0.5
    pr = masked_softmax(s, mask)
    o = jnp.einsum('bgrqk,bkgd->bqgrd', pr.astype(v.dtype), v)
    return o.reshape(b, L, hq, dh)


def diff_attention(q, k_all, v_all, lam, qpos):
    kpos = jnp.arange(k_all.shape[1])

    def attend(qpos_b, q_b):
        s = jnp.einsum('bqhcd,bkhcd->bhcqk', q_b, k_all).astype(jnp.float32) * DIFF_QKDIM ** -0.5
        pr = masked_softmax(s, kpos[None, :] <= qpos_b[:, None])
        a = pr[:, :, 0] - lam * pr[:, :, 1]
        return jnp.einsum('bhqk,bkhd->bqhd', a.astype(v_all.dtype), v_all)

    return sweep_queries(attend, ATTN_QBLOCK, qpos, q)


def trunk_layer(x, hist, p, lam_init, prompt):
    f32 = jnp.float32
    cdt = x.dtype
    b, L, _ = x.shape
    past = hist['kc'].shape[1]
    qpos = past + jnp.arange(L, dtype=jnp.int32)
    h = jnp.einsum('bld,de->ble', x, p['w_in'])
    cuts = np.cumsum(IN_SIZES)[:-1].tolist()
    (a_z, a_x, a_b, a_c, a_dt, b_q, b_kc, b_vc, b_ks, b_vs, b_kw, b_vw, b_g,
     c_q, c_k, c_v, d_x, d_g) = jnp.split(h, cuts, axis=-1)

    xbc, ssd_conv_new = causal_conv(jnp.concatenate([a_x, a_b, a_c], -1), p['ssd_conv_w'], p['ssd_conv_b'], hist['ssd_conv'])
    xbc = jax.nn.silu(xbc)
    s_x, s_b, s_c = jnp.split(xbc, [GROUP_W, GROUP_W + SSD_NGROUPS * SSD_STATE], axis=-1)
    s_x = s_x.reshape(b, L, SSD_HEADS, SSD_HEADDIM)
    dt = jax.nn.softplus((a_dt + p['ssd_dt_bias']).astype(f32))
    a_neg = -jnp.exp(p['ssd_A_log'].astype(f32))
    y, ssd_h_new = ssd_scan(s_x, dt, a_neg, s_b.reshape(b, L, SSD_NGROUPS, SSD_STATE),
                            s_c.reshape(b, L, SSD_NGROUPS, SSD_STATE), hist['ssd_h'])
    y = y + p['ssd_D'].astype(f32)[:, None] * s_x.astype(f32)
    y = y.reshape(b, L, GROUP_W) * jax.nn.silu(a_z.astype(f32))
    y_a = rms_norm(y, p['ssd_norm_w']).astype(cdt)

    kv_shape = (b, L, NSA_KV, NSA_HEAD_DIM)
    n_q = b_q.reshape(b, L, NSA_HEADS, NSA_HEAD_DIM)
    n_qr = apply_rope(n_q, qpos)
    kc_new, vc_new = b_kc.reshape(kv_shape), b_vc.reshape(kv_shape)
    ks_new, vs_new = apply_rope(b_ks.reshape(kv_shape), qpos), b_vs.reshape(kv_shape)
    kw_new, vw_new = apply_rope(b_kw.reshape(kv_shape), qpos), b_vw.reshape(kv_shape)
    o_cmp, imp = nsa_compressed(n_q, jnp.concatenate([hist['kc'], kc_new], 1), jnp.concatenate([hist['vc'], vc_new], 1),
                                p['nsa_w_cmp_k'], p['nsa_w_cmp_v'], qpos)
    o_slc = nsa_select(n_qr, jnp.concatenate([hist['ks'], ks_new], 1), jnp.concatenate([hist['vs'], vs_new], 1), imp, qpos)
    wbuf = hist['kw'].shape[1]
    kw_all = jnp.concatenate([hist['kw'], kw_new], 1)
    vw_all = jnp.concatenate([hist['vw'], vw_new], 1)
    if prompt:
        o_win = window_banded(n_qr, kw_new, vw_new, qpos)
    else:
        o_win = window_dense(n_qr, kw_all, vw_all, qpos, past - wbuf)
    gate = jax.nn.sigmoid(b_g.astype(f32)).reshape(b, L, 3, NSA_HEADS, 1)
    y_b = (gate[:, :, 0] * o_cmp + gate[:, :, 1] * o_slc + gate[:, :, 2] * o_win).reshape(b, L, GROUP_W).astype(cdt)
    wl = min(NSA_WINDOW, past + L)
    kw_buf = kw_all[:, wbuf + L - wl:]
    vw_buf = vw_all[:, wbuf + L - wl:]

    c_qh = apply_rope(c_q.reshape(b, L, 2 * DIFF_HEADS, DIFF_QKDIM), qpos).reshape(b, L, DIFF_HEADS, 2, DIFF_QKDIM)
    c_kh = apply_rope(c_k.reshape(b, L, 2 * DIFF_HEADS, DIFF_QKDIM), qpos).reshape(b, L, DIFF_HEADS, 2, DIFF_QKDIM)
    c_vh = c_v.reshape(b, L, DIFF_HEADS, DIFF_VDIM)
    dk_all = jnp.concatenate([hist['dk'].reshape(b, past, DIFF_HEADS, 2, DIFF_QKDIM), c_kh], 1)
    dv_all = jnp.concatenate([hist['dv'], c_vh], 1)
    lv = p['diff_lambda'].astype(f32)
    lam = jnp.exp(jnp.sum(lv[0] * lv[1])) - jnp.exp(jnp.sum(lv[2] * lv[3])) + lam_init
    o_c = diff_attention(c_qh, dk_all, dv_all, lam, qpos)
    y_c = (rms_norm(o_c, p['diff_norm_w']) * (1.0 - lam_init)).reshape(b, L, GROUP_W).astype(cdt)

    y_d, lru_conv_new, lru_h_new = rglru(d_x, d_g, p['lru_conv_w'], p['lru_conv_b'], p['lru_w_a'], p['lru_b_a'],
                                         p['lru_w_x'], p['lru_b_x'], p['lru_lambda'], hist['lru_conv'], hist['lru_h'])
    y_d = y_d.astype(cdt)

    mix = jnp.einsum('ble,ed->bld', jnp.concatenate([y_a, y_b, y_c, y_d], -1), p['w_out'])
    x = layer_norm(ALPHA * x + mix, p['ln1_g'], p['ln1_b']).astype(cdt)
    f = jnp.einsum('blf,fd->bld', jnp.square(jax.nn.relu(jnp.einsum('bld,df->blf', x, p['w_ff1']))), p['w_ff2'])
    x = layer_norm(ALPHA * x + f, p['ln2_g'], p['ln2_b']).astype(cdt)
    new = {'kc': kc_new, 'vc': vc_new, 'ks': ks_new, 'vs': vs_new,
           'dk': c_kh.reshape(b, L, DIFF_HEADS, 2 * DIFF_QKDIM), 'dv': c_vh,
           'kw': kw_buf, 'vw': vw_buf,
           'ssd_h': ssd_h_new.astype(cdt), 'ssd_conv': ssd_conv_new,
           'lru_h': lru_h_new.astype(cdt), 'lru_conv': lru_conv_new}
    return x, new


def setup_inputs(seed: int = 0) -> dict:
    key = jax.random.key(seed)
    keys = list(jax.random.split(key, 64))
    f32 = jnp.float32

    def nrm(shape, scale):
        return scale * jax.random.normal(keys.pop(), shape, f32)

    def uni(shape, lo, hi):
        return jax.random.uniform(keys.pop(), shape, f32, lo, hi)

    n_pages = PAST_LEN // PAGE_SIZE
    n_used = DEC_BATCH * n_pages
    n_pool = n_used + -(-n_used // 4)
    win_buf = min(NSA_WINDOW, PAST_LEN)
    kv_page = (DEPTH, n_pool, PAGE_SIZE, NSA_KV, NSA_HEAD_DIM)
    kv_win = (DEPTH, DEC_BATCH, win_buf, NSA_KV, NSA_HEAD_DIM)
    dt0 = jnp.exp(uni((DEPTH, SSD_HEADS), math.log(1e-3), math.log(1e-1)))
    a_pow = uni((DEPTH, LRU_W), 0.9, 0.999) ** (1.0 / LRU_C)
    inp = {
        'x_prompt': nrm((BATCH, SEQ, D_MODEL), 1.0),
        'x_sample': nrm((DEC_BATCH, DEC_SEQ, D_MODEL), 1.0),
        'cache_nsa_k_cmp': nrm(kv_page, 1.0),
        'cache_nsa_v_cmp': nrm(kv_page, 1.0),
        'cache_nsa_k_slc': nrm(kv_page, 1.0),
        'cache_nsa_v_slc': nrm(kv_page, 1.0),
        'cache_diff_k': nrm((DEPTH, n_pool, PAGE_SIZE, DIFF_HEADS, 2 * DIFF_QKDIM), 1.0),
        'cache_diff_v': nrm((DEPTH, n_pool, PAGE_SIZE, DIFF_HEADS, DIFF_VDIM), 1.0),
        'cache_nsa_k_win': nrm(kv_win, 1.0),
        'cache_nsa_v_win': nrm(kv_win, 1.0),
        'state_ssd': nrm((DEPTH, DEC_BATCH, SSD_HEADS, SSD_HEADDIM, SSD_STATE), 0.5),
        'state_ssd_conv': nrm((DEPTH, DEC_BATCH, SSD_CONV - 1, SSD_CONV_CH), 1.0),
        'state_lru': nrm((DEPTH, DEC_BATCH, LRU_W), 0.5),
        'state_lru_conv': nrm((DEPTH, DEC_BATCH, LRU_CONV - 1, LRU_W), 1.0),
        'page_table': jax.random.permutation(keys.pop(), n_pool)[:n_used].reshape(DEC_BATCH, n_pages).astype(jnp.int32),
        'w_in': nrm((DEPTH, D_MODEL, D_IN), D_MODEL ** -0.5),
        'ssd_conv_w': nrm((DEPTH, SSD_CONV, SSD_CONV_CH), SSD_CONV ** -0.5),
        'ssd_conv_b': nrm((DEPTH, SSD_CONV_CH), 0.02),
        'ssd_dt_bias': dt0 + jnp.log(-jnp.expm1(-dt0)),
        'ssd_A_log': jnp.log(uni((DEPTH, SSD_HEADS), 1.0, 16.0)),
        'ssd_D': 1.0 + nrm((DEPTH, SSD_HEADS), 0.02),
        'ssd_norm_w': 1.0 + nrm((DEPTH, GROUP_W), 0.02),
        'nsa_w_cmp_k': (1.0 + nrm((DEPTH, NSA_KV, NSA_CMP_BLOCK), 0.1)) / NSA_CMP_BLOCK,
        'nsa_w_cmp_v': (1.0 + nrm((DEPTH, NSA_KV, NSA_CMP_BLOCK), 0.1)) / NSA_CMP_BLOCK,
        'diff_lambda': nrm((DEPTH, 4, DIFF_QKDIM), 0.1),
        'diff_norm_w': 1.0 + nrm((DEPTH, DIFF_VDIM), 0.02),
        'lru_conv_w': nrm((DEPTH, LRU_CONV, LRU_W), LRU_CONV ** -0.5),
        'lru_conv_b': nrm((DEPTH, LRU_W), 0.02),
        'lru_w_a': nrm((DEPTH, LRU_BLOCKS, LRU_BDIM, LRU_BDIM), LRU_BDIM ** -0.5),
        'lru_b_a': nrm((DEPTH, LRU_W), 0.02),
        'lru_w_x': nrm((DEPTH, LRU_BLOCKS, LRU_BDIM, LRU_BDIM), LRU_BDIM ** -0.5),
        'lru_b_x': nrm((DEPTH, LRU_W), 0.02),
        'lru_lambda': jnp.log(a_pow) - jnp.log1p(-a_pow),
        'w_out': nrm((DEPTH, D_MODEL, D_MODEL), D_MODEL ** -0.5 * BETA),
        'ln1_g': 1.0 + nrm((DEPTH, D_MODEL), 0.02),
        'ln1_b': nrm((DEPTH, D_MODEL), 0.02),
        'w_ff1': nrm((DEPTH, D_MODEL, D_FF), D_MODEL ** -0.5),
        'w_ff2': nrm((DEPTH, D_FF, D_MODEL), D_FF ** -0.5 * BETA),
        'ln2_g': 1.0 + nrm((DEPTH, D_MODEL), 0.02),
        'ln2_b': nrm((DEPTH, D_MODEL), 0.02),
    }
    return inp


def reference(x_prompt, x_sample, cache_nsa_k_cmp, cache_nsa_v_cmp, cache_nsa_k_slc, cache_nsa_v_slc,
              cache_diff_k, cache_diff_v, cache_nsa_k_win, cache_nsa_v_win, state_ssd, state_ssd_conv,
              state_lru, state_lru_conv, page_table, w_in, ssd_conv_w, ssd_conv_b, ssd_dt_bias, ssd_A_log,
              ssd_D, ssd_norm_w, nsa_w_cmp_k, nsa_w_cmp_v, diff_lambda, diff_norm_w, lru_conv_w, lru_conv_b,
              lru_w_a, lru_b_a, lru_w_x, lru_b_x, lru_lambda, w_out, ln1_g, ln1_b, w_ff1, w_ff2, ln2_g, ln2_b):
    nbp = x_prompt.shape[0]
    nbs = page_table.shape[0]
    cdt = x_prompt.dtype

    def gather_pages(c):
        g = c[page_table]
        return g.reshape(nbs, -1, *c.shape[2:])

    def empty_kv(h, d):
        return jnp.zeros((nbp, 0, h, d), cdt)

    xp, xs = x_prompt, x_sample
    outs_p, outs_s = [], []
    for l in range(DEPTH):
        p = {'w_in': w_in[l], 'ssd_conv_w': ssd_conv_w[l], 'ssd_conv_b': ssd_conv_b[l],
             'ssd_dt_bias': ssd_dt_bias[l], 'ssd_A_log': ssd_A_log[l], 'ssd_D': ssd_D[l],
             'ssd_norm_w': ssd_norm_w[l], 'nsa_w_cmp_k': nsa_w_cmp_k[l], 'nsa_w_cmp_v': nsa_w_cmp_v[l],
             'diff_lambda': diff_lambda[l], 'diff_norm_w': diff_norm_w[l], 'lru_conv_w': lru_conv_w[l],
             'lru_conv_b': lru_conv_b[l], 'lru_w_a': lru_w_a[l], 'lru_b_a': lru_b_a[l], 'lru_w_x': lru_w_x[l],
             'lru_b_x': lru_b_x[l], 'lru_lambda': lru_lambda[l], 'w_out': w_out[l], 'ln1_g': ln1_g[l],
             'ln1_b': ln1_b[l], 'w_ff1': w_ff1[l], 'w_ff2': w_ff2[l], 'ln2_g': ln2_g[l], 'ln2_b': ln2_b[l]}
        lam_init = 0.8 - 0.6 * math.exp(-0.3 * l)
        hist_p = {'kc': empty_kv(NSA_KV, NSA_HEAD_DIM), 'vc': empty_kv(NSA_KV, NSA_HEAD_DIM),
                  'ks': empty_kv(NSA_KV, NSA_HEAD_DIM), 'vs': empty_kv(NSA_KV, NSA_HEAD_DIM),
                  'kw': empty_kv(NSA_KV, NSA_HEAD_DIM), 'vw': empty_kv(NSA_KV, NSA_HEAD_DIM),
                  'dk': empty_kv(DIFF_HEADS, 2 * DIFF_QKDIM), 'dv': empty_kv(DIFF_HEADS, DIFF_VDIM),
                  'ssd_conv': jnp.zeros((nbp, SSD_CONV - 1, SSD_CONV_CH), cdt),
                  'ssd_h': jnp.zeros((nbp, SSD_HEADS, SSD_HEADDIM, SSD_STATE), cdt),
                  'lru_conv': jnp.zeros((nbp, LRU_CONV - 1, LRU_W), cdt),
                  'lru_h': jnp.zeros((nbp, LRU_W), cdt)}
        hist_s = {'kc': gather_pages(cache_nsa_k_cmp[l]), 'vc': gather_pages(cache_nsa_v_cmp[l]),
                  'ks': gather_pages(cache_nsa_k_slc[l]), 'vs': gather_pages(cache_nsa_v_slc[l]),
                  'kw': cache_nsa_k_win[l], 'vw': cache_nsa_v_win[l],
                  'dk': gather_pages(cache_diff_k[l]), 'dv': gather_pages(cache_diff_v[l]),
                  'ssd_conv': state_ssd_conv[l], 'ssd_h': state_ssd[l],
                  'lru_conv': state_lru_conv[l], 'lru_h': state_lru[l]}
        xp, new_p = trunk_layer(xp, hist_p, p, lam_init, True)
        xs, new_s = trunk_layer(xs, hist_s, p, lam_init, False)
        outs_p.append(new_p)
        outs_s.append(new_s)

    def stk(lst, name):
        return jnp.stack([d[name] for d in lst])

    p_k_cmp, p_v_cmp = stk(outs_p, 'kc'), stk(outs_p, 'vc')
    p_k_slc, p_v_slc = stk(outs_p, 'ks'), stk(outs_p, 'vs')
    p_diff_k, p_diff_v = stk(outs_p, 'dk'), stk(outs_p, 'dv')
    p_k_win, p_v_win = stk(outs_p, 'kw'), stk(outs_p, 'vw')
    p_ssd, p_ssd_conv = stk(outs_p, 'ssd_h'), stk(outs_p, 'ssd_conv')
    p_lru, p_lru_conv = stk(outs_p, 'lru_h'), stk(outs_p, 'lru_conv')
    s_k_cmp, s_v_cmp = stk(outs_s, 'kc'), stk(outs_s, 'vc')
    s_k_slc, s_v_slc = stk(outs_s, 'ks'), stk(outs_s, 'vs')
    s_diff_k, s_diff_v = stk(outs_s, 'dk'), stk(outs_s, 'dv')
    s_k_win, s_v_win = stk(outs_s, 'kw'), stk(outs_s, 'vw')
    s_ssd, s_ssd_conv = stk(outs_s, 'ssd_h'), stk(outs_s, 'ssd_conv')
    s_lru, s_lru_conv = stk(outs_s, 'lru_h'), stk(outs_s, 'lru_conv')
    return (xp, xs,
            p_k_cmp, p_v_cmp, p_k_slc, p_v_slc, p_diff_k, p_diff_v, p_k_win, p_v_win, p_ssd, p_ssd_conv, p_lru, p_lru_conv,
            s_k_cmp, s_v_cmp, s_k_slc, s_v_slc, s_diff_k, s_diff_v, s_k_win, s_v_win, s_ssd, s_ssd_conv, s_lru, s_lru_conv)
```

```python
import functools
import math

import jax
import jax.numpy as jnp
from jax import lax
from jax.experimental import pallas as pl
from jax.experimental.pallas import tpu as pltpu

F32 = jnp.float32
BF16 = jnp.bfloat16
NEG_INF = float("-inf")

D_MODEL = 1024
GROUP_W = 256
SSD_HEADS = 4
SSD_STATE = 128
SSD_CONV_CH = 768
NSA_CMP_BLOCK = 32
NSA_SLC_BLOCK = 64
NSA_TOPN = 16
NSA_WINDOW = 512
FORCE_SCORE = 1e4
LRU_C = 8.0
D_FF = 4096
ROPE_THETA = 500000.0
EPS = 1e-5
PAGE = 128
LANES = 128
VMEM_LIMIT = 56 * 1024 * 1024

W_IN_ORDER = ((0, 1024), (1028, 2052), (2064, 3344))
MISC_COLS = ((1024, 1028), (2052, 2064))
SEG = dict(z=(0, 256), xbc=(256, 1024), qn=(1024, 1280), kc=(1280, 1408), vc=(1408, 1536), ks=(1536, 1664),
           vs=(1664, 1792), kw=(1792, 1920), vw=(1920, 2048), dq=(2048, 2304), dk=(2304, 2560), dv=(2560, 2816),
           lx=(2816, 3072), lg=(3072, 3328), misc=(3328, 3456))
D_IN_PAD = 3456
GATE_LANE0 = 4


def _params(sem):
    return pltpu.CompilerParams(dimension_semantics=sem, vmem_limit_bytes=VMEM_LIMIT)


def _iota(shape, dim):
    return lax.broadcasted_iota(jnp.int32, shape, dim)


def _dot(a, b):
    return jnp.dot(a.astype(BF16), b.astype(BF16), preferred_element_type=F32)


def _dot_nt(a, b):
    return lax.dot_general(a.astype(BF16), b.astype(BF16), (((1,), (1,)), ((), ())), preferred_element_type=F32)


def _split3(x):
    h1 = x.astype(BF16)
    r1 = x - h1.astype(F32)
    h2 = r1.astype(BF16)
    h3 = (r1 - h2.astype(F32)).astype(BF16)
    return h1, h2, h3


def _dot_01(m01, x):
    return sum(jnp.dot(m01, part, preferred_element_type=F32) for part in _split3(x))


def _dot_nt_01(m01, x):
    return sum(lax.dot_general(m01, part, (((1,), (1,)), ((), ())), preferred_element_type=F32)
               for part in _split3(x))


def _eye(n, m):
    return (_iota((n, m), 0) == _iota((n, m), 1)).astype(BF16)


def _softplus(x):
    return jnp.maximum(x, 0.0) + jnp.log1p(jnp.exp(-jnp.abs(x)))


def _head_bcast(cols, h_of_lane, nheads, lane0=0):
    out = jnp.zeros(h_of_lane.shape, F32)
    for h in range(nheads):
        out = jnp.where(h_of_lane == h, cols[:, lane0 + h:lane0 + h + 1], out)
    return out


def _online_update(s, mask, v_b, m_ref, l_ref, acc_ref):
    s = jnp.where(mask, s, NEG_INF)
    m_prev = m_ref[...]
    m_new = jnp.maximum(m_prev, jnp.max(s, axis=-1, keepdims=True))
    m_safe = jnp.where(m_new == NEG_INF, 0.0, m_new)
    alpha = jnp.exp(m_prev - m_safe)
    p = jnp.exp(s - m_safe)
    l_ref[...] = alpha * l_ref[...] + jnp.sum(p, axis=-1, keepdims=True)
    acc_ref[...] = alpha * acc_ref[...] + jnp.dot(p.astype(BF16), v_b, preferred_element_type=F32)
    m_ref[...] = m_new


def _nsa_expand(q):
    lo = _iota((q.shape[0], LANES), 1) < 64
    a, b = q[:, :LANES], q[:, LANES:]
    h0 = jnp.where(lo, a, 0.0)
    h1 = jnp.where(lo, pltpu.roll(a, 64, 1), 0.0)
    h2 = jnp.where(lo, 0.0, pltpu.roll(b, 64, 1))
    h3 = jnp.where(lo, 0.0, b)
    return jnp.concatenate([h0, h1, h2, h3], axis=0)


def _nsa_collect(o, n):
    lo = _iota((n, LANES), 1) < 64
    o0, o1, o2, o3 = o[0:n], o[n:2 * n], o[2 * n:3 * n], o[3 * n:4 * n]
    left = jnp.where(lo, o0, pltpu.roll(o1, 64, 1))
    right = jnp.where(lo, pltpu.roll(o2, 64, 1), o3)
    return jnp.concatenate([left, right], axis=1)


def _layer_norm(v, g, b):
    mu = jnp.mean(v, axis=-1, keepdims=True)
    d = v - mu
    var = jnp.mean(d * d, axis=-1, keepdims=True)
    return d * lax.rsqrt(var + EPS) * g + b


def _rope128(v, c, sa, sb, half):
    return v * c + pltpu.roll(v, LANES - half, 1) * sa + pltpu.roll(v, half, 1) * sb


def _proj_kernel(x_ref, w_ref, cn_ref, san_ref, sbn_ref, cd_ref, sad_ref, sbd_ref,
                 z_o, xbc_o, qn_o, qnr_o, kc_o, vc_o, ks_o, vs_o, kw_o, vw_o, dq_o, dk_o, dv_o, lx_o, lg_o, misc_o):
    xb = x_ref[...].astype(BF16)

    def seg(name):
        a, b = SEG[name]
        return jnp.dot(xb, w_ref[:, a:b], preferred_element_type=F32)

    def rope_n(v):
        return _rope128(v, cn_ref[...], san_ref[...], sbn_ref[...], 8)

    def rope_d(v):
        return _rope128(v, cd_ref[...], sad_ref[...], sbd_ref[...], 4)

    z_o[...] = seg("z")
    xbc_o[...] = seg("xbc")
    qn = seg("qn")
    qn_o[...] = qn
    qnr_o[...] = jnp.concatenate([rope_n(qn[:, :LANES]), rope_n(qn[:, LANES:])], axis=1)
    kc_o[...] = seg("kc")
    vc_o[...] = seg("vc")
    ks_o[...] = rope_n(seg("ks"))
    vs_o[...] = seg("vs")
    kw_o[...] = rope_n(seg("kw"))
    vw_o[...] = seg("vw")
    dq = seg("dq")
    dq_o[...] = jnp.concatenate([rope_d(dq[:, :LANES]), rope_d(dq[:, LANES:])], axis=1)
    dk = seg("dk")
    dk_o[...] = jnp.concatenate([rope_d(dk[:, :LANES]), rope_d(dk[:, LANES:])], axis=1)
    dv_o[...] = seg("dv")
    lx_o[...] = seg("lx")
    lg_o[...] = seg("lg")
    misc_o[...] = seg("misc")


PROJ_OUT = ("z", "xbc", "qn", "qnr", "kc", "vc", "ks", "vs", "kw", "vw", "dq", "dk", "dv", "lx", "lg", "misc")
PROJ_W = dict(z=256, xbc=768, qn=256, qnr=256, kc=128, vc=128, ks=128, vs=128, kw=128, vw=128, dq=256, dk=256,
              dv=256, lx=256, lg=256, misc=128)


def _rope_tables(pos, dh, rows):
    rd = dh // 4
    half = rd // 2
    inv = ROPE_THETA ** (-jnp.arange(half, dtype=F32) / half)
    ang = pos.astype(F32)[:, None] * inv[None, :]
    cos, sin = jnp.cos(ang), jnp.sin(ang)
    n = pos.shape[0]
    zero_h = jnp.zeros((n, half), F32)
    zero_r = jnp.zeros((n, dh - rd), F32)
    c = jnp.concatenate([cos, cos, jnp.ones((n, dh - rd), F32)], 1)
    sa = jnp.concatenate([-sin, zero_h, zero_r], 1)
    sb = jnp.concatenate([zero_h, sin, zero_r], 1)
    reps = (max(rows // n, 1), LANES // dh)
    return tuple(jnp.tile(t, reps) for t in (c, sa, sb))


def _proj(x2d, w_in_p, pos, tm):
    t = x2d.shape[0]
    n_tab = max(pos.shape[0] // tm, 1)
    tabs = _rope_tables(pos, 64, tm) + _rope_tables(pos, 32, tm)
    tab_spec = pl.BlockSpec((tm, LANES), lambda i: (i % n_tab, 0))
    outs = pl.pallas_call(
        _proj_kernel,
        grid=(t // tm,),
        in_specs=[pl.BlockSpec((tm, D_MODEL), lambda i: (i, 0)),
                  pl.BlockSpec((D_MODEL, D_IN_PAD), lambda i: (0, 0))] + [tab_spec] * 6,
        out_specs=[pl.BlockSpec((tm, PROJ_W[k]), lambda i: (i, 0)) for k in PROJ_OUT],
        out_shape=[jax.ShapeDtypeStruct((t, PROJ_W[k]), F32) for k in PROJ_OUT],
        compiler_params=_params(("parallel",)),
        name="proj",
    )(x2d, w_in_p, *tabs)
    return dict(zip(PROJ_OUT, outs))


def _conv_chunk(x_ref, prev_ref, cw_ref, cb_ref, xbuf, q, first):
    @pl.when(first)
    def _():
        xbuf[0:8, :] = prev_ref[0]

    xbuf[8:8 + q, :] = x_ref[0]
    acc = cb_ref[...] + cw_ref[0:1, :] * xbuf[pl.ds(5, q), :]
    for j in range(1, 4):
        acc = acc + cw_ref[j:j + 1, :] * xbuf[pl.ds(5 + j, q), :]
    xbuf[0:8, :] = xbuf[q:q + 8, :]
    return acc


def _ssd_kernel(xbc_ref, z_ref, misc_ref, prev_ref, h0_ref, cw_ref, cb_ref, dtb_ref, alog_ref, dexp_ref, nw_ref,
                y_ref, hout_ref, xbuf, ht, *, q):
    c = pl.program_id(1)

    @pl.when(c == 0)
    def _():
        ht[...] = _dot_nt_01(_eye(SSD_STATE, SSD_STATE), h0_ref[0])

    acc = _conv_chunk(xbc_ref, prev_ref, cw_ref, cb_ref, xbuf, q, c == 0)
    xc = acc * jax.nn.sigmoid(acc)
    sx = xc[:, 0:256]
    sb = (xc[:, 256:384], xc[:, 384:512])
    sc = (xc[:, 512:640], xc[:, 640:768])
    dt = _softplus(misc_ref[0] + dtb_ref[...])
    da = dt * (-jnp.exp(alog_ref[...]))
    causal = _iota((q, q), 0) >= _iota((q, q), 1)
    acum = _dot_01(causal.astype(BF16), da)
    xsel = _eye(8, LANES)
    acum_t = _dot_nt_01(xsel, acum)
    dt_t = _dot_nt_01(xsel, dt)
    a_last = acum[q - 1:q, :]
    wlast = jnp.exp(a_last - acum) * dt
    ea = jnp.exp(acum)
    head = _iota((q, GROUP_W), 1) >> 6
    ht_old = ht[...]
    y = dexp_ref[...] * sx
    for g in range(2):
        cb = _dot_nt(sc[g], sb[g])
        for h in (2 * g, 2 * g + 1):
            seg = acum[:, h:h + 1] - acum_t[h:h + 1, :]
            decay = jnp.exp(jnp.where(causal, seg, NEG_INF))
            m = cb * decay * dt_t[h:h + 1, :]
            y_h = _dot(m, sx) + _dot(sc[g] * ea[:, h:h + 1], ht_old)
            y = y + jnp.where(head == h, y_h, 0.0)
    xw = sx * _head_bcast(wlast, head, SSD_HEADS)
    lane = _iota((q, GROUP_W), 1)
    bt0 = _dot_nt(_eye(SSD_STATE, SSD_STATE), sb[0])
    bt1 = _dot_nt(_eye(SSD_STATE, SSD_STATE), sb[1])
    head1 = _iota((1, GROUP_W), 1) >> 6
    dch = _head_bcast(jnp.exp(a_last), head1, SSD_HEADS)
    ht_new = dch * ht_old + _dot(bt0, jnp.where(lane < 128, xw, 0.0)) + _dot(bt1, jnp.where(lane < 128, 0.0, xw))
    ht[...] = ht_new
    zz = z_ref[0]
    y = y * (zz * jax.nn.sigmoid(zz))
    y_ref[0] = y * lax.rsqrt(jnp.mean(y * y, axis=-1, keepdims=True) + EPS) * nw_ref[...]

    @pl.when(c == pl.num_programs(1) - 1)
    def _():
        hout_ref[0] = _dot_nt_01(_eye(GROUP_W, GROUP_W), ht_new)


def _ssd(pj, nb, sl, prev8, h0, p):
    q = min(128, sl)
    nc = sl // q
    row = lambda w: pl.BlockSpec((1, q, w), lambda b, c: (b, c, 0))
    per_b = lambda r, w: pl.BlockSpec((1, r, w), lambda b, c: (b, 0, 0))
    const = lambda r, w: pl.BlockSpec((r, w), lambda b, c: (0, 0))
    y, hout = pl.pallas_call(
        functools.partial(_ssd_kernel, q=q),
        grid=(nb, nc),
        in_specs=[row(768), row(256), row(128), per_b(8, 768), per_b(256, 128),
                  const(8, 768), const(1, 768), const(1, 128), const(1, 128), const(1, 256), const(1, 256)],
        out_specs=[row(256), per_b(256, 128)],
        out_shape=[jax.ShapeDtypeStruct((nb, sl, 256), F32), jax.ShapeDtypeStruct((nb, 256, 128), F32)],
        scratch_shapes=[pltpu.VMEM((q + 8, 768), F32), pltpu.VMEM((SSD_STATE, GROUP_W), F32)],
        compiler_params=_params(("parallel", "arbitrary")),
        name="ssd",
    )(pj["xbc"].reshape(nb, sl, 768), pj["z"].reshape(nb, sl, 256), pj["misc"].reshape(nb, sl, 128), prev8, h0,
      p["ssd_cw"], p["ssd_cb"], p["ssd_dtb"], p["ssd_alog"], p["ssd_dexp"], p["ssd_nw"])
    return y.reshape(nb * sl, 256), hout


def _lru_kernel(x_ref, g_ref, prev_ref, h0_ref, cw_ref, cb_ref, wa_ref, ba_ref, wx_ref, bx_ref, lam_ref,
                y_ref, hout_ref, xbuf, hc, *, q):
    c = pl.program_id(1)

    @pl.when(c == 0)
    def _():
        hc[...] = h0_ref[0]

    xc = _conv_chunk(x_ref, prev_ref, cw_ref, cb_ref, xbuf, q, c == 0)
    xcb = xc.astype(BF16)
    r = jax.nn.sigmoid(jnp.dot(xcb, wa_ref[...], preferred_element_type=F32) + ba_ref[...])
    i = jax.nn.sigmoid(jnp.dot(xcb, wx_ref[...], preferred_element_type=F32) + bx_ref[...])
    log_a = -LRU_C * r * _softplus(-lam_ref[...])
    a = jnp.exp(log_a)
    u = jnp.sqrt(1.0 - jnp.exp(2.0 * log_a)) * (i * xc)
    row = _iota((q, GROUP_W), 0)
    s = 1
    while s < q:
        a_sh = pltpu.roll(a, s, 0)
        u_sh = pltpu.roll(u, s, 0)
        keep = row >= s
        u = jnp.where(keep, a * u_sh + u, u)
        a = jnp.where(keep, a * a_sh, a)
        s *= 2
    h = u + a * hc[...]
    hc[...] = h[q - 1:q, :]
    gg = g_ref[0]
    gelu = 0.5 * gg * (1.0 + jnp.tanh(math.sqrt(2.0 / math.pi) * (gg + 0.044715 * (gg * gg * gg))))
    y_ref[0] = h * gelu

    @pl.when(c == pl.num_programs(1) - 1)
    def _():
        hout_ref[0] = h[q - 1:q, :]


def _lru(pj, nb, sl, prev8, h0, p):
    q = min(256, sl)
    nc = sl // q
    row = lambda w: pl.BlockSpec((1, q, w), lambda b, c: (b, c, 0))
    per_b = lambda r, w: pl.BlockSpec((1, r, w), lambda b, c: (b, 0, 0))
    const = lambda r, w: pl.BlockSpec((r, w), lambda b, c: (0, 0))
    y, hout = pl.pallas_call(
        functools.partial(_lru_kernel, q=q),
        grid=(nb, nc),
        in_specs=[row(256), row(256), per_b(8, 256), per_b(1, 256), const(8, 256), const(1, 256),
                  const(256, 256), const(1, 256), const(256, 256), const(1, 256), const(1, 256)],
        out_specs=[row(256), per_b(1, 256)],
        out_shape=[jax.ShapeDtypeStruct((nb, sl, 256), F32), jax.ShapeDtypeStruct((nb, 1, 256), F32)],
        scratch_shapes=[pltpu.VMEM((q + 8, 256), F32), pltpu.VMEM((1, 256), F32)],
        compiler_params=_params(("parallel", "arbitrary")),
        name="lru",
    )(pj["lx"].reshape(nb, sl, 256), pj["lg"].reshape(nb, sl, 256), prev8, h0.reshape(nb, 1, 256),
      p["lru_cw"], p["lru_cb"], p["lru_wa"], p["lru_ba"], p["lru_wx"], p["lru_bx"], p["lru_lam"])
    return y.reshape(nb * sl, 256), hout.reshape(nb, 256)


def _summ_kernel(pt_ref, *refs, pps):
    kp, vp = refs[:pps], refs[pps:2 * pps]
    wk_ref, wv_ref, ko_ref, vo_ref = refs[2 * pps:]
    per_page = PAGE // NSA_CMP_BLOCK
    for r in range(pps):
        xk = kp[r][0] * wk_ref[...]
        xv = vp[r][0] * wv_ref[...]
        ko_ref[0, per_page * r:per_page * (r + 1), :] = xk.reshape(per_page, NSA_CMP_BLOCK, LANES).sum(axis=1)
        vo_ref[0, per_page * r:per_page * (r + 1), :] = xv.reshape(per_page, NSA_CMP_BLOCK, LANES).sum(axis=1)


def _summarize(k_pages, v_pages, table, wk_tile, wv_tile):
    nb, n_pages = table.shape
    pps = min(8, n_pages)
    per_page = PAGE // NSA_CMP_BLOCK
    page = lambda r: pl.BlockSpec((1, PAGE, LANES), lambda b, s, pt, r=r: (pt[b, s * pps + r], 0, 0))
    wspec = pl.BlockSpec((PAGE, LANES), lambda b, s, pt: (0, 0))
    ospec = pl.BlockSpec((1, pps * per_page, LANES), lambda b, s, pt: (b, s, 0))
    oshape = jax.ShapeDtypeStruct((nb, n_pages * per_page, LANES), F32)
    return pl.pallas_call(
        functools.partial(_summ_kernel, pps=pps),
        grid_spec=pltpu.PrefetchScalarGridSpec(
            num_scalar_prefetch=1, grid=(nb, n_pages // pps),
            in_specs=[page(r) for r in range(pps)] * 2 + [wspec, wspec],
            out_specs=[ospec, ospec]),
        out_shape=[oshape, oshape],
        compiler_params=_params(("parallel", "arbitrary")),
        name="nsa_summarize",
    )(table, *([k_pages] * pps), *([v_pages] * pps), wk_tile, wv_tile)


def _cmpsel_kernel(q_ref, kc_ref, vc_ref, ocmp_ref, sel_ref, *rest, qt, nc, ns, nsp, past, n_chunks):
    qi = pl.program_id(1)
    q4 = _nsa_expand(q_ref[0] * 0.125)
    s = _dot_nt(q4, kc_ref[0])
    qpos = past + qi * qt + (_iota((4 * qt, nc), 0) & (qt - 1))
    blk_end = (_iota((4 * qt, nc), 1) + 1) * NSA_CMP_BLOCK - 1
    s = jnp.where(blk_end <= qpos, s, NEG_INF)
    m = jnp.max(s, axis=-1, keepdims=True)
    m = jnp.where(m == NEG_INF, 0.0, m)
    e = jnp.exp(s - m)
    p = e / jnp.maximum(jnp.sum(e, axis=-1, keepdims=True), 1e-30)
    ocmp_ref[0] = _nsa_collect(_dot(p, vc_ref[0]), qt)
    impc = jnp.concatenate([p[0:qt] + p[qt:2 * qt], p[2 * qt:3 * qt] + p[3 * qt:4 * qt]], axis=0)
    pair = (_iota((nc, nsp), 0) >> 1 == _iota((nc, nsp), 1)).astype(BF16)
    imp = _dot_01_r(impc, pair)
    blk = _iota((2 * qt, nsp), 1)
    cur = (past + qi * qt + (_iota((2 * qt, nsp), 0) & (qt - 1))) >> 6
    forced = (blk == 0) | (blk == cur) | (blk == cur - 1)
    score = jnp.where(forced, FORCE_SCORE, jnp.where(blk > cur, -1.0, imp))
    score = jnp.where(blk < ns, score, -3.0)

    def body(j, cnt):
        col = jnp.sum(jnp.where(blk == j, score, 0.0), axis=1, keepdims=True)
        beats = (col > score) | ((col == score) & (j < blk))
        return cnt + beats.astype(F32)

    cnt = lax.fori_loop(0, ns, body, jnp.zeros((2 * qt, nsp), F32))
    sel = ((cnt < float(NSA_TOPN)) & (blk < ns)).astype(F32)
    sel_ref[0, 0] = sel[0:qt]
    sel_ref[0, 1] = sel[qt:2 * qt]
    if n_chunks:
        selx_ref = rest[0]
        sel4 = jnp.concatenate([sel[0:qt], sel[0:qt], sel[qt:2 * qt], sel[qt:2 * qt]], axis=0).astype(BF16)
        e16 = (_iota((LANES, 1024), 1) >> 6 == _iota((LANES, 1024), 0)).astype(BF16)
        for t in range(n_chunks):
            pick = ((_iota((nsp, LANES), 0) == 16 * t + _iota((nsp, LANES), 1))
                    & (_iota((nsp, LANES), 1) < 16)).astype(BF16)
            blocks = jnp.dot(sel4, pick, preferred_element_type=F32).astype(BF16)
            selx_ref[0, :, 1024 * t:1024 * (t + 1)] = jnp.dot(blocks, e16, preferred_element_type=F32)


def _dot_01_r(x, m01):
    return sum(jnp.dot(part, m01, preferred_element_type=F32) for part in _split3(x))


def _cmp_select(qn, kcmp, vcmp, nb, sl, qt, past, ns, n_chunks):
    nc = kcmp.shape[1]
    nsp = -(-ns // LANES) * LANES
    out_specs = [pl.BlockSpec((1, qt, 256), lambda b, i: (b, i, 0)),
                 pl.BlockSpec((1, 2, qt, nsp), lambda b, i: (b, 0, i, 0))]
    out_shape = [jax.ShapeDtypeStruct((nb, sl, 256), F32), jax.ShapeDtypeStruct((nb, 2, sl, nsp), F32)]
    if n_chunks:
        out_specs.append(pl.BlockSpec((1, 4 * qt, 1024 * n_chunks), lambda b, i: (b, i, 0)))
        out_shape.append(jax.ShapeDtypeStruct((nb, 4 * sl, 1024 * n_chunks), F32))
    return pl.pallas_call(
        functools.partial(_cmpsel_kernel, qt=qt, nc=nc, ns=ns, nsp=nsp, past=past, n_chunks=n_chunks),
        grid=(nb, sl // qt),
        in_specs=[pl.BlockSpec((1, qt, 256), lambda b, i: (b, i, 0)),
                  pl.BlockSpec((1, nc, LANES), lambda b, i: (b, 0, 0)),
                  pl.BlockSpec((1, nc, LANES), lambda b, i: (b, 0, 0))],
        out_specs=out_specs, out_shape=out_shape,
        compiler_params=_params(("parallel", "parallel")),
        name="nsa_cmp_select",
    )(qn.reshape(nb, sl, 256), kcmp, vcmp)


def _nsap_kernel(q_ref, ks_ref, vs_ref, kw_ref, vw_ref, sel_ref, oslc_ref, owin_ref,
                 q4_s, sel4_s, m1, l1, a1, m2, l2, a2, *, qt, kt):
    qi, kj = pl.program_id(1), pl.program_id(2)
    last = (qi * qt + qt - 1) // kt
    first_w = jnp.maximum(qi * qt - (NSA_WINDOW - 1), 0) // kt

    @pl.when(kj == 0)
    def _():
        q4_s[...] = _nsa_expand(q_ref[0] * 0.125).astype(BF16)
        s0, s1 = sel_ref[0, 0], sel_ref[0, 1]
        sel4_s[...] = jnp.concatenate([s0, s0, s1, s1], axis=0).astype(BF16)
        for m_, l_, a_ in ((m1, l1, a1), (m2, l2, a2)):
            m_[...] = jnp.full(m_.shape, NEG_INF, F32)
            l_[...] = jnp.zeros(l_.shape, F32)
            a_[...] = jnp.zeros(a_.shape, F32)

    qpos = qi * qt + (_iota((4 * qt, kt), 0) & (qt - 1))
    kpos = kj * kt + _iota((4 * qt, kt), 1)

    @pl.when(kj <= last)
    def _():
        expand = (_iota((LANES, kt), 0) == (kj * kt + _iota((LANES, kt), 1)) >> 6).astype(BF16)
        picked = jnp.dot(sel4_s[...], expand, preferred_element_type=F32) > 0.5
        s = lax.dot_general(q4_s[...], ks_ref[0].astype(BF16), (((1,), (1,)), ((), ())), preferred_element_type=F32)
        _online_update(s, picked & (kpos <= qpos), vs_ref[0].astype(BF16), m1, l1, a1)

    @pl.when((kj <= last) & (kj >= first_w))
    def _():
        s = lax.dot_general(q4_s[...], kw_ref[0].astype(BF16), (((1,), (1,)), ((), ())), preferred_element_type=F32)
        _online_update(s, (kpos <= qpos) & (kpos > qpos - NSA_WINDOW), vw_ref[0].astype(BF16), m2, l2, a2)

    @pl.when(kj == last)
    def _():
        oslc_ref[0] = _nsa_collect(a1[...] / jnp.maximum(l1[...], 1e-30), qt)
        owin_ref[0] = _nsa_collect(a2[...] / jnp.maximum(l2[...], 1e-30), qt)


def _nsa_prompt_attn(pj, sel, nb, sl):
    qt, kt = min(128, sl), min(256, sl)
    last = lambda i: (i * qt + qt - 1) // kt
    first_w = lambda i: jnp.maximum(i * qt - (NSA_WINDOW - 1), 0) // kt
    kspec = pl.BlockSpec((1, kt, LANES), lambda b, i, j: (b, jnp.minimum(j, last(i)), 0))
    wspec = pl.BlockSpec((1, kt, LANES), lambda b, i, j: (b, jnp.clip(j, first_w(i), last(i)), 0))
    qspec = pl.BlockSpec((1, qt, 256), lambda b, i, j: (b, i, 0))
    nsp = sel.shape[-1]
    r3 = lambda a, w: a.reshape(nb, sl, w)
    return pl.pallas_call(
        functools.partial(_nsap_kernel, qt=qt, kt=kt),
        grid=(nb, sl // qt, sl // kt),
        in_specs=[qspec, kspec, kspec, wspec, wspec, pl.BlockSpec((1, 2, qt, nsp), lambda b, i, j: (b, 0, i, 0))],
        out_specs=[qspec, qspec],
        out_shape=[jax.ShapeDtypeStruct((nb, sl, 256), F32)] * 2,
        scratch_shapes=[pltpu.VMEM((4 * qt, LANES), BF16), pltpu.VMEM((4 * qt, nsp), BF16)]
        + [pltpu.VMEM((4 * qt, 1), F32), pltpu.VMEM((4 * qt, 1), F32), pltpu.VMEM((4 * qt, LANES), F32)] * 2,
        compiler_params=_params(("parallel", "parallel", "arbitrary")),
        name="nsa_prompt_attn",
    )(r3(pj["qnr"], 256), r3(pj["ks"], 128), r3(pj["vs"], 128), r3(pj["kw"], 128), r3(pj["vw"], 128), sel)


def _combine_kernel(misc_ref, oc_ref, os_ref, ow_ref, y_ref):
    g = jax.nn.sigmoid(misc_ref[...])
    head = _iota(oc_ref.shape, 1) >> 6
    y_ref[...] = (_head_bcast(g, head, 4, GATE_LANE0) * oc_ref[...]
                  + _head_bcast(g, head, 4, GATE_LANE0 + 4) * os_ref[...]
                  + _head_bcast(g, head, 4, GATE_LANE0 + 8) * ow_ref[...])


def _nsa_combine(misc, oc, os_, ow, tm):
    t = misc.shape[0]
    spec = lambda w: pl.BlockSpec((tm, w), lambda i: (i, 0))
    return pl.pallas_call(
        _combine_kernel, grid=(t // tm,),
        in_specs=[spec(128), spec(256), spec(256), spec(256)], out_specs=spec(256),
        out_shape=jax.ShapeDtypeStruct((t, 256), F32),
        compiler_params=_params(("parallel",)), name="nsa_combine",
    )(misc, oc.reshape(t, 256), os_.reshape(t, 256), ow.reshape(t, 256))


def _diff_lambda(lamv_ref, lam_init):
    lv = lamv_ref[...]
    s1 = jnp.sum(lv[0:1] * lv[1:2], axis=1, keepdims=True)
    s2 = jnp.sum(lv[2:3] * lv[3:4], axis=1, keepdims=True)
    return jnp.exp(s1) - jnp.exp(s2) + lam_init


def _diff_finish(o, head, nheads, nw, lam_init):
    inv = jnp.zeros(o.shape, F32)
    for h in range(nheads):
        ms = jnp.sum(jnp.where(head == h, o * o, 0.0), axis=-1, keepdims=True) * (1.0 / 64.0)
        inv = jnp.where(head == h, lax.rsqrt(ms + EPS), inv)
    return o * inv * nw * (1.0 - lam_init)


def _diffp_kernel(q_ref, k_ref, v_ref, lamv_ref, nw_ref, o_ref, q4_s, m_s, l_s, acc_s, *, qt, kt, lam_init):
    qi, kj = pl.program_id(2), pl.program_id(3)
    last = (qi * qt + qt - 1) // kt

    @pl.when(kj == 0)
    def _():
        q = q_ref[0] * (32.0 ** -0.5)
        part = _iota((qt, LANES), 1) >> 5
        q4_s[...] = jnp.concatenate([jnp.where(part == k, q, 0.0) for k in range(4)], axis=0).astype(BF16)
        m_s[...] = jnp.full(m_s.shape, NEG_INF, F32)
        l_s[...] = jnp.zeros(l_s.shape, F32)
        acc_s[...] = jnp.zeros(acc_s.shape, F32)

    @pl.when(kj <= last)
    def _():
        s = lax.dot_general(q4_s[...], k_ref[0].astype(BF16), (((1,), (1,)), ((), ())), preferred_element_type=F32)
        qpos = qi * qt + (_iota((4 * qt, kt), 0) & (qt - 1))
        kpos = kj * kt + _iota((4 * qt, kt), 1)
        _online_update(s, kpos <= qpos, v_ref[0].astype(BF16), m_s, l_s, acc_s)

    @pl.when(kj == last)
    def _():
        lam = _diff_lambda(lamv_ref, lam_init)
        o = acc_s[...] / jnp.maximum(l_s[...], 1e-30)
        oa = o[0:qt] - lam * o[qt:2 * qt]
        ob = o[2 * qt:3 * qt] - lam * o[3 * qt:4 * qt]
        head = _iota((qt, LANES), 1) >> 6
        o_ref[0] = _diff_finish(jnp.where(head == 0, oa, ob), head, 2, nw_ref[...], lam_init)


def _diff_prompt(pj, nb, sl, p, lam_init):
    qt, kt = min(256, sl), min(512, sl)
    last = lambda i: (i * qt + qt - 1) // kt
    qspec = pl.BlockSpec((1, qt, LANES), lambda b, h, i, j: (b, i, h))
    kspec = pl.BlockSpec((1, kt, LANES), lambda b, h, i, j: (b, jnp.minimum(j, last(i)), h))
    r3 = lambda a: a.reshape(nb, sl, 256)
    out = pl.pallas_call(
        functools.partial(_diffp_kernel, qt=qt, kt=kt, lam_init=lam_init),
        grid=(nb, 2, sl // qt, sl // kt),
        in_specs=[qspec, kspec, kspec, pl.BlockSpec((4, 32), lambda b, h, i, j: (0, 0)),
                  pl.BlockSpec((1, LANES), lambda b, h, i, j: (0, 0))],
        out_specs=qspec,
        out_shape=jax.ShapeDtypeStruct((nb, sl, 256), F32),
        scratch_shapes=[pltpu.VMEM((4 * qt, LANES), BF16), pltpu.VMEM((4 * qt, 1), F32),
                        pltpu.VMEM((4 * qt, 1), F32), pltpu.VMEM((4 * qt, LANES), F32)],
        compiler_params=_params(("parallel", "parallel", "parallel", "arbitrary")),
        name="diff_prompt",
    )(r3(pj["dq"]), r3(pj["dk"]), r3(pj["dv"]), p["diff_lambda"], p["diff_nw256"][:, :LANES])
    return out.reshape(nb * sl, 256)


def _paged_kernel(pt_ref, q_ref, *refs, mode, pps, width, kbase, past, new_len, lam_init):
    kp, vp = refs[:pps], refs[pps:2 * pps]
    knew_ref, vnew_ref = refs[2 * pps:2 * pps + 2]
    rest = list(refs[2 * pps + 2:])
    selx_ref = selnew_ref = lamv_ref = nw_ref = None
    if mode == "slc":
        selx_ref, selnew_ref = rest[:2]
        rest = rest[2:]
    if mode == "diff":
        lamv_ref, nw_ref = rest[:2]
        rest = rest[2:]
    o_ref, qe_s, m_s, l_s, acc_s = rest
    nq = new_len
    rows = qe_s.shape[0]
    step = pl.program_id(1)

    @pl.when(step == 0)
    def _():
        if mode == "diff":
            q = q_ref[0] * (32.0 ** -0.5)
            part = _iota((nq, width), 1) >> 5
            qe = jnp.concatenate([jnp.where(part == k, q, 0.0) for k in range(8)], axis=0)
        else:
            qe = _nsa_expand(q_ref[0] * 0.125)
        qe_s[...] = qe.astype(BF16)
        m_s[...] = jnp.full(m_s.shape, NEG_INF, F32)
        l_s[...] = jnp.zeros(l_s.shape, F32)
        acc_s[...] = jnp.zeros(acc_s.shape, F32)

    nk = pps * PAGE
    k_b = jnp.concatenate([r[0].astype(BF16) for r in kp], axis=0)
    v_b = jnp.concatenate([r[0].astype(BF16) for r in vp], axis=0)
    s = lax.dot_general(qe_s[...], k_b, (((1,), (1,)), ((), ())), preferred_element_type=F32)
    qpos = past + (_iota((rows, nk), 0) & (nq - 1))
    kpos = kbase + step * nk + _iota((rows, nk), 1)
    mask = kpos <= qpos
    if mode == "win":
        mask = mask & (kpos > qpos - NSA_WINDOW)
    if mode == "slc":
        mask = mask & (selx_ref[0] > 0.5)
    _online_update(s, mask, v_b, m_s, l_s, acc_s)

    @pl.when(step == pl.num_programs(1) - 1)
    def _():
        s2 = lax.dot_general(qe_s[...], knew_ref[0].astype(BF16), (((1,), (1,)), ((), ())),
                             preferred_element_type=F32)
        qpos2 = past + (_iota((rows, nq), 0) & (nq - 1))
        kpos2 = past + _iota((rows, nq), 1)
        mask2 = kpos2 <= qpos2
        if mode == "win":
            mask2 = mask2 & (kpos2 > qpos2 - NSA_WINDOW)
        if mode == "slc":
            mask2 = mask2 & (selnew_ref[0][:, 0:nq] > 0.5)
        _online_update(s2, mask2, vnew_ref[0].astype(BF16), m_s, l_s, acc_s)
        o = acc_s[...] / jnp.maximum(l_s[...], 1e-30)
        if mode == "diff":
            lam = _diff_lambda(lamv_ref, lam_init)
            head = _iota((nq, width), 1) >> 6
            out = jnp.zeros((nq, width), F32)
            for h in range(4):
                o0 = o[(2 * h) * nq:(2 * h + 1) * nq]
                o1 = o[(2 * h + 1) * nq:(2 * h + 2) * nq]
                out = jnp.where(head == h, o0 - lam * o1, out)
            o_ref[0] = _diff_finish(out, head, 4, nw_ref[...], lam_init)
        else:
            o_ref[0] = _nsa_collect(o, nq)


def _paged_attn(mode, q, k_pages, v_pages, table, knew, vnew, kbase, past, extra=(), lam_init=0.0):
    nb, n_pages = table.shape
    nq = q.shape[1]
    width = k_pages.shape[-1]
    pps = min(8, n_pages)
    rows = 8 * nq if mode == "diff" else 4 * nq
    nk = pps * PAGE
    page = lambda r: pl.BlockSpec((1, PAGE, width), lambda b, s, pt, r=r: (pt[b, s * pps + r], 0, 0))
    per_b = lambda r_, w: pl.BlockSpec((1, r_, w), lambda b, s, pt: (b, 0, 0))
    in_specs = [per_b(nq, 256)] + [page(r) for r in range(pps)] * 2 + [per_b(nq, width), per_b(nq, width)]
    args = [q] + [k_pages] * pps + [v_pages] * pps + [knew, vnew]
    if mode == "slc":
        (selx,) = extra
        new_blk = (past - kbase) // LANES
        in_specs += [pl.BlockSpec((1, rows, nk), lambda b, s, pt: (b, 0, s)),
                     pl.BlockSpec((1, rows, LANES), lambda b, s, pt: (b, 0, new_blk))]
        args += [selx, selx]
    if mode == "diff":
        in_specs += [pl.BlockSpec((4, 32), lambda b, s, pt: (0, 0)), pl.BlockSpec((1, 256), lambda b, s, pt: (0, 0))]
        args += list(extra)
    return pl.pallas_call(
        functools.partial(_paged_kernel, mode=mode, pps=pps, width=width, kbase=kbase, past=past, new_len=nq,
                          lam_init=lam_init),
        grid_spec=pltpu.PrefetchScalarGridSpec(
            num_scalar_prefetch=1, grid=(nb, n_pages // pps), in_specs=in_specs,
            out_specs=per_b(nq, 256),
            scratch_shapes=[pltpu.VMEM((rows, width), BF16), pltpu.VMEM((rows, 1), F32),
                            pltpu.VMEM((rows, 1), F32), pltpu.VMEM((rows, width), F32)]),
        out_shape=jax.ShapeDtypeStruct((nb, nq, 256), F32),
        compiler_params=_params(("parallel", "arbitrary")),
        name="paged_" + mode,
    )(table, *args)


def _post_kernel(x_ref, ya_ref, yb_ref, yc_ref, yd_ref, wo_ref, g1_ref, b1_ref, w1_ref, w2_ref, g2_ref, b2_ref,
                 o_ref, x1_s, x1b_s, acc_s, *, alpha):
    f = pl.program_id(1)

    @pl.when(f == 0)
    def _():
        mix = _dot(ya_ref[...], wo_ref[0:256, :])
        mix = mix + _dot(yb_ref[...], wo_ref[256:512, :])
        mix = mix + _dot(yc_ref[...], wo_ref[512:768, :])
        mix = mix + _dot(yd_ref[...], wo_ref[768:1024, :])
        x1 = _layer_norm(alpha * x_ref[...] + mix, g1_ref[...], b1_ref[...])
        x1_s[...] = x1
        x1b_s[...] = x1.astype(BF16)
        acc_s[...] = jnp.zeros(acc_s.shape, F32)

    h = jnp.dot(x1b_s[...], w1_ref[...], preferred_element_type=F32)
    h = jnp.square(jnp.maximum(h, 0.0))
    acc_s[...] += jnp.dot(h.astype(BF16), w2_ref[...], preferred_element_type=F32)

    @pl.when(f == pl.num_programs(1) - 1)
    def _():
        o_ref[...] = _layer_norm(alpha * x1_s[...] + acc_s[...], g2_ref[...], b2_ref[...])


def _post(x2d, ya, yb, yc, yd, p, tm, alpha):
    t = x2d.shape[0]
    tf = 1024
    row = lambda w: pl.BlockSpec((tm, w), lambda i, f: (i, 0))
    const = lambda r, w: pl.BlockSpec((r, w), lambda i, f: (0, 0))
    return pl.pallas_call(
        functools.partial(_post_kernel, alpha=alpha),
        grid=(t // tm, D_FF // tf),
        in_specs=[row(D_MODEL), row(256), row(256), row(256), row(256), const(D_MODEL, D_MODEL),
                  const(1, D_MODEL), const(1, D_MODEL),
                  pl.BlockSpec((D_MODEL, tf), lambda i, f: (0, f)), pl.BlockSpec((tf, D_MODEL), lambda i, f: (f, 0)),
                  const(1, D_MODEL), const(1, D_MODEL)],
        out_specs=row(D_MODEL),
        out_shape=jax.ShapeDtypeStruct((t, D_MODEL), F32),
        scratch_shapes=[pltpu.VMEM((tm, D_MODEL), F32), pltpu.VMEM((tm, D_MODEL), BF16),
                        pltpu.VMEM((tm, D_MODEL), F32)],
        compiler_params=_params(("parallel", "arbitrary")),
        name="post",
    )(x2d, ya, yb, yc, yd, p["w_out"], p["ln1_g"], p["ln1_b"], p["w_ff1"], p["w_ff2"], p["ln2_g"], p["ln2_b"])


def _prep_params(w, l):
    w_in = w["w_in"][l]
    misc = jnp.concatenate([w_in[:, a:b] for a, b in MISC_COLS], axis=1)
    misc = jnp.pad(misc, ((0, 0), (0, LANES - misc.shape[1])))
    w_in_p = jnp.concatenate([w_in[:, a:b] for a, b in W_IN_ORDER] + [misc], axis=1).astype(BF16)
    row = lambda v: v.reshape(1, -1).astype(F32)
    pad_lanes = lambda v: jnp.pad(v.reshape(1, -1), ((0, 0), (0, LANES - v.shape[-1])))
    blockdiag = lambda m: jax.scipy.linalg.block_diag(*[m[i] for i in range(m.shape[0])]).astype(BF16)
    cmp_tile = lambda wc: jnp.tile(jnp.repeat(wc.T, 64, axis=1), (PAGE // NSA_CMP_BLOCK, 1))
    return dict(
        w_in=w_in_p,
        ssd_cw=jnp.pad(w["ssd_conv_w"][l], ((0, 4), (0, 0))), ssd_cb=row(w["ssd_conv_b"][l]),
        ssd_dtb=pad_lanes(w["ssd_dt_bias"][l]), ssd_alog=pad_lanes(w["ssd_A_log"][l]),
        ssd_dexp=row(jnp.repeat(w["ssd_D"][l], 64)), ssd_nw=row(w["ssd_norm_w"][l]),
        cmp_wk=cmp_tile(w["nsa_w_cmp_k"][l]), cmp_wv=cmp_tile(w["nsa_w_cmp_v"][l]),
        diff_lambda=w["diff_lambda"][l], diff_nw256=row(jnp.tile(w["diff_norm_w"][l], 4)),
        lru_cw=jnp.pad(w["lru_conv_w"][l], ((0, 4), (0, 0))), lru_cb=row(w["lru_conv_b"][l]),
        lru_wa=blockdiag(w["lru_w_a"][l]), lru_ba=row(w["lru_b_a"][l]),
        lru_wx=blockdiag(w["lru_w_x"][l]), lru_bx=row(w["lru_b_x"][l]), lru_lam=row(w["lru_lambda"][l]),
        w_out=w["w_out"][l].astype(BF16), ln1_g=row(w["ln1_g"][l]), ln1_b=row(w["ln1_b"][l]),
        w_ff1=w["w_ff1"][l].astype(BF16), w_ff2=w["w_ff2"][l].astype(BF16),
        ln2_g=row(w["ln2_g"][l]), ln2_b=row(w["ln2_b"][l]),
    )


def _pad_prev(prev):
    return jnp.pad(prev, ((0, 0), (5, 0), (0, 0)))


def _layer(x2d, nb, sl, past, hist, p, lam_init, alpha):
    t = nb * sl
    tm = min(512, t)
    pos = past + jnp.arange(sl, dtype=jnp.int32)
    pj = _proj(x2d, p["w_in"], pos, tm)
    y_a, ssd_h = _ssd(pj, nb, sl, _pad_prev(hist["ssd_conv"]), hist["ssd_h"].reshape(nb, 256, 128), p)
    y_d, lru_h = _lru(pj, nb, sl, _pad_prev(hist["lru_conv"]), hist["lru_h"], p)
    r3 = lambda a, w: a.reshape(nb, sl, w)
    if past == 0:
        n_pages = sl // PAGE
        table = (jnp.arange(nb, dtype=jnp.int32)[:, None] * n_pages + jnp.arange(n_pages, dtype=jnp.int32)[None, :])
        kcmp, vcmp = _summarize(pj["kc"].reshape(nb * n_pages, PAGE, LANES), pj["vc"].reshape(nb * n_pages, PAGE, LANES),
                                table, p["cmp_wk"], p["cmp_wv"])
        ns = sl // NSA_SLC_BLOCK
        o_cmp, sel = _cmp_select(pj["qn"], kcmp, vcmp, nb, sl, min(128, sl), 0, ns, 0)
        o_slc, o_win = _nsa_prompt_attn(pj, sel, nb, sl)
        y_c = _diff_prompt(pj, nb, sl, p, lam_init)
        kw_buf, vw_buf = r3(pj["kw"], 128)[:, sl - NSA_WINDOW:], r3(pj["vw"], 128)[:, sl - NSA_WINDOW:]
    else:
        table = hist["table"]
        kcmp, vcmp = _summarize(hist["kc"], hist["vc"], table, p["cmp_wk"], p["cmp_wv"])
        ns = -(-(past + sl) // NSA_SLC_BLOCK)
        n_chunks = -(-(past + sl) // 1024)
        o_cmp, _, selx = _cmp_select(pj["qn"], kcmp, vcmp, nb, sl, sl, past, ns, n_chunks)
        qnr = r3(pj["qnr"], 256)
        o_slc = _paged_attn("slc", qnr, hist["ks"], hist["vs"], table, r3(pj["ks"], 128), r3(pj["vs"], 128),
                            0, past, extra=(selx,))
        wbuf = hist["kw"].shape[1]
        wtable = (jnp.arange(nb, dtype=jnp.int32)[:, None] * (wbuf // PAGE)
                  + jnp.arange(wbuf // PAGE, dtype=jnp.int32)[None, :])
        o_win = _paged_attn("win", qnr, hist["kw"].reshape(-1, PAGE, LANES), hist["vw"].reshape(-1, PAGE, LANES),
                            wtable, r3(pj["kw"], 128), r3(pj["vw"], 128), past - wbuf, past)
        y_c = _paged_attn("diff", r3(pj["dq"], 256), hist["dk"], hist["dv"], table, r3(pj["dk"], 256),
                          r3(pj["dv"], 256), 0, past, extra=(p["diff_lambda"], p["diff_nw256"]),
                          lam_init=lam_init).reshape(t, 256)
        kw_all = jnp.concatenate([hist["kw"], r3(pj["kw"], 128)], axis=1)
        vw_all = jnp.concatenate([hist["vw"], r3(pj["vw"], 128)], axis=1)
        kw_buf, vw_buf = kw_all[:, -NSA_WINDOW:], vw_all[:, -NSA_WINDOW:]
    y_b = _nsa_combine(pj["misc"], o_cmp, o_slc, o_win, tm)
    x_out = _post(x2d, y_a, y_b, y_c, y_d, p, tm, alpha)
    kv = lambda a: r3(a, 128).reshape(nb, sl, 2, 64)
    new = dict(kc=kv(pj["kc"]), vc=kv(pj["vc"]), ks=kv(pj["ks"]), vs=kv(pj["vs"]),
               dk=r3(pj["dk"], 256).reshape(nb, sl, 4, 64), dv=r3(pj["dv"], 256).reshape(nb, sl, 4, 64),
               kw=kw_buf.reshape(nb, -1, 2, 64), vw=vw_buf.reshape(nb, -1, 2, 64),
               ssd_h=ssd_h.reshape(nb, 4, 64, 128), ssd_conv=r3(pj["xbc"], 768)[:, sl - 3:],
               lru_h=lru_h, lru_conv=r3(pj["lx"], 256)[:, sl - 3:])
    return x_out, new


STATE_ORDER = ("kc", "vc", "ks", "vs", "dk", "dv", "kw", "vw", "ssd_h", "ssd_conv", "lru_h", "lru_conv")


def kernel(x_prompt, x_sample, cache_nsa_k_cmp, cache_nsa_v_cmp, cache_nsa_k_slc, cache_nsa_v_slc, cache_diff_k, cache_diff_v, cache_nsa_k_win, cache_nsa_v_win, state_ssd, state_ssd_conv, state_lru, state_lru_conv, page_table, w_in, ssd_conv_w, ssd_conv_b, ssd_dt_bias, ssd_A_log, ssd_D, ssd_norm_w, nsa_w_cmp_k, nsa_w_cmp_v, diff_lambda, diff_norm_w, lru_conv_w, lru_conv_b, lru_w_a, lru_b_a, lru_w_x, lru_b_x, lru_lambda, w_out, ln1_g, ln1_b, w_ff1, w_ff2, ln2_g, ln2_b):
    weights = dict(w_in=w_in, ssd_conv_w=ssd_conv_w, ssd_conv_b=ssd_conv_b, ssd_dt_bias=ssd_dt_bias,
                   ssd_A_log=ssd_A_log, ssd_D=ssd_D, ssd_norm_w=ssd_norm_w, nsa_w_cmp_k=nsa_w_cmp_k,
                   nsa_w_cmp_v=nsa_w_cmp_v, diff_lambda=diff_lambda, diff_norm_w=diff_norm_w, lru_conv_w=lru_conv_w,
                   lru_conv_b=lru_conv_b, lru_w_a=lru_w_a, lru_b_a=lru_b_a, lru_w_x=lru_w_x, lru_b_x=lru_b_x,
                   lru_lambda=lru_lambda, w_out=w_out, ln1_g=ln1_g, ln1_b=ln1_b, w_ff1=w_ff1, w_ff2=w_ff2,
                   ln2_g=ln2_g, ln2_b=ln2_b)
    depth = w_in.shape[0]
    nbp, slp, _ = x_prompt.shape
    nbs, sls, _ = x_sample.shape
    past = page_table.shape[1] * cache_nsa_k_cmp.shape[2]
    alpha = (2 * depth) ** 0.25
    n_pool = cache_nsa_k_cmp.shape[1]
    xp = x_prompt.reshape(nbp * slp, D_MODEL)
    xs = x_sample.reshape(nbs * sls, D_MODEL)
    outs_p, outs_s = [], []
    for l in range(depth):
        p = _prep_params(weights, l)
        lam_init = 0.8 - 0.6 * math.exp(-0.3 * l)
        hist_p = dict(ssd_conv=jnp.zeros((nbp, 3, SSD_CONV_CH), F32), ssd_h=jnp.zeros((nbp, 4, 64, 128), F32),
                      lru_conv=jnp.zeros((nbp, 3, GROUP_W), F32), lru_h=jnp.zeros((nbp, GROUP_W), F32))
        hist_s = dict(table=page_table,
                      kc=cache_nsa_k_cmp[l].reshape(n_pool, PAGE, LANES), vc=cache_nsa_v_cmp[l].reshape(n_pool, PAGE, LANES),
                      ks=cache_nsa_k_slc[l].reshape(n_pool, PAGE, LANES), vs=cache_nsa_v_slc[l].reshape(n_pool, PAGE, LANES),
                      dk=cache_diff_k[l].reshape(n_pool, PAGE, 256), dv=cache_diff_v[l].reshape(n_pool, PAGE, 256),
                      kw=cache_nsa_k_win[l].reshape(nbs, -1, LANES), vw=cache_nsa_v_win[l].reshape(nbs, -1, LANES),
                      ssd_conv=state_ssd_conv[l], ssd_h=state_ssd[l], lru_conv=state_lru_conv[l], lru_h=state_lru[l])
        xp, new_p = _layer(xp, nbp, slp, 0, hist_p, p, lam_init, alpha)
        xs, new_s = _layer(xs, nbs, sls, past, hist_s, p, lam_init, alpha)
        outs_p.append(new_p)
        outs_s.append(new_s)
    stack = lambda lst, name: jnp.stack([d[name] for d in lst])
    return ((xp.reshape(nbp, slp, D_MODEL), xs.reshape(nbs, sls, D_MODEL))
            + tuple(stack(outs_p, n) for n in STATE_ORDER) + tuple(stack(outs_s, n) for n in STATE_ORDER))
```

```python
import functools
import math

import jax
import jax.numpy as jnp
from jax import lax
from jax.experimental import pallas as pl
from jax.experimental.pallas import tpu as pltpu

F32 = jnp.float32
BF16 = jnp.bfloat16
NEG_INF = float("-inf")

D_MODEL = 1024
GROUP_W = 256
SSD_HEADS = 4
SSD_STATE = 128
SSD_CONV_CH = 768
NSA_CMP_BLOCK = 32
NSA_SLC_BLOCK = 64
NSA_TOPN = 16
NSA_WINDOW = 512
FORCE_SCORE = 1e4
LRU_C = 8.0
D_FF = 4096
ROPE_THETA = 500000.0
EPS = 1e-5
PAGE = 128
LANES = 128
VMEM_LIMIT = 56 * 1024 * 1024

W_IN_ORDER = ((0, 1024), (1028, 2052), (2064, 3344))
MISC_COLS = ((1024, 1028), (2052, 2064))
SEG = dict(z=(0, 256), xbc=(256, 1024), qn=(1024, 1280), kc=(1280, 1408), vc=(1408, 1536), ks=(1536, 1664),
           vs=(1664, 1792), kw=(1792, 1920), vw=(1920, 2048), dq=(2048, 2304), dk=(2304, 2560), dv=(2560, 2816),
           lx=(2816, 3072), lg=(3072, 3328), misc=(3328, 3456))
D_IN_PAD = 3456
GATE_LANE0 = 4


def _params(sem):
    return pltpu.CompilerParams(dimension_semantics=sem, vmem_limit_bytes=VMEM_LIMIT)


def _iota(shape, dim):
    return lax.broadcasted_iota(jnp.int32, shape, dim)


def _dot(a, b):
    return jnp.dot(a.astype(BF16), b.astype(BF16), preferred_element_type=F32)


def _dot_nt(a, b):
    return lax.dot_general(a.astype(BF16), b.astype(BF16), (((1,), (1,)), ((), ())), preferred_element_type=F32)


def _split2(x):
    h1 = x.astype(BF16)
    return h1, (x - h1.astype(F32)).astype(BF16)


def _split3(x):
    h1 = x.astype(BF16)
    r1 = x - h1.astype(F32)
    h2 = r1.astype(BF16)
    h3 = (r1 - h2.astype(F32)).astype(BF16)
    return h1, h2, h3


def _dot_01(m01, x):
    return sum(jnp.dot(m01, part, preferred_element_type=F32) for part in _split3(x))


def _dot_01_r(x, m01):
    return sum(jnp.dot(part, m01, preferred_element_type=F32) for part in _split3(x))


def _dot_nt_01(m01, x):
    return sum(lax.dot_general(m01, part, (((1,), (1,)), ((), ())), preferred_element_type=F32)
               for part in _split3(x))


def _eye(n, m):
    return (_iota((n, m), 0) == _iota((n, m), 1)).astype(BF16)


def _softplus(x):
    return jnp.maximum(x, 0.0) + jnp.log1p(jnp.exp(-jnp.abs(x)))


def _head_bcast(cols, h_of_lane, nheads, lane0=0):
    out = jnp.zeros(h_of_lane.shape, F32)
    for h in range(nheads):
        out = jnp.where(h_of_lane == h, cols[:, lane0 + h:lane0 + h + 1], out)
    return out


def _online_update(s, mask, v_b, m_ref, l_ref, acc_ref, v_transposed=False):
    s = jnp.where(mask, s, NEG_INF)
    m_prev = m_ref[...]
    m_new = jnp.maximum(m_prev, jnp.max(s, axis=-1, keepdims=True))
    m_safe = jnp.where(m_new == NEG_INF, 0.0, m_new)
    alpha = jnp.exp(m_prev - m_safe)
    p = jnp.exp(s - m_safe)
    l_ref[...] = alpha * l_ref[...] + jnp.sum(p, axis=-1, keepdims=True)
    if v_transposed:
        pv = lax.dot_general(p.astype(BF16), v_b, (((1,), (1,)), ((), ())), preferred_element_type=F32)
    else:
        pv = jnp.dot(p.astype(BF16), v_b, preferred_element_type=F32)
    acc_ref[...] = alpha * acc_ref[...] + pv
    m_ref[...] = m_new


def _online_update_t(s, mask, vt_b, m_ref, l_ref, acc_ref):
    s = jnp.where(mask, s, NEG_INF)
    m_prev = m_ref[...]
    m_new = jnp.maximum(m_prev, jnp.max(s, axis=0, keepdims=True))
    m_safe = jnp.where(m_new == NEG_INF, 0.0, m_new)
    alpha = jnp.exp(m_prev - m_safe)
    p = jnp.exp(s - m_safe)
    l_ref[...] = alpha * l_ref[...] + jnp.sum(p, axis=0, keepdims=True)
    acc_ref[...] = alpha * acc_ref[...] + jnp.dot(vt_b, p.astype(BF16), preferred_element_type=F32)
    m_ref[...] = m_new


def _init_stats(*triples):
    for m_, l_, a_ in triples:
        m_[...] = jnp.full(m_.shape, NEG_INF, F32)
        l_[...] = jnp.zeros(l_.shape, F32)
        a_[...] = jnp.zeros(a_.shape, F32)


def _nsa_expand(q):
    lo = _iota((q.shape[0], LANES), 1) < 64
    a, b = q[:, :LANES], q[:, LANES:]
    h0 = jnp.where(lo, a, 0.0)
    h1 = jnp.where(lo, pltpu.roll(a, 64, 1), 0.0)
    h2 = jnp.where(lo, 0.0, pltpu.roll(b, 64, 1))
    h3 = jnp.where(lo, 0.0, b)
    return jnp.concatenate([h0, h1, h2, h3], axis=0)


def _nsa_collect(o, n):
    lo = _iota((n, LANES), 1) < 64
    o0, o1, o2, o3 = o[0:n], o[n:2 * n], o[2 * n:3 * n], o[3 * n:4 * n]
    left = jnp.where(lo, o0, pltpu.roll(o1, 64, 1))
    right = jnp.where(lo, pltpu.roll(o2, 64, 1), o3)
    return jnp.concatenate([left, right], axis=1)


def _collect_t(o_t, n):
    return jnp.concatenate([o_t[0:64, 0:n], o_t[0:64, n:2 * n], o_t[64:128, 2 * n:3 * n], o_t[64:128, 3 * n:4 * n]],
                           axis=0)


def _layer_norm(v, g, b):
    mu = jnp.mean(v, axis=-1, keepdims=True)
    d = v - mu
    var = jnp.mean(d * d, axis=-1, keepdims=True)
    return d * lax.rsqrt(var + EPS) * g + b


def _rope128(v, c, sa, sb, half):
    return v * c + pltpu.roll(v, LANES - half, 1) * sa + pltpu.roll(v, half, 1) * sb


PROJ_OUT = ("z", "xbc", "qn", "qnr", "kc", "vc", "ks", "vs", "kw", "vw", "dq", "dk", "dv", "lx", "lg", "misc")
PROJ_OUT_T = ("kcT", "vcT", "ksT", "vsT", "kwT", "vwT", "dkT", "dvT")
PROJ_W = dict(z=256, xbc=768, qn=256, qnr=256, kc=128, vc=128, ks=128, vs=128, kw=128, vw=128, dq=256, dk=256,
              dv=256, lx=256, lg=256, misc=128)


def _proj_kernel(x_ref, w_ref, cn_ref, san_ref, sbn_ref, cd_ref, sad_ref, sbd_ref, *outs, flat_t):
    o = dict(zip(PROJ_OUT + PROJ_OUT_T, outs))
    xb = x_ref[...].astype(BF16)

    def seg(name):
        a, b = SEG[name]
        return jnp.dot(xb, w_ref[:, a:b], preferred_element_type=F32)

    def rope_n(v):
        return _rope128(v, cn_ref[...], san_ref[...], sbn_ref[...], 8)

    def rope_d(v):
        return _rope128(v, cd_ref[...], sad_ref[...], sbd_ref[...], 4)

    def put(name, v):
        o[name][...] = v
        if name + "T" in o:
            if flat_t:
                o[name + "T"][...] = v.T
            else:
                o[name + "T"][0] = v.T

    put("z", seg("z"))
    put("xbc", seg("xbc"))
    qn = seg("qn")
    put("qn", qn)
    put("qnr", jnp.concatenate([rope_n(qn[:, :LANES]), rope_n(qn[:, LANES:])], axis=1))
    put("kc", seg("kc"))
    put("vc", seg("vc"))
    put("ks", rope_n(seg("ks")))
    put("vs", seg("vs"))
    put("kw", rope_n(seg("kw")))
    put("vw", seg("vw"))
    dq = seg("dq")
    put("dq", jnp.concatenate([rope_d(dq[:, :LANES]), rope_d(dq[:, LANES:])], axis=1))
    dk = seg("dk")
    put("dk", jnp.concatenate([rope_d(dk[:, :LANES]), rope_d(dk[:, LANES:])], axis=1))
    put("dv", seg("dv"))
    put("lx", seg("lx"))
    put("lg", seg("lg"))
    put("misc", seg("misc"))


def _rope_tables(pos, dh, rows):
    rd = dh // 4
    half = rd // 2
    inv = ROPE_THETA ** (-jnp.arange(half, dtype=F32) / half)
    ang = pos.astype(F32)[:, None] * inv[None, :]
    cos, sin = jnp.cos(ang), jnp.sin(ang)
    n = pos.shape[0]
    zero_h = jnp.zeros((n, half), F32)
    zero_r = jnp.zeros((n, dh - rd), F32)
    c = jnp.concatenate([cos, cos, jnp.ones((n, dh - rd), F32)], 1)
    sa = jnp.concatenate([-sin, zero_h, zero_r], 1)
    sb = jnp.concatenate([zero_h, sin, zero_r], 1)
    reps = (max(rows // n, 1), LANES // dh)
    return tuple(jnp.tile(t, reps) for t in (c, sa, sb))


def _proj(x2d, w_in_p, pos, nb, sl, tm):
    t = x2d.shape[0]
    n_tab = max(sl // tm, 1)
    flat_t = sl < tm
    tabs = _rope_tables(pos, 64, tm) + _rope_tables(pos, 32, tm)
    tab_spec = pl.BlockSpec((tm, LANES), lambda i: (i % n_tab, 0))
    out_specs = [pl.BlockSpec((tm, PROJ_W[k]), lambda i: (i, 0)) for k in PROJ_OUT]
    out_shape = [jax.ShapeDtypeStruct((t, PROJ_W[k]), F32) for k in PROJ_OUT]
    for k in PROJ_OUT_T:
        c = PROJ_W[k[:-1]]
        if flat_t:
            out_specs.append(pl.BlockSpec((c, tm), lambda i: (0, i)))
            out_shape.append(jax.ShapeDtypeStruct((c, t), F32))
        else:
            out_specs.append(pl.BlockSpec((1, c, tm), lambda i: (i // n_tab, 0, i % n_tab)))
            out_shape.append(jax.ShapeDtypeStruct((nb, c, sl), F32))
    outs = pl.pallas_call(
        functools.partial(_proj_kernel, flat_t=flat_t),
        grid=(t // tm,),
        in_specs=[pl.BlockSpec((tm, D_MODEL), lambda i: (i, 0)),
                  pl.BlockSpec((D_MODEL, D_IN_PAD), lambda i: (0, 0))] + [tab_spec] * 6,
        out_specs=out_specs, out_shape=out_shape,
        compiler_params=_params(("parallel",)),
        name="proj",
    )(x2d, w_in_p, *tabs)
    pj = dict(zip(PROJ_OUT + PROJ_OUT_T, outs))
    if flat_t:
        for k in PROJ_OUT_T:
            pj[k] = pj[k].reshape(-1, nb, sl).transpose(1, 0, 2)
    return pj


def _conv_chunk(x_ref, prev_ref, cw_ref, cb_ref, xbuf, q, first):
    @pl.when(first)
    def _():
        xbuf[0:8, :] = prev_ref[0]

    xbuf[8:8 + q, :] = x_ref[0]
    acc = cb_ref[...] + cw_ref[0:1, :] * xbuf[pl.ds(5, q), :]
    for j in range(1, 4):
        acc = acc + cw_ref[j:j + 1, :] * xbuf[pl.ds(5 + j, q), :]
    xbuf[0:8, :] = xbuf[q:q + 8, :]
    return acc


def _ssd_kernel(xbc_ref, z_ref, misc_ref, prev_ref, h0_ref, cw_ref, cb_ref, dtb_ref, alog_ref, dexp_ref, nw_ref,
                y_ref, hout_ref, xbuf, ht, *, q):
    c = pl.program_id(1)

    @pl.when(c == 0)
    def _():
        ht[...] = _dot_nt_01(_eye(SSD_STATE, SSD_STATE), h0_ref[0])

    acc = _conv_chunk(xbc_ref, prev_ref, cw_ref, cb_ref, xbuf, q, c == 0)
    xc = acc * jax.nn.sigmoid(acc)
    sx = xc[:, 0:256]
    sb = (xc[:, 256:384], xc[:, 384:512])
    sc = (xc[:, 512:640], xc[:, 640:768])
    dt = _softplus(misc_ref[0] + dtb_ref[...])
    da = dt * (-jnp.exp(alog_ref[...]))
    causal = _iota((q, q), 0) >= _iota((q, q), 1)
    acum = _dot_01(causal.astype(BF16), da)
    xsel = _eye(8, LANES)
    acum_t = _dot_nt_01(xsel, acum)
    dt_t = _dot_nt_01(xsel, dt)
    a_last = acum[q - 1:q, :]
    wlast = jnp.exp(a_last - acum) * dt
    ea = jnp.exp(acum)
    head = _iota((q, GROUP_W), 1) >> 6
    ht_old = ht[...]
    y = dexp_ref[...] * sx
    for g in range(2):
        cb = _dot_nt(sc[g], sb[g])
        for h in (2 * g, 2 * g + 1):
            seg = acum[:, h:h + 1] - acum_t[h:h + 1, :]
            decay = jnp.exp(jnp.where(causal, seg, NEG_INF))
            m = cb * decay * dt_t[h:h + 1, :]
            y_h = _dot(m, sx) + _dot(sc[g] * ea[:, h:h + 1], ht_old)
            y = y + jnp.where(head == h, y_h, 0.0)
    xw = sx * _head_bcast(wlast, head, SSD_HEADS)
    lane = _iota((q, GROUP_W), 1)
    bt0 = _dot_nt(_eye(SSD_STATE, SSD_STATE), sb[0])
    bt1 = _dot_nt(_eye(SSD_STATE, SSD_STATE), sb[1])
    head1 = _iota((1, GROUP_W), 1) >> 6
    dch = _head_bcast(jnp.exp(a_last), head1, SSD_HEADS)
    ht_new = dch * ht_old + _dot(bt0, jnp.where(lane < 128, xw, 0.0)) + _dot(bt1, jnp.where(lane < 128, 0.0, xw))
    ht[...] = ht_new
    zz = z_ref[0]
    y = y * (zz * jax.nn.sigmoid(zz))
    y_ref[0] = y * lax.rsqrt(jnp.mean(y * y, axis=-1, keepdims=True) + EPS) * nw_ref[...]

    @pl.when(c == pl.num_programs(1) - 1)
    def _():
        hout_ref[0] = _dot_nt_01(_eye(GROUP_W, GROUP_W), ht_new)


def _ssd(pj, nb, sl, prev8, h0, p):
    q = min(128, sl)
    nc = sl // q
    row = lambda w: pl.BlockSpec((1, q, w), lambda b, c: (b, c, 0))
    per_b = lambda r, w: pl.BlockSpec((1, r, w), lambda b, c: (b, 0, 0))
    const = lambda r, w: pl.BlockSpec((r, w), lambda b, c: (0, 0))
    y, hout = pl.pallas_call(
        functools.partial(_ssd_kernel, q=q),
        grid=(nb, nc),
        in_specs=[row(768), row(256), row(128), per_b(8, 768), per_b(256, 128),
                  const(8, 768), const(1, 768), const(1, 128), const(1, 128), const(1, 256), const(1, 256)],
        out_specs=[row(256), per_b(256, 128)],
        out_shape=[jax.ShapeDtypeStruct((nb, sl, 256), F32), jax.ShapeDtypeStruct((nb, 256, 128), F32)],
        scratch_shapes=[pltpu.VMEM((q + 8, 768), F32), pltpu.VMEM((SSD_STATE, GROUP_W), F32)],
        compiler_params=_params(("parallel", "arbitrary")),
        name="ssd",
    )(pj["xbc"].reshape(nb, sl, 768), pj["z"].reshape(nb, sl, 256), pj["misc"].reshape(nb, sl, 128), prev8, h0,
      p["ssd_cw"], p["ssd_cb"], p["ssd_dtb"], p["ssd_alog"], p["ssd_dexp"], p["ssd_nw"])
    return y.reshape(nb * sl, 256), hout


def _lru_kernel(x_ref, g_ref, prev_ref, h0_ref, cw_ref, cb_ref, wa_ref, ba_ref, wx_ref, bx_ref, lam_ref,
                y_ref, hout_ref, xbuf, hc, *, q):
    c = pl.program_id(1)

    @pl.when(c == 0)
    def _():
        hc[...] = h0_ref[0]

    xc = _conv_chunk(x_ref, prev_ref, cw_ref, cb_ref, xbuf, q, c == 0)
    xcb = xc.astype(BF16)
    r = jax.nn.sigmoid(jnp.dot(xcb, wa_ref[...], preferred_element_type=F32) + ba_ref[...])
    i = jax.nn.sigmoid(jnp.dot(xcb, wx_ref[...], preferred_element_type=F32) + bx_ref[...])
    log_a = -LRU_C * r * _softplus(-lam_ref[...])
    a = jnp.exp(log_a)
    u = jnp.sqrt(1.0 - jnp.exp(2.0 * log_a)) * (i * xc)
    row = _iota((q, GROUP_W), 0)
    s = 1
    while s < q:
        a_sh = pltpu.roll(a, s, 0)
        u_sh = pltpu.roll(u, s, 0)
        keep = row >= s
        u = jnp.where(keep, a * u_sh + u, u)
        a = jnp.where(keep, a * a_sh, a)
        s *= 2
    h = u + a * hc[...]
    hc[...] = h[q - 1:q, :]
    gg = g_ref[0]
    gelu = 0.5 * gg * (1.0 + jnp.tanh(math.sqrt(2.0 / math.pi) * (gg + 0.044715 * (gg * gg * gg))))
    y_ref[0] = h * gelu

    @pl.when(c == pl.num_programs(1) - 1)
    def _():
        hout_ref[0] = h[q - 1:q, :]


def _lru(pj, nb, sl, prev8, h0, p):
    q = min(256, sl)
    nc = sl // q
    row = lambda w: pl.BlockSpec((1, q, w), lambda b, c: (b, c, 0))
    per_b = lambda r, w: pl.BlockSpec((1, r, w), lambda b, c: (b, 0, 0))
    const = lambda r, w: pl.BlockSpec((r, w), lambda b, c: (0, 0))
    y, hout = pl.pallas_call(
        functools.partial(_lru_kernel, q=q),
        grid=(nb, nc),
        in_specs=[row(256), row(256), per_b(8, 256), per_b(1, 256), const(8, 256), const(1, 256),
                  const(256, 256), const(1, 256), const(256, 256), const(1, 256), const(1, 256)],
        out_specs=[row(256), per_b(1, 256)],
        out_shape=[jax.ShapeDtypeStruct((nb, sl, 256), F32), jax.ShapeDtypeStruct((nb, 1, 256), F32)],
        scratch_shapes=[pltpu.VMEM((q + 8, 256), F32), pltpu.VMEM((1, 256), F32)],
        compiler_params=_params(("parallel", "arbitrary")),
        name="lru",
    )(pj["lx"].reshape(nb, sl, 256), pj["lg"].reshape(nb, sl, 256), prev8, h0.reshape(nb, 1, 256),
      p["lru_cw"], p["lru_cb"], p["lru_wa"], p["lru_ba"], p["lru_wx"], p["lru_bx"], p["lru_lam"])
    return y.reshape(nb * sl, 256), hout.reshape(nb, 256)


def _summ_kernel(pt_ref, *refs, n_parts):
    kp, vp = refs[:n_parts], refs[n_parts:2 * n_parts]
    wk_ref, wv_ref, bsum_ref, ko_ref, vo_ref = refs[2 * n_parts:]

    def one(parts, w_ref, o_ref):
        w = w_ref[...]
        chunks = []
        for r in parts:
            x = r[...]
            x = x.reshape(x.shape[-2], x.shape[-1])
            for c in range(x.shape[-1] // LANES):
                chunks.append(x[:, c * LANES:(c + 1) * LANES] * w)
        xw = jnp.concatenate(chunks, axis=1)
        o_ref[0] = sum(jnp.dot(part, bsum_ref[...], preferred_element_type=F32) for part in _split2(xw))

    one(kp, wk_ref, ko_ref)
    one(vp, wv_ref, vo_ref)


def _summarize(k_src, v_src, part_specs, part_w, nb, total, table, wk_tile, wv_tile):
    n_parts = len(part_specs)
    step_w = n_parts * part_w
    n_blk = step_w // NSA_CMP_BLOCK
    bsum = (jnp.arange(step_w, dtype=jnp.int32)[:, None] // NSA_CMP_BLOCK
            == jnp.arange(n_blk, dtype=jnp.int32)[None, :]).astype(BF16)
    const = lambda r, c: pl.BlockSpec((r, c), lambda b, s, pt: (0, 0))
    ospec = pl.BlockSpec((1, LANES, n_blk), lambda b, s, pt: (b, 0, s))
    oshape = jax.ShapeDtypeStruct((nb, LANES, total // NSA_CMP_BLOCK), F32)
    return pl.pallas_call(
        functools.partial(_summ_kernel, n_parts=n_parts),
        grid_spec=pltpu.PrefetchScalarGridSpec(
            num_scalar_prefetch=1, grid=(nb, total // step_w),
            in_specs=list(part_specs) * 2 + [const(LANES, LANES), const(LANES, LANES), const(step_w, n_blk)],
            out_specs=[ospec, ospec]),
        out_shape=[oshape, oshape],
        compiler_params=_params(("parallel", "arbitrary")),
        name="nsa_summarize",
    )(table, *([k_src] * n_parts), *([v_src] * n_parts), wk_tile, wv_tile, bsum)


def _cmpsel_t_kernel(q_ref, kct_ref, vct_ref, ocmp_ref, sel_ref, *, qt, nc, ns):
    qi = pl.program_id(1)
    q4 = _nsa_expand(q_ref[0] * 0.125)
    s = _dot_nt(kct_ref[0].T, q4)
    qpos = qi * qt + (_iota((nc, 4 * qt), 1) & (qt - 1))
    blk_end = (_iota((nc, 4 * qt), 0) + 1) * NSA_CMP_BLOCK - 1
    s = jnp.where(blk_end <= qpos, s, NEG_INF)
    m = jnp.max(s, axis=0, keepdims=True)
    m = jnp.where(m == NEG_INF, 0.0, m)
    e = jnp.exp(s - m)
    p = e / jnp.maximum(jnp.sum(e, axis=0, keepdims=True), 1e-30)
    ocmp_ref[0] = _collect_t(_dot(vct_ref[0], p), qt).T
    impc = jnp.concatenate([p[:, 0:qt] + p[:, qt:2 * qt], p[:, 2 * qt:3 * qt] + p[:, 3 * qt:4 * qt]], axis=1)
    pair = (_iota((ns, nc), 1) >> 1 == _iota((ns, nc), 0)).astype(BF16)
    imp = _dot_01(pair, impc)
    blk = _iota((ns, 2 * qt), 0)
    cur = (qi * qt + (_iota((ns, 2 * qt), 1) & (qt - 1))) >> 6
    forced = (blk == 0) | (blk == cur) | (blk == cur - 1)
    score = jnp.where(forced, FORCE_SCORE, jnp.where(blk > cur, -1.0, imp))
    cnt = jnp.zeros((ns, 2 * qt), F32)
    for j in range(ns):
        row = score[j:j + 1, :]
        cnt = cnt + ((row > score) | ((row == score) & (blk > j))).astype(F32)
    sel = (cnt < float(NSA_TOPN)).astype(F32)
    sel_ref[0, 0] = sel[:, 0:qt]
    sel_ref[0, 1] = sel[:, qt:2 * qt]


def _cmp_select_prompt(qn, kcmp_t, vcmp_t, nb, sl, qt):
    nc = kcmp_t.shape[2]
    ns = sl // NSA_SLC_BLOCK
    return pl.pallas_call(
        functools.partial(_cmpsel_t_kernel, qt=qt, nc=nc, ns=ns),
        grid=(nb, sl // qt),
        in_specs=[pl.BlockSpec((1, qt, 256), lambda b, i: (b, i, 0)),
                  pl.BlockSpec((1, LANES, nc), lambda b, i: (b, 0, 0)),
                  pl.BlockSpec((1, LANES, nc), lambda b, i: (b, 0, 0))],
        out_specs=[pl.BlockSpec((1, qt, 256), lambda b, i: (b, i, 0)),
                   pl.BlockSpec((1, 2, ns, qt), lambda b, i: (b, 0, 0, i))],
        out_shape=[jax.ShapeDtypeStruct((nb, sl, 256), F32), jax.ShapeDtypeStruct((nb, 2, ns, sl), F32)],
        compiler_params=_params(("parallel", "parallel")),
        name="nsa_cmp_select_prompt",
    )(qn.reshape(nb, sl, 256), kcmp_t, vcmp_t)


def _cmpsel_kernel(q_ref, kct_ref, vct_ref, ocmp_ref, selx_ref, *, qt, nc, ns, nsp, past, n_chunks):
    q4 = _nsa_expand(q_ref[0] * 0.125)
    s = _dot(q4, kct_ref[0])
    qpos = past + (_iota((4 * qt, nc), 0) & (qt - 1))
    blk_end = (_iota((4 * qt, nc), 1) + 1) * NSA_CMP_BLOCK - 1
    s = jnp.where(blk_end <= qpos, s, NEG_INF)
    m = jnp.max(s, axis=-1, keepdims=True)
    m = jnp.where(m == NEG_INF, 0.0, m)
    e = jnp.exp(s - m)
    p = e / jnp.maximum(jnp.sum(e, axis=-1, keepdims=True), 1e-30)
    ocmp_ref[0] = _nsa_collect(_dot_nt(p, vct_ref[0]), qt)
    impc = jnp.concatenate([p[0:qt] + p[qt:2 * qt], p[2 * qt:3 * qt] + p[3 * qt:4 * qt]], axis=0)
    pair = (_iota((nc, nsp), 0) >> 1 == _iota((nc, nsp), 1)).astype(BF16)
    imp = _dot_01_r(impc, pair)
    blk = _iota((2 * qt, nsp), 1)
    cur = (past + (_iota((2 * qt, nsp), 0) & (qt - 1))) >> 6
    forced = (blk == 0) | (blk == cur) | (blk == cur - 1)
    score = jnp.where(forced, FORCE_SCORE, jnp.where(blk > cur, -1.0, imp))
    score = jnp.where(blk < ns, score, -3.0)

    def body(j, cnt):
        col = jnp.sum(jnp.where(blk == j, score, 0.0), axis=1, keepdims=True)
        beats = (col > score) | ((col == score) & (j < blk))
        return cnt + beats.astype(F32)

    cnt = lax.fori_loop(0, ns, body, jnp.zeros((2 * qt, nsp), F32))
    sel = ((cnt < float(NSA_TOPN)) & (blk < ns)).astype(F32)
    sel4 = jnp.concatenate([sel[0:qt], sel[0:qt], sel[qt:2 * qt], sel[qt:2 * qt]], axis=0).astype(BF16)
    e16 = (_iota((LANES, 1024), 1) >> 6 == _iota((LANES, 1024), 0)).astype(BF16)
    for t in range(n_chunks):
        pick = ((_iota((nsp, LANES), 0) == 16 * t + _iota((nsp, LANES), 1))
                & (_iota((nsp, LANES), 1) < 16)).astype(BF16)
        blocks = jnp.dot(sel4, pick, preferred_element_type=F32).astype(BF16)
        selx_ref[0, :, 1024 * t:1024 * (t + 1)] = jnp.dot(blocks, e16, preferred_element_type=F32)


def _cmp_select_sample(qn, kcmp_t, vcmp_t, nb, sl, past):
    nc = kcmp_t.shape[2]
    ns = -(-(past + sl) // NSA_SLC_BLOCK)
    nsp = -(-ns // LANES) * LANES
    n_chunks = -(-(past + sl) // 1024)
    return pl.pallas_call(
        functools.partial(_cmpsel_kernel, qt=sl, nc=nc, ns=ns, nsp=nsp, past=past, n_chunks=n_chunks),
        grid=(nb,),
        in_specs=[pl.BlockSpec((1, sl, 256), lambda b: (b, 0, 0)),
                  pl.BlockSpec((1, LANES, nc), lambda b: (b, 0, 0)),
                  pl.BlockSpec((1, LANES, nc), lambda b: (b, 0, 0))],
        out_specs=[pl.BlockSpec((1, sl, 256), lambda b: (b, 0, 0)),
                   pl.BlockSpec((1, 4 * sl, 1024 * n_chunks), lambda b: (b, 0, 0))],
        out_shape=[jax.ShapeDtypeStruct((nb, sl, 256), F32),
                   jax.ShapeDtypeStruct((nb, 4 * sl, 1024 * n_chunks), F32)],
        compiler_params=_params(("parallel",)),
        name="nsa_cmp_select_sample",
    )(qn.reshape(nb, sl, 256), kcmp_t, vcmp_t)


def _nsap_kernel(q_ref, ks_ref, vst_ref, kw_ref, vwt_ref, sel_ref, oslc_ref, owin_ref,
                 q4_s, m1, l1, a1, m2, l2, a2, *, qt, kt):
    qi, kj = pl.program_id(1), pl.program_id(2)
    last = (qi * qt + qt - 1) // kt
    first_w = jnp.maximum(qi * qt - (NSA_WINDOW - 1), 0) // kt

    @pl.when(kj == 0)
    def _():
        q4_s[...] = _nsa_expand(q_ref[0] * 0.125).astype(BF16)
        _init_stats((m1, l1, a1), (m2, l2, a2))

    kpos = kj * kt + _iota((kt, 4 * qt), 0)
    qpos = qi * qt + (_iota((kt, 4 * qt), 1) & (qt - 1))

    @pl.when(kj <= last)
    def _():
        s = lax.dot_general(ks_ref[0].astype(BF16), q4_s[...], (((1,), (1,)), ((), ())), preferred_element_type=F32)
        rows = []
        for u in range(kt // NSA_SLC_BLOCK):
            blk = kj * (kt // NSA_SLC_BLOCK) + u
            r0 = sel_ref[0, 0, pl.ds(blk, 1), :]
            r1 = sel_ref[0, 1, pl.ds(blk, 1), :]
            rows.append(jnp.broadcast_to(jnp.concatenate([r0, r0, r1, r1], axis=1), (NSA_SLC_BLOCK, 4 * qt)))
        picked = jnp.concatenate(rows, axis=0) > 0.5
        _online_update_t(s, picked & (kpos <= qpos), vst_ref[0].astype(BF16), m1, l1, a1)

    @pl.when((kj <= last) & (kj >= first_w))
    def _():
        s = lax.dot_general(kw_ref[0].astype(BF16), q4_s[...], (((1,), (1,)), ((), ())), preferred_element_type=F32)
        _online_update_t(s, (kpos <= qpos) & (kpos > qpos - NSA_WINDOW), vwt_ref[0].astype(BF16), m2, l2, a2)

    @pl.when(kj == last)
    def _():
        oslc_ref[0] = _collect_t(a1[...] / jnp.maximum(l1[...], 1e-30), qt).T
        owin_ref[0] = _collect_t(a2[...] / jnp.maximum(l2[...], 1e-30), qt).T


def _nsa_prompt_attn(pj, sel, nb, sl):
    qt, kt = min(256, sl), min(256, sl)
    last = lambda i: (i * qt + qt - 1) // kt
    first_w = lambda i: jnp.maximum(i * qt - (NSA_WINDOW - 1), 0) // kt
    kspec = pl.BlockSpec((1, kt, LANES), lambda b, i, j: (b, jnp.minimum(j, last(i)), 0))
    vspec = pl.BlockSpec((1, LANES, kt), lambda b, i, j: (b, 0, jnp.minimum(j, last(i))))
    kwspec = pl.BlockSpec((1, kt, LANES), lambda b, i, j: (b, jnp.clip(j, first_w(i), last(i)), 0))
    vwspec = pl.BlockSpec((1, LANES, kt), lambda b, i, j: (b, 0, jnp.clip(j, first_w(i), last(i))))
    qspec = pl.BlockSpec((1, qt, 256), lambda b, i, j: (b, i, 0))
    ns = sel.shape[2]
    r3 = lambda a, w: a.reshape(nb, sl, w)
    stats = [pltpu.VMEM((1, 4 * qt), F32), pltpu.VMEM((1, 4 * qt), F32), pltpu.VMEM((LANES, 4 * qt), F32)]
    return pl.pallas_call(
        functools.partial(_nsap_kernel, qt=qt, kt=kt),
        grid=(nb, sl // qt, sl // kt),
        in_specs=[qspec, kspec, vspec, kwspec, vwspec, pl.BlockSpec((1, 2, ns, qt), lambda b, i, j: (b, 0, 0, i))],
        out_specs=[qspec, qspec],
        out_shape=[jax.ShapeDtypeStruct((nb, sl, 256), F32)] * 2,
        scratch_shapes=[pltpu.VMEM((4 * qt, LANES), BF16)] + stats * 2,
        compiler_params=_params(("parallel", "parallel", "arbitrary")),
        name="nsa_prompt_attn",
    )(r3(pj["qnr"], 256), r3(pj["ks"], 128), pj["vsT"], r3(pj["kw"], 128), pj["vwT"], sel)


def _combine_kernel(misc_ref, oc_ref, os_ref, ow_ref, y_ref):
    g = jax.nn.sigmoid(misc_ref[...])
    head = _iota(oc_ref.shape, 1) >> 6
    y_ref[...] = (_head_bcast(g, head, 4, GATE_LANE0) * oc_ref[...]
                  + _head_bcast(g, head, 4, GATE_LANE0 + 4) * os_ref[...]
                  + _head_bcast(g, head, 4, GATE_LANE0 + 8) * ow_ref[...])


def _nsa_combine(misc, oc, os_, ow, tm):
    t = misc.shape[0]
    spec = lambda w: pl.BlockSpec((tm, w), lambda i: (i, 0))
    return pl.pallas_call(
        _combine_kernel, grid=(t // tm,),
        in_specs=[spec(128), spec(256), spec(256), spec(256)], out_specs=spec(256),
        out_shape=jax.ShapeDtypeStruct((t, 256), F32),
        compiler_params=_params(("parallel",)), name="nsa_combine",
    )(misc, oc.reshape(t, 256), os_.reshape(t, 256), ow.reshape(t, 256))


def _diff_lambda(lamv_ref, lam_init):
    lv = lamv_ref[...]
    s1 = jnp.sum(lv[0:1] * lv[1:2], axis=1, keepdims=True)
    s2 = jnp.sum(lv[2:3] * lv[3:4], axis=1, keepdims=True)
    return jnp.exp(s1) - jnp.exp(s2) + lam_init


def _diff_finish(o, head, nheads, nw, lam_init):
    inv = jnp.zeros(o.shape, F32)
    for h in range(nheads):
        ms = jnp.sum(jnp.where(head == h, o * o, 0.0), axis=-1, keepdims=True) * (1.0 / 64.0)
        inv = jnp.where(head == h, lax.rsqrt(ms + EPS), inv)
    return o * inv * nw * (1.0 - lam_init)


def _diffp_kernel(q_ref, k_ref, vt_ref, lamv_ref, nwt_ref, o_ref, q4_s, m_s, l_s, acc_s, *, qt, kt, lam_init):
    qi, kj = pl.program_id(2), pl.program_id(3)
    last = (qi * qt + qt - 1) // kt

    @pl.when(kj == 0)
    def _():
        q = q_ref[0] * (32.0 ** -0.5)
        part = _iota((qt, LANES), 1) >> 5
        q4_s[...] = jnp.concatenate([jnp.where(part == k, q, 0.0) for k in range(4)], axis=0).astype(BF16)
        _init_stats((m_s, l_s, acc_s))

    @pl.when(kj <= last)
    def _():
        s = lax.dot_general(k_ref[0].astype(BF16), q4_s[...], (((1,), (1,)), ((), ())), preferred_element_type=F32)
        kpos = kj * kt + _iota((kt, 4 * qt), 0)
        qpos = qi * qt + (_iota((kt, 4 * qt), 1) & (qt - 1))
        _online_update_t(s, kpos <= qpos, vt_ref[0].astype(BF16), m_s, l_s, acc_s)

    @pl.when(kj == last)
    def _():
        lam = _diff_lambda(lamv_ref, lam_init)
        o = acc_s[...] / jnp.maximum(l_s[...], 1e-30)
        halves = []
        for h in range(2):
            rows = slice(64 * h, 64 * (h + 1))
            oh = o[rows, 2 * h * qt:(2 * h + 1) * qt] - lam * o[rows, (2 * h + 1) * qt:(2 * h + 2) * qt]
            ms = jnp.mean(oh * oh, axis=0, keepdims=True)
            halves.append(oh * lax.rsqrt(ms + EPS))
        o_ref[0] = (jnp.concatenate(halves, axis=0) * nwt_ref[...] * (1.0 - lam_init)).T


def _diff_prompt(pj, nb, sl, p, lam_init):
    qt, kt = min(256, sl), min(512, sl)
    last = lambda i: (i * qt + qt - 1) // kt
    qspec = pl.BlockSpec((1, qt, LANES), lambda b, h, i, j: (b, i, h))
    kspec = pl.BlockSpec((1, kt, LANES), lambda b, h, i, j: (b, jnp.minimum(j, last(i)), h))
    vspec = pl.BlockSpec((1, LANES, kt), lambda b, h, i, j: (b, h, jnp.minimum(j, last(i))))
    r3 = lambda a: a.reshape(nb, sl, 256)
    out = pl.pallas_call(
        functools.partial(_diffp_kernel, qt=qt, kt=kt, lam_init=lam_init),
        grid=(nb, 2, sl // qt, sl // kt),
        in_specs=[qspec, kspec, vspec, pl.BlockSpec((4, 32), lambda b, h, i, j: (0, 0)),
                  pl.BlockSpec((LANES, 1), lambda b, h, i, j: (0, 0))],
        out_specs=qspec,
        out_shape=jax.ShapeDtypeStruct((nb, sl, 256), F32),
        scratch_shapes=[pltpu.VMEM((4 * qt, LANES), BF16), pltpu.VMEM((1, 4 * qt), F32),
                        pltpu.VMEM((1, 4 * qt), F32), pltpu.VMEM((LANES, 4 * qt), F32)],
        compiler_params=_params(("parallel", "parallel", "parallel", "arbitrary")),
        name="diff_prompt",
    )(r3(pj["dq"]), r3(pj["dk"]), pj["dvT"], p["diff_lambda"], p["diff_nw256"][:, :LANES].reshape(LANES, 1))
    return out.reshape(nb * sl, 256)


def _paged_kernel(pt_ref, q_ref, *refs, mode, pps, width, kbase, past, new_len, lam_init):
    kp, vp = refs[:pps], refs[pps:2 * pps]
    knew_ref, vnew_ref = refs[2 * pps:2 * pps + 2]
    rest = list(refs[2 * pps + 2:])
    selx_ref = selnew_ref = lamv_ref = nw_ref = None
    if mode == "slc":
        selx_ref, selnew_ref = rest[:2]
        rest = rest[2:]
    if mode == "diff":
        lamv_ref, nw_ref = rest[:2]
        rest = rest[2:]
    o_ref, qe_s, m_s, l_s, acc_s = rest
    nq = new_len
    rows = qe_s.shape[0]
    step = pl.program_id(1)

    @pl.when(step == 0)
    def _():
        if mode == "diff":
            q = q_ref[0] * (32.0 ** -0.5)
            part = _iota((nq, width), 1) >> 5
            qe = jnp.concatenate([jnp.where(part == k, q, 0.0) for k in range(8)], axis=0)
        else:
            qe = _nsa_expand(q_ref[0] * 0.125)
        qe_s[...] = qe.astype(BF16)
        _init_stats((m_s, l_s, acc_s))

    nk = pps * PAGE
    kt_b = jnp.concatenate([r[0, 0].astype(BF16) for r in kp], axis=1)
    vt_b = jnp.concatenate([r[0, 0].astype(BF16) for r in vp], axis=1)
    s = jnp.dot(qe_s[...], kt_b, preferred_element_type=F32)
    qpos = past + (_iota((rows, nk), 0) & (nq - 1))
    kpos = kbase + step * nk + _iota((rows, nk), 1)
    mask = kpos <= qpos
    if mode == "win":
        mask = mask & (kpos > qpos - NSA_WINDOW)
    if mode == "slc":
        mask = mask & (selx_ref[0] > 0.5)
    _online_update(s, mask, vt_b, m_s, l_s, acc_s, v_transposed=True)

    @pl.when(step == pl.num_programs(1) - 1)
    def _():
        s2 = lax.dot_general(qe_s[...], knew_ref[0].astype(BF16), (((1,), (1,)), ((), ())),
                             preferred_element_type=F32)
        qpos2 = past + (_iota((rows, nq), 0) & (nq - 1))
        kpos2 = past + _iota((rows, nq), 1)
        mask2 = kpos2 <= qpos2
        if mode == "win":
            mask2 = mask2 & (kpos2 > qpos2 - NSA_WINDOW)
        if mode == "slc":
            mask2 = mask2 & (selnew_ref[0][:, 0:nq] > 0.5)
        _online_update(s2, mask2, vnew_ref[0].astype(BF16), m_s, l_s, acc_s)
        o = acc_s[...] / jnp.maximum(l_s[...], 1e-30)
        if mode == "diff":
            lam = _diff_lambda(lamv_ref, lam_init)
            head = _iota((nq, width), 1) >> 6
            out = jnp.zeros((nq, width), F32)
            for h in range(4):
                o0 = o[(2 * h) * nq:(2 * h + 1) * nq]
                o1 = o[(2 * h + 1) * nq:(2 * h + 2) * nq]
                out = jnp.where(head == h, o0 - lam * o1, out)
            o_ref[0] = _diff_finish(out, head, 4, nw_ref[...], lam_init)
        else:
            o_ref[0] = _nsa_collect(o, nq)


def _paged_attn(mode, q, k_src, v_src, page_spec, n_pages, table, knew, vnew, kbase, past, extra=(), lam_init=0.0):
    nb, nq = q.shape[0], q.shape[1]
    width = knew.shape[-1]
    pps = min(8, n_pages)
    rows = 8 * nq if mode == "diff" else 4 * nq
    nk = pps * PAGE
    per_b = lambda r_, w: pl.BlockSpec((1, r_, w), lambda b, s, pt: (b, 0, 0))
    in_specs = [per_b(nq, 256)] + [page_spec(r, pps) for r in range(pps)] * 2 + [per_b(nq, width), per_b(nq, width)]
    args = [q] + [k_src] * pps + [v_src] * pps + [knew, vnew]
    if mode == "slc":
        (selx,) = extra
        new_blk = (past - kbase) // LANES
        in_specs += [pl.BlockSpec((1, rows, nk), lambda b, s, pt: (b, 0, s)),
                     pl.BlockSpec((1, rows, LANES), lambda b, s, pt: (b, 0, new_blk))]
        args += [selx, selx]
    if mode == "diff":
        in_specs += [pl.BlockSpec((4, 32), lambda b, s, pt: (0, 0)), pl.BlockSpec((1, 256), lambda b, s, pt: (0, 0))]
        args += list(extra)
    return pl.pallas_call(
        functools.partial(_paged_kernel, mode=mode, pps=pps, width=width, kbase=kbase, past=past, new_len=nq,
                          lam_init=lam_init),
        grid_spec=pltpu.PrefetchScalarGridSpec(
            num_scalar_prefetch=1, grid=(nb, n_pages // pps), in_specs=in_specs,
            out_specs=per_b(nq, 256),
            scratch_shapes=[pltpu.VMEM((rows, width), BF16), pltpu.VMEM((rows, 1), F32),
                            pltpu.VMEM((rows, 1), F32), pltpu.VMEM((rows, width), F32)]),
        out_shape=jax.ShapeDtypeStruct((nb, nq, 256), F32),
        compiler_params=_params(("parallel", "arbitrary")),
        name="paged_" + mode,
    )(table, *args)


def _post_kernel(x_ref, ya_ref, yb_ref, yc_ref, yd_ref, wo_ref, g1_ref, b1_ref, w1_ref, w2_ref, g2_ref, b2_ref,
                 o_ref, x1_s, x1b_s, acc_s, *, alpha):
    f = pl.program_id(1)

    @pl.when(f == 0)
    def _():
        mix = _dot(ya_ref[...], wo_ref[0:256, :])
        mix = mix + _dot(yb_ref[...], wo_ref[256:512, :])
        mix = mix + _dot(yc_ref[...], wo_ref[512:768, :])
        mix = mix + _dot(yd_ref[...], wo_ref[768:1024, :])
        x1 = _layer_norm(alpha * x_ref[...] + mix, g1_ref[...], b1_ref[...])
        x1_s[...] = x1
        x1b_s[...] = x1.astype(BF16)
        acc_s[...] = jnp.zeros(acc_s.shape, F32)

    h = jnp.dot(x1b_s[...], w1_ref[...], preferred_element_type=F32)
    h = jnp.square(jnp.maximum(h, 0.0))
    acc_s[...] += jnp.dot(h.astype(BF16), w2_ref[...], preferred_element_type=F32)

    @pl.when(f == pl.num_programs(1) - 1)
    def _():
        o_ref[...] = _layer_norm(alpha * x1_s[...] + acc_s[...], g2_ref[...], b2_ref[...])


def _post(x2d, ya, yb, yc, yd, p, tm, alpha):
    t = x2d.shape[0]
    tf = 1024
    row = lambda w: pl.BlockSpec((tm, w), lambda i, f: (i, 0))
    const = lambda r, w: pl.BlockSpec((r, w), lambda i, f: (0, 0))
    return pl.pallas_call(
        functools.partial(_post_kernel, alpha=alpha),
        grid=(t // tm, D_FF // tf),
        in_specs=[row(D_MODEL), row(256), row(256), row(256), row(256), const(D_MODEL, D_MODEL),
                  const(1, D_MODEL), const(1, D_MODEL),
                  pl.BlockSpec((D_MODEL, tf), lambda i, f: (0, f)), pl.BlockSpec((tf, D_MODEL), lambda i, f: (f, 0)),
                  const(1, D_MODEL), const(1, D_MODEL)],
        out_specs=row(D_MODEL),
        out_shape=jax.ShapeDtypeStruct((t, D_MODEL), F32),
        scratch_shapes=[pltpu.VMEM((tm, D_MODEL), F32), pltpu.VMEM((tm, D_MODEL), BF16),
                        pltpu.VMEM((tm, D_MODEL), F32)],
        compiler_params=_params(("parallel", "arbitrary")),
        name="post",
    )(x2d, ya, yb, yc, yd, p["w_out"], p["ln1_g"], p["ln1_b"], p["w_ff1"], p["w_ff2"], p["ln2_g"], p["ln2_b"])


def _prep_params(w, l):
    w_in = w["w_in"][l]
    misc = jnp.concatenate([w_in[:, a:b] for a, b in MISC_COLS], axis=1)
    misc = jnp.pad(misc, ((0, 0), (0, LANES - misc.shape[1])))
    w_in_p = jnp.concatenate([w_in[:, a:b] for a, b in W_IN_ORDER] + [misc], axis=1).astype(BF16)
    row = lambda v: v.reshape(1, -1).astype(F32)
    pad_lanes = lambda v: jnp.pad(v.reshape(1, -1), ((0, 0), (0, LANES - v.shape[-1])))
    blockdiag = lambda m: jax.scipy.linalg.block_diag(*[m[i] for i in range(m.shape[0])]).astype(BF16)
    cmp_tile = lambda wc: jnp.tile(jnp.repeat(wc, 64, axis=0), (1, PAGE // NSA_CMP_BLOCK))
    return dict(
        w_in=w_in_p,
        ssd_cw=jnp.pad(w["ssd_conv_w"][l], ((0, 4), (0, 0))), ssd_cb=row(w["ssd_conv_b"][l]),
        ssd_dtb=pad_lanes(w["ssd_dt_bias"][l]), ssd_alog=pad_lanes(w["ssd_A_log"][l]),
        ssd_dexp=row(jnp.repeat(w["ssd_D"][l], 64)), ssd_nw=row(w["ssd_norm_w"][l]),
        cmp_wk=cmp_tile(w["nsa_w_cmp_k"][l]), cmp_wv=cmp_tile(w["nsa_w_cmp_v"][l]),
        diff_lambda=w["diff_lambda"][l], diff_nw256=row(jnp.tile(w["diff_norm_w"][l], 4)),
        lru_cw=jnp.pad(w["lru_conv_w"][l], ((0, 4), (0, 0))), lru_cb=row(w["lru_conv_b"][l]),
        lru_wa=blockdiag(w["lru_w_a"][l]), lru_ba=row(w["lru_b_a"][l]),
        lru_wx=blockdiag(w["lru_w_x"][l]), lru_bx=row(w["lru_b_x"][l]), lru_lam=row(w["lru_lambda"][l]),
        w_out=w["w_out"][l].astype(BF16), ln1_g=row(w["ln1_g"][l]), ln1_b=row(w["ln1_b"][l]),
        w_ff1=w["w_ff1"][l].astype(BF16), w_ff2=w["w_ff2"][l].astype(BF16),
        ln2_g=row(w["ln2_g"][l]), ln2_b=row(w["ln2_b"][l]),
    )


def _pad_prev(prev):
    return jnp.pad(prev, ((0, 0), (5, 0), (0, 0)))


def _token_minor(a):
    a = jnp.moveaxis(a, -3, -1)
    return a.reshape(*a.shape[:-3], a.shape[-3] * a.shape[-2], a.shape[-1])


def _token_major(a_t, heads):
    nb, _, n = a_t.shape
    return jnp.moveaxis(a_t.reshape(nb, heads, 64, n), -1, 1)


def _layer(x2d, nb, sl, past, hist, p, lam_init, alpha):
    t = nb * sl
    tm = min(256, t)
    pos = past + jnp.arange(sl, dtype=jnp.int32)
    pj = _proj(x2d, p["w_in"], pos, nb, sl, tm)
    y_a, ssd_h = _ssd(pj, nb, sl, _pad_prev(hist["ssd_conv"]), hist["ssd_h"].reshape(nb, 256, 128), p)
    y_d, lru_h = _lru(pj, nb, sl, _pad_prev(hist["lru_conv"]), hist["lru_h"], p)
    r3 = lambda a, w: a.reshape(nb, sl, w)
    dummy = jnp.zeros((1, 1), jnp.int32)
    if past == 0:
        step_w = min(4096, sl)
        spec = pl.BlockSpec((1, LANES, step_w), lambda b, s, pt: (b, 0, s))
        kcmp_t, vcmp_t = _summarize(pj["kcT"], pj["vcT"], [spec], step_w, nb, sl, dummy, p["cmp_wk"], p["cmp_wv"])
        o_cmp, sel = _cmp_select_prompt(pj["qn"], kcmp_t, vcmp_t, nb, sl, min(256, sl))
        o_slc, o_win = _nsa_prompt_attn(pj, sel, nb, sl)
        y_c = _diff_prompt(pj, nb, sl, p, lam_init)
        kw_t, vw_t = pj["kwT"][:, :, sl - NSA_WINDOW:], pj["vwT"][:, :, sl - NSA_WINDOW:]
        new = dict(kc=_token_major(pj["kcT"], 2), vc=_token_major(pj["vcT"], 2), ks=_token_major(pj["ksT"], 2),
                   vs=_token_major(pj["vsT"], 2), dk=_token_major(pj["dkT"], 4), dv=_token_major(pj["dvT"], 4))
    else:
        table, l = hist["table"], hist["layer"]
        n_pages = table.shape[1]
        npp = min(32, n_pages)
        cache_spec = lambda r, per: pl.BlockSpec(
            (1, 1, LANES, PAGE), lambda b, s, pt, r=r, per=per: (l, pt[b, s * per + r], 0, 0))
        diff_spec = lambda r, per: pl.BlockSpec(
            (1, 1, 256, PAGE), lambda b, s, pt, r=r, per=per: (l, pt[b, s * per + r], 0, 0))
        win_spec = lambda r, per: pl.BlockSpec(
            (1, 1, LANES, PAGE), lambda b, s, pt, r=r, per=per: (l, b, 0, s * per + r))
        kcmp_t, vcmp_t = _summarize(hist["kc"], hist["vc"], [cache_spec(r, npp) for r in range(npp)], PAGE, nb,
                                    n_pages * PAGE, table, p["cmp_wk"], p["cmp_wv"])
        o_cmp, selx = _cmp_select_sample(pj["qn"], kcmp_t, vcmp_t, nb, sl, past)
        qnr = r3(pj["qnr"], 256)
        o_slc = _paged_attn("slc", qnr, hist["ks"], hist["vs"], cache_spec, n_pages, table,
                            r3(pj["ks"], 128), r3(pj["vs"], 128), 0, past, extra=(selx,))
        wbuf = hist["kw"].shape[-1]
        o_win = _paged_attn("win", qnr, hist["kw"], hist["vw"], win_spec, wbuf // PAGE, dummy,
                            r3(pj["kw"], 128), r3(pj["vw"], 128), past - wbuf, past)
        y_c = _paged_attn("diff", r3(pj["dq"], 256), hist["dk"], hist["dv"], diff_spec, n_pages, table,
                          r3(pj["dk"], 256), r3(pj["dv"], 256), 0, past,
                          extra=(p["diff_lambda"], p["diff_nw256"]), lam_init=lam_init).reshape(t, 256)
        kw_t = jnp.concatenate([hist["kw"][l], pj["kwT"]], axis=2)[:, :, -NSA_WINDOW:]
        vw_t = jnp.concatenate([hist["vw"][l], pj["vwT"]], axis=2)[:, :, -NSA_WINDOW:]
        kv = lambda a: r3(a, 128).reshape(nb, sl, 2, 64)
        new = dict(kc=kv(pj["kc"]), vc=kv(pj["vc"]), ks=kv(pj["ks"]), vs=kv(pj["vs"]),
                   dk=r3(pj["dk"], 256).reshape(nb, sl, 4, 64), dv=r3(pj["dv"], 256).reshape(nb, sl, 4, 64))
    y_b = _nsa_combine(pj["misc"], o_cmp, o_slc, o_win, tm)
    x_out = _post(x2d, y_a, y_b, y_c, y_d, p, min(512, t), alpha)
    new.update(kw=_token_major(kw_t, 2), vw=_token_major(vw_t, 2),
               ssd_h=ssd_h.reshape(nb, 4, 64, 128), ssd_conv=r3(pj["xbc"], 768)[:, sl - 3:],
               lru_h=lru_h, lru_conv=r3(pj["lx"], 256)[:, sl - 3:])
    return x_out, new


STATE_ORDER = ("kc", "vc", "ks", "vs", "dk", "dv", "kw", "vw", "ssd_h", "ssd_conv", "lru_h", "lru_conv")


def kernel(x_prompt, x_sample, cache_nsa_k_cmp, cache_nsa_v_cmp, cache_nsa_k_slc, cache_nsa_v_slc, cache_diff_k, cache_diff_v, cache_nsa_k_win, cache_nsa_v_win, state_ssd, state_ssd_conv, state_lru, state_lru_conv, page_table, w_in, ssd_conv_w, ssd_conv_b, ssd_dt_bias, ssd_A_log, ssd_D, ssd_norm_w, nsa_w_cmp_k, nsa_w_cmp_v, diff_lambda, diff_norm_w, lru_conv_w, lru_conv_b, lru_w_a, lru_b_a, lru_w_x, lru_b_x, lru_lambda, w_out, ln1_g, ln1_b, w_ff1, w_ff2, ln2_g, ln2_b):
    weights = dict(w_in=w_in, ssd_conv_w=ssd_conv_w, ssd_conv_b=ssd_conv_b, ssd_dt_bias=ssd_dt_bias,
                   ssd_A_log=ssd_A_log, ssd_D=ssd_D, ssd_norm_w=ssd_norm_w, nsa_w_cmp_k=nsa_w_cmp_k,
                   nsa_w_cmp_v=nsa_w_cmp_v, diff_lambda=diff_lambda, diff_norm_w=diff_norm_w, lru_conv_w=lru_conv_w,
                   lru_conv_b=lru_conv_b, lru_w_a=lru_w_a, lru_b_a=lru_b_a, lru_w_x=lru_w_x, lru_b_x=lru_b_x,
                   lru_lambda=lru_lambda, w_out=w_out, ln1_g=ln1_g, ln1_b=ln1_b, w_ff1=w_ff1, w_ff2=w_ff2,
                   ln2_g=ln2_g, ln2_b=ln2_b)
    depth = w_in.shape[0]
    nbp, slp, _ = x_prompt.shape
    nbs, sls, _ = x_sample.shape
    past = page_table.shape[1] * cache_nsa_k_cmp.shape[2]
    assert past % PAGE == 0 and sls < NSA_CMP_BLOCK and slp % 256 == 0
    alpha = (2 * depth) ** 0.25
    caches = dict(kc=_token_minor(cache_nsa_k_cmp), vc=_token_minor(cache_nsa_v_cmp), ks=_token_minor(cache_nsa_k_slc),
                  vs=_token_minor(cache_nsa_v_slc), dk=_token_minor(cache_diff_k), dv=_token_minor(cache_diff_v),
                  kw=_token_minor(cache_nsa_k_win), vw=_token_minor(cache_nsa_v_win))
    xp = x_prompt.reshape(nbp * slp, D_MODEL)
    xs = x_sample.reshape(nbs * sls, D_MODEL)
    outs_p, outs_s = [], []
    for l in range(depth):
        p = _prep_params(weights, l)
        lam_init = 0.8 - 0.6 * math.exp(-0.3 * l)
        hist_p = dict(ssd_conv=jnp.zeros((nbp, 3, SSD_CONV_CH), F32), ssd_h=jnp.zeros((nbp, 4, 64, 128), F32),
                      lru_conv=jnp.zeros((nbp, 3, GROUP_W), F32), lru_h=jnp.zeros((nbp, GROUP_W), F32))
        hist_s = dict(table=page_table, layer=l, ssd_conv=state_ssd_conv[l], ssd_h=state_ssd[l],
                      lru_conv=state_lru_conv[l], lru_h=state_lru[l], **caches)
        xp, new_p = _layer(xp, nbp, slp, 0, hist_p, p, lam_init, alpha)
        xs, new_s = _layer(xs, nbs, sls, past, hist_s, p, lam_init, alpha)
        outs_p.append(new_p)
        outs_s.append(new_s)
    stack = lambda lst, name: jnp.stack([d[name] for d in lst])
    return ((xp.reshape(nbp, slp, D_MODEL), xs.reshape(nbs, sls, D_MODEL))
            + tuple(stack(outs_p, n) for n in STATE_ORDER) + tuple(stack(outs_s, n) for n in STATE_ORDER))
```

```python
import functools
import math

import jax
import jax.numpy as jnp
from jax import lax
from jax.experimental import pallas as pl
from jax.experimental.pallas import tpu as pltpu

F32 = jnp.float32
BF16 = jnp.bfloat16
NEG_INF = float("-inf")

D_MODEL = 1024
GROUP_W = 256
SSD_HEADS = 4
SSD_STATE = 128
SSD_CONV_CH = 768
NSA_CMP_BLOCK = 32
NSA_SLC_BLOCK = 64
NSA_TOPN = 16
NSA_WINDOW = 512
FORCE_SCORE = 1e4
LRU_C = 8.0
D_FF = 4096
ROPE_THETA = 500000.0
EPS = 1e-5
PAGE = 128
LANES = 128
VMEM_LIMIT = 56 * 1024 * 1024
PAGES_PER_STEP = 32
LOG2E = 1.4426950408889634
NSA_QSCALE = 64.0 ** -0.5 * LOG2E
DIFF_QSCALE = 32.0 ** -0.5 * LOG2E

W_IN_ORDER = ((0, 1024), (1028, 2052), (2064, 3344))
MISC_COLS = ((1024, 1028), (2052, 2064))
SEG = dict(z=(0, 256), xbc=(256, 1024), qn=(1024, 1280), kc=(1280, 1408), vc=(1408, 1536), ks=(1536, 1664),
           vs=(1664, 1792), kw=(1792, 1920), vw=(1920, 2048), dq=(2048, 2304), dk=(2304, 2560), dv=(2560, 2816),
           lx=(2816, 3072), lg=(3072, 3328), misc=(3328, 3456))
D_IN_PAD = 3456
GATE_LANE0 = 4


def _params(sem):
    return pltpu.CompilerParams(dimension_semantics=sem, vmem_limit_bytes=VMEM_LIMIT)


def _iota(shape, dim):
    return lax.broadcasted_iota(jnp.int32, shape, dim)


def _dot(a, b):
    return jnp.dot(a.astype(BF16), b.astype(BF16), preferred_element_type=F32)


def _dot_nt(a, b):
    return lax.dot_general(a.astype(BF16), b.astype(BF16), (((1,), (1,)), ((), ())), preferred_element_type=F32)


def _split3(x):
    h1 = x.astype(BF16)
    r1 = x - h1.astype(F32)
    h2 = r1.astype(BF16)
    h3 = (r1 - h2.astype(F32)).astype(BF16)
    return h1, h2, h3


def _dot_01(m01, x):
    return sum(jnp.dot(m01, part, preferred_element_type=F32) for part in _split3(x))


def _dot_01_r(x, m01):
    return sum(jnp.dot(part, m01, preferred_element_type=F32) for part in _split3(x))


def _dot_nt_01(m01, x):
    return sum(lax.dot_general(m01, part, (((1,), (1,)), ((), ())), preferred_element_type=F32)
               for part in _split3(x))


def _eye(n, m):
    return (_iota((n, m), 0) == _iota((n, m), 1)).astype(BF16)


def _softplus(x):
    return jnp.maximum(x, 0.0) + jnp.log1p(jnp.exp(-jnp.abs(x)))


def _head_bcast(cols, h_of_lane, nheads, lane0=0):
    out = jnp.zeros(h_of_lane.shape, F32)
    for h in range(nheads):
        out = jnp.where(h_of_lane == h, cols[:, lane0 + h:lane0 + h + 1], out)
    return out


def _online_update(s, mask, v_b, m_ref, l_ref, acc_ref, v_transposed=False):
    s = jnp.where(mask, s, NEG_INF)
    m_prev = m_ref[...]
    m_new = jnp.maximum(m_prev, jnp.max(s, axis=-1, keepdims=True))
    m_safe = jnp.where(m_new == NEG_INF, 0.0, m_new)
    alpha = jnp.exp2(m_prev - m_safe)
    p = jnp.exp2(s - m_safe)
    l_ref[...] = alpha * l_ref[...] + jnp.sum(p, axis=-1, keepdims=True)
    if v_transposed:
        pv = lax.dot_general(p.astype(BF16), v_b, (((1,), (1,)), ((), ())), preferred_element_type=F32)
    else:
        pv = jnp.dot(p.astype(BF16), v_b, preferred_element_type=F32)
    acc_ref[...] = alpha * acc_ref[...] + pv
    m_ref[...] = m_new


def _online_update_t(s, mask, vt_b, m_ref, l_ref, acc_ref):
    if mask is not None:
        s = jnp.where(mask, s, NEG_INF)
    m_prev = m_ref[...]
    m_new = jnp.maximum(m_prev, jnp.max(s, axis=0, keepdims=True))
    m_safe = jnp.where(m_new == NEG_INF, 0.0, m_new)
    alpha = jnp.exp2(m_prev - m_safe)
    p = jnp.exp2(s - m_safe)
    l_ref[...] = alpha * l_ref[...] + jnp.sum(p, axis=0, keepdims=True)
    acc_ref[...] = alpha * acc_ref[...] + jnp.dot(vt_b, p.astype(BF16), preferred_element_type=F32)
    m_ref[...] = m_new


def _init_stats(*triples):
    for m_, l_, a_ in triples:
        m_[...] = jnp.full(m_.shape, NEG_INF, F32)
        l_[...] = jnp.zeros(l_.shape, F32)
        a_[...] = jnp.zeros(a_.shape, F32)


def _nsa_expand(q):
    lo = _iota((q.shape[0], LANES), 1) < 64
    a, b = q[:, :LANES], q[:, LANES:]
    h0 = jnp.where(lo, a, 0.0)
    h1 = jnp.where(lo, pltpu.roll(a, 64, 1), 0.0)
    h2 = jnp.where(lo, 0.0, pltpu.roll(b, 64, 1))
    h3 = jnp.where(lo, 0.0, b)
    return jnp.concatenate([h0, h1, h2, h3], axis=0)


def _nsa_collect(o, n):
    lo = _iota((n, LANES), 1) < 64
    o0, o1, o2, o3 = o[0:n], o[n:2 * n], o[2 * n:3 * n], o[3 * n:4 * n]
    left = jnp.where(lo, o0, pltpu.roll(o1, 64, 1))
    right = jnp.where(lo, pltpu.roll(o2, 64, 1), o3)
    return jnp.concatenate([left, right], axis=1)


def _collect_t(o_t, n):
    return jnp.concatenate([o_t[0:64, 0:n], o_t[0:64, n:2 * n], o_t[64:128, 2 * n:3 * n], o_t[64:128, 3 * n:4 * n]],
                           axis=0)


def _layer_norm(v, g, b):
    mu = jnp.mean(v, axis=-1, keepdims=True)
    d = v - mu
    var = jnp.mean(d * d, axis=-1, keepdims=True)
    return d * lax.rsqrt(var + EPS) * g + b


def _rope128(v, c, sa, sb, half):
    return v * c + pltpu.roll(v, LANES - half, 1) * sa + pltpu.roll(v, half, 1) * sb


PROJ_OUT = ("z", "xbc", "qn", "qnr", "kc", "vc", "ks", "vs", "kw", "vw", "dq", "dk", "dv", "lx", "lg", "misc")
PROJ_OUT_T = ("kcT", "vcT", "ksT", "vsT", "kwT", "vwT", "dkT", "dvT")
PROJ_W = dict(z=256, xbc=768, qn=256, qnr=256, kc=128, vc=128, ks=128, vs=128, kw=128, vw=128, dq=256, dk=256,
              dv=256, lx=256, lg=256, misc=128)


def _proj_kernel(x_ref, w_ref, cn_ref, san_ref, sbn_ref, cd_ref, sad_ref, sbd_ref, *outs, flat_t):
    o = dict(zip(PROJ_OUT + PROJ_OUT_T, outs))
    xb = x_ref[...].astype(BF16)

    def seg(name):
        a, b = SEG[name]
        return jnp.dot(xb, w_ref[:, a:b], preferred_element_type=F32)

    def rope_n(v):
        return _rope128(v, cn_ref[...], san_ref[...], sbn_ref[...], 8)

    def rope_d(v):
        return _rope128(v, cd_ref[...], sad_ref[...], sbd_ref[...], 4)

    def put(name, v):
        o[name][...] = v
        if name + "T" in o:
            if flat_t:
                o[name + "T"][...] = v.T
            else:
                o[name + "T"][0] = v.T

    put("z", seg("z"))
    put("xbc", seg("xbc"))
    qn = seg("qn")
    put("qn", qn)
    put("qnr", jnp.concatenate([rope_n(qn[:, :LANES]), rope_n(qn[:, LANES:])], axis=1))
    put("kc", seg("kc"))
    put("vc", seg("vc"))
    put("ks", rope_n(seg("ks")))
    put("vs", seg("vs"))
    put("kw", rope_n(seg("kw")))
    put("vw", seg("vw"))
    dq = seg("dq")
    put("dq", jnp.concatenate([rope_d(dq[:, :LANES]), rope_d(dq[:, LANES:])], axis=1))
    dk = seg("dk")
    put("dk", jnp.concatenate([rope_d(dk[:, :LANES]), rope_d(dk[:, LANES:])], axis=1))
    put("dv", seg("dv"))
    put("lx", seg("lx"))
    put("lg", seg("lg"))
    put("misc", seg("misc"))


def _rope_tables(pos, dh, rows):
    rd = dh // 4
    half = rd // 2
    inv = ROPE_THETA ** (-jnp.arange(half, dtype=F32) / half)
    ang = pos.astype(F32)[:, None] * inv[None, :]
    cos, sin = jnp.cos(ang), jnp.sin(ang)
    n = pos.shape[0]
    zero_h = jnp.zeros((n, half), F32)
    zero_r = jnp.zeros((n, dh - rd), F32)
    c = jnp.concatenate([cos, cos, jnp.ones((n, dh - rd), F32)], 1)
    sa = jnp.concatenate([-sin, zero_h, zero_r], 1)
    sb = jnp.concatenate([zero_h, sin, zero_r], 1)
    reps = (max(rows // n, 1), LANES // dh)
    return tuple(jnp.tile(t, reps) for t in (c, sa, sb))


def _proj(x2d, w_in_p, pos, nb, sl, tm):
    t = x2d.shape[0]
    n_tab = max(sl // tm, 1)
    flat_t = sl < tm
    tabs = _rope_tables(pos, 64, tm) + _rope_tables(pos, 32, tm)
    tab_spec = pl.BlockSpec((tm, LANES), lambda i: (i % n_tab, 0))
    out_specs = [pl.BlockSpec((tm, PROJ_W[k]), lambda i: (i, 0)) for k in PROJ_OUT]
    out_shape = [jax.ShapeDtypeStruct((t, PROJ_W[k]), F32) for k in PROJ_OUT]
    for k in PROJ_OUT_T:
        c = PROJ_W[k[:-1]]
        if flat_t:
            out_specs.append(pl.BlockSpec((c, tm), lambda i: (0, i)))
            out_shape.append(jax.ShapeDtypeStruct((c, t), F32))
        else:
            out_specs.append(pl.BlockSpec((1, c, tm), lambda i: (i // n_tab, 0, i % n_tab)))
            out_shape.append(jax.ShapeDtypeStruct((nb, c, sl), F32))
    outs = pl.pallas_call(
        functools.partial(_proj_kernel, flat_t=flat_t),
        grid=(t // tm,),
        in_specs=[pl.BlockSpec((tm, D_MODEL), lambda i: (i, 0)),
                  pl.BlockSpec((D_MODEL, D_IN_PAD), lambda i: (0, 0))] + [tab_spec] * 6,
        out_specs=out_specs, out_shape=out_shape,
        compiler_params=_params(("parallel",)),
        name="proj",
    )(x2d, w_in_p, *tabs)
    pj = dict(zip(PROJ_OUT + PROJ_OUT_T, outs))
    if flat_t:
        for k in PROJ_OUT_T:
            pj[k] = pj[k].reshape(-1, nb, sl).transpose(1, 0, 2)
    return pj


def _conv_chunk(x_ref, prev_ref, cw_ref, cb_ref, xbuf, q, first):
    @pl.when(first)
    def _():
        xbuf[0:8, :] = prev_ref[0]

    xbuf[8:8 + q, :] = x_ref[0]
    acc = cb_ref[...] + cw_ref[0:1, :] * xbuf[pl.ds(5, q), :]
    for j in range(1, 4):
        acc = acc + cw_ref[j:j + 1, :] * xbuf[pl.ds(5 + j, q), :]
    xbuf[0:8, :] = xbuf[q:q + 8, :]
    return acc


def _ssd_kernel(xbc_ref, z_ref, misc_ref, prev_ref, h0_ref, cw_ref, cb_ref, dtb_ref, alog_ref, dexp_ref, nw_ref,
                y_ref, hout_ref, xbuf, ht, *, q):
    c = pl.program_id(1)

    @pl.when(c == 0)
    def _():
        ht[...] = _dot_nt_01(_eye(SSD_STATE, SSD_STATE), h0_ref[0])

    acc = _conv_chunk(xbc_ref, prev_ref, cw_ref, cb_ref, xbuf, q, c == 0)
    xc = acc * jax.nn.sigmoid(acc)
    sx = xc[:, 0:256]
    sb = (xc[:, 256:384], xc[:, 384:512])
    sc = (xc[:, 512:640], xc[:, 640:768])
    dt = _softplus(misc_ref[0] + dtb_ref[...])
    da = dt * (-jnp.exp(alog_ref[...]))
    causal = _iota((q, q), 0) >= _iota((q, q), 1)
    acum = _dot_01(causal.astype(BF16), da)
    xsel = _eye(8, LANES)
    acum_t = _dot_nt_01(xsel, acum)
    dt_t = _dot_nt_01(xsel, dt)
    a_last = acum[q - 1:q, :]
    wlast = jnp.exp(a_last - acum) * dt
    ea = jnp.exp(acum)
    head = _iota((q, GROUP_W), 1) >> 6
    ht_old = ht[...]
    y = dexp_ref[...] * sx
    for g in range(2):
        cb = _dot_nt(sc[g], sb[g])
        for h in (2 * g, 2 * g + 1):
            seg = acum[:, h:h + 1] - acum_t[h:h + 1, :]
            decay = jnp.exp(jnp.where(causal, seg, NEG_INF))
            m = cb * decay * dt_t[h:h + 1, :]
            y_h = _dot(m, sx) + _dot(sc[g] * ea[:, h:h + 1], ht_old)
            y = y + jnp.where(head == h, y_h, 0.0)
    xw = sx * _head_bcast(wlast, head, SSD_HEADS)
    lane = _iota((q, GROUP_W), 1)
    bt0 = _dot_nt(_eye(SSD_STATE, SSD_STATE), sb[0])
    bt1 = _dot_nt(_eye(SSD_STATE, SSD_STATE), sb[1])
    head1 = _iota((1, GROUP_W), 1) >> 6
    dch = _head_bcast(jnp.exp(a_last), head1, SSD_HEADS)
    ht_new = dch * ht_old + _dot(bt0, jnp.where(lane < 128, xw, 0.0)) + _dot(bt1, jnp.where(lane < 128, 0.0, xw))
    ht[...] = ht_new
    zz = z_ref[0]
    y = y * (zz * jax.nn.sigmoid(zz))
    y_ref[0] = y * lax.rsqrt(jnp.mean(y * y, axis=-1, keepdims=True) + EPS) * nw_ref[...]

    @pl.when(c == pl.num_programs(1) - 1)
    def _():
        hout_ref[0] = _dot_nt_01(_eye(GROUP_W, GROUP_W), ht_new)


def _ssd(pj, nb, sl, prev8, h0, p):
    q = min(128, sl)
    nc = sl // q
    row = lambda w: pl.BlockSpec((1, q, w), lambda b, c: (b, c, 0))
    per_b = lambda r, w: pl.BlockSpec((1, r, w), lambda b, c: (b, 0, 0))
    const = lambda r, w: pl.BlockSpec((r, w), lambda b, c: (0, 0))
    y, hout = pl.pallas_call(
        functools.partial(_ssd_kernel, q=q),
        grid=(nb, nc),
        in_specs=[row(768), row(256), row(128), per_b(8, 768), per_b(256, 128),
                  const(8, 768), const(1, 768), const(1, 128), const(1, 128), const(1, 256), const(1, 256)],
        out_specs=[row(256), per_b(256, 128)],
        out_shape=[jax.ShapeDtypeStruct((nb, sl, 256), F32), jax.ShapeDtypeStruct((nb, 256, 128), F32)],
        scratch_shapes=[pltpu.VMEM((q + 8, 768), F32), pltpu.VMEM((SSD_STATE, GROUP_W), F32)],
        compiler_params=_params(("parallel", "arbitrary")),
        name="ssd",
    )(pj["xbc"].reshape(nb, sl, 768), pj["z"].reshape(nb, sl, 256), pj["misc"].reshape(nb, sl, 128), prev8, h0,
      p["ssd_cw"], p["ssd_cb"], p["ssd_dtb"], p["ssd_alog"], p["ssd_dexp"], p["ssd_nw"])
    return y.reshape(nb * sl, 256), hout


def _lru_kernel(x_ref, g_ref, prev_ref, h0_ref, cw_ref, cb_ref, wa_ref, ba_ref, wx_ref, bx_ref, lam_ref,
                y_ref, hout_ref, xbuf, hc, *, q):
    c = pl.program_id(1)

    @pl.when(c == 0)
    def _():
        hc[...] = h0_ref[0]

    xc = _conv_chunk(x_ref, prev_ref, cw_ref, cb_ref, xbuf, q, c == 0)
    xcb = xc.astype(BF16)
    r = jax.nn.sigmoid(jnp.dot(xcb, wa_ref[...], preferred_element_type=F32) + ba_ref[...])
    i = jax.nn.sigmoid(jnp.dot(xcb, wx_ref[...], preferred_element_type=F32) + bx_ref[...])
    log_a = -LRU_C * r * _softplus(-lam_ref[...])
    a = jnp.exp(log_a)
    u = jnp.sqrt(1.0 - jnp.exp(2.0 * log_a)) * (i * xc)
    row = _iota((q, GROUP_W), 0)
    s = 1
    while s < q:
        a_sh = pltpu.roll(a, s, 0)
        u_sh = pltpu.roll(u, s, 0)
        keep = row >= s
        u = jnp.where(keep, a * u_sh + u, u)
        a = jnp.where(keep, a * a_sh, a)
        s *= 2
    h = u + a * hc[...]
    hc[...] = h[q - 1:q, :]
    gg = g_ref[0]
    gelu = 0.5 * gg * (1.0 + jnp.tanh(math.sqrt(2.0 / math.pi) * (gg + 0.044715 * (gg * gg * gg))))
    y_ref[0] = h * gelu

    @pl.when(c == pl.num_programs(1) - 1)
    def _():
        hout_ref[0] = h[q - 1:q, :]


def _lru(pj, nb, sl, prev8, h0, p):
    q = min(256, sl)
    nc = sl // q
    row = lambda w: pl.BlockSpec((1, q, w), lambda b, c: (b, c, 0))
    per_b = lambda r, w: pl.BlockSpec((1, r, w), lambda b, c: (b, 0, 0))
    const = lambda r, w: pl.BlockSpec((r, w), lambda b, c: (0, 0))
    y, hout = pl.pallas_call(
        functools.partial(_lru_kernel, q=q),
        grid=(nb, nc),
        in_specs=[row(256), row(256), per_b(8, 256), per_b(1, 256), const(8, 256), const(1, 256),
                  const(256, 256), const(1, 256), const(256, 256), const(1, 256), const(1, 256)],
        out_specs=[row(256), per_b(1, 256)],
        out_shape=[jax.ShapeDtypeStruct((nb, sl, 256), F32), jax.ShapeDtypeStruct((nb, 1, 256), F32)],
        scratch_shapes=[pltpu.VMEM((q + 8, 256), F32), pltpu.VMEM((1, 256), F32)],
        compiler_params=_params(("parallel", "arbitrary")),
        name="lru",
    )(pj["lx"].reshape(nb, sl, 256), pj["lg"].reshape(nb, sl, 256), prev8, h0.reshape(nb, 1, 256),
      p["lru_cw"], p["lru_cb"], p["lru_wa"], p["lru_ba"], p["lru_wx"], p["lru_bx"], p["lru_lam"])
    return y.reshape(nb * sl, 256), hout.reshape(nb, 256)


def _summ_kernel(pt_ref, *refs, n_parts):
    kp, vp = refs[:n_parts], refs[n_parts:2 * n_parts]
    mk_ref, mv_ref, ko_ref, vo_ref = refs[2 * n_parts:]

    def one(parts, m_ref, o_ref):
        chunks = []
        for r in parts:
            x = r[...]
            chunks.append(x.reshape(x.shape[-2], x.shape[-1]).astype(BF16))
        x = jnp.concatenate(chunks, axis=1) if len(chunks) > 1 else chunks[0]
        top = jnp.dot(x[0:64], m_ref[0], preferred_element_type=F32)
        bot = jnp.dot(x[64:128], m_ref[1], preferred_element_type=F32)
        o_ref[0] = jnp.concatenate([top, bot], axis=0)

    one(kp, mk_ref, ko_ref)
    one(vp, mv_ref, vo_ref)


def _summarize(k_src, v_src, part_specs, part_w, nb, total, table, w_k, w_v):
    n_parts = len(part_specs)
    step_w = n_parts * part_w
    n_blk = step_w // NSA_CMP_BLOCK
    tok = jnp.arange(step_w, dtype=jnp.int32)
    in_block = tok[:, None] // NSA_CMP_BLOCK == jnp.arange(n_blk, dtype=jnp.int32)[None, :]
    weights = lambda w: jnp.where(in_block[None], w[:, tok % NSA_CMP_BLOCK][:, :, None], 0.0).astype(BF16)
    wspec = pl.BlockSpec((2, step_w, n_blk), lambda b, s, pt: (0, 0, 0))
    ospec = pl.BlockSpec((1, LANES, n_blk), lambda b, s, pt: (b, 0, s))
    oshape = jax.ShapeDtypeStruct((nb, LANES, total // NSA_CMP_BLOCK), F32)
    return pl.pallas_call(
        functools.partial(_summ_kernel, n_parts=n_parts),
        grid_spec=pltpu.PrefetchScalarGridSpec(
            num_scalar_prefetch=1, grid=(nb, total // step_w),
            in_specs=list(part_specs) * 2 + [wspec, wspec],
            out_specs=[ospec, ospec]),
        out_shape=[oshape, oshape],
        compiler_params=_params(("parallel", "arbitrary")),
        name="nsa_summarize",
    )(table, *([k_src] * n_parts), *([v_src] * n_parts), weights(w_k), weights(w_v))


def _cmpsel_t_kernel(q_ref, kct_ref, vct_ref, ocmp_ref, sel_ref, *, qt, nc, ns):
    qi = pl.program_id(1)
    q4 = _nsa_expand(q_ref[0] * 0.125)
    s = _dot_nt(kct_ref[0].T, q4)
    qpos = qi * qt + (_iota((nc, 4 * qt), 1) & (qt - 1))
    blk_end = (_iota((nc, 4 * qt), 0) + 1) * NSA_CMP_BLOCK - 1
    s = jnp.where(blk_end <= qpos, s, NEG_INF)
    m = jnp.max(s, axis=0, keepdims=True)
    m = jnp.where(m == NEG_INF, 0.0, m)
    e = jnp.exp(s - m)
    p = e / jnp.maximum(jnp.sum(e, axis=0, keepdims=True), 1e-30)
    ocmp_ref[0] = _collect_t(_dot(vct_ref[0], p), qt).T
    impc = jnp.concatenate([p[:, 0:qt] + p[:, qt:2 * qt], p[:, 2 * qt:3 * qt] + p[:, 3 * qt:4 * qt]], axis=1)
    pair = (_iota((ns, nc), 1) >> 1 == _iota((ns, nc), 0)).astype(BF16)
    imp = _dot_01(pair, impc)
    blk = _iota((ns, 2 * qt), 0)
    cur = (qi * qt + (_iota((ns, 2 * qt), 1) & (qt - 1))) >> 6
    forced = (blk == 0) | (blk == cur) | (blk == cur - 1)
    score = jnp.where(forced, FORCE_SCORE, jnp.where(blk > cur, -1.0, imp))
    cnt = jnp.zeros((ns, 2 * qt), F32)
    for j in range(ns):
        row = score[j:j + 1, :]
        cnt = cnt + ((row > score) | ((row == score) & (blk > j))).astype(F32)
    sel = (cnt < float(NSA_TOPN)).astype(F32)
    sel_ref[0, 0] = sel[:, 0:qt]
    sel_ref[0, 1] = sel[:, qt:2 * qt]


def _cmp_select_prompt(qn, kcmp_t, vcmp_t, nb, sl, qt):
    nc = kcmp_t.shape[2]
    ns = sl // NSA_SLC_BLOCK
    return pl.pallas_call(
        functools.partial(_cmpsel_t_kernel, qt=qt, nc=nc, ns=ns),
        grid=(nb, sl // qt),
        in_specs=[pl.BlockSpec((1, qt, 256), lambda b, i: (b, i, 0)),
                  pl.BlockSpec((1, LANES, nc), lambda b, i: (b, 0, 0)),
                  pl.BlockSpec((1, LANES, nc), lambda b, i: (b, 0, 0))],
        out_specs=[pl.BlockSpec((1, qt, 256), lambda b, i: (b, i, 0)),
                   pl.BlockSpec((1, 2, ns, qt), lambda b, i: (b, 0, 0, i))],
        out_shape=[jax.ShapeDtypeStruct((nb, sl, 256), F32), jax.ShapeDtypeStruct((nb, 2, ns, sl), F32)],
        compiler_params=_params(("parallel", "parallel")),
        name="nsa_cmp_select_prompt",
    )(qn.reshape(nb, sl, 256), kcmp_t, vcmp_t)


def _cmpsel_kernel(q_ref, kct_ref, vct_ref, ocmp_ref, selx_ref, *, qt, nc, ns, nsp, past, n_chunks):
    q4 = _nsa_expand(q_ref[0] * 0.125)
    s = _dot(q4, kct_ref[0])
    qpos = past + (_iota((4 * qt, nc), 0) & (qt - 1))
    blk_end = (_iota((4 * qt, nc), 1) + 1) * NSA_CMP_BLOCK - 1
    s = jnp.where(blk_end <= qpos, s, NEG_INF)
    m = jnp.max(s, axis=-1, keepdims=True)
    m = jnp.where(m == NEG_INF, 0.0, m)
    e = jnp.exp(s - m)
    p = e / jnp.maximum(jnp.sum(e, axis=-1, keepdims=True), 1e-30)
    ocmp_ref[0] = _nsa_collect(_dot_nt(p, vct_ref[0]), qt)
    impc = jnp.concatenate([p[0:qt] + p[qt:2 * qt], p[2 * qt:3 * qt] + p[3 * qt:4 * qt]], axis=0)
    pair = (_iota((nc, nsp), 0) >> 1 == _iota((nc, nsp), 1)).astype(BF16)
    imp = _dot_01_r(impc, pair)
    blk = _iota((2 * qt, nsp), 1)
    cur = (past + (_iota((2 * qt, nsp), 0) & (qt - 1))) >> 6
    forced = (blk == 0) | (blk == cur) | (blk == cur - 1)
    score = jnp.where(forced, FORCE_SCORE, jnp.where(blk > cur, -1.0, imp))
    score = jnp.where(blk < ns, score, -3.0)

    cnt = jnp.zeros((2 * qt, nsp), F32)
    for j in range(ns):
        col = score[:, j:j + 1]
        cnt = cnt + ((col > score) | ((col == score) & (blk > j))).astype(F32)
    sel = ((cnt < float(NSA_TOPN)) & (blk < ns)).astype(F32)
    sel4 = jnp.concatenate([sel[0:qt], sel[0:qt], sel[qt:2 * qt], sel[qt:2 * qt]], axis=0).astype(BF16)
    e16 = (_iota((LANES, 1024), 1) >> 6 == _iota((LANES, 1024), 0)).astype(BF16)
    for t in range(n_chunks):
        pick = ((_iota((nsp, LANES), 0) == 16 * t + _iota((nsp, LANES), 1))
                & (_iota((nsp, LANES), 1) < 16)).astype(BF16)
        blocks = jnp.dot(sel4, pick, preferred_element_type=F32).astype(BF16)
        selx_ref[0, :, 1024 * t:1024 * (t + 1)] = jnp.dot(blocks, e16, preferred_element_type=F32)


def _cmp_select_sample(qn, kcmp_t, vcmp_t, nb, sl, past):
    nc = kcmp_t.shape[2]
    ns = -(-(past + sl) // NSA_SLC_BLOCK)
    nsp = -(-ns // LANES) * LANES
    n_chunks = -(-(past + sl) // 1024)
    return pl.pallas_call(
        functools.partial(_cmpsel_kernel, qt=sl, nc=nc, ns=ns, nsp=nsp, past=past, n_chunks=n_chunks),
        grid=(nb,),
        in_specs=[pl.BlockSpec((1, sl, 256), lambda b: (b, 0, 0)),
                  pl.BlockSpec((1, LANES, nc), lambda b: (b, 0, 0)),
                  pl.BlockSpec((1, LANES, nc), lambda b: (b, 0, 0))],
        out_specs=[pl.BlockSpec((1, sl, 256), lambda b: (b, 0, 0)),
                   pl.BlockSpec((1, 4 * sl, 1024 * n_chunks), lambda b: (b, 0, 0))],
        out_shape=[jax.ShapeDtypeStruct((nb, sl, 256), F32),
                   jax.ShapeDtypeStruct((nb, 4 * sl, 1024 * n_chunks), F32)],
        compiler_params=_params(("parallel",)),
        name="nsa_cmp_select_sample",
    )(qn.reshape(nb, sl, 256), kcmp_t, vcmp_t)


def _nsap_kernel(q_ref, ks_ref, vst_ref, kw_ref, vwt_ref, sel_ref, oslc_ref, owin_ref,
                 q4_s, m1, l1, a1, m2, l2, a2, *, qt, kt):
    qi, kj = pl.program_id(1), pl.program_id(2)
    last = (qi * qt + qt - 1) // kt
    first_w = jnp.maximum(qi * qt - (NSA_WINDOW - 1), 0) // kt

    @pl.when(kj == 0)
    def _():
        q4_s[...] = _nsa_expand(q_ref[0] * NSA_QSCALE).astype(BF16)
        _init_stats((m1, l1, a1), (m2, l2, a2))

    def positions():
        return kj * kt + _iota((kt, 4 * qt), 0), qi * qt + (_iota((kt, 4 * qt), 1) & (qt - 1))

    def scores(k_ref):
        return lax.dot_general(k_ref[0].astype(BF16), q4_s[...], (((1,), (1,)), ((), ())),
                               preferred_element_type=F32)

    def picked():
        rows = []
        for u in range(kt // NSA_SLC_BLOCK):
            blk = kj * (kt // NSA_SLC_BLOCK) + u
            r0 = sel_ref[0, 0, pl.ds(blk, 1), :]
            r1 = sel_ref[0, 1, pl.ds(blk, 1), :]
            rows.append(jnp.broadcast_to(jnp.concatenate([r0, r0, r1, r1], axis=1), (NSA_SLC_BLOCK, 4 * qt)))
        return jnp.concatenate(rows, axis=0) > 0.5

    @pl.when(kj < last)
    def _():
        _online_update_t(scores(ks_ref), picked(), vst_ref[0].astype(BF16), m1, l1, a1)

    @pl.when(kj == last)
    def _():
        kpos, qpos = positions()
        _online_update_t(scores(ks_ref), picked() & (kpos <= qpos), vst_ref[0].astype(BF16), m1, l1, a1)

    full_w = (kj < last) & (kj * kt > qi * qt + qt - 1 - NSA_WINDOW)

    @pl.when((kj >= first_w) & full_w)
    def _():
        _online_update_t(scores(kw_ref), None, vwt_ref[0].astype(BF16), m2, l2, a2)

    @pl.when((kj >= first_w) & (kj <= last) & jnp.logical_not(full_w))
    def _():
        kpos, qpos = positions()
        _online_update_t(scores(kw_ref), (kpos <= qpos) & (kpos > qpos - NSA_WINDOW), vwt_ref[0].astype(BF16),
                         m2, l2, a2)

    @pl.when(kj == last)
    def _():
        oslc_ref[0] = _collect_t(a1[...] / jnp.maximum(l1[...], 1e-30), qt).T
        owin_ref[0] = _collect_t(a2[...] / jnp.maximum(l2[...], 1e-30), qt).T


def _nsa_prompt_attn(pj, sel, nb, sl):
    qt, kt = min(256, sl), min(256, sl)
    last = lambda i: (i * qt + qt - 1) // kt
    first_w = lambda i: jnp.maximum(i * qt - (NSA_WINDOW - 1), 0) // kt
    kspec = pl.BlockSpec((1, kt, LANES), lambda b, i, j: (b, jnp.minimum(j, last(i)), 0))
    vspec = pl.BlockSpec((1, LANES, kt), lambda b, i, j: (b, 0, jnp.minimum(j, last(i))))
    kwspec = pl.BlockSpec((1, kt, LANES), lambda b, i, j: (b, jnp.clip(j, first_w(i), last(i)), 0))
    vwspec = pl.BlockSpec((1, LANES, kt), lambda b, i, j: (b, 0, jnp.clip(j, first_w(i), last(i))))
    qspec = pl.BlockSpec((1, qt, 256), lambda b, i, j: (b, i, 0))
    ns = sel.shape[2]
    r3 = lambda a, w: a.reshape(nb, sl, w)
    stats = [pltpu.VMEM((1, 4 * qt), F32), pltpu.VMEM((1, 4 * qt), F32), pltpu.VMEM((LANES, 4 * qt), F32)]
    return pl.pallas_call(
        functools.partial(_nsap_kernel, qt=qt, kt=kt),
        grid=(nb, sl // qt, sl // kt),
        in_specs=[qspec, kspec, vspec, kwspec, vwspec, pl.BlockSpec((1, 2, ns, qt), lambda b, i, j: (b, 0, 0, i))],
        out_specs=[qspec, qspec],
        out_shape=[jax.ShapeDtypeStruct((nb, sl, 256), F32)] * 2,
        scratch_shapes=[pltpu.VMEM((4 * qt, LANES), BF16)] + stats * 2,
        compiler_params=_params(("parallel", "parallel", "arbitrary")),
        name="nsa_prompt_attn",
    )(r3(pj["qnr"], 256), r3(pj["ks"], 128), pj["vsT"], r3(pj["kw"], 128), pj["vwT"], sel)


def _combine_kernel(misc_ref, oc_ref, os_ref, ow_ref, y_ref):
    g = jax.nn.sigmoid(misc_ref[...])
    head = _iota(oc_ref.shape, 1) >> 6
    y_ref[...] = (_head_bcast(g, head, 4, GATE_LANE0) * oc_ref[...]
                  + _head_bcast(g, head, 4, GATE_LANE0 + 4) * os_ref[...]
                  + _head_bcast(g, head, 4, GATE_LANE0 + 8) * ow_ref[...])


def _nsa_combine(misc, oc, os_, ow, tm):
    t = misc.shape[0]
    spec = lambda w: pl.BlockSpec((tm, w), lambda i: (i, 0))
    return pl.pallas_call(
        _combine_kernel, grid=(t // tm,),
        in_specs=[spec(128), spec(256), spec(256), spec(256)], out_specs=spec(256),
        out_shape=jax.ShapeDtypeStruct((t, 256), F32),
        compiler_params=_params(("parallel",)), name="nsa_combine",
    )(misc, oc.reshape(t, 256), os_.reshape(t, 256), ow.reshape(t, 256))


def _diff_lambda(lamv_ref, lam_init):
    lv = lamv_ref[...]
    s1 = jnp.sum(lv[0:1] * lv[1:2], axis=1, keepdims=True)
    s2 = jnp.sum(lv[2:3] * lv[3:4], axis=1, keepdims=True)
    return jnp.exp(s1) - jnp.exp(s2) + lam_init


def _diff_finish(o, head, nheads, nw, lam_init):
    inv = jnp.zeros(o.shape, F32)
    for h in range(nheads):
        ms = jnp.sum(jnp.where(head == h, o * o, 0.0), axis=-1, keepdims=True) * (1.0 / 64.0)
        inv = jnp.where(head == h, lax.rsqrt(ms + EPS), inv)
    return o * inv * nw * (1.0 - lam_init)


def _diffp_kernel(q_ref, k_ref, vt_ref, lamv_ref, nwt_ref, o_ref, q4_s, m_s, l_s, acc_s, *, qt, kt, lam_init):
    qi, kj = pl.program_id(2), pl.program_id(3)
    last = (qi * qt + qt - 1) // kt

    @pl.when(kj == 0)
    def _():
        q = q_ref[0] * DIFF_QSCALE
        part = _iota((qt, LANES), 1) >> 5
        q4_s[...] = jnp.concatenate([jnp.where(part == k, q, 0.0) for k in range(4)], axis=0).astype(BF16)
        _init_stats((m_s, l_s, acc_s))

    def scores():
        return lax.dot_general(k_ref[0].astype(BF16), q4_s[...], (((1,), (1,)), ((), ())),
                               preferred_element_type=F32)

    @pl.when(kj < last)
    def _():
        _online_update_t(scores(), None, vt_ref[0].astype(BF16), m_s, l_s, acc_s)

    @pl.when(kj == last)
    def _():
        kpos = kj * kt + _iota((kt, 4 * qt), 0)
        qpos = qi * qt + (_iota((kt, 4 * qt), 1) & (qt - 1))
        _online_update_t(scores(), kpos <= qpos, vt_ref[0].astype(BF16), m_s, l_s, acc_s)
        lam = _diff_lambda(lamv_ref, lam_init)
        o = acc_s[...] / jnp.maximum(l_s[...], 1e-30)
        halves = []
        for h in range(2):
            rows = slice(64 * h, 64 * (h + 1))
            oh = o[rows, 2 * h * qt:(2 * h + 1) * qt] - lam * o[rows, (2 * h + 1) * qt:(2 * h + 2) * qt]
            ms = jnp.mean(oh * oh, axis=0, keepdims=True)
            halves.append(oh * lax.rsqrt(ms + EPS))
        o_ref[0] = (jnp.concatenate(halves, axis=0) * nwt_ref[...] * (1.0 - lam_init)).T


def _diff_prompt(pj, nb, sl, p, lam_init):
    qt, kt = min(256, sl), min(512, sl)
    last = lambda i: (i * qt + qt - 1) // kt
    qspec = pl.BlockSpec((1, qt, LANES), lambda b, h, i, j: (b, i, h))
    kspec = pl.BlockSpec((1, kt, LANES), lambda b, h, i, j: (b, jnp.minimum(j, last(i)), h))
    vspec = pl.BlockSpec((1, LANES, kt), lambda b, h, i, j: (b, h, jnp.minimum(j, last(i))))
    r3 = lambda a: a.reshape(nb, sl, 256)
    out = pl.pallas_call(
        functools.partial(_diffp_kernel, qt=qt, kt=kt, lam_init=lam_init),
        grid=(nb, 2, sl // qt, sl // kt),
        in_specs=[qspec, kspec, vspec, pl.BlockSpec((4, 32), lambda b, h, i, j: (0, 0)),
                  pl.BlockSpec((LANES, 1), lambda b, h, i, j: (0, 0))],
        out_specs=qspec,
        out_shape=jax.ShapeDtypeStruct((nb, sl, 256), F32),
        scratch_shapes=[pltpu.VMEM((4 * qt, LANES), BF16), pltpu.VMEM((1, 4 * qt), F32),
                        pltpu.VMEM((1, 4 * qt), F32), pltpu.VMEM((LANES, 4 * qt), F32)],
        compiler_params=_params(("parallel", "parallel", "parallel", "arbitrary")),
        name="diff_prompt",
    )(r3(pj["dq"]), r3(pj["dk"]), pj["dvT"], p["diff_lambda"], p["diff_nw256"][:, :LANES].reshape(LANES, 1))
    return out.reshape(nb * sl, 256)


def _paged_kernel(pt_ref, q_ref, *refs, mode, pps, width, kbase, past, new_len, lam_init):
    kp, vp = refs[:pps], refs[pps:2 * pps]
    knew_ref, vnew_ref = refs[2 * pps:2 * pps + 2]
    rest = list(refs[2 * pps + 2:])
    selx_ref = selnew_ref = lamv_ref = nw_ref = None
    if mode == "slc":
        selx_ref, selnew_ref = rest[:2]
        rest = rest[2:]
    if mode == "diff":
        lamv_ref, nw_ref = rest[:2]
        rest = rest[2:]
    o_ref, qe_s, m_s, l_s, acc_s = rest
    nq = new_len
    rows = qe_s.shape[0]
    step = pl.program_id(1)

    @pl.when(step == 0)
    def _():
        if mode == "diff":
            q = q_ref[0] * DIFF_QSCALE
            part = _iota((nq, width), 1) >> 5
            qe = jnp.concatenate([jnp.where(part == k, q, 0.0) for k in range(8)], axis=0)
        else:
            qe = _nsa_expand(q_ref[0] * NSA_QSCALE)
        qe_s[...] = qe.astype(BF16)
        _init_stats((m_s, l_s, acc_s))

    nk = pps * PAGE
    kt_b = jnp.concatenate([r[0, 0].astype(BF16) for r in kp], axis=1)
    vt_b = jnp.concatenate([r[0, 0].astype(BF16) for r in vp], axis=1)
    s = jnp.dot(qe_s[...], kt_b, preferred_element_type=F32)
    qpos = past + (_iota((rows, nk), 0) & (nq - 1))
    kpos = kbase + step * nk + _iota((rows, nk), 1)
    mask = kpos <= qpos
    if mode == "win":
        mask = mask & (kpos > qpos - NSA_WINDOW)
    if mode == "slc":
        mask = mask & (selx_ref[0] > 0.5)
    _online_update(s, mask, vt_b, m_s, l_s, acc_s, v_transposed=True)

    @pl.when(step == pl.num_programs(1) - 1)
    def _():
        s2 = lax.dot_general(qe_s[...], knew_ref[0].astype(BF16), (((1,), (1,)), ((), ())),
                             preferred_element_type=F32)
        qpos2 = past + (_iota((rows, nq), 0) & (nq - 1))
        kpos2 = past + _iota((rows, nq), 1)
        mask2 = kpos2 <= qpos2
        if mode == "win":
            mask2 = mask2 & (kpos2 > qpos2 - NSA_WINDOW)
        if mode == "slc":
            mask2 = mask2 & (selnew_ref[0][:, 0:nq] > 0.5)
        _online_update(s2, mask2, vnew_ref[0].astype(BF16), m_s, l_s, acc_s)
        o = acc_s[...] / jnp.maximum(l_s[...], 1e-30)
        if mode == "diff":
            lam = _diff_lambda(lamv_ref, lam_init)
            head = _iota((nq, width), 1) >> 6
            out = jnp.zeros((nq, width), F32)
            for h in range(4):
                o0 = o[(2 * h) * nq:(2 * h + 1) * nq]
                o1 = o[(2 * h + 1) * nq:(2 * h + 2) * nq]
                out = jnp.where(head == h, o0 - lam * o1, out)
            o_ref[0] = _diff_finish(out, head, 4, nw_ref[...], lam_init)
        else:
            o_ref[0] = _nsa_collect(o, nq)


def _paged_attn(mode, q, k_src, v_src, page_spec, n_pages, table, knew, vnew, kbase, past, extra=(), lam_init=0.0):
    nb, nq = q.shape[0], q.shape[1]
    width = knew.shape[-1]
    pps = min(PAGES_PER_STEP, n_pages)
    rows = 8 * nq if mode == "diff" else 4 * nq
    nk = pps * PAGE
    per_b = lambda r_, w: pl.BlockSpec((1, r_, w), lambda b, s, pt: (b, 0, 0))
    in_specs = [per_b(nq, 256)] + [page_spec(r, pps) for r in range(pps)] * 2 + [per_b(nq, width), per_b(nq, width)]
    args = [q] + [k_src] * pps + [v_src] * pps + [knew, vnew]
    if mode == "slc":
        (selx,) = extra
        new_blk = (past - kbase) // LANES
        in_specs += [pl.BlockSpec((1, rows, nk), lambda b, s, pt: (b, 0, s)),
                     pl.BlockSpec((1, rows, LANES), lambda b, s, pt: (b, 0, new_blk))]
        args += [selx, selx]
    if mode == "diff":
        in_specs += [pl.BlockSpec((4, 32), lambda b, s, pt: (0, 0)), pl.BlockSpec((1, 256), lambda b, s, pt: (0, 0))]
        args += list(extra)
    return pl.pallas_call(
        functools.partial(_paged_kernel, mode=mode, pps=pps, width=width, kbase=kbase, past=past, new_len=nq,
                          lam_init=lam_init),
        grid_spec=pltpu.PrefetchScalarGridSpec(
            num_scalar_prefetch=1, grid=(nb, n_pages // pps), in_specs=in_specs,
            out_specs=per_b(nq, 256),
            scratch_shapes=[pltpu.VMEM((rows, width), BF16), pltpu.VMEM((rows, 1), F32),
                            pltpu.VMEM((rows, 1), F32), pltpu.VMEM((rows, width), F32)]),
        out_shape=jax.ShapeDtypeStruct((nb, nq, 256), F32),
        compiler_params=_params(("parallel", "arbitrary")),
        name="paged_" + mode,
    )(table, *args)


def _post_kernel(x_ref, ya_ref, yb_ref, yc_ref, yd_ref, wo_ref, g1_ref, b1_ref, w1_ref, w2_ref, g2_ref, b2_ref,
                 o_ref, x1_s, x1b_s, acc_s, *, alpha):
    f = pl.program_id(1)

    @pl.when(f == 0)
    def _():
        mix = _dot(ya_ref[...], wo_ref[0:256, :])
        mix = mix + _dot(yb_ref[...], wo_ref[256:512, :])
        mix = mix + _dot(yc_ref[...], wo_ref[512:768, :])
        mix = mix + _dot(yd_ref[...], wo_ref[768:1024, :])
        x1 = _layer_norm(alpha * x_ref[...] + mix, g1_ref[...], b1_ref[...])
        x1_s[...] = x1
        x1b_s[...] = x1.astype(BF16)
        acc_s[...] = jnp.zeros(acc_s.shape, F32)

    h = jnp.dot(x1b_s[...], w1_ref[...], preferred_element_type=F32)
    h = jnp.square(jnp.maximum(h, 0.0))
    acc_s[...] += jnp.dot(h.astype(BF16), w2_ref[...], preferred_element_type=F32)

    @pl.when(f == pl.num_programs(1) - 1)
    def _():
        o_ref[...] = _layer_norm(alpha * x1_s[...] + acc_s[...], g2_ref[...], b2_ref[...])


def _post(x2d, ya, yb, yc, yd, p, tm, alpha):
    t = x2d.shape[0]
    tf = 1024
    row = lambda w: pl.BlockSpec((tm, w), lambda i, f: (i, 0))
    const = lambda r, w: pl.BlockSpec((r, w), lambda i, f: (0, 0))
    return pl.pallas_call(
        functools.partial(_post_kernel, alpha=alpha),
        grid=(t // tm, D_FF // tf),
        in_specs=[row(D_MODEL), row(256), row(256), row(256), row(256), const(D_MODEL, D_MODEL),
                  const(1, D_MODEL), const(1, D_MODEL),
                  pl.BlockSpec((D_MODEL, tf), lambda i, f: (0, f)), pl.BlockSpec((tf, D_MODEL), lambda i, f: (f, 0)),
                  const(1, D_MODEL), const(1, D_MODEL)],
        out_specs=row(D_MODEL),
        out_shape=jax.ShapeDtypeStruct((t, D_MODEL), F32),
        scratch_shapes=[pltpu.VMEM((tm, D_MODEL), F32), pltpu.VMEM((tm, D_MODEL), BF16),
                        pltpu.VMEM((tm, D_MODEL), F32)],
        compiler_params=_params(("parallel", "arbitrary")),
        name="post",
    )(x2d, ya, yb, yc, yd, p["w_out"], p["ln1_g"], p["ln1_b"], p["w_ff1"], p["w_ff2"], p["ln2_g"], p["ln2_b"])


def _prep_params(w, l):
    w_in = w["w_in"][l]
    misc = jnp.concatenate([w_in[:, a:b] for a, b in MISC_COLS], axis=1)
    misc = jnp.pad(misc, ((0, 0), (0, LANES - misc.shape[1])))
    w_in_p = jnp.concatenate([w_in[:, a:b] for a, b in W_IN_ORDER] + [misc], axis=1).astype(BF16)
    row = lambda v: v.reshape(1, -1).astype(F32)
    pad_lanes = lambda v: jnp.pad(v.reshape(1, -1), ((0, 0), (0, LANES - v.shape[-1])))
    blockdiag = lambda m: jax.scipy.linalg.block_diag(*[m[i] for i in range(m.shape[0])]).astype(BF16)
    return dict(
        w_in=w_in_p,
        ssd_cw=jnp.pad(w["ssd_conv_w"][l], ((0, 4), (0, 0))), ssd_cb=row(w["ssd_conv_b"][l]),
        ssd_dtb=pad_lanes(w["ssd_dt_bias"][l]), ssd_alog=pad_lanes(w["ssd_A_log"][l]),
        ssd_dexp=row(jnp.repeat(w["ssd_D"][l], 64)), ssd_nw=row(w["ssd_norm_w"][l]),
        cmp_wk=w["nsa_w_cmp_k"][l], cmp_wv=w["nsa_w_cmp_v"][l],
        diff_lambda=w["diff_lambda"][l], diff_nw256=row(jnp.tile(w["diff_norm_w"][l], 4)),
        lru_cw=jnp.pad(w["lru_conv_w"][l], ((0, 4), (0, 0))), lru_cb=row(w["lru_conv_b"][l]),
        lru_wa=blockdiag(w["lru_w_a"][l]), lru_ba=row(w["lru_b_a"][l]),
        lru_wx=blockdiag(w["lru_w_x"][l]), lru_bx=row(w["lru_b_x"][l]), lru_lam=row(w["lru_lambda"][l]),
        w_out=w["w_out"][l].astype(BF16), ln1_g=row(w["ln1_g"][l]), ln1_b=row(w["ln1_b"][l]),
        w_ff1=w["w_ff1"][l].astype(BF16), w_ff2=w["w_ff2"][l].astype(BF16),
        ln2_g=row(w["ln2_g"][l]), ln2_b=row(w["ln2_b"][l]),
    )


def _pad_prev(prev):
    return jnp.pad(prev, ((0, 0), (5, 0), (0, 0)))


def _token_minor(a):
    a = jnp.moveaxis(a, -3, -1)
    return a.reshape(*a.shape[:-3], a.shape[-3] * a.shape[-2], a.shape[-1])


def _token_major(a_t, heads):
    nb, _, n = a_t.shape
    return jnp.moveaxis(a_t.reshape(nb, heads, 64, n), -1, 1)


def _layer(x2d, nb, sl, past, hist, p, lam_init, alpha):
    t = nb * sl
    tm = min(256, t)
    pos = past + jnp.arange(sl, dtype=jnp.int32)
    pj = _proj(x2d, p["w_in"], pos, nb, sl, tm)
    y_a, ssd_h = _ssd(pj, nb, sl, _pad_prev(hist["ssd_conv"]), hist["ssd_h"].reshape(nb, 256, 128), p)
    y_d, lru_h = _lru(pj, nb, sl, _pad_prev(hist["lru_conv"]), hist["lru_h"], p)
    r3 = lambda a, w: a.reshape(nb, sl, w)
    dummy = jnp.zeros((1, 1), jnp.int32)
    if past == 0:
        step_w = min(4096, sl)
        spec = pl.BlockSpec((1, LANES, step_w), lambda b, s, pt: (b, 0, s))
        kcmp_t, vcmp_t = _summarize(pj["kcT"], pj["vcT"], [spec], step_w, nb, sl, dummy, p["cmp_wk"], p["cmp_wv"])
        o_cmp, sel = _cmp_select_prompt(pj["qn"], kcmp_t, vcmp_t, nb, sl, min(256, sl))
        o_slc, o_win = _nsa_prompt_attn(pj, sel, nb, sl)
        y_c = _diff_prompt(pj, nb, sl, p, lam_init)
        kw_t, vw_t = pj["kwT"][:, :, sl - NSA_WINDOW:], pj["vwT"][:, :, sl - NSA_WINDOW:]
        new = dict(kc=_token_major(pj["kcT"], 2), vc=_token_major(pj["vcT"], 2), ks=_token_major(pj["ksT"], 2),
                   vs=_token_major(pj["vsT"], 2), dk=_token_major(pj["dkT"], 4), dv=_token_major(pj["dvT"], 4))
    else:
        table, l = hist["table"], hist["layer"]
        n_pages = table.shape[1]
        npp = min(PAGES_PER_STEP, n_pages)
        cache_spec = lambda r, per: pl.BlockSpec(
            (1, 1, LANES, PAGE), lambda b, s, pt, r=r, per=per: (l, pt[b, s * per + r], 0, 0))
        diff_spec = lambda r, per: pl.BlockSpec(
            (1, 1, 256, PAGE), lambda b, s, pt, r=r, per=per: (l, pt[b, s * per + r], 0, 0))
        win_spec = lambda r, per: pl.BlockSpec(
            (1, 1, LANES, PAGE), lambda b, s, pt, r=r, per=per: (l, b, 0, s * per + r))
        kcmp_t, vcmp_t = _summarize(hist["kc"], hist["vc"], [cache_spec(r, npp) for r in range(npp)], PAGE, nb,
                                    n_pages * PAGE, table, p["cmp_wk"], p["cmp_wv"])
        o_cmp, selx = _cmp_select_sample(pj["qn"], kcmp_t, vcmp_t, nb, sl, past)
        qnr = r3(pj["qnr"], 256)
        o_slc = _paged_attn("slc", qnr, hist["ks"], hist["vs"], cache_spec, n_pages, table,
                            r3(pj["ks"], 128), r3(pj["vs"], 128), 0, past, extra=(selx,))
        wbuf = hist["kw"].shape[-1]
        o_win = _paged_attn("win", qnr, hist["kw"], hist["vw"], win_spec, wbuf // PAGE, dummy,
                            r3(pj["kw"], 128), r3(pj["vw"], 128), past - wbuf, past)
        y_c = _paged_attn("diff", r3(pj["dq"], 256), hist["dk"], hist["dv"], diff_spec, n_pages, table,
                          r3(pj["dk"], 256), r3(pj["dv"], 256), 0, past,
                          extra=(p["diff_lambda"], p["diff_nw256"]), lam_init=lam_init).reshape(t, 256)
        kw_t = jnp.concatenate([hist["kw"][l], pj["kwT"]], axis=2)[:, :, -NSA_WINDOW:]
        vw_t = jnp.concatenate([hist["vw"][l], pj["vwT"]], axis=2)[:, :, -NSA_WINDOW:]
        kv = lambda a: r3(a, 128).reshape(nb, sl, 2, 64)
        new = dict(kc=kv(pj["kc"]), vc=kv(pj["vc"]), ks=kv(pj["ks"]), vs=kv(pj["vs"]),
                   dk=r3(pj["dk"], 256).reshape(nb, sl, 4, 64), dv=r3(pj["dv"], 256).reshape(nb, sl, 4, 64))
    y_b = _nsa_combine(pj["misc"], o_cmp, o_slc, o_win, tm)
    x_out = _post(x2d, y_a, y_b, y_c, y_d, p, min(512, t), alpha)
    new.update(kw=_token_major(kw_t, 2), vw=_token_major(vw_t, 2),
               ssd_h=ssd_h.reshape(nb, 4, 64, 128), ssd_conv=r3(pj["xbc"], 768)[:, sl - 3:],
               lru_h=lru_h, lru_conv=r3(pj["lx"], 256)[:, sl - 3:])
    return x_out, new


STATE_ORDER = ("kc", "vc", "ks", "vs", "dk", "dv", "kw", "vw", "ssd_h", "ssd_conv", "lru_h", "lru_conv")


def kernel(x_prompt, x_sample, cache_nsa_k_cmp, cache_nsa_v_cmp, cache_nsa_k_slc, cache_nsa_v_slc, cache_diff_k, cache_diff_v, cache_nsa_k_win, cache_nsa_v_win, state_ssd, state_ssd_conv, state_lru, state_lru_conv, page_table, w_in, ssd_conv_w, ssd_conv_b, ssd_dt_bias, ssd_A_log, ssd_D, ssd_norm_w, nsa_w_cmp_k, nsa_w_cmp_v, diff_lambda, diff_norm_w, lru_conv_w, lru_conv_b, lru_w_a, lru_b_a, lru_w_x, lru_b_x, lru_lambda, w_out, ln1_g, ln1_b, w_ff1, w_ff2, ln2_g, ln2_b):
    weights = dict(w_in=w_in, ssd_conv_w=ssd_conv_w, ssd_conv_b=ssd_conv_b, ssd_dt_bias=ssd_dt_bias,
                   ssd_A_log=ssd_A_log, ssd_D=ssd_D, ssd_norm_w=ssd_norm_w, nsa_w_cmp_k=nsa_w_cmp_k,
                   nsa_w_cmp_v=nsa_w_cmp_v, diff_lambda=diff_lambda, diff_norm_w=diff_norm_w, lru_conv_w=lru_conv_w,
                   lru_conv_b=lru_conv_b, lru_w_a=lru_w_a, lru_b_a=lru_b_a, lru_w_x=lru_w_x, lru_b_x=lru_b_x,
                   lru_lambda=lru_lambda, w_out=w_out, ln1_g=ln1_g, ln1_b=ln1_b, w_ff1=w_ff1, w_ff2=w_ff2,
                   ln2_g=ln2_g, ln2_b=ln2_b)
    depth = w_in.shape[0]
    nbp, slp, _ = x_prompt.shape
    nbs, sls, _ = x_sample.shape
    past = page_table.shape[1] * cache_nsa_k_cmp.shape[2]
    assert past % PAGE == 0 and sls < NSA_CMP_BLOCK and slp % 256 == 0
    alpha = (2 * depth) ** 0.25
    caches = dict(kc=_token_minor(cache_nsa_k_cmp), vc=_token_minor(cache_nsa_v_cmp), ks=_token_minor(cache_nsa_k_slc),
                  vs=_token_minor(cache_nsa_v_slc), dk=_token_minor(cache_diff_k), dv=_token_minor(cache_diff_v),
                  kw=_token_minor(cache_nsa_k_win), vw=_token_minor(cache_nsa_v_win))
    xp = x_prompt.reshape(nbp * slp, D_MODEL)
    xs = x_sample.reshape(nbs * sls, D_MODEL)
    outs_p, outs_s = [], []
    for l in range(depth):
        p = _prep_params(weights, l)
        lam_init = 0.8 - 0.6 * math.exp(-0.3 * l)
        hist_p = dict(ssd_conv=jnp.zeros((nbp, 3, SSD_CONV_CH), F32), ssd_h=jnp.zeros((nbp, 4, 64, 128), F32),
                      lru_conv=jnp.zeros((nbp, 3, GROUP_W), F32), lru_h=jnp.zeros((nbp, GROUP_W), F32))
        hist_s = dict(table=page_table, layer=l, ssd_conv=state_ssd_conv[l], ssd_h=state_ssd[l],
                      lru_conv=state_lru_conv[l], lru_h=state_lru[l], **caches)
        xp, new_p = _layer(xp, nbp, slp, 0, hist_p, p, lam_init, alpha)
        xs, new_s = _layer(xs, nbs, sls, past, hist_s, p, lam_init, alpha)
        outs_p.append(new_p)
        outs_s.append(new_s)
    stack = lambda lst, name: jnp.stack([d[name] for d in lst])
    return ((xp.reshape(nbp, slp, D_MODEL), xs.reshape(nbs, sls, D_MODEL))
            + tuple(stack(outs_p, n) for n in STATE_ORDER) + tuple(stack(outs_s, n) for n in STATE_ORDER))
```

```python
import functools
import math

import jax
import jax.numpy as jnp
from jax import lax
from jax.experimental import pallas as pl
from jax.experimental.pallas import tpu as pltpu

F32 = jnp.float32
BF16 = jnp.bfloat16
NEG_INF = float("-inf")

D_MODEL = 1024
GROUP_W = 256
SSD_HEADS = 4
SSD_STATE = 128
SSD_CONV_CH = 768
NSA_CMP_BLOCK = 32
NSA_SLC_BLOCK = 64
NSA_TOPN = 16
NSA_WINDOW = 512
FORCE_SCORE = 1e4
LRU_C = 8.0
D_FF = 4096
ROPE_THETA = 500000.0
EPS = 1e-5
PAGE = 128
LANES = 128
VMEM_LIMIT = 56 * 1024 * 1024
PAGES_PER_STEP = 32
LOG2E = 1.4426950408889634
NSA_QSCALE = 64.0 ** -0.5 * LOG2E
DIFF_QSCALE = 32.0 ** -0.5 * LOG2E

W_IN_ORDER = ((0, 1024), (1028, 2052), (2064, 3344))
MISC_COLS = ((1024, 1028), (2052, 2064))
SEG = dict(z=(0, 256), xbc=(256, 1024), qn=(1024, 1280), kc=(1280, 1408), vc=(1408, 1536), ks=(1536, 1664),
           vs=(1664, 1792), kw=(1792, 1920), vw=(1920, 2048), dq=(2048, 2304), dk=(2304, 2560), dv=(2560, 2816),
           lx=(2816, 3072), lg=(3072, 3328), misc=(3328, 3456))
D_IN_PAD = 3456
GATE_LANE0 = 4


def _params(sem):
    return pltpu.CompilerParams(dimension_semantics=sem, vmem_limit_bytes=VMEM_LIMIT)


def _iota(shape, dim):
    return lax.broadcasted_iota(jnp.int32, shape, dim)


def _dot(a, b):
    return jnp.dot(a.astype(BF16), b.astype(BF16), preferred_element_type=F32)


def _dot_nt(a, b):
    return lax.dot_general(a.astype(BF16), b.astype(BF16), (((1,), (1,)), ((), ())), preferred_element_type=F32)


def _split3(x):
    h1 = x.astype(BF16)
    r1 = x - h1.astype(F32)
    h2 = r1.astype(BF16)
    h3 = (r1 - h2.astype(F32)).astype(BF16)
    return h1, h2, h3


def _dot_01(m01, x):
    return sum(jnp.dot(m01, part, preferred_element_type=F32) for part in _split3(x))


def _dot_01_r(x, m01):
    return sum(jnp.dot(part, m01, preferred_element_type=F32) for part in _split3(x))


def _dot_nt_01(m01, x):
    return sum(lax.dot_general(m01, part, (((1,), (1,)), ((), ())), preferred_element_type=F32)
               for part in _split3(x))


def _eye(n, m):
    return (_iota((n, m), 0) == _iota((n, m), 1)).astype(BF16)


def _softplus(x):
    return jnp.maximum(x, 0.0) + jnp.log1p(jnp.exp(-jnp.abs(x)))


def _head_bcast(cols, h_of_lane, nheads, lane0=0):
    out = jnp.zeros(h_of_lane.shape, F32)
    for h in range(nheads):
        out = jnp.where(h_of_lane == h, cols[:, lane0 + h:lane0 + h + 1], out)
    return out


def _online_update(s, mask, v_b, m_ref, l_ref, acc_ref, v_transposed=False):
    s = jnp.where(mask, s, NEG_INF)
    m_prev = m_ref[...]
    m_new = jnp.maximum(m_prev, jnp.max(s, axis=-1, keepdims=True))
    m_safe = jnp.where(m_new == NEG_INF, 0.0, m_new)
    alpha = jnp.exp2(m_prev - m_safe)
    p = jnp.exp2(s - m_safe)
    l_ref[...] = alpha * l_ref[...] + jnp.sum(p, axis=-1, keepdims=True)
    if v_transposed:
        pv = lax.dot_general(p.astype(BF16), v_b, (((1,), (1,)), ((), ())), preferred_element_type=F32)
    else:
        pv = jnp.dot(p.astype(BF16), v_b, preferred_element_type=F32)
    acc_ref[...] = alpha * acc_ref[...] + pv
    m_ref[...] = m_new


def _online_update_t(s, mask, vt_b, m_ref, l_ref, acc_ref):
    if mask is not None:
        s = jnp.where(mask, s, NEG_INF)
    m_prev = m_ref[...]
    m_new = jnp.maximum(m_prev, jnp.max(s, axis=0, keepdims=True))
    m_safe = jnp.where(m_new == NEG_INF, 0.0, m_new)
    alpha = jnp.exp2(m_prev - m_safe)
    p = jnp.exp2(s - m_safe)
    l_ref[...] = alpha * l_ref[...] + jnp.sum(p, axis=0, keepdims=True)
    acc_ref[...] = alpha * acc_ref[...] + jnp.dot(vt_b, p.astype(BF16), preferred_element_type=F32)
    m_ref[...] = m_new


def _causal_pairs(n_q, qt, kt):
    qi, kj = [], []
    for i in range(n_q):
        for j in range((i * qt + qt - 1) // kt + 1):
            qi.append(i)
            kj.append(j)
    return jnp.asarray(qi, jnp.int32), jnp.asarray(kj, jnp.int32)


def _init_stats(*triples):
    for m_, l_, a_ in triples:
        m_[...] = jnp.full(m_.shape, NEG_INF, F32)
        l_[...] = jnp.zeros(l_.shape, F32)
        a_[...] = jnp.zeros(a_.shape, F32)


def _nsa_expand(q):
    lo = _iota((q.shape[0], LANES), 1) < 64
    a, b = q[:, :LANES], q[:, LANES:]
    h0 = jnp.where(lo, a, 0.0)
    h1 = jnp.where(lo, pltpu.roll(a, 64, 1), 0.0)
    h2 = jnp.where(lo, 0.0, pltpu.roll(b, 64, 1))
    h3 = jnp.where(lo, 0.0, b)
    return jnp.concatenate([h0, h1, h2, h3], axis=0)


def _nsa_collect(o, n):
    lo = _iota((n, LANES), 1) < 64
    o0, o1, o2, o3 = o[0:n], o[n:2 * n], o[2 * n:3 * n], o[3 * n:4 * n]
    left = jnp.where(lo, o0, pltpu.roll(o1, 64, 1))
    right = jnp.where(lo, pltpu.roll(o2, 64, 1), o3)
    return jnp.concatenate([left, right], axis=1)


def _collect_t(o_t, n):
    return jnp.concatenate([o_t[0:64, 0:n], o_t[0:64, n:2 * n], o_t[64:128, 2 * n:3 * n], o_t[64:128, 3 * n:4 * n]],
                           axis=0)


def _layer_norm(v, g, b):
    mu = jnp.mean(v, axis=-1, keepdims=True)
    d = v - mu
    var = jnp.mean(d * d, axis=-1, keepdims=True)
    return d * lax.rsqrt(var + EPS) * g + b


def _rope128(v, c, sa, sb, half):
    return v * c + pltpu.roll(v, LANES - half, 1) * sa + pltpu.roll(v, half, 1) * sb


PROJ_OUT = ("z", "xbc", "qn", "qnr", "kc", "vc", "ks", "vs", "kw", "vw", "dq", "dk", "dv", "lx", "lg", "misc")
PROJ_OUT_T = ("kcT", "vcT", "ksT", "vsT", "kwT", "vwT", "dkT", "dvT")
PROJ_W = dict(z=256, xbc=768, qn=256, qnr=256, kc=128, vc=128, ks=128, vs=128, kw=128, vw=128, dq=256, dk=256,
              dv=256, lx=256, lg=256, misc=128)


def _proj_kernel(x_ref, w_ref, cn_ref, san_ref, sbn_ref, cd_ref, sad_ref, sbd_ref, *outs, flat_t):
    o = dict(zip(PROJ_OUT + PROJ_OUT_T, outs))
    xb = x_ref[...].astype(BF16)

    def seg(name):
        a, b = SEG[name]
        return jnp.dot(xb, w_ref[:, a:b], preferred_element_type=F32)

    def rope_n(v):
        return _rope128(v, cn_ref[...], san_ref[...], sbn_ref[...], 8)

    def rope_d(v):
        return _rope128(v, cd_ref[...], sad_ref[...], sbd_ref[...], 4)

    def put(name, v):
        o[name][...] = v
        if name + "T" in o:
            if flat_t:
                o[name + "T"][...] = v.T
            else:
                o[name + "T"][0] = v.T

    put("z", seg("z"))
    put("xbc", seg("xbc"))
    qn = seg("qn")
    put("qn", qn)
    put("qnr", jnp.concatenate([rope_n(qn[:, :LANES]), rope_n(qn[:, LANES:])], axis=1))
    put("kc", seg("kc"))
    put("vc", seg("vc"))
    put("ks", rope_n(seg("ks")))
    put("vs", seg("vs"))
    put("kw", rope_n(seg("kw")))
    put("vw", seg("vw"))
    dq = seg("dq")
    put("dq", jnp.concatenate([rope_d(dq[:, :LANES]), rope_d(dq[:, LANES:])], axis=1))
    dk = seg("dk")
    put("dk", jnp.concatenate([rope_d(dk[:, :LANES]), rope_d(dk[:, LANES:])], axis=1))
    put("dv", seg("dv"))
    put("lx", seg("lx"))
    put("lg", seg("lg"))
    put("misc", seg("misc"))


def _rope_tables(pos, dh, rows):
    rd = dh // 4
    half = rd // 2
    inv = ROPE_THETA ** (-jnp.arange(half, dtype=F32) / half)
    ang = pos.astype(F32)[:, None] * inv[None, :]
    cos, sin = jnp.cos(ang), jnp.sin(ang)
    n = pos.shape[0]
    zero_h = jnp.zeros((n, half), F32)
    zero_r = jnp.zeros((n, dh - rd), F32)
    c = jnp.concatenate([cos, cos, jnp.ones((n, dh - rd), F32)], 1)
    sa = jnp.concatenate([-sin, zero_h, zero_r], 1)
    sb = jnp.concatenate([zero_h, sin, zero_r], 1)
    reps = (max(rows // n, 1), LANES // dh)
    return tuple(jnp.tile(t, reps) for t in (c, sa, sb))


def _proj(x2d, w_in_p, pos, nb, sl, tm):
    t = x2d.shape[0]
    n_tab = max(sl // tm, 1)
    flat_t = sl < tm
    tabs = _rope_tables(pos, 64, tm) + _rope_tables(pos, 32, tm)
    tab_spec = pl.BlockSpec((tm, LANES), lambda i: (i % n_tab, 0))
    out_specs = [pl.BlockSpec((tm, PROJ_W[k]), lambda i: (i, 0)) for k in PROJ_OUT]
    out_shape = [jax.ShapeDtypeStruct((t, PROJ_W[k]), F32) for k in PROJ_OUT]
    for k in PROJ_OUT_T:
        c = PROJ_W[k[:-1]]
        if flat_t:
            out_specs.append(pl.BlockSpec((c, tm), lambda i: (0, i)))
            out_shape.append(jax.ShapeDtypeStruct((c, t), F32))
        else:
            out_specs.append(pl.BlockSpec((1, c, tm), lambda i: (i // n_tab, 0, i % n_tab)))
            out_shape.append(jax.ShapeDtypeStruct((nb, c, sl), F32))
    outs = pl.pallas_call(
        functools.partial(_proj_kernel, flat_t=flat_t),
        grid=(t // tm,),
        in_specs=[pl.BlockSpec((tm, D_MODEL), lambda i: (i, 0)),
                  pl.BlockSpec((D_MODEL, D_IN_PAD), lambda i: (0, 0))] + [tab_spec] * 6,
        out_specs=out_specs, out_shape=out_shape,
        compiler_params=_params(("parallel",)),
        name="proj",
    )(x2d, w_in_p, *tabs)
    pj = dict(zip(PROJ_OUT + PROJ_OUT_T, outs))
    if flat_t:
        for k in PROJ_OUT_T:
            pj[k] = pj[k].reshape(-1, nb, sl).transpose(1, 0, 2)
    return pj


def _conv_chunk(x_ref, prev_ref, cw_ref, cb_ref, xbuf, q, first):
    @pl.when(first)
    def _():
        xbuf[0:8, :] = prev_ref[0]

    xbuf[8:8 + q, :] = x_ref[0]
    acc = cb_ref[...] + cw_ref[0:1, :] * xbuf[pl.ds(5, q), :]
    for j in range(1, 4):
        acc = acc + cw_ref[j:j + 1, :] * xbuf[pl.ds(5 + j, q), :]
    xbuf[0:8, :] = xbuf[q:q + 8, :]
    return acc


def _ssd_kernel(xbc_ref, z_ref, misc_ref, prev_ref, h0_ref, cw_ref, cb_ref, dtb_ref, alog_ref, dexp_ref, nw_ref,
                y_ref, hout_ref, xbuf, ht, *, q):
    c = pl.program_id(1)

    @pl.when(c == 0)
    def _():
        ht[...] = _dot_nt_01(_eye(SSD_STATE, SSD_STATE), h0_ref[0])

    acc = _conv_chunk(xbc_ref, prev_ref, cw_ref, cb_ref, xbuf, q, c == 0)
    xc = acc * jax.nn.sigmoid(acc)
    sx = xc[:, 0:256]
    sb = (xc[:, 256:384], xc[:, 384:512])
    sc = (xc[:, 512:640], xc[:, 640:768])
    dt = _softplus(misc_ref[0] + dtb_ref[...])
    da = dt * (-jnp.exp(alog_ref[...]))
    causal = _iota((q, q), 0) >= _iota((q, q), 1)
    acum = _dot_01(causal.astype(BF16), da)
    xsel = _eye(8, LANES)
    acum_t = _dot_nt_01(xsel, acum)
    dt_t = _dot_nt_01(xsel, dt)
    a_last = acum[q - 1:q, :]
    wlast = jnp.exp(a_last - acum) * dt
    ea = jnp.exp(acum)
    head = _iota((q, GROUP_W), 1) >> 6
    ht_old = ht[...]
    y = dexp_ref[...] * sx
    for g in range(2):
        cb = _dot_nt(sc[g], sb[g])
        for h in (2 * g, 2 * g + 1):
            seg = acum[:, h:h + 1] - acum_t[h:h + 1, :]
            decay = jnp.exp(jnp.where(causal, seg, NEG_INF))
            m = cb * decay * dt_t[h:h + 1, :]
            y_h = _dot(m, sx) + _dot(sc[g] * ea[:, h:h + 1], ht_old)
            y = y + jnp.where(head == h, y_h, 0.0)
    xw = sx * _head_bcast(wlast, head, SSD_HEADS)
    lane = _iota((q, GROUP_W), 1)
    bt0 = _dot_nt(_eye(SSD_STATE, SSD_STATE), sb[0])
    bt1 = _dot_nt(_eye(SSD_STATE, SSD_STATE), sb[1])
    head1 = _iota((1, GROUP_W), 1) >> 6
    dch = _head_bcast(jnp.exp(a_last), head1, SSD_HEADS)
    ht_new = dch * ht_old + _dot(bt0, jnp.where(lane < 128, xw, 0.0)) + _dot(bt1, jnp.where(lane < 128, 0.0, xw))
    ht[...] = ht_new
    zz = z_ref[0]
    y = y * (zz * jax.nn.sigmoid(zz))
    y_ref[0] = y * lax.rsqrt(jnp.mean(y * y, axis=-1, keepdims=True) + EPS) * nw_ref[...]

    @pl.when(c == pl.num_programs(1) - 1)
    def _():
        hout_ref[0] = _dot_nt_01(_eye(GROUP_W, GROUP_W), ht_new)


def _ssd(pj, nb, sl, prev8, h0, p):
    q = min(128, sl)
    nc = sl // q
    row = lambda w: pl.BlockSpec((1, q, w), lambda b, c: (b, c, 0))
    per_b = lambda r, w: pl.BlockSpec((1, r, w), lambda b, c: (b, 0, 0))
    const = lambda r, w: pl.BlockSpec((r, w), lambda b, c: (0, 0))
    y, hout = pl.pallas_call(
        functools.partial(_ssd_kernel, q=q),
        grid=(nb, nc),
        in_specs=[row(768), row(256), row(128), per_b(8, 768), per_b(256, 128),
                  const(8, 768), const(1, 768), const(1, 128), const(1, 128), const(1, 256), const(1, 256)],
        out_specs=[row(256), per_b(256, 128)],
        out_shape=[jax.ShapeDtypeStruct((nb, sl, 256), F32), jax.ShapeDtypeStruct((nb, 256, 128), F32)],
        scratch_shapes=[pltpu.VMEM((q + 8, 768), F32), pltpu.VMEM((SSD_STATE, GROUP_W), F32)],
        compiler_params=_params(("parallel", "arbitrary")),
        name="ssd",
    )(pj["xbc"].reshape(nb, sl, 768), pj["z"].reshape(nb, sl, 256), pj["misc"].reshape(nb, sl, 128), prev8, h0,
      p["ssd_cw"], p["ssd_cb"], p["ssd_dtb"], p["ssd_alog"], p["ssd_dexp"], p["ssd_nw"])
    return y.reshape(nb * sl, 256), hout


def _lru_kernel(x_ref, g_ref, prev_ref, h0_ref, cw_ref, cb_ref, wa_ref, ba_ref, wx_ref, bx_ref, lam_ref,
                y_ref, hout_ref, xbuf, hc, *, q):
    c = pl.program_id(1)

    @pl.when(c == 0)
    def _():
        hc[...] = h0_ref[0]

    xc = _conv_chunk(x_ref, prev_ref, cw_ref, cb_ref, xbuf, q, c == 0)
    xcb = xc.astype(BF16)
    r = jax.nn.sigmoid(jnp.dot(xcb, wa_ref[...], preferred_element_type=F32) + ba_ref[...])
    i = jax.nn.sigmoid(jnp.dot(xcb, wx_ref[...], preferred_element_type=F32) + bx_ref[...])
    log_a = -LRU_C * r * _softplus(-lam_ref[...])
    a = jnp.exp(log_a)
    u = jnp.sqrt(1.0 - jnp.exp(2.0 * log_a)) * (i * xc)
    row = _iota((q, GROUP_W), 0)
    s = 1
    while s < q:
        a_sh = pltpu.roll(a, s, 0)
        u_sh = pltpu.roll(u, s, 0)
        keep = row >= s
        u = jnp.where(keep, a * u_sh + u, u)
        a = jnp.where(keep, a * a_sh, a)
        s *= 2
    h = u + a * hc[...]
    hc[...] = h[q - 1:q, :]
    gg = g_ref[0]
    gelu = 0.5 * gg * (1.0 + jnp.tanh(math.sqrt(2.0 / math.pi) * (gg + 0.044715 * (gg * gg * gg))))
    y_ref[0] = h * gelu

    @pl.when(c == pl.num_programs(1) - 1)
    def _():
        hout_ref[0] = h[q - 1:q, :]


def _lru(pj, nb, sl, prev8, h0, p):
    q = min(256, sl)
    nc = sl // q
    row = lambda w: pl.BlockSpec((1, q, w), lambda b, c: (b, c, 0))
    per_b = lambda r, w: pl.BlockSpec((1, r, w), lambda b, c: (b, 0, 0))
    const = lambda r, w: pl.BlockSpec((r, w), lambda b, c: (0, 0))
    y, hout = pl.pallas_call(
        functools.partial(_lru_kernel, q=q),
        grid=(nb, nc),
        in_specs=[row(256), row(256), per_b(8, 256), per_b(1, 256), const(8, 256), const(1, 256),
                  const(256, 256), const(1, 256), const(256, 256), const(1, 256), const(1, 256)],
        out_specs=[row(256), per_b(1, 256)],
        out_shape=[jax.ShapeDtypeStruct((nb, sl, 256), F32), jax.ShapeDtypeStruct((nb, 1, 256), F32)],
        scratch_shapes=[pltpu.VMEM((q + 8, 256), F32), pltpu.VMEM((1, 256), F32)],
        compiler_params=_params(("parallel", "arbitrary")),
        name="lru",
    )(pj["lx"].reshape(nb, sl, 256), pj["lg"].reshape(nb, sl, 256), prev8, h0.reshape(nb, 1, 256),
      p["lru_cw"], p["lru_cb"], p["lru_wa"], p["lru_ba"], p["lru_wx"], p["lru_bx"], p["lru_lam"])
    return y.reshape(nb * sl, 256), hout.reshape(nb, 256)


def _summ_kernel(pt_ref, *refs, n_parts):
    kp, vp = refs[:n_parts], refs[n_parts:2 * n_parts]
    mk_ref, mv_ref, ko_ref, vo_ref = refs[2 * n_parts:]

    def one(parts, m_ref, o_ref):
        chunks = []
        for r in parts:
            x = r[...]
            chunks.append(x.reshape(x.shape[-2], x.shape[-1]).astype(BF16))
        x = jnp.concatenate(chunks, axis=1) if len(chunks) > 1 else chunks[0]
        top = jnp.dot(x[0:64], m_ref[0], preferred_element_type=F32)
        bot = jnp.dot(x[64:128], m_ref[1], preferred_element_type=F32)
        o_ref[0] = jnp.concatenate([top, bot], axis=0)

    one(kp, mk_ref, ko_ref)
    one(vp, mv_ref, vo_ref)


def _summarize(k_src, v_src, part_specs, part_w, nb, total, table, w_k, w_v):
    n_parts = len(part_specs)
    step_w = n_parts * part_w
    n_blk = step_w // NSA_CMP_BLOCK
    tok = jnp.arange(step_w, dtype=jnp.int32)
    in_block = tok[:, None] // NSA_CMP_BLOCK == jnp.arange(n_blk, dtype=jnp.int32)[None, :]
    weights = lambda w: jnp.where(in_block[None], w[:, tok % NSA_CMP_BLOCK][:, :, None], 0.0).astype(BF16)
    wspec = pl.BlockSpec((2, step_w, n_blk), lambda b, s, pt: (0, 0, 0))
    ospec = pl.BlockSpec((1, LANES, n_blk), lambda b, s, pt: (b, 0, s))
    oshape = jax.ShapeDtypeStruct((nb, LANES, total // NSA_CMP_BLOCK), F32)
    return pl.pallas_call(
        functools.partial(_summ_kernel, n_parts=n_parts),
        grid_spec=pltpu.PrefetchScalarGridSpec(
            num_scalar_prefetch=1, grid=(nb, total // step_w),
            in_specs=list(part_specs) * 2 + [wspec, wspec],
            out_specs=[ospec, ospec]),
        out_shape=[oshape, oshape],
        compiler_params=_params(("parallel", "arbitrary")),
        name="nsa_summarize",
    )(table, *([k_src] * n_parts), *([v_src] * n_parts), weights(w_k), weights(w_v))


def _cmpsel_t_kernel(q_ref, kct_ref, vct_ref, ocmp_ref, sel_ref, *, qt, nc, ns):
    qi = pl.program_id(1)
    q4 = _nsa_expand(q_ref[0] * 0.125)
    s = _dot_nt(kct_ref[0].T, q4)
    qpos = qi * qt + (_iota((nc, 4 * qt), 1) & (qt - 1))
    blk_end = (_iota((nc, 4 * qt), 0) + 1) * NSA_CMP_BLOCK - 1
    s = jnp.where(blk_end <= qpos, s, NEG_INF)
    m = jnp.max(s, axis=0, keepdims=True)
    m = jnp.where(m == NEG_INF, 0.0, m)
    e = jnp.exp(s - m)
    p = e / jnp.maximum(jnp.sum(e, axis=0, keepdims=True), 1e-30)
    ocmp_ref[0] = _collect_t(_dot(vct_ref[0], p), qt).T
    impc = jnp.concatenate([p[:, 0:qt] + p[:, qt:2 * qt], p[:, 2 * qt:3 * qt] + p[:, 3 * qt:4 * qt]], axis=1)
    pair = (_iota((ns, nc), 1) >> 1 == _iota((ns, nc), 0)).astype(BF16)
    imp = _dot_01(pair, impc)
    blk = _iota((ns, 2 * qt), 0)
    cur = (qi * qt + (_iota((ns, 2 * qt), 1) & (qt - 1))) >> 6
    forced = (blk == 0) | (blk == cur) | (blk == cur - 1)
    score = jnp.where(forced, FORCE_SCORE, jnp.where(blk > cur, -1.0, imp))
    cnt = jnp.zeros((ns, 2 * qt), F32)
    for j in range(ns):
        row = score[j:j + 1, :]
        cnt = cnt + ((row > score) | ((row == score) & (blk > j))).astype(F32)
    sel = (cnt < float(NSA_TOPN)).astype(F32)
    sel_ref[0, 0] = sel[:, 0:qt]
    sel_ref[0, 1] = sel[:, qt:2 * qt]


def _cmp_select_prompt(qn, kcmp_t, vcmp_t, nb, sl, qt):
    nc = kcmp_t.shape[2]
    ns = sl // NSA_SLC_BLOCK
    return pl.pallas_call(
        functools.partial(_cmpsel_t_kernel, qt=qt, nc=nc, ns=ns),
        grid=(nb, sl // qt),
        in_specs=[pl.BlockSpec((1, qt, 256), lambda b, i: (b, i, 0)),
                  pl.BlockSpec((1, LANES, nc), lambda b, i: (b, 0, 0)),
                  pl.BlockSpec((1, LANES, nc), lambda b, i: (b, 0, 0))],
        out_specs=[pl.BlockSpec((1, qt, 256), lambda b, i: (b, i, 0)),
                   pl.BlockSpec((1, 2, ns, qt), lambda b, i: (b, 0, 0, i))],
        out_shape=[jax.ShapeDtypeStruct((nb, sl, 256), F32), jax.ShapeDtypeStruct((nb, 2, ns, sl), F32)],
        compiler_params=_params(("parallel", "parallel")),
        name="nsa_cmp_select_prompt",
    )(qn.reshape(nb, sl, 256), kcmp_t, vcmp_t)


def _cmpsel_kernel(q_ref, kct_ref, vct_ref, ocmp_ref, selx_ref, *, qt, nc, ns, nsp, past, n_chunks):
    q4 = _nsa_expand(q_ref[0] * 0.125)
    s = _dot(q4, kct_ref[0])
    qpos = past + (_iota((4 * qt, nc), 0) & (qt - 1))
    blk_end = (_iota((4 * qt, nc), 1) + 1) * NSA_CMP_BLOCK - 1
    s = jnp.where(blk_end <= qpos, s, NEG_INF)
    m = jnp.max(s, axis=-1, keepdims=True)
    m = jnp.where(m == NEG_INF, 0.0, m)
    e = jnp.exp(s - m)
    p = e / jnp.maximum(jnp.sum(e, axis=-1, keepdims=True), 1e-30)
    ocmp_ref[0] = _nsa_collect(_dot_nt(p, vct_ref[0]), qt)
    impc = jnp.concatenate([p[0:qt] + p[qt:2 * qt], p[2 * qt:3 * qt] + p[3 * qt:4 * qt]], axis=0)
    pair = (_iota((nc, nsp), 0) >> 1 == _iota((nc, nsp), 1)).astype(BF16)
    imp = _dot_01_r(impc, pair)
    blk = _iota((2 * qt, nsp), 1)
    cur = (past + (_iota((2 * qt, nsp), 0) & (qt - 1))) >> 6
    forced = (blk == 0) | (blk == cur) | (blk == cur - 1)
    score = jnp.where(forced, FORCE_SCORE, jnp.where(blk > cur, -1.0, imp))
    score = jnp.where(blk < ns, score, -3.0)

    cnt = jnp.zeros((2 * qt, nsp), F32)
    for j in range(ns):
        col = score[:, j:j + 1]
        cnt = cnt + ((col > score) | ((col == score) & (blk > j))).astype(F32)
    sel = ((cnt < float(NSA_TOPN)) & (blk < ns)).astype(F32)
    sel4 = jnp.concatenate([sel[0:qt], sel[0:qt], sel[qt:2 * qt], sel[qt:2 * qt]], axis=0).astype(BF16)
    e16 = (_iota((LANES, 1024), 1) >> 6 == _iota((LANES, 1024), 0)).astype(BF16)
    for t in range(n_chunks):
        pick = ((_iota((nsp, LANES), 0) == 16 * t + _iota((nsp, LANES), 1))
                & (_iota((nsp, LANES), 1) < 16)).astype(BF16)
        blocks = jnp.dot(sel4, pick, preferred_element_type=F32).astype(BF16)
        selx_ref[0, :, 1024 * t:1024 * (t + 1)] = jnp.dot(blocks, e16, preferred_element_type=F32)


def _cmp_select_sample(qn, kcmp_t, vcmp_t, nb, sl, past):
    nc = kcmp_t.shape[2]
    ns = -(-(past + sl) // NSA_SLC_BLOCK)
    nsp = -(-ns // LANES) * LANES
    n_chunks = -(-(past + sl) // 1024)
    return pl.pallas_call(
        functools.partial(_cmpsel_kernel, qt=sl, nc=nc, ns=ns, nsp=nsp, past=past, n_chunks=n_chunks),
        grid=(nb,),
        in_specs=[pl.BlockSpec((1, sl, 256), lambda b: (b, 0, 0)),
                  pl.BlockSpec((1, LANES, nc), lambda b: (b, 0, 0)),
                  pl.BlockSpec((1, LANES, nc), lambda b: (b, 0, 0))],
        out_specs=[pl.BlockSpec((1, sl, 256), lambda b: (b, 0, 0)),
                   pl.BlockSpec((1, 4 * sl, 1024 * n_chunks), lambda b: (b, 0, 0))],
        out_shape=[jax.ShapeDtypeStruct((nb, sl, 256), F32),
                   jax.ShapeDtypeStruct((nb, 4 * sl, 1024 * n_chunks), F32)],
        compiler_params=_params(("parallel",)),
        name="nsa_cmp_select_sample",
    )(qn.reshape(nb, sl, 256), kcmp_t, vcmp_t)


def _nsap_kernel(qi_ref, kj_ref, q_ref, ks_ref, vst_ref, kw_ref, vwt_ref, sel_ref, oslc_ref, owin_ref,
                 q4_s, m1, l1, a1, m2, l2, a2, *, qt, kt):
    step = pl.program_id(1)
    qi, kj = qi_ref[step], kj_ref[step]
    last = (qi * qt + qt - 1) // kt
    first_w = jnp.maximum(qi * qt - (NSA_WINDOW - 1), 0) // kt

    @pl.when(kj == 0)
    def _():
        q4_s[...] = _nsa_expand(q_ref[0] * NSA_QSCALE).astype(BF16)
        _init_stats((m1, l1, a1), (m2, l2, a2))

    def positions():
        return kj * kt + _iota((kt, qt), 0), qi * qt + _iota((kt, qt), 1)

    def picked(g):
        rows = []
        for u in range(kt // NSA_SLC_BLOCK):
            r = sel_ref[0, g, pl.ds(kj * (kt // NSA_SLC_BLOCK) + u, 1), :]
            rows.append(jnp.broadcast_to(r, (NSA_SLC_BLOCK, qt)))
        return jnp.concatenate(rows, axis=0) > 0.5

    def update(k_ref, vt_ref, masks, m_, l_, a_):
        k_b, vt_b = k_ref[0].astype(BF16), vt_ref[0].astype(BF16)
        mask = None
        if masks[0] is not None:
            mask = jnp.concatenate([masks[0], masks[0], masks[1], masks[1]], axis=1)
        half = kt // 2
        for part in (slice(0, half), slice(half, kt)):
            s = lax.dot_general(k_b[part], q4_s[...], (((1,), (1,)), ((), ())), preferred_element_type=F32)
            _online_update_t(s, None if mask is None else mask[part], vt_b[:, part], m_, l_, a_)

    @pl.when(kj < last)
    def _():
        update(ks_ref, vst_ref, (picked(0), picked(1)), m1, l1, a1)

    @pl.when(kj == last)
    def _():
        kpos, qpos = positions()
        causal = kpos <= qpos
        update(ks_ref, vst_ref, (picked(0) & causal, picked(1) & causal), m1, l1, a1)

    full_w = (kj < last) & (kj * kt > qi * qt + qt - 1 - NSA_WINDOW)

    @pl.when((kj >= first_w) & full_w)
    def _():
        update(kw_ref, vwt_ref, (None, None), m2, l2, a2)

    @pl.when((kj >= first_w) & jnp.logical_not(full_w))
    def _():
        kpos, qpos = positions()
        band = (kpos <= qpos) & (kpos > qpos - NSA_WINDOW)
        update(kw_ref, vwt_ref, (band, band), m2, l2, a2)

    @pl.when(kj == last)
    def _():
        oslc_ref[0] = _collect_t(a1[...] / jnp.maximum(l1[...], 1e-30), qt).T
        owin_ref[0] = _collect_t(a2[...] / jnp.maximum(l2[...], 1e-30), qt).T


def _nsa_prompt_attn(pj, sel, nb, sl):
    qt, kt = min(256, sl), min(512, sl)
    assert kt % qt == 0
    qi_tab, kj_tab = _causal_pairs(sl // qt, qt, kt)
    first_w = lambda i: jnp.maximum(i * qt - (NSA_WINDOW - 1), 0) // kt
    kspec = pl.BlockSpec((1, kt, LANES), lambda b, s, qi, kj: (b, kj[s], 0))
    vspec = pl.BlockSpec((1, LANES, kt), lambda b, s, qi, kj: (b, 0, kj[s]))
    kwspec = pl.BlockSpec((1, kt, LANES), lambda b, s, qi, kj: (b, jnp.maximum(kj[s], first_w(qi[s])), 0))
    vwspec = pl.BlockSpec((1, LANES, kt), lambda b, s, qi, kj: (b, 0, jnp.maximum(kj[s], first_w(qi[s]))))
    qspec = pl.BlockSpec((1, qt, 256), lambda b, s, qi, kj: (b, qi[s], 0))
    ns = sel.shape[2]
    r3 = lambda a, w: a.reshape(nb, sl, w)
    stats = [pltpu.VMEM((1, 4 * qt), F32), pltpu.VMEM((1, 4 * qt), F32), pltpu.VMEM((LANES, 4 * qt), F32)]
    return pl.pallas_call(
        functools.partial(_nsap_kernel, qt=qt, kt=kt),
        grid_spec=pltpu.PrefetchScalarGridSpec(
            num_scalar_prefetch=2, grid=(nb, qi_tab.shape[0]),
            in_specs=[qspec, kspec, vspec, kwspec, vwspec,
                      pl.BlockSpec((1, 2, ns, qt), lambda b, s, qi, kj: (b, 0, 0, qi[s]))],
            out_specs=[qspec, qspec],
            scratch_shapes=[pltpu.VMEM((4 * qt, LANES), BF16)] + stats * 2),
        out_shape=[jax.ShapeDtypeStruct((nb, sl, 256), F32)] * 2,
        compiler_params=_params(("parallel", "arbitrary")),
        name="nsa_prompt_attn",
    )(qi_tab, kj_tab, r3(pj["qnr"], 256), r3(pj["ks"], 128), pj["vsT"], r3(pj["kw"], 128), pj["vwT"], sel)


def _nsa_gate(misc, oc, os_, ow):
    g = jax.nn.sigmoid(misc)
    head = _iota(oc.shape, 1) >> 6
    return (_head_bcast(g, head, 4, GATE_LANE0) * oc + _head_bcast(g, head, 4, GATE_LANE0 + 4) * os_
            + _head_bcast(g, head, 4, GATE_LANE0 + 8) * ow)


def _diff_lambda(lamv_ref, lam_init):
    lv = lamv_ref[...]
    s1 = jnp.sum(lv[0:1] * lv[1:2], axis=1, keepdims=True)
    s2 = jnp.sum(lv[2:3] * lv[3:4], axis=1, keepdims=True)
    return jnp.exp(s1) - jnp.exp(s2) + lam_init


def _diff_finish(o, head, nheads, nw, lam_init):
    inv = jnp.zeros(o.shape, F32)
    for h in range(nheads):
        ms = jnp.sum(jnp.where(head == h, o * o, 0.0), axis=-1, keepdims=True) * (1.0 / 64.0)
        inv = jnp.where(head == h, lax.rsqrt(ms + EPS), inv)
    return o * inv * nw * (1.0 - lam_init)


def _diffp_kernel(qi_ref, kj_ref, q_ref, k_ref, vt_ref, lamv_ref, nwt_ref, o_ref, q4_s, m_s, l_s, acc_s,
                  *, qt, kt, lam_init):
    step = pl.program_id(2)
    qi, kj = qi_ref[step], kj_ref[step]
    last = (qi * qt + qt - 1) // kt

    @pl.when(kj == 0)
    def _():
        q = q_ref[0] * DIFF_QSCALE
        part = _iota((qt, LANES), 1) >> 5
        q4_s[...] = jnp.concatenate([jnp.where(part == k, q, 0.0) for k in range(4)], axis=0).astype(BF16)
        _init_stats((m_s, l_s, acc_s))

    def update(mask):
        k_b, vt_b = k_ref[0].astype(BF16), vt_ref[0].astype(BF16)
        if mask is not None:
            mask = jnp.concatenate([mask] * 4, axis=1)
        half = kt // 2
        s_a = lax.dot_general(k_b[:half], q4_s[...], (((1,), (1,)), ((), ())), preferred_element_type=F32)
        s_b = lax.dot_general(k_b[half:], q4_s[...], (((1,), (1,)), ((), ())), preferred_element_type=F32)
        _online_update_t(s_a, None if mask is None else mask[:half], vt_b[:, :half], m_s, l_s, acc_s)
        _online_update_t(s_b, None if mask is None else mask[half:], vt_b[:, half:], m_s, l_s, acc_s)

    @pl.when(kj < last)
    def _():
        update(None)

    @pl.when(kj == last)
    def _():
        update(kj * kt + _iota((kt, qt), 0) <= qi * qt + _iota((kt, qt), 1))
        lam = _diff_lambda(lamv_ref, lam_init)
        o = acc_s[...] / jnp.maximum(l_s[...], 1e-30)
        halves = []
        for h in range(2):
            rows = slice(64 * h, 64 * (h + 1))
            oh = o[rows, 2 * h * qt:(2 * h + 1) * qt] - lam * o[rows, (2 * h + 1) * qt:(2 * h + 2) * qt]
            ms = jnp.mean(oh * oh, axis=0, keepdims=True)
            halves.append(oh * lax.rsqrt(ms + EPS))
        o_ref[0] = (jnp.concatenate(halves, axis=0) * nwt_ref[...] * (1.0 - lam_init)).T


def _diff_prompt(pj, nb, sl, p, lam_init):
    qt, kt = min(256, sl), min(512, sl)
    assert kt % qt == 0
    qi_tab, kj_tab = _causal_pairs(sl // qt, qt, kt)
    qspec = pl.BlockSpec((1, qt, LANES), lambda b, h, s, qi, kj: (b, qi[s], h))
    kspec = pl.BlockSpec((1, kt, LANES), lambda b, h, s, qi, kj: (b, kj[s], h))
    vspec = pl.BlockSpec((1, LANES, kt), lambda b, h, s, qi, kj: (b, h, kj[s]))
    r3 = lambda a: a.reshape(nb, sl, 256)
    out = pl.pallas_call(
        functools.partial(_diffp_kernel, qt=qt, kt=kt, lam_init=lam_init),
        grid_spec=pltpu.PrefetchScalarGridSpec(
            num_scalar_prefetch=2, grid=(nb, 2, qi_tab.shape[0]),
            in_specs=[qspec, kspec, vspec, pl.BlockSpec((4, 32), lambda b, h, s, qi, kj: (0, 0)),
                      pl.BlockSpec((LANES, 1), lambda b, h, s, qi, kj: (0, 0))],
            out_specs=qspec,
            scratch_shapes=[pltpu.VMEM((4 * qt, LANES), BF16), pltpu.VMEM((1, 4 * qt), F32),
                            pltpu.VMEM((1, 4 * qt), F32), pltpu.VMEM((LANES, 4 * qt), F32)]),
        out_shape=jax.ShapeDtypeStruct((nb, sl, 256), F32),
        compiler_params=_params(("parallel", "parallel", "arbitrary")),
        name="diff_prompt",
    )(qi_tab, kj_tab, r3(pj["dq"]), r3(pj["dk"]), pj["dvT"], p["diff_lambda"],
      p["diff_nw256"][:, :LANES].reshape(LANES, 1))
    return out.reshape(nb * sl, 256)


def _paged_kernel(pt_ref, q_ref, *refs, mode, pps, width, kbase, past, new_len, lam_init):
    kp, vp = refs[:pps], refs[pps:2 * pps]
    knew_ref, vnew_ref = refs[2 * pps:2 * pps + 2]
    rest = list(refs[2 * pps + 2:])
    selx_ref = selnew_ref = lamv_ref = nw_ref = None
    if mode == "slc":
        selx_ref, selnew_ref = rest[:2]
        rest = rest[2:]
    if mode == "diff":
        lamv_ref, nw_ref = rest[:2]
        rest = rest[2:]
    o_ref, qe_s, m_s, l_s, acc_s = rest
    nq = new_len
    rows = qe_s.shape[0]
    step = pl.program_id(1)

    @pl.when(step == 0)
    def _():
        if mode == "diff":
            q = q_ref[0] * DIFF_QSCALE
            part = _iota((nq, width), 1) >> 5
            qe = jnp.concatenate([jnp.where(part == k, q, 0.0) for k in range(8)], axis=0)
        else:
            qe = _nsa_expand(q_ref[0] * NSA_QSCALE)
        qe_s[...] = qe.astype(BF16)
        _init_stats((m_s, l_s, acc_s))

    nk = pps * PAGE
    kt_b = jnp.concatenate([r[0, 0].astype(BF16) for r in kp], axis=1)
    vt_b = jnp.concatenate([r[0, 0].astype(BF16) for r in vp], axis=1)
    s = jnp.dot(qe_s[...], kt_b, preferred_element_type=F32)
    qpos = past + (_iota((rows, nk), 0) & (nq - 1))
    kpos = kbase + step * nk + _iota((rows, nk), 1)
    mask = kpos <= qpos
    if mode == "win":
        mask = mask & (kpos > qpos - NSA_WINDOW)
    if mode == "slc":
        mask = mask & (selx_ref[0] > 0.5)
    _online_update(s, mask, vt_b, m_s, l_s, acc_s, v_transposed=True)

    @pl.when(step == pl.num_programs(1) - 1)
    def _():
        s2 = lax.dot_general(qe_s[...], knew_ref[0].astype(BF16), (((1,), (1,)), ((), ())),
                             preferred_element_type=F32)
        qpos2 = past + (_iota((rows, nq), 0) & (nq - 1))
        kpos2 = past + _iota((rows, nq), 1)
        mask2 = kpos2 <= qpos2
        if mode == "win":
            mask2 = mask2 & (kpos2 > qpos2 - NSA_WINDOW)
        if mode == "slc":
            mask2 = mask2 & (selnew_ref[0][:, 0:nq] > 0.5)
        _online_update(s2, mask2, vnew_ref[0].astype(BF16), m_s, l_s, acc_s)
        o = acc_s[...] / jnp.maximum(l_s[...], 1e-30)
        if mode == "diff":
            lam = _diff_lambda(lamv_ref, lam_init)
            head = _iota((nq, width), 1) >> 6
            out = jnp.zeros((nq, width), F32)
            for h in range(4):
                o0 = o[(2 * h) * nq:(2 * h + 1) * nq]
                o1 = o[(2 * h + 1) * nq:(2 * h + 2) * nq]
                out = jnp.where(head == h, o0 - lam * o1, out)
            o_ref[0] = _diff_finish(out, head, 4, nw_ref[...], lam_init)
        else:
            o_ref[0] = _nsa_collect(o, nq)


def _paged_attn(mode, q, k_src, v_src, page_spec, n_pages, table, knew, vnew, kbase, past, extra=(), lam_init=0.0):
    nb, nq = q.shape[0], q.shape[1]
    width = knew.shape[-1]
    pps = min(PAGES_PER_STEP, n_pages)
    rows = 8 * nq if mode == "diff" else 4 * nq
    nk = pps * PAGE
    per_b = lambda r_, w: pl.BlockSpec((1, r_, w), lambda b, s, pt: (b, 0, 0))
    in_specs = [per_b(nq, 256)] + [page_spec(r, pps) for r in range(pps)] * 2 + [per_b(nq, width), per_b(nq, width)]
    args = [q] + [k_src] * pps + [v_src] * pps + [knew, vnew]
    if mode == "slc":
        (selx,) = extra
        new_blk = (past - kbase) // LANES
        in_specs += [pl.BlockSpec((1, rows, nk), lambda b, s, pt: (b, 0, s)),
                     pl.BlockSpec((1, rows, LANES), lambda b, s, pt: (b, 0, new_blk))]
        args += [selx, selx]
    if mode == "diff":
        in_specs += [pl.BlockSpec((4, 32), lambda b, s, pt: (0, 0)), pl.BlockSpec((1, 256), lambda b, s, pt: (0, 0))]
        args += list(extra)
    return pl.pallas_call(
        functools.partial(_paged_kernel, mode=mode, pps=pps, width=width, kbase=kbase, past=past, new_len=nq,
                          lam_init=lam_init),
        grid_spec=pltpu.PrefetchScalarGridSpec(
            num_scalar_prefetch=1, grid=(nb, n_pages // pps), in_specs=in_specs,
            out_specs=per_b(nq, 256),
            scratch_shapes=[pltpu.VMEM((rows, width), BF16), pltpu.VMEM((rows, 1), F32),
                            pltpu.VMEM((rows, 1), F32), pltpu.VMEM((rows, width), F32)]),
        out_shape=jax.ShapeDtypeStruct((nb, nq, 256), F32),
        compiler_params=_params(("parallel", "arbitrary")),
        name="paged_" + mode,
    )(table, *args)


def _post_kernel(x_ref, ya_ref, misc_ref, oc_ref, os_ref, ow_ref, yc_ref, yd_ref, wo_ref, g1_ref, b1_ref,
                 w1_ref, w2_ref, g2_ref, b2_ref, o_ref, x1_s, x1b_s, acc_s, *, alpha):
    f = pl.program_id(1)

    @pl.when(f == 0)
    def _():
        yb = _nsa_gate(misc_ref[...], oc_ref[...], os_ref[...], ow_ref[...])
        mix = _dot(ya_ref[...], wo_ref[0:256, :])
        mix = mix + _dot(yb, wo_ref[256:512, :])
        mix = mix + _dot(yc_ref[...], wo_ref[512:768, :])
        mix = mix + _dot(yd_ref[...], wo_ref[768:1024, :])
        x1 = _layer_norm(alpha * x_ref[...] + mix, g1_ref[...], b1_ref[...])
        x1_s[...] = x1
        x1b_s[...] = x1.astype(BF16)
        acc_s[...] = jnp.zeros(acc_s.shape, F32)

    h = jnp.dot(x1b_s[...], w1_ref[...], preferred_element_type=F32)
    h = jnp.square(jnp.maximum(h, 0.0))
    acc_s[...] += jnp.dot(h.astype(BF16), w2_ref[...], preferred_element_type=F32)

    @pl.when(f == pl.num_programs(1) - 1)
    def _():
        o_ref[...] = _layer_norm(alpha * x1_s[...] + acc_s[...], g2_ref[...], b2_ref[...])


def _post(x2d, ya, misc, oc, os_, ow, yc, yd, p, tm, alpha):
    t = x2d.shape[0]
    tf = 1024
    row = lambda w: pl.BlockSpec((tm, w), lambda i, f: (i, 0))
    const = lambda r, w: pl.BlockSpec((r, w), lambda i, f: (0, 0))
    return pl.pallas_call(
        functools.partial(_post_kernel, alpha=alpha),
        grid=(t // tm, D_FF // tf),
        in_specs=[row(D_MODEL), row(256), row(LANES), row(256), row(256), row(256), row(256), row(256),
                  const(D_MODEL, D_MODEL), const(1, D_MODEL), const(1, D_MODEL),
                  pl.BlockSpec((D_MODEL, tf), lambda i, f: (0, f)), pl.BlockSpec((tf, D_MODEL), lambda i, f: (f, 0)),
                  const(1, D_MODEL), const(1, D_MODEL)],
        out_specs=row(D_MODEL),
        out_shape=jax.ShapeDtypeStruct((t, D_MODEL), F32),
        scratch_shapes=[pltpu.VMEM((tm, D_MODEL), F32), pltpu.VMEM((tm, D_MODEL), BF16),
                        pltpu.VMEM((tm, D_MODEL), F32)],
        compiler_params=_params(("parallel", "arbitrary")),
        name="post",
    )(x2d, ya, misc, oc.reshape(t, 256), os_.reshape(t, 256), ow.reshape(t, 256), yc, yd,
      p["w_out"], p["ln1_g"], p["ln1_b"], p["w_ff1"], p["w_ff2"], p["ln2_g"], p["ln2_b"])


def _prep_params(w, l):
    w_in = w["w_in"][l]
    misc = jnp.concatenate([w_in[:, a:b] for a, b in MISC_COLS], axis=1)
    misc = jnp.pad(misc, ((0, 0), (0, LANES - misc.shape[1])))
    w_in_p = jnp.concatenate([w_in[:, a:b] for a, b in W_IN_ORDER] + [misc], axis=1).astype(BF16)
    row = lambda v: v.reshape(1, -1).astype(F32)
    pad_lanes = lambda v: jnp.pad(v.reshape(1, -1), ((0, 0), (0, LANES - v.shape[-1])))
    blockdiag = lambda m: jax.scipy.linalg.block_diag(*[m[i] for i in range(m.shape[0])]).astype(BF16)
    return dict(
        w_in=w_in_p,
        ssd_cw=jnp.pad(w["ssd_conv_w"][l], ((0, 4), (0, 0))), ssd_cb=row(w["ssd_conv_b"][l]),
        ssd_dtb=pad_lanes(w["ssd_dt_bias"][l]), ssd_alog=pad_lanes(w["ssd_A_log"][l]),
        ssd_dexp=row(jnp.repeat(w["ssd_D"][l], 64)), ssd_nw=row(w["ssd_norm_w"][l]),
        cmp_wk=w["nsa_w_cmp_k"][l], cmp_wv=w["nsa_w_cmp_v"][l],
        diff_lambda=w["diff_lambda"][l], diff_nw256=row(jnp.tile(w["diff_norm_w"][l], 4)),
        lru_cw=jnp.pad(w["lru_conv_w"][l], ((0, 4), (0, 0))), lru_cb=row(w["lru_conv_b"][l]),
        lru_wa=blockdiag(w["lru_w_a"][l]), lru_ba=row(w["lru_b_a"][l]),
        lru_wx=blockdiag(w["lru_w_x"][l]), lru_bx=row(w["lru_b_x"][l]), lru_lam=row(w["lru_lambda"][l]),
        w_out=w["w_out"][l].astype(BF16), ln1_g=row(w["ln1_g"][l]), ln1_b=row(w["ln1_b"][l]),
        w_ff1=w["w_ff1"][l].astype(BF16), w_ff2=w["w_ff2"][l].astype(BF16),
        ln2_g=row(w["ln2_g"][l]), ln2_b=row(w["ln2_b"][l]),
    )


def _pad_prev(prev):
    return jnp.pad(prev, ((0, 0), (5, 0), (0, 0)))


def _token_minor(a):
    a = jnp.moveaxis(a, -3, -1)
    return a.reshape(*a.shape[:-3], a.shape[-3] * a.shape[-2], a.shape[-1])


def _token_major(a_t, heads):
    nb, _, n = a_t.shape
    return jnp.moveaxis(a_t.reshape(nb, heads, 64, n), -1, 1)


def _layer(x2d, nb, sl, past, hist, p, lam_init, alpha):
    t = nb * sl
    tm = min(256, t)
    pos = past + jnp.arange(sl, dtype=jnp.int32)
    pj = _proj(x2d, p["w_in"], pos, nb, sl, tm)
    y_a, ssd_h = _ssd(pj, nb, sl, _pad_prev(hist["ssd_conv"]), hist["ssd_h"].reshape(nb, 256, 128), p)
    y_d, lru_h = _lru(pj, nb, sl, _pad_prev(hist["lru_conv"]), hist["lru_h"], p)
    r3 = lambda a, w: a.reshape(nb, sl, w)
    dummy = jnp.zeros((1, 1), jnp.int32)
    if past == 0:
        step_w = min(4096, sl)
        spec = pl.BlockSpec((1, LANES, step_w), lambda b, s, pt: (b, 0, s))
        kcmp_t, vcmp_t = _summarize(pj["kcT"], pj["vcT"], [spec], step_w, nb, sl, dummy, p["cmp_wk"], p["cmp_wv"])
        o_cmp, sel = _cmp_select_prompt(pj["qn"], kcmp_t, vcmp_t, nb, sl, min(256, sl))
        o_slc, o_win = _nsa_prompt_attn(pj, sel, nb, sl)
        y_c = _diff_prompt(pj, nb, sl, p, lam_init)
        kw_t, vw_t = pj["kwT"][:, :, sl - NSA_WINDOW:], pj["vwT"][:, :, sl - NSA_WINDOW:]
        new = dict(kc=_token_major(pj["kcT"], 2), vc=_token_major(pj["vcT"], 2), ks=_token_major(pj["ksT"], 2),
                   vs=_token_major(pj["vsT"], 2), dk=_token_major(pj["dkT"], 4), dv=_token_major(pj["dvT"], 4))
    else:
        table, l = hist["table"], hist["layer"]
        n_pages = table.shape[1]
        npp = min(PAGES_PER_STEP, n_pages)
        cache_spec = lambda r, per: pl.BlockSpec(
            (1, 1, LANES, PAGE), lambda b, s, pt, r=r, per=per: (l, pt[b, s * per + r], 0, 0))
        diff_spec = lambda r, per: pl.BlockSpec(
            (1, 1, 256, PAGE), lambda b, s, pt, r=r, per=per: (l, pt[b, s * per + r], 0, 0))
        win_spec = lambda r, per: pl.BlockSpec(
            (1, 1, LANES, PAGE), lambda b, s, pt, r=r, per=per: (l, b, 0, s * per + r))
        kcmp_t, vcmp_t = _summarize(hist["kc"], hist["vc"], [cache_spec(r, npp) for r in range(npp)], PAGE, nb,
                                    n_pages * PAGE, table, p["cmp_wk"], p["cmp_wv"])
        o_cmp, selx = _cmp_select_sample(pj["qn"], kcmp_t, vcmp_t, nb, sl, past)
        qnr = r3(pj["qnr"], 256)
        o_slc = _paged_attn("slc", qnr, hist["ks"], hist["vs"], cache_spec, n_pages, table,
                            r3(pj["ks"], 128), r3(pj["vs"], 128), 0, past, extra=(selx,))
        wbuf = hist["kw"].shape[-1]
        o_win = _paged_attn("win", qnr, hist["kw"], hist["vw"], win_spec, wbuf // PAGE, dummy,
                            r3(pj["kw"], 128), r3(pj["vw"], 128), past - wbuf, past)
        y_c = _paged_attn("diff", r3(pj["dq"], 256), hist["dk"], hist["dv"], diff_spec, n_pages, table,
                          r3(pj["dk"], 256), r3(pj["dv"], 256), 0, past,
                          extra=(p["diff_lambda"], p["diff_nw256"]), lam_init=lam_init).reshape(t, 256)
        kw_t = jnp.concatenate([hist["kw"][l], pj["kwT"]], axis=2)[:, :, -NSA_WINDOW:]
        vw_t = jnp.concatenate([hist["vw"][l], pj["vwT"]], axis=2)[:, :, -NSA_WINDOW:]
        kv = lambda a: r3(a, 128).reshape(nb, sl, 2, 64)
        new = dict(kc=kv(pj["kc"]), vc=kv(pj["vc"]), ks=kv(pj["ks"]), vs=kv(pj["vs"]),
                   dk=r3(pj["dk"], 256).reshape(nb, sl, 4, 64), dv=r3(pj["dv"], 256).reshape(nb, sl, 4, 64))
    x_out = _post(x2d, y_a, pj["misc"], o_cmp, o_slc, o_win, y_c, y_d, p, min(512, t), alpha)
    new.update(kw=_token_major(kw_t, 2), vw=_token_major(vw_t, 2),
               ssd_h=ssd_h.reshape(nb, 4, 64, 128), ssd_conv=r3(pj["xbc"], 768)[:, sl - 3:],
               lru_h=lru_h, lru_conv=r3(pj["lx"], 256)[:, sl - 3:])
    return x_out, new


STATE_ORDER = ("kc", "vc", "ks", "vs", "dk", "dv", "kw", "vw", "ssd_h", "ssd_conv", "lru_h", "lru_conv")


def kernel(x_prompt, x_sample, cache_nsa_k_cmp, cache_nsa_v_cmp, cache_nsa_k_slc, cache_nsa_v_slc, cache_diff_k, cache_diff_v, cache_nsa_k_win, cache_nsa_v_win, state_ssd, state_ssd_conv, state_lru, state_lru_conv, page_table, w_in, ssd_conv_w, ssd_conv_b, ssd_dt_bias, ssd_A_log, ssd_D, ssd_norm_w, nsa_w_cmp_k, nsa_w_cmp_v, diff_lambda, diff_norm_w, lru_conv_w, lru_conv_b, lru_w_a, lru_b_a, lru_w_x, lru_b_x, lru_lambda, w_out, ln1_g, ln1_b, w_ff1, w_ff2, ln2_g, ln2_b):
    weights = dict(w_in=w_in, ssd_conv_w=ssd_conv_w, ssd_conv_b=ssd_conv_b, ssd_dt_bias=ssd_dt_bias,
                   ssd_A_log=ssd_A_log, ssd_D=ssd_D, ssd_norm_w=ssd_norm_w, nsa_w_cmp_k=nsa_w_cmp_k,
                   nsa_w_cmp_v=nsa_w_cmp_v, diff_lambda=diff_lambda, diff_norm_w=diff_norm_w, lru_conv_w=lru_conv_w,
                   lru_conv_b=lru_conv_b, lru_w_a=lru_w_a, lru_b_a=lru_b_a, lru_w_x=lru_w_x, lru_b_x=lru_b_x,
                   lru_lambda=lru_lambda, w_out=w_out, ln1_g=ln1_g, ln1_b=ln1_b, w_ff1=w_ff1, w_ff2=w_ff2,
                   ln2_g=ln2_g, ln2_b=ln2_b)
    depth = w_in.shape[0]
    nbp, slp, _ = x_prompt.shape
    nbs, sls, _ = x_sample.shape
    past = page_table.shape[1] * cache_nsa_k_cmp.shape[2]
    assert past % PAGE == 0 and sls < NSA_CMP_BLOCK and slp % 256 == 0
    alpha = (2 * depth) ** 0.25
    caches = dict(kc=_token_minor(cache_nsa_k_cmp), vc=_token_minor(cache_nsa_v_cmp), ks=_token_minor(cache_nsa_k_slc),
                  vs=_token_minor(cache_nsa_v_slc), dk=_token_minor(cache_diff_k), dv=_token_minor(cache_diff_v),
                  kw=_token_minor(cache_nsa_k_win), vw=_token_minor(cache_nsa_v_win))
    xp = x_prompt.reshape(nbp * slp, D_MODEL)
    xs = x_sample.reshape(nbs * sls, D_MODEL)
    outs_p, outs_s = [], []
    for l in range(depth):
        p = _prep_params(weights, l)
        lam_init = 0.8 - 0.6 * math.exp(-0.3 * l)
        hist_p = dict(ssd_conv=jnp.zeros((nbp, 3, SSD_CONV_CH), F32), ssd_h=jnp.zeros((nbp, 4, 64, 128), F32),
                      lru_conv=jnp.zeros((nbp, 3, GROUP_W), F32), lru_h=jnp.zeros((nbp, GROUP_W), F32))
        hist_s = dict(table=page_table, layer=l, ssd_conv=state_ssd_conv[l], ssd_h=state_ssd[l],
                      lru_conv=state_lru_conv[l], lru_h=state_lru[l], **caches)
        xp, new_p = _layer(xp, nbp, slp, 0, hist_p, p, lam_init, alpha)
        xs, new_s = _layer(xs, nbs, sls, past, hist_s, p, lam_init, alpha)
        outs_p.append(new_p)
        outs_s.append(new_s)
    stack = lambda lst, name: jnp.stack([d[name] for d in lst])
    return ((xp.reshape(nbp, slp, D_MODEL), xs.reshape(nbs, sls, D_MODEL))
            + tuple(stack(outs_p, n) for n in STATE_ORDER) + tuple(stack(outs_s, n) for n in STATE_ORDER))
```

```python
import functools
import math

import jax
import jax.numpy as jnp
from jax import lax
from jax.experimental import pallas as pl
from jax.experimental.pallas import tpu as pltpu

F32 = jnp.float32
BF16 = jnp.bfloat16
NEG_INF = float("-inf")

D_MODEL = 1024
GROUP_W = 256
SSD_HEADS = 4
SSD_STATE = 128
SSD_CONV_CH = 768
NSA_CMP_BLOCK = 32
NSA_SLC_BLOCK = 64
NSA_TOPN = 16
NSA_WINDOW = 512
FORCE_SCORE = 1e4
LRU_C = 8.0
D_FF = 4096
ROPE_THETA = 500000.0
EPS = 1e-5
PAGE = 128
LANES = 128
VMEM_LIMIT = 56 * 1024 * 1024
PAGES_PER_STEP = 32
LOG2E = 1.4426950408889634
NSA_QSCALE = 64.0 ** -0.5 * LOG2E
DIFF_QSCALE = 32.0 ** -0.5 * LOG2E

W_IN_ORDER = ((0, 1024), (1028, 2052), (2064, 3344))
MISC_COLS = ((1024, 1028), (2052, 2064))
SEG = dict(z=(0, 256), xbc=(256, 1024), qn=(1024, 1280), kc=(1280, 1408), vc=(1408, 1536), ks=(1536, 1664),
           vs=(1664, 1792), kw=(1792, 1920), vw=(1920, 2048), dq=(2048, 2304), dk=(2304, 2560), dv=(2560, 2816),
           lx=(2816, 3072), lg=(3072, 3328), misc=(3328, 3456))
D_IN_PAD = 3456
GATE_LANE0 = 4


def _params(sem):
    return pltpu.CompilerParams(dimension_semantics=sem, vmem_limit_bytes=VMEM_LIMIT)


def _iota(shape, dim):
    return lax.broadcasted_iota(jnp.int32, shape, dim)


def _dot(a, b):
    return jnp.dot(a.astype(BF16), b.astype(BF16), preferred_element_type=F32)


def _dot_nt(a, b):
    return lax.dot_general(a.astype(BF16), b.astype(BF16), (((1,), (1,)), ((), ())), preferred_element_type=F32)


def _split3(x):
    h1 = x.astype(BF16)
    r1 = x - h1.astype(F32)
    h2 = r1.astype(BF16)
    h3 = (r1 - h2.astype(F32)).astype(BF16)
    return h1, h2, h3


def _dot_01(m01, x):
    return sum(jnp.dot(m01, part, preferred_element_type=F32) for part in _split3(x))


def _dot_01_r(x, m01):
    return sum(jnp.dot(part, m01, preferred_element_type=F32) for part in _split3(x))


def _dot_nt_01(m01, x):
    return sum(lax.dot_general(m01, part, (((1,), (1,)), ((), ())), preferred_element_type=F32)
               for part in _split3(x))


def _eye(n, m):
    return (_iota((n, m), 0) == _iota((n, m), 1)).astype(BF16)


def _softplus(x):
    return jnp.maximum(x, 0.0) + jnp.log1p(jnp.exp(-jnp.abs(x)))


def _head_bcast(cols, h_of_lane, nheads, lane0=0):
    out = jnp.zeros(h_of_lane.shape, F32)
    for h in range(nheads):
        out = jnp.where(h_of_lane == h, cols[:, lane0 + h:lane0 + h + 1], out)
    return out


def _online_update(s, mask, v_b, m_ref, l_ref, acc_ref, v_transposed=False):
    s = jnp.where(mask, s, NEG_INF)
    m_prev = m_ref[...]
    m_new = jnp.maximum(m_prev, jnp.max(s, axis=-1, keepdims=True))
    m_safe = jnp.where(m_new == NEG_INF, 0.0, m_new)
    alpha = jnp.exp2(m_prev - m_safe)
    p = jnp.exp2(s - m_safe)
    l_ref[...] = alpha * l_ref[...] + jnp.sum(p, axis=-1, keepdims=True)
    if v_transposed:
        pv = lax.dot_general(p.astype(BF16), v_b, (((1,), (1,)), ((), ())), preferred_element_type=F32)
    else:
        pv = jnp.dot(p.astype(BF16), v_b, preferred_element_type=F32)
    acc_ref[...] = alpha * acc_ref[...] + pv
    m_ref[...] = m_new


def _online_update_t(s, mask, vt_b, m_ref, l_ref, acc_ref):
    if mask is not None:
        s = jnp.where(mask, s, NEG_INF)
    m_prev = m_ref[...]
    m_new = jnp.maximum(m_prev, jnp.max(s, axis=0, keepdims=True))
    m_safe = jnp.where(m_new == NEG_INF, 0.0, m_new)
    alpha = jnp.exp2(m_prev - m_safe)
    p = jnp.exp2(s - m_safe)
    l_ref[...] = alpha * l_ref[...] + jnp.sum(p, axis=0, keepdims=True)
    acc_ref[...] = alpha * acc_ref[...] + jnp.dot(vt_b, p.astype(BF16), preferred_element_type=F32)
    m_ref[...] = m_new


def _causal_pairs(n_q, qt, kt):
    qi, kj = [], []
    for i in range(n_q):
        for j in range((i * qt + qt - 1) // kt + 1):
            qi.append(i)
            kj.append(j)
    return jnp.asarray(qi, jnp.int32), jnp.asarray(kj, jnp.int32)


def _init_stats(*triples):
    for m_, l_, a_ in triples:
        m_[...] = jnp.full(m_.shape, NEG_INF, F32)
        l_[...] = jnp.zeros(l_.shape, F32)
        a_[...] = jnp.zeros(a_.shape, F32)


def _nsa_expand(q):
    lo = _iota((q.shape[0], LANES), 1) < 64
    a, b = q[:, :LANES], q[:, LANES:]
    h0 = jnp.where(lo, a, 0.0)
    h1 = jnp.where(lo, pltpu.roll(a, 64, 1), 0.0)
    h2 = jnp.where(lo, 0.0, pltpu.roll(b, 64, 1))
    h3 = jnp.where(lo, 0.0, b)
    return jnp.concatenate([h0, h1, h2, h3], axis=0)


def _nsa_collect(o, n):
    lo = _iota((n, LANES), 1) < 64
    o0, o1, o2, o3 = o[0:n], o[n:2 * n], o[2 * n:3 * n], o[3 * n:4 * n]
    left = jnp.where(lo, o0, pltpu.roll(o1, 64, 1))
    right = jnp.where(lo, pltpu.roll(o2, 64, 1), o3)
    return jnp.concatenate([left, right], axis=1)


def _collect_t(o_t, n):
    return jnp.concatenate([o_t[0:64, 0:n], o_t[0:64, n:2 * n], o_t[64:128, 2 * n:3 * n], o_t[64:128, 3 * n:4 * n]],
                           axis=0)


def _layer_norm(v, g, b):
    mu = jnp.mean(v, axis=-1, keepdims=True)
    d = v - mu
    var = jnp.mean(d * d, axis=-1, keepdims=True)
    return d * lax.rsqrt(var + EPS) * g + b


def _rope128(v, c, sa, sb, half):
    return v * c + pltpu.roll(v, LANES - half, 1) * sa + pltpu.roll(v, half, 1) * sb


PROJ_OUT = ("z", "xbc", "qn", "qnr", "kc", "vc", "ks", "vs", "kw", "vw", "dq", "dk", "dv", "lx", "lg", "misc")
PROJ_OUT_T = ("kcT", "vcT", "ksT", "vsT", "kwT", "vwT", "dkT", "dvT")
PROJ_W = dict(z=256, xbc=768, qn=256, qnr=256, kc=128, vc=128, ks=128, vs=128, kw=128, vw=128, dq=256, dk=256,
              dv=256, lx=256, lg=256, misc=128)


def _proj_kernel(x_ref, w_ref, cn_ref, san_ref, sbn_ref, cd_ref, sad_ref, sbd_ref, *outs, flat_t):
    o = dict(zip(PROJ_OUT + PROJ_OUT_T, outs))
    xb = x_ref[...].astype(BF16)

    def seg(name):
        a, b = SEG[name]
        return jnp.dot(xb, w_ref[:, a:b], preferred_element_type=F32)

    def rope_n(v):
        return _rope128(v, cn_ref[...], san_ref[...], sbn_ref[...], 8)

    def rope_d(v):
        return _rope128(v, cd_ref[...], sad_ref[...], sbd_ref[...], 4)

    def put(name, v):
        o[name][...] = v
        if name + "T" in o:
            if flat_t:
                o[name + "T"][...] = v.T
            else:
                o[name + "T"][0] = v.T

    put("z", seg("z"))
    put("xbc", seg("xbc"))
    qn = seg("qn")
    put("qn", qn)
    put("qnr", jnp.concatenate([rope_n(qn[:, :LANES]), rope_n(qn[:, LANES:])], axis=1))
    put("kc", seg("kc"))
    put("vc", seg("vc"))
    put("ks", rope_n(seg("ks")))
    put("vs", seg("vs"))
    put("kw", rope_n(seg("kw")))
    put("vw", seg("vw"))
    dq = seg("dq")
    put("dq", jnp.concatenate([rope_d(dq[:, :LANES]), rope_d(dq[:, LANES:])], axis=1))
    dk = seg("dk")
    put("dk", jnp.concatenate([rope_d(dk[:, :LANES]), rope_d(dk[:, LANES:])], axis=1))
    put("dv", seg("dv"))
    put("lx", seg("lx"))
    put("lg", seg("lg"))
    put("misc", seg("misc"))


def _rope_tables(pos, dh, rows):
    rd = dh // 4
    half = rd // 2
    inv = ROPE_THETA ** (-jnp.arange(half, dtype=F32) / half)
    ang = pos.astype(F32)[:, None] * inv[None, :]
    cos, sin = jnp.cos(ang), jnp.sin(ang)
    n = pos.shape[0]
    zero_h = jnp.zeros((n, half), F32)
    zero_r = jnp.zeros((n, dh - rd), F32)
    c = jnp.concatenate([cos, cos, jnp.ones((n, dh - rd), F32)], 1)
    sa = jnp.concatenate([-sin, zero_h, zero_r], 1)
    sb = jnp.concatenate([zero_h, sin, zero_r], 1)
    reps = (max(rows // n, 1), LANES // dh)
    return tuple(jnp.tile(t, reps) for t in (c, sa, sb))


def _proj(x2d, w_in_p, pos, nb, sl, tm):
    t = x2d.shape[0]
    n_tab = max(sl // tm, 1)
    flat_t = sl < tm
    tabs = _rope_tables(pos, 64, tm) + _rope_tables(pos, 32, tm)
    tab_spec = pl.BlockSpec((tm, LANES), lambda i: (i % n_tab, 0))
    out_specs = [pl.BlockSpec((tm, PROJ_W[k]), lambda i: (i, 0)) for k in PROJ_OUT]
    out_shape = [jax.ShapeDtypeStruct((t, PROJ_W[k]), F32) for k in PROJ_OUT]
    for k in PROJ_OUT_T:
        c = PROJ_W[k[:-1]]
        if flat_t:
            out_specs.append(pl.BlockSpec((c, tm), lambda i: (0, i)))
            out_shape.append(jax.ShapeDtypeStruct((c, t), F32))
        else:
            out_specs.append(pl.BlockSpec((1, c, tm), lambda i: (i // n_tab, 0, i % n_tab)))
            out_shape.append(jax.ShapeDtypeStruct((nb, c, sl), F32))
    outs = pl.pallas_call(
        functools.partial(_proj_kernel, flat_t=flat_t),
        grid=(t // tm,),
        in_specs=[pl.BlockSpec((tm, D_MODEL), lambda i: (i, 0)),
                  pl.BlockSpec((D_MODEL, D_IN_PAD), lambda i: (0, 0))] + [tab_spec] * 6,
        out_specs=out_specs, out_shape=out_shape,
        compiler_params=_params(("parallel",)),
        name="proj",
    )(x2d, w_in_p, *tabs)
    pj = dict(zip(PROJ_OUT + PROJ_OUT_T, outs))
    if flat_t:
        for k in PROJ_OUT_T:
            pj[k] = pj[k].reshape(-1, nb, sl).transpose(1, 0, 2)
    return pj


def _conv_chunk(x_ref, prev_ref, cw_ref, cb_ref, xbuf, q, first):
    @pl.when(first)
    def _():
        xbuf[0:8, :] = prev_ref[0]

    xbuf[8:8 + q, :] = x_ref[0]
    acc = cb_ref[...] + cw_ref[0:1, :] * xbuf[pl.ds(5, q), :]
    for j in range(1, 4):
        acc = acc + cw_ref[j:j + 1, :] * xbuf[pl.ds(5 + j, q), :]
    xbuf[0:8, :] = xbuf[q:q + 8, :]
    return acc


def _ssd_kernel(xbc_ref, z_ref, misc_ref, prev_ref, h0_ref, cw_ref, cb_ref, dtb_ref, alog_ref, dexp_ref, nw_ref,
                y_ref, hout_ref, xbuf, ht, *, q):
    c = pl.program_id(1)

    @pl.when(c == 0)
    def _():
        ht[...] = _dot_nt_01(_eye(SSD_STATE, SSD_STATE), h0_ref[0])

    acc = _conv_chunk(xbc_ref, prev_ref, cw_ref, cb_ref, xbuf, q, c == 0)
    xc = acc * jax.nn.sigmoid(acc)
    sx = xc[:, 0:256]
    sb = (xc[:, 256:384], xc[:, 384:512])
    sc = (xc[:, 512:640], xc[:, 640:768])
    dt = _softplus(misc_ref[0] + dtb_ref[...])
    da = dt * (-jnp.exp(alog_ref[...]))
    causal = _iota((q, q), 0) >= _iota((q, q), 1)
    acum = _dot_01(causal.astype(BF16), da)
    xsel = _eye(8, LANES)
    acum_t = _dot_nt_01(xsel, acum)
    dt_t = _dot_nt_01(xsel, dt)
    a_last = acum[q - 1:q, :]
    wlast = jnp.exp(a_last - acum) * dt
    ea = jnp.exp(acum)
    head = _iota((q, GROUP_W), 1) >> 6
    ht_old = ht[...]
    y = dexp_ref[...] * sx
    for g in range(2):
        cb = _dot_nt(sc[g], sb[g])
        for h in (2 * g, 2 * g + 1):
            seg = acum[:, h:h + 1] - acum_t[h:h + 1, :]
            decay = jnp.exp(jnp.where(causal, seg, NEG_INF))
            m = cb * decay * dt_t[h:h + 1, :]
            y_h = _dot(m, sx) + _dot(sc[g] * ea[:, h:h + 1], ht_old)
            y = y + jnp.where(head == h, y_h, 0.0)
    xw = sx * _head_bcast(wlast, head, SSD_HEADS)
    lane = _iota((q, GROUP_W), 1)
    bt0 = _dot_nt(_eye(SSD_STATE, SSD_STATE), sb[0])
    bt1 = _dot_nt(_eye(SSD_STATE, SSD_STATE), sb[1])
    head1 = _iota((1, GROUP_W), 1) >> 6
    dch = _head_bcast(jnp.exp(a_last), head1, SSD_HEADS)
    ht_new = dch * ht_old + _dot(bt0, jnp.where(lane < 128, xw, 0.0)) + _dot(bt1, jnp.where(lane < 128, 0.0, xw))
    ht[...] = ht_new
    zz = z_ref[0]
    y = y * (zz * jax.nn.sigmoid(zz))
    y_ref[0] = y * lax.rsqrt(jnp.mean(y * y, axis=-1, keepdims=True) + EPS) * nw_ref[...]

    @pl.when(c == pl.num_programs(1) - 1)
    def _():
        hout_ref[0] = _dot_nt_01(_eye(GROUP_W, GROUP_W), ht_new)


def _ssd(pj, nb, sl, prev8, h0, p):
    q = min(128, sl)
    nc = sl // q
    row = lambda w: pl.BlockSpec((1, q, w), lambda b, c: (b, c, 0))
    per_b = lambda r, w: pl.BlockSpec((1, r, w), lambda b, c: (b, 0, 0))
    const = lambda r, w: pl.BlockSpec((r, w), lambda b, c: (0, 0))
    y, hout = pl.pallas_call(
        functools.partial(_ssd_kernel, q=q),
        grid=(nb, nc),
        in_specs=[row(768), row(256), row(128), per_b(8, 768), per_b(256, 128),
                  const(8, 768), const(1, 768), const(1, 128), const(1, 128), const(1, 256), const(1, 256)],
        out_specs=[row(256), per_b(256, 128)],
        out_shape=[jax.ShapeDtypeStruct((nb, sl, 256), F32), jax.ShapeDtypeStruct((nb, 256, 128), F32)],
        scratch_shapes=[pltpu.VMEM((q + 8, 768), F32), pltpu.VMEM((SSD_STATE, GROUP_W), F32)],
        compiler_params=_params(("parallel", "arbitrary")),
        name="ssd",
    )(pj["xbc"].reshape(nb, sl, 768), pj["z"].reshape(nb, sl, 256), pj["misc"].reshape(nb, sl, 128), prev8, h0,
      p["ssd_cw"], p["ssd_cb"], p["ssd_dtb"], p["ssd_alog"], p["ssd_dexp"], p["ssd_nw"])
    return y.reshape(nb * sl, 256), hout


def _lru_kernel(x_ref, g_ref, prev_ref, h0_ref, cw_ref, cb_ref, wa_ref, ba_ref, wx_ref, bx_ref, lam_ref,
                y_ref, hout_ref, xbuf, hc, *, q):
    c = pl.program_id(1)

    @pl.when(c == 0)
    def _():
        hc[...] = h0_ref[0]

    xc = _conv_chunk(x_ref, prev_ref, cw_ref, cb_ref, xbuf, q, c == 0)
    xcb = xc.astype(BF16)
    r = jax.nn.sigmoid(jnp.dot(xcb, wa_ref[...], preferred_element_type=F32) + ba_ref[...])
    i = jax.nn.sigmoid(jnp.dot(xcb, wx_ref[...], preferred_element_type=F32) + bx_ref[...])
    log_a = -LRU_C * r * _softplus(-lam_ref[...])
    a = jnp.exp(log_a)
    u = jnp.sqrt(1.0 - jnp.exp(2.0 * log_a)) * (i * xc)
    row = _iota((q, GROUP_W), 0)
    s = 1
    while s < q:
        a_sh = pltpu.roll(a, s, 0)
        u_sh = pltpu.roll(u, s, 0)
        keep = row >= s
        u = jnp.where(keep, a * u_sh + u, u)
        a = jnp.where(keep, a * a_sh, a)
        s *= 2
    h = u + a * hc[...]
    hc[...] = h[q - 1:q, :]
    gg = g_ref[0]
    gelu = 0.5 * gg * (1.0 + jnp.tanh(math.sqrt(2.0 / math.pi) * (gg + 0.044715 * (gg * gg * gg))))
    y_ref[0] = h * gelu

    @pl.when(c == pl.num_programs(1) - 1)
    def _():
        hout_ref[0] = h[q - 1:q, :]


def _lru(pj, nb, sl, prev8, h0, p):
    q = min(256, sl)
    nc = sl // q
    row = lambda w: pl.BlockSpec((1, q, w), lambda b, c: (b, c, 0))
    per_b = lambda r, w: pl.BlockSpec((1, r, w), lambda b, c: (b, 0, 0))
    const = lambda r, w: pl.BlockSpec((r, w), lambda b, c: (0, 0))
    y, hout = pl.pallas_call(
        functools.partial(_lru_kernel, q=q),
        grid=(nb, nc),
        in_specs=[row(256), row(256), per_b(8, 256), per_b(1, 256), const(8, 256), const(1, 256),
                  const(256, 256), const(1, 256), const(256, 256), const(1, 256), const(1, 256)],
        out_specs=[row(256), per_b(1, 256)],
        out_shape=[jax.ShapeDtypeStruct((nb, sl, 256), F32), jax.ShapeDtypeStruct((nb, 1, 256), F32)],
        scratch_shapes=[pltpu.VMEM((q + 8, 256), F32), pltpu.VMEM((1, 256), F32)],
        compiler_params=_params(("parallel", "arbitrary")),
        name="lru",
    )(pj["lx"].reshape(nb, sl, 256), pj["lg"].reshape(nb, sl, 256), prev8, h0.reshape(nb, 1, 256),
      p["lru_cw"], p["lru_cb"], p["lru_wa"], p["lru_ba"], p["lru_wx"], p["lru_bx"], p["lru_lam"])
    return y.reshape(nb * sl, 256), hout.reshape(nb, 256)


def _summ_kernel(pt_ref, *refs, n_parts):
    kp, vp = refs[:n_parts], refs[n_parts:2 * n_parts]
    mk_ref, mv_ref, ko_ref, vo_ref = refs[2 * n_parts:]

    def one(parts, m_ref, o_ref):
        chunks = []
        for r in parts:
            x = r[...]
            chunks.append(x.reshape(x.shape[-2], x.shape[-1]).astype(BF16))
        x = jnp.concatenate(chunks, axis=1) if len(chunks) > 1 else chunks[0]
        top = jnp.dot(x[0:64], m_ref[0], preferred_element_type=F32)
        bot = jnp.dot(x[64:128], m_ref[1], preferred_element_type=F32)
        o_ref[0] = jnp.concatenate([top, bot], axis=0)

    one(kp, mk_ref, ko_ref)
    one(vp, mv_ref, vo_ref)


def _summarize(k_src, v_src, part_specs, part_w, nb, total, table, w_k, w_v):
    n_parts = len(part_specs)
    step_w = n_parts * part_w
    n_blk = step_w // NSA_CMP_BLOCK
    tok = jnp.arange(step_w, dtype=jnp.int32)
    in_block = tok[:, None] // NSA_CMP_BLOCK == jnp.arange(n_blk, dtype=jnp.int32)[None, :]
    weights = lambda w: jnp.where(in_block[None], w[:, tok % NSA_CMP_BLOCK][:, :, None], 0.0).astype(BF16)
    wspec = pl.BlockSpec((2, step_w, n_blk), lambda b, s, pt: (0, 0, 0))
    ospec = pl.BlockSpec((1, LANES, n_blk), lambda b, s, pt: (b, 0, s))
    oshape = jax.ShapeDtypeStruct((nb, LANES, total // NSA_CMP_BLOCK), F32)
    return pl.pallas_call(
        functools.partial(_summ_kernel, n_parts=n_parts),
        grid_spec=pltpu.PrefetchScalarGridSpec(
            num_scalar_prefetch=1, grid=(nb, total // step_w),
            in_specs=list(part_specs) * 2 + [wspec, wspec],
            out_specs=[ospec, ospec]),
        out_shape=[oshape, oshape],
        compiler_params=_params(("parallel", "arbitrary")),
        name="nsa_summarize",
    )(table, *([k_src] * n_parts), *([v_src] * n_parts), weights(w_k), weights(w_v))


def _cmpsel_t_kernel(q_ref, kct_ref, vct_ref, ocmp_ref, sel_ref, *, qt, nc, ns):
    qi = pl.program_id(1)
    q4 = _nsa_expand(q_ref[0] * 0.125)
    s = _dot_nt(kct_ref[0].T, q4)
    qpos = qi * qt + (_iota((nc, 4 * qt), 1) & (qt - 1))
    blk_end = (_iota((nc, 4 * qt), 0) + 1) * NSA_CMP_BLOCK - 1
    s = jnp.where(blk_end <= qpos, s, NEG_INF)
    m = jnp.max(s, axis=0, keepdims=True)
    m = jnp.where(m == NEG_INF, 0.0, m)
    e = jnp.exp(s - m)
    p = e / jnp.maximum(jnp.sum(e, axis=0, keepdims=True), 1e-30)
    ocmp_ref[0] = _collect_t(_dot(vct_ref[0], p), qt).T
    impc = jnp.concatenate([p[:, 0:qt] + p[:, qt:2 * qt], p[:, 2 * qt:3 * qt] + p[:, 3 * qt:4 * qt]], axis=1)
    pair = (_iota((ns, nc), 1) >> 1 == _iota((ns, nc), 0)).astype(BF16)
    imp = _dot_01(pair, impc)
    blk = _iota((ns, 2 * qt), 0)
    cur = (qi * qt + (_iota((ns, 2 * qt), 1) & (qt - 1))) >> 6
    forced = (blk == 0) | (blk == cur) | (blk == cur - 1)
    score = jnp.where(forced, FORCE_SCORE, jnp.where(blk > cur, -1.0, imp))
    cnt = jnp.zeros((ns, 2 * qt), F32)
    for j in range(ns):
        row = score[j:j + 1, :]
        cnt = cnt + ((row > score) | ((row == score) & (blk > j))).astype(F32)
    sel = (cnt < float(NSA_TOPN)).astype(F32)
    sel_ref[0, 0] = sel[:, 0:qt]
    sel_ref[0, 1] = sel[:, qt:2 * qt]


def _cmp_select_prompt(qn, kcmp_t, vcmp_t, nb, sl, qt):
    nc = kcmp_t.shape[2]
    ns = sl // NSA_SLC_BLOCK
    return pl.pallas_call(
        functools.partial(_cmpsel_t_kernel, qt=qt, nc=nc, ns=ns),
        grid=(nb, sl // qt),
        in_specs=[pl.BlockSpec((1, qt, 256), lambda b, i: (b, i, 0)),
                  pl.BlockSpec((1, LANES, nc), lambda b, i: (b, 0, 0)),
                  pl.BlockSpec((1, LANES, nc), lambda b, i: (b, 0, 0))],
        out_specs=[pl.BlockSpec((1, qt, 256), lambda b, i: (b, i, 0)),
                   pl.BlockSpec((1, 2, ns, qt), lambda b, i: (b, 0, 0, i))],
        out_shape=[jax.ShapeDtypeStruct((nb, sl, 256), F32), jax.ShapeDtypeStruct((nb, 2, ns, sl), F32)],
        compiler_params=_params(("parallel", "parallel")),
        name="nsa_cmp_select_prompt",
    )(qn.reshape(nb, sl, 256), kcmp_t, vcmp_t)


def _cmpsel_kernel(q_ref, kct_ref, vct_ref, ocmp_ref, selx_ref, *, qt, nc, ns, nsp, past, n_chunks):
    q4 = _nsa_expand(q_ref[0] * 0.125)
    s = _dot(q4, kct_ref[0])
    qpos = past + (_iota((4 * qt, nc), 0) & (qt - 1))
    blk_end = (_iota((4 * qt, nc), 1) + 1) * NSA_CMP_BLOCK - 1
    s = jnp.where(blk_end <= qpos, s, NEG_INF)
    m = jnp.max(s, axis=-1, keepdims=True)
    m = jnp.where(m == NEG_INF, 0.0, m)
    e = jnp.exp(s - m)
    p = e / jnp.maximum(jnp.sum(e, axis=-1, keepdims=True), 1e-30)
    ocmp_ref[0] = _nsa_collect(_dot_nt(p, vct_ref[0]), qt)
    impc = jnp.concatenate([p[0:qt] + p[qt:2 * qt], p[2 * qt:3 * qt] + p[3 * qt:4 * qt]], axis=0)
    pair = (_iota((nc, nsp), 0) >> 1 == _iota((nc, nsp), 1)).astype(BF16)
    imp = _dot_01_r(impc, pair)
    blk = _iota((2 * qt, nsp), 1)
    cur = (past + (_iota((2 * qt, nsp), 0) & (qt - 1))) >> 6
    forced = (blk == 0) | (blk == cur) | (blk == cur - 1)
    score = jnp.where(forced, FORCE_SCORE, jnp.where(blk > cur, -1.0, imp))
    score = jnp.where(blk < ns, score, -3.0)

    cnt = jnp.zeros((2 * qt, nsp), F32)
    for j in range(ns):
        col = score[:, j:j + 1]
        cnt = cnt + ((col > score) | ((col == score) & (blk > j))).astype(F32)
    sel = ((cnt < float(NSA_TOPN)) & (blk < ns)).astype(F32)
    sel4 = jnp.concatenate([sel[0:qt], sel[0:qt], sel[qt:2 * qt], sel[qt:2 * qt]], axis=0).astype(BF16)
    e16 = (_iota((LANES, 1024), 1) >> 6 == _iota((LANES, 1024), 0)).astype(BF16)
    for t in range(n_chunks):
        pick = ((_iota((nsp, LANES), 0) == 16 * t + _iota((nsp, LANES), 1))
                & (_iota((nsp, LANES), 1) < 16)).astype(BF16)
        blocks = jnp.dot(sel4, pick, preferred_element_type=F32).astype(BF16)
        selx_ref[0, :, 1024 * t:1024 * (t + 1)] = jnp.dot(blocks, e16, preferred_element_type=F32)


def _cmp_select_sample(qn, kcmp_t, vcmp_t, nb, sl, past):
    nc = kcmp_t.shape[2]
    ns = -(-(past + sl) // NSA_SLC_BLOCK)
    nsp = -(-ns // LANES) * LANES
    n_chunks = -(-(past + sl) // 1024)
    return pl.pallas_call(
        functools.partial(_cmpsel_kernel, qt=sl, nc=nc, ns=ns, nsp=nsp, past=past, n_chunks=n_chunks),
        grid=(nb,),
        in_specs=[pl.BlockSpec((1, sl, 256), lambda b: (b, 0, 0)),
                  pl.BlockSpec((1, LANES, nc), lambda b: (b, 0, 0)),
                  pl.BlockSpec((1, LANES, nc), lambda b: (b, 0, 0))],
        out_specs=[pl.BlockSpec((1, sl, 256), lambda b: (b, 0, 0)),
                   pl.BlockSpec((1, 4 * sl, 1024 * n_chunks), lambda b: (b, 0, 0))],
        out_shape=[jax.ShapeDtypeStruct((nb, sl, 256), F32),
                   jax.ShapeDtypeStruct((nb, 4 * sl, 1024 * n_chunks), F32)],
        compiler_params=_params(("parallel",)),
        name="nsa_cmp_select_sample",
    )(qn.reshape(nb, sl, 256), kcmp_t, vcmp_t)


def _nsap_kernel(qi_ref, kj_ref, q_ref, ks_ref, vst_ref, kw_ref, vwt_ref, sel_ref, oslc_ref, owin_ref,
                 q4_s, m1, l1, a1, m2, l2, a2, *, qt, kt):
    step = pl.program_id(1)
    qi, kj = qi_ref[step], kj_ref[step]
    last = (qi * qt + qt - 1) // kt
    first_w = jnp.maximum(qi * qt - (NSA_WINDOW - 1), 0) // kt

    @pl.when(kj == 0)
    def _():
        q4_s[...] = _nsa_expand(q_ref[0] * NSA_QSCALE).astype(BF16)
        _init_stats((m1, l1, a1), (m2, l2, a2))

    def positions():
        return kj * kt + _iota((kt, qt), 0), qi * qt + _iota((kt, qt), 1)

    def picked(g):
        rows = []
        for u in range(kt // NSA_SLC_BLOCK):
            r = sel_ref[0, g, pl.ds(kj * (kt // NSA_SLC_BLOCK) + u, 1), :]
            rows.append(jnp.broadcast_to(r, (NSA_SLC_BLOCK, qt)))
        return jnp.concatenate(rows, axis=0) > 0.5

    def update(k_ref, vt_ref, masks, m_, l_, a_):
        k_b, vt_b = k_ref[0].astype(BF16), vt_ref[0].astype(BF16)
        mask = None
        if masks[0] is not None:
            mask = jnp.concatenate([masks[0], masks[0], masks[1], masks[1]], axis=1)
        half = kt // 2
        for part in (slice(0, half), slice(half, kt)):
            s = lax.dot_general(k_b[part], q4_s[...], (((1,), (1,)), ((), ())), preferred_element_type=F32)
            _online_update_t(s, None if mask is None else mask[part], vt_b[:, part], m_, l_, a_)

    @pl.when(kj < last)
    def _():
        update(ks_ref, vst_ref, (picked(0), picked(1)), m1, l1, a1)

    @pl.when(kj == last)
    def _():
        kpos, qpos = positions()
        causal = kpos <= qpos
        update(ks_ref, vst_ref, (picked(0) & causal, picked(1) & causal), m1, l1, a1)

    full_w = (kj < last) & (kj * kt > qi * qt + qt - 1 - NSA_WINDOW)

    @pl.when((kj >= first_w) & full_w)
    def _():
        update(kw_ref, vwt_ref, (None, None), m2, l2, a2)

    @pl.when((kj >= first_w) & jnp.logical_not(full_w))
    def _():
        kpos, qpos = positions()
        band = (kpos <= qpos) & (kpos > qpos - NSA_WINDOW)
        update(kw_ref, vwt_ref, (band, band), m2, l2, a2)

    @pl.when(kj == last)
    def _():
        oslc_ref[0] = _collect_t(a1[...] / jnp.maximum(l1[...], 1e-30), qt).T
        owin_ref[0] = _collect_t(a2[...] / jnp.maximum(l2[...], 1e-30), qt).T


def _nsa_prompt_attn(pj, sel, nb, sl):
    qt, kt = min(256, sl), min(512, sl)
    assert kt % qt == 0
    qi_tab, kj_tab = _causal_pairs(sl // qt, qt, kt)
    first_w = lambda i: jnp.maximum(i * qt - (NSA_WINDOW - 1), 0) // kt
    kspec = pl.BlockSpec((1, kt, LANES), lambda b, s, qi, kj: (b, kj[s], 0))
    vspec = pl.BlockSpec((1, LANES, kt), lambda b, s, qi, kj: (b, 0, kj[s]))
    kwspec = pl.BlockSpec((1, kt, LANES), lambda b, s, qi, kj: (b, jnp.maximum(kj[s], first_w(qi[s])), 0))
    vwspec = pl.BlockSpec((1, LANES, kt), lambda b, s, qi, kj: (b, 0, jnp.maximum(kj[s], first_w(qi[s]))))
    qspec = pl.BlockSpec((1, qt, 256), lambda b, s, qi, kj: (b, qi[s], 0))
    ns = sel.shape[2]
    r3 = lambda a, w: a.reshape(nb, sl, w)
    stats = [pltpu.VMEM((1, 4 * qt), F32), pltpu.VMEM((1, 4 * qt), F32), pltpu.VMEM((LANES, 4 * qt), F32)]
    return pl.pallas_call(
        functools.partial(_nsap_kernel, qt=qt, kt=kt),
        grid_spec=pltpu.PrefetchScalarGridSpec(
            num_scalar_prefetch=2, grid=(nb, qi_tab.shape[0]),
            in_specs=[qspec, kspec, vspec, kwspec, vwspec,
                      pl.BlockSpec((1, 2, ns, qt), lambda b, s, qi, kj: (b, 0, 0, qi[s]))],
            out_specs=[qspec, qspec],
            scratch_shapes=[pltpu.VMEM((4 * qt, LANES), BF16)] + stats * 2),
        out_shape=[jax.ShapeDtypeStruct((nb, sl, 256), F32)] * 2,
        compiler_params=_params(("parallel", "arbitrary")),
        name="nsa_prompt_attn",
    )(qi_tab, kj_tab, r3(pj["qnr"], 256), r3(pj["ks"], 128), pj["vsT"], r3(pj["kw"], 128), pj["vwT"], sel)


def _nsa_gate(misc, oc, os_, ow):
    g = jax.nn.sigmoid(misc)
    head = _iota(oc.shape, 1) >> 6
    return (_head_bcast(g, head, 4, GATE_LANE0) * oc + _head_bcast(g, head, 4, GATE_LANE0 + 4) * os_
            + _head_bcast(g, head, 4, GATE_LANE0 + 8) * ow)


def _diff_lambda(lamv_ref, lam_init):
    lv = lamv_ref[...]
    s1 = jnp.sum(lv[0:1] * lv[1:2], axis=1, keepdims=True)
    s2 = jnp.sum(lv[2:3] * lv[3:4], axis=1, keepdims=True)
    return jnp.exp(s1) - jnp.exp(s2) + lam_init


def _diff_finish(o, head, nheads, nw, lam_init):
    inv = jnp.zeros(o.shape, F32)
    for h in range(nheads):
        ms = jnp.sum(jnp.where(head == h, o * o, 0.0), axis=-1, keepdims=True) * (1.0 / 64.0)
        inv = jnp.where(head == h, lax.rsqrt(ms + EPS), inv)
    return o * inv * nw * (1.0 - lam_init)


def _diffp_kernel(qi_ref, kj_ref, q_ref, k_ref, vt_ref, lamv_ref, nwt_ref, o_ref, q4_s, m_s, l_s, acc_s,
                  *, qt, kt, lam_init):
    step = pl.program_id(2)
    qi, kj = qi_ref[step], kj_ref[step]
    last = (qi * qt + qt - 1) // kt

    @pl.when(kj == 0)
    def _():
        q = q_ref[0] * DIFF_QSCALE
        part = _iota((qt, LANES), 1) >> 5
        q4_s[...] = jnp.concatenate([jnp.where(part == k, q, 0.0) for k in range(4)], axis=0).astype(BF16)
        _init_stats((m_s, l_s, acc_s))

    def update(mask):
        k_b, vt_b = k_ref[0].astype(BF16), vt_ref[0].astype(BF16)
        if mask is not None:
            mask = jnp.concatenate([mask] * 4, axis=1)
        half = kt // 2
        s_a = lax.dot_general(k_b[:half], q4_s[...], (((1,), (1,)), ((), ())), preferred_element_type=F32)
        s_b = lax.dot_general(k_b[half:], q4_s[...], (((1,), (1,)), ((), ())), preferred_element_type=F32)
        _online_update_t(s_a, None if mask is None else mask[:half], vt_b[:, :half], m_s, l_s, acc_s)
        _online_update_t(s_b, None if mask is None else mask[half:], vt_b[:, half:], m_s, l_s, acc_s)

    @pl.when(kj < last)
    def _():
        update(None)

    @pl.when(kj == last)
    def _():
        update(kj * kt + _iota((kt, qt), 0) <= qi * qt + _iota((kt, qt), 1))
        lam = _diff_lambda(lamv_ref, lam_init)
        o = acc_s[...] / jnp.maximum(l_s[...], 1e-30)
        halves = []
        for h in range(2):
            rows = slice(64 * h, 64 * (h + 1))
            oh = o[rows, 2 * h * qt:(2 * h + 1) * qt] - lam * o[rows, (2 * h + 1) * qt:(2 * h + 2) * qt]
            ms = jnp.mean(oh * oh, axis=0, keepdims=True)
            halves.append(oh * lax.rsqrt(ms + EPS))
        o_ref[0] = (jnp.concatenate(halves, axis=0) * nwt_ref[...] * (1.0 - lam_init)).T


def _diff_prompt(pj, nb, sl, p, lam_init):
    qt, kt = min(256, sl), min(512, sl)
    assert kt % qt == 0
    qi_tab, kj_tab = _causal_pairs(sl // qt, qt, kt)
    qspec = pl.BlockSpec((1, qt, LANES), lambda b, h, s, qi, kj: (b, qi[s], h))
    kspec = pl.BlockSpec((1, kt, LANES), lambda b, h, s, qi, kj: (b, kj[s], h))
    vspec = pl.BlockSpec((1, LANES, kt), lambda b, h, s, qi, kj: (b, h, kj[s]))
    r3 = lambda a: a.reshape(nb, sl, 256)
    out = pl.pallas_call(
        functools.partial(_diffp_kernel, qt=qt, kt=kt, lam_init=lam_init),
        grid_spec=pltpu.PrefetchScalarGridSpec(
            num_scalar_prefetch=2, grid=(nb, 2, qi_tab.shape[0]),
            in_specs=[qspec, kspec, vspec, pl.BlockSpec((4, 32), lambda b, h, s, qi, kj: (0, 0)),
                      pl.BlockSpec((LANES, 1), lambda b, h, s, qi, kj: (0, 0))],
            out_specs=qspec,
            scratch_shapes=[pltpu.VMEM((4 * qt, LANES), BF16), pltpu.VMEM((1, 4 * qt), F32),
                            pltpu.VMEM((1, 4 * qt), F32), pltpu.VMEM((LANES, 4 * qt), F32)]),
        out_shape=jax.ShapeDtypeStruct((nb, sl, 256), F32),
        compiler_params=_params(("parallel", "parallel", "arbitrary")),
        name="diff_prompt",
    )(qi_tab, kj_tab, r3(pj["dq"]), r3(pj["dk"]), pj["dvT"], p["diff_lambda"],
      p["diff_nw256"][:, :LANES].reshape(LANES, 1))
    return out.reshape(nb * sl, 256)


def _paged_kernel(pt_ref, q_ref, *refs, mode, pps, width, kbase, past, new_len, lam_init):
    kp, vp = refs[:pps], refs[pps:2 * pps]
    knew_ref, vnew_ref = refs[2 * pps:2 * pps + 2]
    rest = list(refs[2 * pps + 2:])
    selx_ref = selnew_ref = lamv_ref = nw_ref = None
    if mode == "slc":
        selx_ref, selnew_ref = rest[:2]
        rest = rest[2:]
    if mode == "diff":
        lamv_ref, nw_ref = rest[:2]
        rest = rest[2:]
    o_ref, qe_s, m_s, l_s, acc_s = rest
    nq = new_len
    rows = qe_s.shape[0]
    step = pl.program_id(1)

    @pl.when(step == 0)
    def _():
        if mode == "diff":
            q = q_ref[0] * DIFF_QSCALE
            part = _iota((nq, width), 1) >> 5
            qe = jnp.concatenate([jnp.where(part == k, q, 0.0) for k in range(8)], axis=0)
        else:
            qe = _nsa_expand(q_ref[0] * NSA_QSCALE)
        qe_s[...] = qe.astype(BF16)
        _init_stats((m_s, l_s, acc_s))

    nk = pps * PAGE
    kt_b = jnp.concatenate([r[0, 0].astype(BF16) for r in kp], axis=1)
    vt_b = jnp.concatenate([r[0, 0].astype(BF16) for r in vp], axis=1)
    s = jnp.dot(qe_s[...], kt_b, preferred_element_type=F32)
    qpos = past + (_iota((rows, nk), 0) & (nq - 1))
    kpos = kbase + step * nk + _iota((rows, nk), 1)
    mask = kpos <= qpos
    if mode == "win":
        mask = mask & (kpos > qpos - NSA_WINDOW)
    if mode == "slc":
        mask = mask & (selx_ref[0] > 0.5)
    _online_update(s, mask, vt_b, m_s, l_s, acc_s, v_transposed=True)

    @pl.when(step == pl.num_programs(1) - 1)
    def _():
        s2 = lax.dot_general(qe_s[...], knew_ref[0].astype(BF16), (((1,), (1,)), ((), ())),
                             preferred_element_type=F32)
        qpos2 = past + (_iota((rows, nq), 0) & (nq - 1))
        kpos2 = past + _iota((rows, nq), 1)
        mask2 = kpos2 <= qpos2
        if mode == "win":
            mask2 = mask2 & (kpos2 > qpos2 - NSA_WINDOW)
        if mode == "slc":
            mask2 = mask2 & (selnew_ref[0][:, 0:nq] > 0.5)
        _online_update(s2, mask2, vnew_ref[0].astype(BF16), m_s, l_s, acc_s)
        o = acc_s[...] / jnp.maximum(l_s[...], 1e-30)
        if mode == "diff":
            lam = _diff_lambda(lamv_ref, lam_init)
            head = _iota((nq, width), 1) >> 6
            out = jnp.zeros((nq, width), F32)
            for h in range(4):
                o0 = o[(2 * h) * nq:(2 * h + 1) * nq]
                o1 = o[(2 * h + 1) * nq:(2 * h + 2) * nq]
                out = jnp.where(head == h, o0 - lam * o1, out)
            o_ref[0] = _diff_finish(out, head, 4, nw_ref[...], lam_init)
        else:
            o_ref[0] = _nsa_collect(o, nq)


def _paged_attn(mode, q, k_src, v_src, page_spec, n_pages, table, knew, vnew, kbase, past, extra=(), lam_init=0.0):
    nb, nq = q.shape[0], q.shape[1]
    width = knew.shape[-1]
    pps = min(PAGES_PER_STEP, n_pages)
    rows = 8 * nq if mode == "diff" else 4 * nq
    nk = pps * PAGE
    per_b = lambda r_, w: pl.BlockSpec((1, r_, w), lambda b, s, pt: (b, 0, 0))
    in_specs = [per_b(nq, 256)] + [page_spec(r, pps) for r in range(pps)] * 2 + [per_b(nq, width), per_b(nq, width)]
    args = [q] + [k_src] * pps + [v_src] * pps + [knew, vnew]
    if mode == "slc":
        (selx,) = extra
        new_blk = (past - kbase) // LANES
        in_specs += [pl.BlockSpec((1, rows, nk), lambda b, s, pt: (b, 0, s)),
                     pl.BlockSpec((1, rows, LANES), lambda b, s, pt: (b, 0, new_blk))]
        args += [selx, selx]
    if mode == "diff":
        in_specs += [pl.BlockSpec((4, 32), lambda b, s, pt: (0, 0)), pl.BlockSpec((1, 256), lambda b, s, pt: (0, 0))]
        args += list(extra)
    return pl.pallas_call(
        functools.partial(_paged_kernel, mode=mode, pps=pps, width=width, kbase=kbase, past=past, new_len=nq,
                          lam_init=lam_init),
        grid_spec=pltpu.PrefetchScalarGridSpec(
            num_scalar_prefetch=1, grid=(nb, n_pages // pps), in_specs=in_specs,
            out_specs=per_b(nq, 256),
            scratch_shapes=[pltpu.VMEM((rows, width), BF16), pltpu.VMEM((rows, 1), F32),
                            pltpu.VMEM((rows, 1), F32), pltpu.VMEM((rows, width), F32)]),
        out_shape=jax.ShapeDtypeStruct((nb, nq, 256), F32),
        compiler_params=_params(("parallel", "arbitrary")),
        name="paged_" + mode,
    )(table, *args)


def _page_copy(src_hbm, layer, page, buf, slot, r, sem):
    return pltpu.make_async_copy(src_hbm.at[layer, page], buf.at[slot, r], sem.at[slot])


def _start_pages(srcs, bufs, sems, pt_ref, layer, b, first, slot, n):
    for r in range(n):
        page = pt_ref[b, first + r]
        for src, buf, sem in zip(srcs, bufs, sems):
            _page_copy(src, layer, page, buf, slot, r, sem).start()


def _wait_pages(srcs, bufs, sems, layer, slot, n):
    for r in range(n):
        for src, buf, sem in zip(srcs, bufs, sems):
            _page_copy(src, layer, 0, buf, slot, r, sem).wait()


def _page_ring(pt_ref, srcs, bufs, sems, layer, pps, n_steps):
    b, nb = pl.program_id(0), pl.num_programs(0)
    slot_of = lambda s: (b * n_steps + s) & 1

    @pl.when(b == 0)
    def _():
        _start_pages(srcs, bufs, sems, pt_ref, layer, 0, 0, 0, pps)

    def advance(s):
        nxt = 1 - slot_of(s)
        if s + 1 < n_steps:
            _start_pages(srcs, bufs, sems, pt_ref, layer, b, (s + 1) * pps, nxt, pps)
        else:
            @pl.when(b + 1 < nb)
            def _():
                _start_pages(srcs, bufs, sems, pt_ref, layer, b + 1, 0, nxt, pps)
        _wait_pages(srcs, bufs, sems, layer, slot_of(s), pps)

    return slot_of, advance


def _paged2_kernel(pt_ref, q_ref, k_hbm, v_hbm, knew_ref, vnew_ref, *rest, mode, layer, pps, n_steps, width, past,
                   new_len, lam_init):
    selx_ref = lamv_ref = nw_ref = None
    if mode == "slc":
        selx_ref, rest = rest[0], rest[1:]
    if mode == "diff":
        lamv_ref, nw_ref, rest = rest[0], rest[1], rest[2:]
    o_ref, kbuf, vbuf, ksem, vsem, qe_s, m_s, l_s, acc_s = rest
    nq = new_len
    rows = qe_s.shape[0]
    nk = pps * PAGE
    slot_of, advance = _page_ring(pt_ref, (k_hbm, v_hbm), (kbuf, vbuf), (ksem, vsem), layer, pps, n_steps)

    if mode == "diff":
        q = q_ref[0] * DIFF_QSCALE
        part = _iota((nq, width), 1) >> 5
        qe = jnp.concatenate([jnp.where(part == k, q, 0.0) for k in range(8)], axis=0)
    else:
        qe = _nsa_expand(q_ref[0] * NSA_QSCALE)
    qe_s[...] = qe.astype(BF16)
    _init_stats((m_s, l_s, acc_s))

    for s in range(n_steps):
        advance(s)
        slot = slot_of(s)
        kt_b = jnp.concatenate([kbuf[slot, r].astype(BF16) for r in range(pps)], axis=1)
        vt_b = jnp.concatenate([vbuf[slot, r].astype(BF16) for r in range(pps)], axis=1)
        sc = jnp.dot(qe_s[...], kt_b, preferred_element_type=F32)
        qpos = past + (_iota((rows, nk), 0) & (nq - 1))
        kpos = s * nk + _iota((rows, nk), 1)
        mask = kpos <= qpos
        if mode == "slc":
            mask = mask & (selx_ref[0, :, s * nk:(s + 1) * nk] > 0.5)
        _online_update(sc, mask, vt_b, m_s, l_s, acc_s, v_transposed=True)

    s2 = lax.dot_general(qe_s[...], knew_ref[0].astype(BF16), (((1,), (1,)), ((), ())), preferred_element_type=F32)
    qpos2 = past + (_iota((rows, nq), 0) & (nq - 1))
    kpos2 = past + _iota((rows, nq), 1)
    mask2 = kpos2 <= qpos2
    if mode == "slc":
        mask2 = mask2 & (selx_ref[0, :, past:past + nq] > 0.5)
    _online_update(s2, mask2, vnew_ref[0].astype(BF16), m_s, l_s, acc_s)
    o = acc_s[...] / jnp.maximum(l_s[...], 1e-30)
    if mode == "diff":
        lam = _diff_lambda(lamv_ref, lam_init)
        head = _iota((nq, width), 1) >> 6
        out = jnp.zeros((nq, width), F32)
        for h in range(4):
            out = jnp.where(head == h, o[(2 * h) * nq:(2 * h + 1) * nq] - lam * o[(2 * h + 1) * nq:(2 * h + 2) * nq],
                            out)
        o_ref[0] = _diff_finish(out, head, 4, nw_ref[...], lam_init)
    else:
        o_ref[0] = _nsa_collect(o, nq)


def _paged2_attn(mode, q, k_src, v_src, layer, table, knew, vnew, past, extra=(), lam_init=0.0):
    nb, nq = q.shape[0], q.shape[1]
    n_pages = table.shape[1]
    width = knew.shape[-1]
    pps = min(PAGES_PER_STEP, n_pages)
    n_steps = n_pages // pps
    assert n_pages % pps == 0 and past == n_pages * PAGE
    rows = 8 * nq if mode == "diff" else 4 * nq
    per_b = lambda r_, w: pl.BlockSpec((1, r_, w), lambda b, pt: (b, 0, 0))
    hbm = pl.BlockSpec(memory_space=pl.ANY)
    in_specs = [per_b(nq, 256), hbm, hbm, per_b(nq, width), per_b(nq, width)]
    args = [q, k_src, v_src, knew, vnew]
    if mode == "slc":
        (selx,) = extra
        in_specs.append(per_b(rows, selx.shape[-1]))
        args.append(selx)
    if mode == "diff":
        in_specs += [pl.BlockSpec((4, 32), lambda b, pt: (0, 0)), pl.BlockSpec((1, 256), lambda b, pt: (0, 0))]
        args += list(extra)
    return pl.pallas_call(
        functools.partial(_paged2_kernel, mode=mode, layer=layer, pps=pps, n_steps=n_steps, width=width, past=past,
                          new_len=nq, lam_init=lam_init),
        grid_spec=pltpu.PrefetchScalarGridSpec(
            num_scalar_prefetch=1, grid=(nb,), in_specs=in_specs, out_specs=per_b(nq, 256),
            scratch_shapes=[pltpu.VMEM((2, pps, width, PAGE), F32), pltpu.VMEM((2, pps, width, PAGE), F32),
                            pltpu.SemaphoreType.DMA((2,)), pltpu.SemaphoreType.DMA((2,)),
                            pltpu.VMEM((rows, width), BF16), pltpu.VMEM((rows, 1), F32),
                            pltpu.VMEM((rows, 1), F32), pltpu.VMEM((rows, width), F32)]),
        out_shape=jax.ShapeDtypeStruct((nb, nq, 256), F32),
        compiler_params=_params(("arbitrary",)),
        name="paged2_" + mode,
    )(table, *args)


def _summ2_kernel(pt_ref, k_hbm, v_hbm, mk_ref, mv_ref, ko_ref, vo_ref, kbuf, vbuf, ksem, vsem, *, layer, pps, n_steps):
    slot_of, advance = _page_ring(pt_ref, (k_hbm, v_hbm), (kbuf, vbuf), (ksem, vsem), layer, pps, n_steps)
    n_blk = pps * PAGE // NSA_CMP_BLOCK
    for s in range(n_steps):
        advance(s)
        slot = slot_of(s)
        for buf, m_ref, o_ref in ((kbuf, mk_ref, ko_ref), (vbuf, mv_ref, vo_ref)):
            x = jnp.concatenate([buf[slot, r].astype(BF16) for r in range(pps)], axis=1)
            top = jnp.dot(x[0:64], m_ref[0], preferred_element_type=F32)
            bot = jnp.dot(x[64:128], m_ref[1], preferred_element_type=F32)
            o_ref[0, :, s * n_blk:(s + 1) * n_blk] = jnp.concatenate([top, bot], axis=0)


def _summarize_pages(k_src, v_src, layer, table, w_k, w_v):
    nb, n_pages = table.shape
    pps = min(PAGES_PER_STEP, n_pages)
    n_steps = n_pages // pps
    step_w = pps * PAGE
    n_blk = step_w // NSA_CMP_BLOCK
    tok = jnp.arange(step_w, dtype=jnp.int32)
    in_block = tok[:, None] // NSA_CMP_BLOCK == jnp.arange(n_blk, dtype=jnp.int32)[None, :]
    weights = lambda w: jnp.where(in_block[None], w[:, tok % NSA_CMP_BLOCK][:, :, None], 0.0).astype(BF16)
    hbm = pl.BlockSpec(memory_space=pl.ANY)
    wspec = pl.BlockSpec((2, step_w, n_blk), lambda b, pt: (0, 0, 0))
    ospec = pl.BlockSpec((1, LANES, n_steps * n_blk), lambda b, pt: (b, 0, 0))
    oshape = jax.ShapeDtypeStruct((nb, LANES, n_steps * n_blk), F32)
    return pl.pallas_call(
        functools.partial(_summ2_kernel, layer=layer, pps=pps, n_steps=n_steps),
        grid_spec=pltpu.PrefetchScalarGridSpec(
            num_scalar_prefetch=1, grid=(nb,), in_specs=[hbm, hbm, wspec, wspec], out_specs=[ospec, ospec],
            scratch_shapes=[pltpu.VMEM((2, pps, LANES, PAGE), F32), pltpu.VMEM((2, pps, LANES, PAGE), F32),
                            pltpu.SemaphoreType.DMA((2,)), pltpu.SemaphoreType.DMA((2,))]),
        out_shape=[oshape, oshape],
        compiler_params=_params(("arbitrary",)),
        name="nsa_summarize_pages",
    )(table, k_src, v_src, weights(w_k), weights(w_v))


def _post_kernel(x_ref, ya_ref, misc_ref, oc_ref, os_ref, ow_ref, yc_ref, yd_ref, wo_ref, g1_ref, b1_ref,
                 w1_ref, w2_ref, g2_ref, b2_ref, o_ref, x1_s, x1b_s, acc_s, *, alpha):
    f = pl.program_id(1)

    @pl.when(f == 0)
    def _():
        yb = _nsa_gate(misc_ref[...], oc_ref[...], os_ref[...], ow_ref[...])
        mix = _dot(ya_ref[...], wo_ref[0:256, :])
        mix = mix + _dot(yb, wo_ref[256:512, :])
        mix = mix + _dot(yc_ref[...], wo_ref[512:768, :])
        mix = mix + _dot(yd_ref[...], wo_ref[768:1024, :])
        x1 = _layer_norm(alpha * x_ref[...] + mix, g1_ref[...], b1_ref[...])
        x1_s[...] = x1
        x1b_s[...] = x1.astype(BF16)
        acc_s[...] = jnp.zeros(acc_s.shape, F32)

    h = jnp.dot(x1b_s[...], w1_ref[...], preferred_element_type=F32)
    h = jnp.square(jnp.maximum(h, 0.0))
    acc_s[...] += jnp.dot(h.astype(BF16), w2_ref[...], preferred_element_type=F32)

    @pl.when(f == pl.num_programs(1) - 1)
    def _():
        o_ref[...] = _layer_norm(alpha * x1_s[...] + acc_s[...], g2_ref[...], b2_ref[...])


def _post(x2d, ya, misc, oc, os_, ow, yc, yd, p, tm, alpha):
    t = x2d.shape[0]
    tf = 1024
    row = lambda w: pl.BlockSpec((tm, w), lambda i, f: (i, 0))
    const = lambda r, w: pl.BlockSpec((r, w), lambda i, f: (0, 0))
    return pl.pallas_call(
        functools.partial(_post_kernel, alpha=alpha),
        grid=(t // tm, D_FF // tf),
        in_specs=[row(D_MODEL), row(256), row(LANES), row(256), row(256), row(256), row(256), row(256),
                  const(D_MODEL, D_MODEL), const(1, D_MODEL), const(1, D_MODEL),
                  pl.BlockSpec((D_MODEL, tf), lambda i, f: (0, f)), pl.BlockSpec((tf, D_MODEL), lambda i, f: (f, 0)),
                  const(1, D_MODEL), const(1, D_MODEL)],
        out_specs=row(D_MODEL),
        out_shape=jax.ShapeDtypeStruct((t, D_MODEL), F32),
        scratch_shapes=[pltpu.VMEM((tm, D_MODEL), F32), pltpu.VMEM((tm, D_MODEL), BF16),
                        pltpu.VMEM((tm, D_MODEL), F32)],
        compiler_params=_params(("parallel", "arbitrary")),
        name="post",
    )(x2d, ya, misc, oc.reshape(t, 256), os_.reshape(t, 256), ow.reshape(t, 256), yc, yd,
      p["w_out"], p["ln1_g"], p["ln1_b"], p["w_ff1"], p["w_ff2"], p["ln2_g"], p["ln2_b"])


def _prep_params(w, l):
    w_in = w["w_in"][l]
    misc = jnp.concatenate([w_in[:, a:b] for a, b in MISC_COLS], axis=1)
    misc = jnp.pad(misc, ((0, 0), (0, LANES - misc.shape[1])))
    w_in_p = jnp.concatenate([w_in[:, a:b] for a, b in W_IN_ORDER] + [misc], axis=1).astype(BF16)
    row = lambda v: v.reshape(1, -1).astype(F32)
    pad_lanes = lambda v: jnp.pad(v.reshape(1, -1), ((0, 0), (0, LANES - v.shape[-1])))
    blockdiag = lambda m: jax.scipy.linalg.block_diag(*[m[i] for i in range(m.shape[0])]).astype(BF16)
    return dict(
        w_in=w_in_p,
        ssd_cw=jnp.pad(w["ssd_conv_w"][l], ((0, 4), (0, 0))), ssd_cb=row(w["ssd_conv_b"][l]),
        ssd_dtb=pad_lanes(w["ssd_dt_bias"][l]), ssd_alog=pad_lanes(w["ssd_A_log"][l]),
        ssd_dexp=row(jnp.repeat(w["ssd_D"][l], 64)), ssd_nw=row(w["ssd_norm_w"][l]),
        cmp_wk=w["nsa_w_cmp_k"][l], cmp_wv=w["nsa_w_cmp_v"][l],
        diff_lambda=w["diff_lambda"][l], diff_nw256=row(jnp.tile(w["diff_norm_w"][l], 4)),
        lru_cw=jnp.pad(w["lru_conv_w"][l], ((0, 4), (0, 0))), lru_cb=row(w["lru_conv_b"][l]),
        lru_wa=blockdiag(w["lru_w_a"][l]), lru_ba=row(w["lru_b_a"][l]),
        lru_wx=blockdiag(w["lru_w_x"][l]), lru_bx=row(w["lru_b_x"][l]), lru_lam=row(w["lru_lambda"][l]),
        w_out=w["w_out"][l].astype(BF16), ln1_g=row(w["ln1_g"][l]), ln1_b=row(w["ln1_b"][l]),
        w_ff1=w["w_ff1"][l].astype(BF16), w_ff2=w["w_ff2"][l].astype(BF16),
        ln2_g=row(w["ln2_g"][l]), ln2_b=row(w["ln2_b"][l]),
    )


def _pad_prev(prev):
    return jnp.pad(prev, ((0, 0), (5, 0), (0, 0)))


def _token_minor(a):
    a = jnp.moveaxis(a, -3, -1)
    return a.reshape(*a.shape[:-3], a.shape[-3] * a.shape[-2], a.shape[-1])


def _token_major(a_t, heads):
    nb, _, n = a_t.shape
    return jnp.moveaxis(a_t.reshape(nb, heads, 64, n), -1, 1)


def _layer(x2d, nb, sl, past, hist, p, lam_init, alpha):
    t = nb * sl
    tm = min(256, t)
    pos = past + jnp.arange(sl, dtype=jnp.int32)
    pj = _proj(x2d, p["w_in"], pos, nb, sl, tm)
    y_a, ssd_h = _ssd(pj, nb, sl, _pad_prev(hist["ssd_conv"]), hist["ssd_h"].reshape(nb, 256, 128), p)
    y_d, lru_h = _lru(pj, nb, sl, _pad_prev(hist["lru_conv"]), hist["lru_h"], p)
    r3 = lambda a, w: a.reshape(nb, sl, w)
    dummy = jnp.zeros((1, 1), jnp.int32)
    if past == 0:
        step_w = min(4096, sl)
        spec = pl.BlockSpec((1, LANES, step_w), lambda b, s, pt: (b, 0, s))
        kcmp_t, vcmp_t = _summarize(pj["kcT"], pj["vcT"], [spec], step_w, nb, sl, dummy, p["cmp_wk"], p["cmp_wv"])
        o_cmp, sel = _cmp_select_prompt(pj["qn"], kcmp_t, vcmp_t, nb, sl, min(256, sl))
        o_slc, o_win = _nsa_prompt_attn(pj, sel, nb, sl)
        y_c = _diff_prompt(pj, nb, sl, p, lam_init)
        kw_t, vw_t = pj["kwT"][:, :, sl - NSA_WINDOW:], pj["vwT"][:, :, sl - NSA_WINDOW:]
        new = dict(kc=_token_major(pj["kcT"], 2), vc=_token_major(pj["vcT"], 2), ks=_token_major(pj["ksT"], 2),
                   vs=_token_major(pj["vsT"], 2), dk=_token_major(pj["dkT"], 4), dv=_token_major(pj["dvT"], 4))
    else:
        table, l = hist["table"], hist["layer"]
        n_pages = table.shape[1]
        win_spec = lambda r, per: pl.BlockSpec(
            (1, 1, LANES, PAGE), lambda b, s, pt, r=r, per=per: (l, b, 0, s * per + r))
        kcmp_t, vcmp_t = _summarize_pages(hist["kc"], hist["vc"], l, table, p["cmp_wk"], p["cmp_wv"])
        o_cmp, selx = _cmp_select_sample(pj["qn"], kcmp_t, vcmp_t, nb, sl, past)
        qnr = r3(pj["qnr"], 256)
        o_slc = _paged2_attn("slc", qnr, hist["ks"], hist["vs"], l, table, r3(pj["ks"], 128), r3(pj["vs"], 128),
                             past, extra=(selx,))
        wbuf = hist["kw"].shape[-1]
        o_win = _paged_attn("win", qnr, hist["kw"], hist["vw"], win_spec, wbuf // PAGE, dummy,
                            r3(pj["kw"], 128), r3(pj["vw"], 128), past - wbuf, past)
        y_c = _paged2_attn("diff", r3(pj["dq"], 256), hist["dk"], hist["dv"], l, table, r3(pj["dk"], 256),
                           r3(pj["dv"], 256), past, extra=(p["diff_lambda"], p["diff_nw256"]),
                           lam_init=lam_init).reshape(t, 256)
        kw_t = jnp.concatenate([hist["kw"][l], pj["kwT"]], axis=2)[:, :, -NSA_WINDOW:]
        vw_t = jnp.concatenate([hist["vw"][l], pj["vwT"]], axis=2)[:, :, -NSA_WINDOW:]
        kv = lambda a: r3(a, 128).reshape(nb, sl, 2, 64)
        new = dict(kc=kv(pj["kc"]), vc=kv(pj["vc"]), ks=kv(pj["ks"]), vs=kv(pj["vs"]),
                   dk=r3(pj["dk"], 256).reshape(nb, sl, 4, 64), dv=r3(pj["dv"], 256).reshape(nb, sl, 4, 64))
    x_out = _post(x2d, y_a, pj["misc"], o_cmp, o_slc, o_win, y_c, y_d, p, min(512, t), alpha)
    new.update(kw=_token_major(kw_t, 2), vw=_token_major(vw_t, 2),
               ssd_h=ssd_h.reshape(nb, 4, 64, 128), ssd_conv=r3(pj["xbc"], 768)[:, sl - 3:],
               lru_h=lru_h, lru_conv=r3(pj["lx"], 256)[:, sl - 3:])
    return x_out, new


STATE_ORDER = ("kc", "vc", "ks", "vs", "dk", "dv", "kw", "vw", "ssd_h", "ssd_conv", "lru_h", "lru_conv")


def kernel(x_prompt, x_sample, cache_nsa_k_cmp, cache_nsa_v_cmp, cache_nsa_k_slc, cache_nsa_v_slc, cache_diff_k, cache_diff_v, cache_nsa_k_win, cache_nsa_v_win, state_ssd, state_ssd_conv, state_lru, state_lru_conv, page_table, w_in, ssd_conv_w, ssd_conv_b, ssd_dt_bias, ssd_A_log, ssd_D, ssd_norm_w, nsa_w_cmp_k, nsa_w_cmp_v, diff_lambda, diff_norm_w, lru_conv_w, lru_conv_b, lru_w_a, lru_b_a, lru_w_x, lru_b_x, lru_lambda, w_out, ln1_g, ln1_b, w_ff1, w_ff2, ln2_g, ln2_b):
    weights = dict(w_in=w_in, ssd_conv_w=ssd_conv_w, ssd_conv_b=ssd_conv_b, ssd_dt_bias=ssd_dt_bias,
                   ssd_A_log=ssd_A_log, ssd_D=ssd_D, ssd_norm_w=ssd_norm_w, nsa_w_cmp_k=nsa_w_cmp_k,
                   nsa_w_cmp_v=nsa_w_cmp_v, diff_lambda=diff_lambda, diff_norm_w=diff_norm_w, lru_conv_w=lru_conv_w,
                   lru_conv_b=lru_conv_b, lru_w_a=lru_w_a, lru_b_a=lru_b_a, lru_w_x=lru_w_x, lru_b_x=lru_b_x,
                   lru_lambda=lru_lambda, w_out=w_out, ln1_g=ln1_g, ln1_b=ln1_b, w_ff1=w_ff1, w_ff2=w_ff2,
                   ln2_g=ln2_g, ln2_b=ln2_b)
    depth = w_in.shape[0]
    nbp, slp, _ = x_prompt.shape
    nbs, sls, _ = x_sample.shape
    past = page_table.shape[1] * cache_nsa_k_cmp.shape[2]
    assert past % PAGE == 0 and sls < NSA_CMP_BLOCK and slp % 256 == 0
    alpha = (2 * depth) ** 0.25
    caches = dict(kc=_token_minor(cache_nsa_k_cmp), vc=_token_minor(cache_nsa_v_cmp), ks=_token_minor(cache_nsa_k_slc),
                  vs=_token_minor(cache_nsa_v_slc), dk=_token_minor(cache_diff_k), dv=_token_minor(cache_diff_v),
                  kw=_token_minor(cache_nsa_k_win), vw=_token_minor(cache_nsa_v_win))
    xp = x_prompt.reshape(nbp * slp, D_MODEL)
    xs = x_sample.reshape(nbs * sls, D_MODEL)
    outs_p, outs_s = [], []
    for l in range(depth):
        p = _prep_params(weights, l)
        lam_init = 0.8 - 0.6 * math.exp(-0.3 * l)
        hist_p = dict(ssd_conv=jnp.zeros((nbp, 3, SSD_CONV_CH), F32), ssd_h=jnp.zeros((nbp, 4, 64, 128), F32),
                      lru_conv=jnp.zeros((nbp, 3, GROUP_W), F32), lru_h=jnp.zeros((nbp, GROUP_W), F32))
        hist_s = dict(table=page_table, layer=l, ssd_conv=state_ssd_conv[l], ssd_h=state_ssd[l],
                      lru_conv=state_lru_conv[l], lru_h=state_lru[l], **caches)
        xp, new_p = _layer(xp, nbp, slp, 0, hist_p, p, lam_init, alpha)
        xs, new_s = _layer(xs, nbs, sls, past, hist_s, p, lam_init, alpha)
        outs_p.append(new_p)
        outs_s.append(new_s)
    stack = lambda lst, name: jnp.stack([d[name] for d in lst])
    return ((xp.reshape(nbp, slp, D_MODEL), xs.reshape(nbs, sls, D_MODEL))
            + tuple(stack(outs_p, n) for n in STATE_ORDER) + tuple(stack(outs_s, n) for n in STATE_ORDER))
```

```python
import functools
import math

import jax
import jax.numpy as jnp
from jax import lax
from jax.experimental import pallas as pl
from jax.experimental.pallas import tpu as pltpu

F32 = jnp.float32
BF16 = jnp.bfloat16
NEG_INF = float("-inf")

D_MODEL = 1024
GROUP_W = 256
SSD_HEADS = 4
SSD_STATE = 128
SSD_CONV_CH = 768
NSA_CMP_BLOCK = 32
NSA_SLC_BLOCK = 64
NSA_TOPN = 16
NSA_WINDOW = 512
FORCE_SCORE = 1e4
LRU_C = 8.0
D_FF = 4096
ROPE_THETA = 500000.0
EPS = 1e-5
PAGE = 128
LANES = 128
VMEM_LIMIT = 56 * 1024 * 1024
PAGES_PER_STEP = 32
LAZY_MAX_RISE = 64.0
LOG2E = 1.4426950408889634
NSA_QSCALE = 64.0 ** -0.5 * LOG2E
DIFF_QSCALE = 32.0 ** -0.5 * LOG2E

W_IN_ORDER = ((0, 1024), (1028, 2052), (2064, 3344))
MISC_COLS = ((1024, 1028), (2052, 2064))
SEG = dict(z=(0, 256), xbc=(256, 1024), qn=(1024, 1280), kc=(1280, 1408), vc=(1408, 1536), ks=(1536, 1664),
           vs=(1664, 1792), kw=(1792, 1920), vw=(1920, 2048), dq=(2048, 2304), dk=(2304, 2560), dv=(2560, 2816),
           lx=(2816, 3072), lg=(3072, 3328), misc=(3328, 3456))
D_IN_PAD = 3456
GATE_LANE0 = 4


def _params(sem):
    return pltpu.CompilerParams(dimension_semantics=sem, vmem_limit_bytes=VMEM_LIMIT)


def _iota(shape, dim):
    return lax.broadcasted_iota(jnp.int32, shape, dim)


def _dot(a, b):
    return jnp.dot(a.astype(BF16), b.astype(BF16), preferred_element_type=F32)


def _dot_nt(a, b):
    return lax.dot_general(a.astype(BF16), b.astype(BF16), (((1,), (1,)), ((), ())), preferred_element_type=F32)


def _split3(x):
    h1 = x.astype(BF16)
    r1 = x - h1.astype(F32)
    h2 = r1.astype(BF16)
    h3 = (r1 - h2.astype(F32)).astype(BF16)
    return h1, h2, h3


def _dot_01(m01, x):
    return sum(jnp.dot(m01, part, preferred_element_type=F32) for part in _split3(x))


def _dot_01_r(x, m01):
    return sum(jnp.dot(part, m01, preferred_element_type=F32) for part in _split3(x))


def _dot_nt_01(m01, x):
    return sum(lax.dot_general(m01, part, (((1,), (1,)), ((), ())), preferred_element_type=F32)
               for part in _split3(x))


def _eye(n, m):
    return (_iota((n, m), 0) == _iota((n, m), 1)).astype(BF16)


def _softplus(x):
    return jnp.maximum(x, 0.0) + jnp.log1p(jnp.exp(-jnp.abs(x)))


def _head_bcast(cols, h_of_lane, nheads, lane0=0):
    out = jnp.zeros(h_of_lane.shape, F32)
    for h in range(nheads):
        out = jnp.where(h_of_lane == h, cols[:, lane0 + h:lane0 + h + 1], out)
    return out


def _online_update(s, mask, v_b, m_ref, l_ref, acc_ref, v_transposed=False):
    s = jnp.where(mask, s, NEG_INF)
    m_prev = m_ref[...]
    m_new = jnp.maximum(m_prev, jnp.max(s, axis=-1, keepdims=True))
    m_safe = jnp.where(m_new == NEG_INF, 0.0, m_new)
    alpha = jnp.exp2(m_prev - m_safe)
    p = jnp.exp2(s - m_safe)
    l_ref[...] = alpha * l_ref[...] + jnp.sum(p, axis=-1, keepdims=True)
    if v_transposed:
        pv = lax.dot_general(p.astype(BF16), v_b, (((1,), (1,)), ((), ())), preferred_element_type=F32)
    else:
        pv = jnp.dot(p.astype(BF16), v_b, preferred_element_type=F32)
    acc_ref[...] = alpha * acc_ref[...] + pv
    m_ref[...] = m_new


def _online_update_t(s, mask, vt_b, m_ref, l_ref, acc_ref):
    if mask is not None:
        s = jnp.where(mask, s, NEG_INF)
    m_prev = m_ref[...]
    m_new = jnp.maximum(m_prev, jnp.max(s, axis=0, keepdims=True))
    m_safe = jnp.where(m_new == NEG_INF, 0.0, m_new)
    alpha = jnp.exp2(m_prev - m_safe)
    p = jnp.exp2(s - m_safe)
    l_ref[...] = alpha * l_ref[...] + jnp.sum(p, axis=0, keepdims=True)
    acc_ref[...] = alpha * acc_ref[...] + jnp.dot(vt_b, p.astype(BF16), preferred_element_type=F32)
    m_ref[...] = m_new


def _online_update_t_lazy(scores, mask, vt_b, m_ref, l_ref, acc_ref):
    s = scores()
    if mask is not None:
        s = jnp.where(mask, s, NEG_INF)
    m_prev = m_ref[...]
    tile_max = jnp.max(s, axis=0, keepdims=True)
    p = jnp.exp2(s - m_prev)
    p_sum = jnp.sum(p, axis=0, keepdims=True)
    pv = jnp.dot(vt_b, p.astype(BF16), preferred_element_type=F32)
    safe = jnp.max(tile_max - m_prev) <= LAZY_MAX_RISE

    @pl.when(safe)
    def _():
        m_new = jnp.maximum(m_prev, tile_max)
        alpha = jnp.exp2(m_prev - m_new)
        l_ref[...] = (l_ref[...] + p_sum) * alpha
        acc_ref[...] = (acc_ref[...] + pv) * alpha
        m_ref[...] = m_new

    @pl.when(jnp.logical_not(safe))
    def _():
        _online_update_t(scores(), mask, vt_b, m_ref, l_ref, acc_ref)


def _causal_pairs(n_q, qt, kt):
    qi, kj = [], []
    for i in range(n_q):
        for j in range((i * qt + qt - 1) // kt + 1):
            qi.append(i)
            kj.append(j)
    return jnp.asarray(qi, jnp.int32), jnp.asarray(kj, jnp.int32)


def _init_stats(*triples):
    for m_, l_, a_ in triples:
        m_[...] = jnp.full(m_.shape, NEG_INF, F32)
        l_[...] = jnp.zeros(l_.shape, F32)
        a_[...] = jnp.zeros(a_.shape, F32)


def _nsa_expand(q):
    lo = _iota((q.shape[0], LANES), 1) < 64
    a, b = q[:, :LANES], q[:, LANES:]
    h0 = jnp.where(lo, a, 0.0)
    h1 = jnp.where(lo, pltpu.roll(a, 64, 1), 0.0)
    h2 = jnp.where(lo, 0.0, pltpu.roll(b, 64, 1))
    h3 = jnp.where(lo, 0.0, b)
    return jnp.concatenate([h0, h1, h2, h3], axis=0)


def _nsa_collect(o, n):
    lo = _iota((n, LANES), 1) < 64
    o0, o1, o2, o3 = o[0:n], o[n:2 * n], o[2 * n:3 * n], o[3 * n:4 * n]
    left = jnp.where(lo, o0, pltpu.roll(o1, 64, 1))
    right = jnp.where(lo, pltpu.roll(o2, 64, 1), o3)
    return jnp.concatenate([left, right], axis=1)


def _collect_t(o_t, n):
    return jnp.concatenate([o_t[0:64, 0:n], o_t[0:64, n:2 * n], o_t[64:128, 2 * n:3 * n], o_t[64:128, 3 * n:4 * n]],
                           axis=0)


def _layer_norm(v, g, b):
    mu = jnp.mean(v, axis=-1, keepdims=True)
    d = v - mu
    var = jnp.mean(d * d, axis=-1, keepdims=True)
    return d * lax.rsqrt(var + EPS) * g + b


def _rope128(v, c, sa, sb, half):
    return v * c + pltpu.roll(v, LANES - half, 1) * sa + pltpu.roll(v, half, 1) * sb


PROJ_OUT = ("z", "xbc", "qn", "qnr", "kc", "vc", "ks", "vs", "kw", "vw", "dq", "dk", "dv", "lx", "lg", "misc")
PROJ_OUT_T = ("kcT", "vcT", "ksT", "vsT", "kwT", "vwT", "dkT", "dvT")
PROJ_W = dict(z=256, xbc=768, qn=256, qnr=256, kc=128, vc=128, ks=128, vs=128, kw=128, vw=128, dq=256, dk=256,
              dv=256, lx=256, lg=256, misc=128)


def _proj_kernel(x_ref, w_ref, cn_ref, san_ref, sbn_ref, cd_ref, sad_ref, sbd_ref, *outs, flat_t):
    o = dict(zip(PROJ_OUT + PROJ_OUT_T, outs))
    xb = x_ref[...].astype(BF16)

    def seg(name):
        a, b = SEG[name]
        return jnp.dot(xb, w_ref[:, a:b], preferred_element_type=F32)

    def rope_n(v):
        return _rope128(v, cn_ref[...], san_ref[...], sbn_ref[...], 8)

    def rope_d(v):
        return _rope128(v, cd_ref[...], sad_ref[...], sbd_ref[...], 4)

    def put(name, v):
        o[name][...] = v
        if name + "T" in o:
            if flat_t:
                o[name + "T"][...] = v.T
            else:
                o[name + "T"][0] = v.T

    put("z", seg("z"))
    put("xbc", seg("xbc"))
    qn = seg("qn")
    put("qn", qn)
    put("qnr", jnp.concatenate([rope_n(qn[:, :LANES]), rope_n(qn[:, LANES:])], axis=1))
    put("kc", seg("kc"))
    put("vc", seg("vc"))
    put("ks", rope_n(seg("ks")))
    put("vs", seg("vs"))
    put("kw", rope_n(seg("kw")))
    put("vw", seg("vw"))
    dq = seg("dq")
    put("dq", jnp.concatenate([rope_d(dq[:, :LANES]), rope_d(dq[:, LANES:])], axis=1))
    dk = seg("dk")
    put("dk", jnp.concatenate([rope_d(dk[:, :LANES]), rope_d(dk[:, LANES:])], axis=1))
    put("dv", seg("dv"))
    put("lx", seg("lx"))
    put("lg", seg("lg"))
    put("misc", seg("misc"))


def _rope_tables(pos, dh, rows):
    rd = dh // 4
    half = rd // 2
    inv = ROPE_THETA ** (-jnp.arange(half, dtype=F32) / half)
    ang = pos.astype(F32)[:, None] * inv[None, :]
    cos, sin = jnp.cos(ang), jnp.sin(ang)
    n = pos.shape[0]
    zero_h = jnp.zeros((n, half), F32)
    zero_r = jnp.zeros((n, dh - rd), F32)
    c = jnp.concatenate([cos, cos, jnp.ones((n, dh - rd), F32)], 1)
    sa = jnp.concatenate([-sin, zero_h, zero_r], 1)
    sb = jnp.concatenate([zero_h, sin, zero_r], 1)
    reps = (max(rows // n, 1), LANES // dh)
    return tuple(jnp.tile(t, reps) for t in (c, sa, sb))


def _proj(x2d, w_in_p, pos, nb, sl, tm):
    t = x2d.shape[0]
    n_tab = max(sl // tm, 1)
    flat_t = sl < tm
    tabs = _rope_tables(pos, 64, tm) + _rope_tables(pos, 32, tm)
    tab_spec = pl.BlockSpec((tm, LANES), lambda i: (i % n_tab, 0))
    out_specs = [pl.BlockSpec((tm, PROJ_W[k]), lambda i: (i, 0)) for k in PROJ_OUT]
    out_shape = [jax.ShapeDtypeStruct((t, PROJ_W[k]), F32) for k in PROJ_OUT]
    for k in PROJ_OUT_T:
        c = PROJ_W[k[:-1]]
        if flat_t:
            out_specs.append(pl.BlockSpec((c, tm), lambda i: (0, i)))
            out_shape.append(jax.ShapeDtypeStruct((c, t), F32))
        else:
            out_specs.append(pl.BlockSpec((1, c, tm), lambda i: (i // n_tab, 0, i % n_tab)))
            out_shape.append(jax.ShapeDtypeStruct((nb, c, sl), F32))
    outs = pl.pallas_call(
        functools.partial(_proj_kernel, flat_t=flat_t),
        grid=(t // tm,),
        in_specs=[pl.BlockSpec((tm, D_MODEL), lambda i: (i, 0)),
                  pl.BlockSpec((D_MODEL, D_IN_PAD), lambda i: (0, 0))] + [tab_spec] * 6,
        out_specs=out_specs, out_shape=out_shape,
        compiler_params=_params(("parallel",)),
        name="proj",
    )(x2d, w_in_p, *tabs)
    pj = dict(zip(PROJ_OUT + PROJ_OUT_T, outs))
    if flat_t:
        for k in PROJ_OUT_T:
            pj[k] = pj[k].reshape(-1, nb, sl).transpose(1, 0, 2)
    return pj


def _conv_chunk(x_ref, prev_ref, cw_ref, cb_ref, xbuf, q, first):
    @pl.when(first)
    def _():
        xbuf[0:8, :] = prev_ref[0]

    xbuf[8:8 + q, :] = x_ref[0]
    acc = cb_ref[...] + cw_ref[0:1, :] * xbuf[pl.ds(5, q), :]
    for j in range(1, 4):
        acc = acc + cw_ref[j:j + 1, :] * xbuf[pl.ds(5 + j, q), :]
    xbuf[0:8, :] = xbuf[q:q + 8, :]
    return acc


def _ssd_kernel(xbc_ref, z_ref, misc_ref, prev_ref, h0_ref, cw_ref, cb_ref, dtb_ref, alog_ref, dexp_ref, nw_ref,
                y_ref, hout_ref, xbuf, ht, *, q):
    c = pl.program_id(1)

    @pl.when(c == 0)
    def _():
        ht[...] = _dot_nt_01(_eye(SSD_STATE, SSD_STATE), h0_ref[0])

    acc = _conv_chunk(xbc_ref, prev_ref, cw_ref, cb_ref, xbuf, q, c == 0)
    xc = acc * jax.nn.sigmoid(acc)
    sx = xc[:, 0:256]
    sb = (xc[:, 256:384], xc[:, 384:512])
    sc = (xc[:, 512:640], xc[:, 640:768])
    dt = _softplus(misc_ref[0] + dtb_ref[...])
    da = dt * (-jnp.exp(alog_ref[...]))
    causal = _iota((q, q), 0) >= _iota((q, q), 1)
    acum = _dot_01(causal.astype(BF16), da)
    xsel = _eye(8, LANES)
    acum_t = _dot_nt_01(xsel, acum)
    dt_t = _dot_nt_01(xsel, dt)
    a_last = acum[q - 1:q, :]
    wlast = jnp.exp(a_last - acum) * dt
    ea = jnp.exp(acum)
    head = _iota((q, GROUP_W), 1) >> 6
    ht_old = ht[...]
    y = dexp_ref[...] * sx
    for g in range(2):
        cb = _dot_nt(sc[g], sb[g])
        for h in (2 * g, 2 * g + 1):
            seg = acum[:, h:h + 1] - acum_t[h:h + 1, :]
            decay = jnp.exp(jnp.where(causal, seg, NEG_INF))
            m = cb * decay * dt_t[h:h + 1, :]
            y_h = _dot(m, sx) + _dot(sc[g] * ea[:, h:h + 1], ht_old)
            y = y + jnp.where(head == h, y_h, 0.0)
    xw = sx * _head_bcast(wlast, head, SSD_HEADS)
    lane = _iota((q, GROUP_W), 1)
    bt0 = _dot_nt(_eye(SSD_STATE, SSD_STATE), sb[0])
    bt1 = _dot_nt(_eye(SSD_STATE, SSD_STATE), sb[1])
    head1 = _iota((1, GROUP_W), 1) >> 6
    dch = _head_bcast(jnp.exp(a_last), head1, SSD_HEADS)
    ht_new = dch * ht_old + _dot(bt0, jnp.where(lane < 128, xw, 0.0)) + _dot(bt1, jnp.where(lane < 128, 0.0, xw))
    ht[...] = ht_new
    zz = z_ref[0]
    y = y * (zz * jax.nn.sigmoid(zz))
    y_ref[0] = y * lax.rsqrt(jnp.mean(y * y, axis=-1, keepdims=True) + EPS) * nw_ref[...]

    @pl.when(c == pl.num_programs(1) - 1)
    def _():
        hout_ref[0] = _dot_nt_01(_eye(GROUP_W, GROUP_W), ht_new)


def _ssd(pj, nb, sl, prev8, h0, p):
    q = min(128, sl)
    nc = sl // q
    row = lambda w: pl.BlockSpec((1, q, w), lambda b, c: (b, c, 0))
    per_b = lambda r, w: pl.BlockSpec((1, r, w), lambda b, c: (b, 0, 0))
    const = lambda r, w: pl.BlockSpec((r, w), lambda b, c: (0, 0))
    y, hout = pl.pallas_call(
        functools.partial(_ssd_kernel, q=q),
        grid=(nb, nc),
        in_specs=[row(768), row(256), row(128), per_b(8, 768), per_b(256, 128),
                  const(8, 768), const(1, 768), const(1, 128), const(1, 128), const(1, 256), const(1, 256)],
        out_specs=[row(256), per_b(256, 128)],
        out_shape=[jax.ShapeDtypeStruct((nb, sl, 256), F32), jax.ShapeDtypeStruct((nb, 256, 128), F32)],
        scratch_shapes=[pltpu.VMEM((q + 8, 768), F32), pltpu.VMEM((SSD_STATE, GROUP_W), F32)],
        compiler_params=_params(("parallel", "arbitrary")),
        name="ssd",
    )(pj["xbc"].reshape(nb, sl, 768), pj["z"].reshape(nb, sl, 256), pj["misc"].reshape(nb, sl, 128), prev8, h0,
      p["ssd_cw"], p["ssd_cb"], p["ssd_dtb"], p["ssd_alog"], p["ssd_dexp"], p["ssd_nw"])
    return y.reshape(nb * sl, 256), hout


def _lru_kernel(x_ref, g_ref, prev_ref, h0_ref, cw_ref, cb_ref, wa_ref, ba_ref, wx_ref, bx_ref, lam_ref,
                y_ref, hout_ref, xbuf, hc, *, q):
    c = pl.program_id(1)

    @pl.when(c == 0)
    def _():
        hc[...] = h0_ref[0]

    xc = _conv_chunk(x_ref, prev_ref, cw_ref, cb_ref, xbuf, q, c == 0)
    xcb = xc.astype(BF16)
    r = jax.nn.sigmoid(jnp.dot(xcb, wa_ref[...], preferred_element_type=F32) + ba_ref[...])
    i = jax.nn.sigmoid(jnp.dot(xcb, wx_ref[...], preferred_element_type=F32) + bx_ref[...])
    log_a = -LRU_C * r * _softplus(-lam_ref[...])
    a = jnp.exp(log_a)
    u = jnp.sqrt(1.0 - jnp.exp(2.0 * log_a)) * (i * xc)
    row = _iota((q, GROUP_W), 0)
    s = 1
    while s < q:
        a_sh = pltpu.roll(a, s, 0)
        u_sh = pltpu.roll(u, s, 0)
        keep = row >= s
        u = jnp.where(keep, a * u_sh + u, u)
        a = jnp.where(keep, a * a_sh, a)
        s *= 2
    h = u + a * hc[...]
    hc[...] = h[q - 1:q, :]
    gg = g_ref[0]
    gelu = 0.5 * gg * (1.0 + jnp.tanh(math.sqrt(2.0 / math.pi) * (gg + 0.044715 * (gg * gg * gg))))
    y_ref[0] = h * gelu

    @pl.when(c == pl.num_programs(1) - 1)
    def _():
        hout_ref[0] = h[q - 1:q, :]


def _lru(pj, nb, sl, prev8, h0, p):
    q = min(256, sl)
    nc = sl // q
    row = lambda w: pl.BlockSpec((1, q, w), lambda b, c: (b, c, 0))
    per_b = lambda r, w: pl.BlockSpec((1, r, w), lambda b, c: (b, 0, 0))
    const = lambda r, w: pl.BlockSpec((r, w), lambda b, c: (0, 0))
    y, hout = pl.pallas_call(
        functools.partial(_lru_kernel, q=q),
        grid=(nb, nc),
        in_specs=[row(256), row(256), per_b(8, 256), per_b(1, 256), const(8, 256), const(1, 256),
                  const(256, 256), const(1, 256), const(256, 256), const(1, 256), const(1, 256)],
        out_specs=[row(256), per_b(1, 256)],
        out_shape=[jax.ShapeDtypeStruct((nb, sl, 256), F32), jax.ShapeDtypeStruct((nb, 1, 256), F32)],
        scratch_shapes=[pltpu.VMEM((q + 8, 256), F32), pltpu.VMEM((1, 256), F32)],
        compiler_params=_params(("parallel", "arbitrary")),
        name="lru",
    )(pj["lx"].reshape(nb, sl, 256), pj["lg"].reshape(nb, sl, 256), prev8, h0.reshape(nb, 1, 256),
      p["lru_cw"], p["lru_cb"], p["lru_wa"], p["lru_ba"], p["lru_wx"], p["lru_bx"], p["lru_lam"])
    return y.reshape(nb * sl, 256), hout.reshape(nb, 256)


def _summ_kernel(pt_ref, *refs, n_parts):
    kp, vp = refs[:n_parts], refs[n_parts:2 * n_parts]
    mk_ref, mv_ref, ko_ref, vo_ref = refs[2 * n_parts:]

    def one(parts, m_ref, o_ref):
        chunks = []
        for r in parts:
            x = r[...]
            chunks.append(x.reshape(x.shape[-2], x.shape[-1]).astype(BF16))
        x = jnp.concatenate(chunks, axis=1) if len(chunks) > 1 else chunks[0]
        top = jnp.dot(x[0:64], m_ref[0], preferred_element_type=F32)
        bot = jnp.dot(x[64:128], m_ref[1], preferred_element_type=F32)
        o_ref[0] = jnp.concatenate([top, bot], axis=0)

    one(kp, mk_ref, ko_ref)
    one(vp, mv_ref, vo_ref)


def _summarize(k_src, v_src, part_specs, part_w, nb, total, table, w_k, w_v):
    n_parts = len(part_specs)
    step_w = n_parts * part_w
    n_blk = step_w // NSA_CMP_BLOCK
    tok = jnp.arange(step_w, dtype=jnp.int32)
    in_block = tok[:, None] // NSA_CMP_BLOCK == jnp.arange(n_blk, dtype=jnp.int32)[None, :]
    weights = lambda w: jnp.where(in_block[None], w[:, tok % NSA_CMP_BLOCK][:, :, None], 0.0).astype(BF16)
    wspec = pl.BlockSpec((2, step_w, n_blk), lambda b, s, pt: (0, 0, 0))
    ospec = pl.BlockSpec((1, LANES, n_blk), lambda b, s, pt: (b, 0, s))
    oshape = jax.ShapeDtypeStruct((nb, LANES, total // NSA_CMP_BLOCK), F32)
    return pl.pallas_call(
        functools.partial(_summ_kernel, n_parts=n_parts),
        grid_spec=pltpu.PrefetchScalarGridSpec(
            num_scalar_prefetch=1, grid=(nb, total // step_w),
            in_specs=list(part_specs) * 2 + [wspec, wspec],
            out_specs=[ospec, ospec]),
        out_shape=[oshape, oshape],
        compiler_params=_params(("parallel", "arbitrary")),
        name="nsa_summarize",
    )(table, *([k_src] * n_parts), *([v_src] * n_parts), weights(w_k), weights(w_v))


def _cmpsel_t_kernel(q_ref, kct_ref, vct_ref, ocmp_ref, sel_ref, *, qt, nc, ns):
    qi = pl.program_id(1)
    q4 = _nsa_expand(q_ref[0] * 0.125)
    s = _dot_nt(kct_ref[0].T, q4)
    qpos = qi * qt + (_iota((nc, 4 * qt), 1) & (qt - 1))
    blk_end = (_iota((nc, 4 * qt), 0) + 1) * NSA_CMP_BLOCK - 1
    s = jnp.where(blk_end <= qpos, s, NEG_INF)
    m = jnp.max(s, axis=0, keepdims=True)
    m = jnp.where(m == NEG_INF, 0.0, m)
    e = jnp.exp(s - m)
    p = e / jnp.maximum(jnp.sum(e, axis=0, keepdims=True), 1e-30)
    ocmp_ref[0] = _collect_t(_dot(vct_ref[0], p), qt).T
    impc = jnp.concatenate([p[:, 0:qt] + p[:, qt:2 * qt], p[:, 2 * qt:3 * qt] + p[:, 3 * qt:4 * qt]], axis=1)
    pair = (_iota((ns, nc), 1) >> 1 == _iota((ns, nc), 0)).astype(BF16)
    imp = _dot_01(pair, impc)
    blk = _iota((ns, 2 * qt), 0)
    cur = (qi * qt + (_iota((ns, 2 * qt), 1) & (qt - 1))) >> 6
    forced = (blk == 0) | (blk == cur) | (blk == cur - 1)
    score = jnp.where(forced, FORCE_SCORE, jnp.where(blk > cur, -1.0, imp))
    cnt = jnp.zeros((ns, 2 * qt), F32)
    for j in range(ns):
        row = score[j:j + 1, :]
        cnt = cnt + ((row > score) | ((row == score) & (blk > j))).astype(F32)
    sel = (cnt < float(NSA_TOPN)).astype(F32)
    sel_ref[0, 0] = sel[:, 0:qt]
    sel_ref[0, 1] = sel[:, qt:2 * qt]


def _cmp_select_prompt(qn, kcmp_t, vcmp_t, nb, sl, qt):
    nc = kcmp_t.shape[2]
    ns = sl // NSA_SLC_BLOCK
    return pl.pallas_call(
        functools.partial(_cmpsel_t_kernel, qt=qt, nc=nc, ns=ns),
        grid=(nb, sl // qt),
        in_specs=[pl.BlockSpec((1, qt, 256), lambda b, i: (b, i, 0)),
                  pl.BlockSpec((1, LANES, nc), lambda b, i: (b, 0, 0)),
                  pl.BlockSpec((1, LANES, nc), lambda b, i: (b, 0, 0))],
        out_specs=[pl.BlockSpec((1, qt, 256), lambda b, i: (b, i, 0)),
                   pl.BlockSpec((1, 2, ns, qt), lambda b, i: (b, 0, 0, i))],
        out_shape=[jax.ShapeDtypeStruct((nb, sl, 256), F32), jax.ShapeDtypeStruct((nb, 2, ns, sl), F32)],
        compiler_params=_params(("parallel", "parallel")),
        name="nsa_cmp_select_prompt",
    )(qn.reshape(nb, sl, 256), kcmp_t, vcmp_t)


def _cmpsel_kernel(q_ref, kct_ref, vct_ref, ocmp_ref, selx_ref, *, qt, nc, ns, nsp, past, n_chunks):
    q4 = _nsa_expand(q_ref[0] * 0.125)
    s = _dot(q4, kct_ref[0])
    qpos = past + (_iota((4 * qt, nc), 0) & (qt - 1))
    blk_end = (_iota((4 * qt, nc), 1) + 1) * NSA_CMP_BLOCK - 1
    s = jnp.where(blk_end <= qpos, s, NEG_INF)
    m = jnp.max(s, axis=-1, keepdims=True)
    m = jnp.where(m == NEG_INF, 0.0, m)
    e = jnp.exp(s - m)
    p = e / jnp.maximum(jnp.sum(e, axis=-1, keepdims=True), 1e-30)
    ocmp_ref[0] = _nsa_collect(_dot_nt(p, vct_ref[0]), qt)
    impc = jnp.concatenate([p[0:qt] + p[qt:2 * qt], p[2 * qt:3 * qt] + p[3 * qt:4 * qt]], axis=0)
    pair = (_iota((nc, nsp), 0) >> 1 == _iota((nc, nsp), 1)).astype(BF16)
    imp = _dot_01_r(impc, pair)
    blk = _iota((2 * qt, nsp), 1)
    cur = (past + (_iota((2 * qt, nsp), 0) & (qt - 1))) >> 6
    forced = (blk == 0) | (blk == cur) | (blk == cur - 1)
    score = jnp.where(forced, FORCE_SCORE, jnp.where(blk > cur, -1.0, imp))
    score = jnp.where(blk < ns, score, -3.0)

    cnt = jnp.zeros((2 * qt, nsp), F32)
    for j in range(ns):
        col = score[:, j:j + 1]
        cnt = cnt + ((col > score) | ((col == score) & (blk > j))).astype(F32)
    sel = ((cnt < float(NSA_TOPN)) & (blk < ns)).astype(F32)
    sel4 = jnp.concatenate([sel[0:qt], sel[0:qt], sel[qt:2 * qt], sel[qt:2 * qt]], axis=0).astype(BF16)
    e16 = (_iota((LANES, 1024), 1) >> 6 == _iota((LANES, 1024), 0)).astype(BF16)
    for t in range(n_chunks):
        pick = ((_iota((nsp, LANES), 0) == 16 * t + _iota((nsp, LANES), 1))
                & (_iota((nsp, LANES), 1) < 16)).astype(BF16)
        blocks = jnp.dot(sel4, pick, preferred_element_type=F32).astype(BF16)
        selx_ref[0, :, 1024 * t:1024 * (t + 1)] = jnp.dot(blocks, e16, preferred_element_type=F32)


def _cmp_select_sample(qn, kcmp_t, vcmp_t, nb, sl, past):
    nc = kcmp_t.shape[2]
    ns = -(-(past + sl) // NSA_SLC_BLOCK)
    nsp = -(-ns // LANES) * LANES
    n_chunks = -(-(past + sl) // 1024)
    return pl.pallas_call(
        functools.partial(_cmpsel_kernel, qt=sl, nc=nc, ns=ns, nsp=nsp, past=past, n_chunks=n_chunks),
        grid=(nb,),
        in_specs=[pl.BlockSpec((1, sl, 256), lambda b: (b, 0, 0)),
                  pl.BlockSpec((1, LANES, nc), lambda b: (b, 0, 0)),
                  pl.BlockSpec((1, LANES, nc), lambda b: (b, 0, 0))],
        out_specs=[pl.BlockSpec((1, sl, 256), lambda b: (b, 0, 0)),
                   pl.BlockSpec((1, 4 * sl, 1024 * n_chunks), lambda b: (b, 0, 0))],
        out_shape=[jax.ShapeDtypeStruct((nb, sl, 256), F32),
                   jax.ShapeDtypeStruct((nb, 4 * sl, 1024 * n_chunks), F32)],
        compiler_params=_params(("parallel",)),
        name="nsa_cmp_select_sample",
    )(qn.reshape(nb, sl, 256), kcmp_t, vcmp_t)


def _nsap_kernel(qi_ref, kj_ref, q_ref, ks_ref, vst_ref, kw_ref, vwt_ref, sel_ref, oslc_ref, owin_ref,
                 q4_s, m1, l1, a1, m2, l2, a2, *, qt, kt):
    step = pl.program_id(1)
    qi, kj = qi_ref[step], kj_ref[step]
    last = (qi * qt + qt - 1) // kt
    first_w = jnp.maximum(qi * qt - (NSA_WINDOW - 1), 0) // kt

    @pl.when(kj == 0)
    def _():
        q4_s[...] = _nsa_expand(q_ref[0] * NSA_QSCALE).astype(BF16)
        _init_stats((m1, l1, a1), (m2, l2, a2))

    def positions():
        return kj * kt + _iota((kt, qt), 0), qi * qt + _iota((kt, qt), 1)

    def picked(g):
        rows = []
        for u in range(kt // NSA_SLC_BLOCK):
            r = sel_ref[0, g, pl.ds(kj * (kt // NSA_SLC_BLOCK) + u, 1), :]
            rows.append(jnp.broadcast_to(r, (NSA_SLC_BLOCK, qt)))
        return jnp.concatenate(rows, axis=0) > 0.5

    def update(k_ref, vt_ref, masks, m_, l_, a_):
        k_b, vt_b = k_ref[0].astype(BF16), vt_ref[0].astype(BF16)
        mask = None
        if masks[0] is not None:
            mask = jnp.concatenate([masks[0], masks[0], masks[1], masks[1]], axis=1)
        half = kt // 2
        for part in (slice(0, half), slice(half, kt)):
            s = lax.dot_general(k_b[part], q4_s[...], (((1,), (1,)), ((), ())), preferred_element_type=F32)
            _online_update_t(s, None if mask is None else mask[part], vt_b[:, part], m_, l_, a_)

    def slc_mask(with_causal):
        m0, m1_ = picked(0), picked(1)
        if with_causal:
            kpos, qpos = positions()
            m0, m1_ = m0 & (kpos <= qpos), m1_ & (kpos <= qpos)
        return jnp.concatenate([m0, m0, m1_, m1_], axis=1)

    def slc_scores():
        return lax.dot_general(ks_ref[0].astype(BF16), q4_s[...], (((1,), (1,)), ((), ())),
                               preferred_element_type=F32)

    @pl.when((kj == 0) & (last > 0))
    def _():
        _online_update_t(slc_scores(), slc_mask(False), vst_ref[0].astype(BF16), m1, l1, a1)

    @pl.when((kj == 0) & (last == 0))
    def _():
        _online_update_t(slc_scores(), slc_mask(True), vst_ref[0].astype(BF16), m1, l1, a1)

    @pl.when((kj > 0) & (kj < last))
    def _():
        _online_update_t_lazy(slc_scores, slc_mask(False), vst_ref[0].astype(BF16), m1, l1, a1)

    @pl.when((kj > 0) & (kj == last))
    def _():
        _online_update_t_lazy(slc_scores, slc_mask(True), vst_ref[0].astype(BF16), m1, l1, a1)

    full_w = (kj < last) & (kj * kt > qi * qt + qt - 1 - NSA_WINDOW)

    @pl.when((kj >= first_w) & full_w)
    def _():
        update(kw_ref, vwt_ref, (None, None), m2, l2, a2)

    @pl.when((kj >= first_w) & jnp.logical_not(full_w))
    def _():
        kpos, qpos = positions()
        band = (kpos <= qpos) & (kpos > qpos - NSA_WINDOW)
        update(kw_ref, vwt_ref, (band, band), m2, l2, a2)

    @pl.when(kj == last)
    def _():
        oslc_ref[0] = _collect_t(a1[...] / jnp.maximum(l1[...], 1e-30), qt).T
        owin_ref[0] = _collect_t(a2[...] / jnp.maximum(l2[...], 1e-30), qt).T


def _nsa_prompt_attn(pj, sel, nb, sl):
    qt, kt = min(256, sl), min(512, sl)
    assert kt % qt == 0
    qi_tab, kj_tab = _causal_pairs(sl // qt, qt, kt)
    first_w = lambda i: jnp.maximum(i * qt - (NSA_WINDOW - 1), 0) // kt
    kspec = pl.BlockSpec((1, kt, LANES), lambda b, s, qi, kj: (b, kj[s], 0))
    vspec = pl.BlockSpec((1, LANES, kt), lambda b, s, qi, kj: (b, 0, kj[s]))
    kwspec = pl.BlockSpec((1, kt, LANES), lambda b, s, qi, kj: (b, jnp.maximum(kj[s], first_w(qi[s])), 0))
    vwspec = pl.BlockSpec((1, LANES, kt), lambda b, s, qi, kj: (b, 0, jnp.maximum(kj[s], first_w(qi[s]))))
    qspec = pl.BlockSpec((1, qt, 256), lambda b, s, qi, kj: (b, qi[s], 0))
    ns = sel.shape[2]
    r3 = lambda a, w: a.reshape(nb, sl, w)
    stats = [pltpu.VMEM((1, 4 * qt), F32), pltpu.VMEM((1, 4 * qt), F32), pltpu.VMEM((LANES, 4 * qt), F32)]
    return pl.pallas_call(
        functools.partial(_nsap_kernel, qt=qt, kt=kt),
        grid_spec=pltpu.PrefetchScalarGridSpec(
            num_scalar_prefetch=2, grid=(nb, qi_tab.shape[0]),
            in_specs=[qspec, kspec, vspec, kwspec, vwspec,
                      pl.BlockSpec((1, 2, ns, qt), lambda b, s, qi, kj: (b, 0, 0, qi[s]))],
            out_specs=[qspec, qspec],
            scratch_shapes=[pltpu.VMEM((4 * qt, LANES), BF16)] + stats * 2),
        out_shape=[jax.ShapeDtypeStruct((nb, sl, 256), F32)] * 2,
        compiler_params=_params(("parallel", "arbitrary")),
        name="nsa_prompt_attn",
    )(qi_tab, kj_tab, r3(pj["qnr"], 256), r3(pj["ks"], 128), pj["vsT"], r3(pj["kw"], 128), pj["vwT"], sel)


def _nsa_gate(misc, oc, os_, ow):
    g = jax.nn.sigmoid(misc)
    head = _iota(oc.shape, 1) >> 6
    return (_head_bcast(g, head, 4, GATE_LANE0) * oc + _head_bcast(g, head, 4, GATE_LANE0 + 4) * os_
            + _head_bcast(g, head, 4, GATE_LANE0 + 8) * ow)


def _diff_lambda(lamv_ref, lam_init):
    lv = lamv_ref[...]
    s1 = jnp.sum(lv[0:1] * lv[1:2], axis=1, keepdims=True)
    s2 = jnp.sum(lv[2:3] * lv[3:4], axis=1, keepdims=True)
    return jnp.exp(s1) - jnp.exp(s2) + lam_init


def _diff_finish(o, head, nheads, nw, lam_init):
    inv = jnp.zeros(o.shape, F32)
    for h in range(nheads):
        ms = jnp.sum(jnp.where(head == h, o * o, 0.0), axis=-1, keepdims=True) * (1.0 / 64.0)
        inv = jnp.where(head == h, lax.rsqrt(ms + EPS), inv)
    return o * inv * nw * (1.0 - lam_init)


def _diffp_kernel(qi_ref, kj_ref, q_ref, k_ref, vt_ref, lamv_ref, nwt_ref, o_ref, q4_s, m_s, l_s, acc_s,
                  *, qt, kt, lam_init):
    step = pl.program_id(2)
    qi, kj = qi_ref[step], kj_ref[step]
    last = (qi * qt + qt - 1) // kt

    @pl.when(kj == 0)
    def _():
        q = q_ref[0] * DIFF_QSCALE
        part = _iota((qt, LANES), 1) >> 5
        q4_s[...] = jnp.concatenate([jnp.where(part == k, q, 0.0) for k in range(4)], axis=0).astype(BF16)
        _init_stats((m_s, l_s, acc_s))

    def scores():
        return lax.dot_general(k_ref[0].astype(BF16), q4_s[...], (((1,), (1,)), ((), ())),
                               preferred_element_type=F32)

    def causal():
        mask = kj * kt + _iota((kt, qt), 0) <= qi * qt + _iota((kt, qt), 1)
        return jnp.concatenate([mask] * 4, axis=1)

    @pl.when((kj == 0) & (last > 0))
    def _():
        _online_update_t(scores(), None, vt_ref[0].astype(BF16), m_s, l_s, acc_s)

    @pl.when((kj == 0) & (last == 0))
    def _():
        _online_update_t(scores(), causal(), vt_ref[0].astype(BF16), m_s, l_s, acc_s)

    @pl.when((kj > 0) & (kj < last))
    def _():
        _online_update_t_lazy(scores, None, vt_ref[0].astype(BF16), m_s, l_s, acc_s)

    @pl.when((kj > 0) & (kj == last))
    def _():
        _online_update_t_lazy(scores, causal(), vt_ref[0].astype(BF16), m_s, l_s, acc_s)

    @pl.when(kj == last)
    def _():
        lam = _diff_lambda(lamv_ref, lam_init)
        o = acc_s[...] / jnp.maximum(l_s[...], 1e-30)
        halves = []
        for h in range(2):
            rows = slice(64 * h, 64 * (h + 1))
            oh = o[rows, 2 * h * qt:(2 * h + 1) * qt] - lam * o[rows, (2 * h + 1) * qt:(2 * h + 2) * qt]
            ms = jnp.mean(oh * oh, axis=0, keepdims=True)
            halves.append(oh * lax.rsqrt(ms + EPS))
        o_ref[0] = (jnp.concatenate(halves, axis=0) * nwt_ref[...] * (1.0 - lam_init)).T


def _diff_prompt(pj, nb, sl, p, lam_init):
    qt, kt = min(256, sl), min(512, sl)
    assert kt % qt == 0
    qi_tab, kj_tab = _causal_pairs(sl // qt, qt, kt)
    qspec = pl.BlockSpec((1, qt, LANES), lambda b, h, s, qi, kj: (b, qi[s], h))
    kspec = pl.BlockSpec((1, kt, LANES), lambda b, h, s, qi, kj: (b, kj[s], h))
    vspec = pl.BlockSpec((1, LANES, kt), lambda b, h, s, qi, kj: (b, h, kj[s]))
    r3 = lambda a: a.reshape(nb, sl, 256)
    out = pl.pallas_call(
        functools.partial(_diffp_kernel, qt=qt, kt=kt, lam_init=lam_init),
        grid_spec=pltpu.PrefetchScalarGridSpec(
            num_scalar_prefetch=2, grid=(nb, 2, qi_tab.shape[0]),
            in_specs=[qspec, kspec, vspec, pl.BlockSpec((4, 32), lambda b, h, s, qi, kj: (0, 0)),
                      pl.BlockSpec((LANES, 1), lambda b, h, s, qi, kj: (0, 0))],
            out_specs=qspec,
            scratch_shapes=[pltpu.VMEM((4 * qt, LANES), BF16), pltpu.VMEM((1, 4 * qt), F32),
                            pltpu.VMEM((1, 4 * qt), F32), pltpu.VMEM((LANES, 4 * qt), F32)]),
        out_shape=jax.ShapeDtypeStruct((nb, sl, 256), F32),
        compiler_params=_params(("parallel", "parallel", "arbitrary")),
        name="diff_prompt",
    )(qi_tab, kj_tab, r3(pj["dq"]), r3(pj["dk"]), pj["dvT"], p["diff_lambda"],
      p["diff_nw256"][:, :LANES].reshape(LANES, 1))
    return out.reshape(nb * sl, 256)


def _paged_kernel(pt_ref, q_ref, *refs, mode, pps, width, kbase, past, new_len, lam_init):
    kp, vp = refs[:pps], refs[pps:2 * pps]
    knew_ref, vnew_ref = refs[2 * pps:2 * pps + 2]
    rest = list(refs[2 * pps + 2:])
    selx_ref = selnew_ref = lamv_ref = nw_ref = None
    if mode == "slc":
        selx_ref, selnew_ref = rest[:2]
        rest = rest[2:]
    if mode == "diff":
        lamv_ref, nw_ref = rest[:2]
        rest = rest[2:]
    o_ref, qe_s, m_s, l_s, acc_s = rest
    nq = new_len
    rows = qe_s.shape[0]
    step = pl.program_id(1)

    @pl.when(step == 0)
    def _():
        if mode == "diff":
            q = q_ref[0] * DIFF_QSCALE
            part = _iota((nq, width), 1) >> 5
            qe = jnp.concatenate([jnp.where(part == k, q, 0.0) for k in range(8)], axis=0)
        else:
            qe = _nsa_expand(q_ref[0] * NSA_QSCALE)
        qe_s[...] = qe.astype(BF16)
        _init_stats((m_s, l_s, acc_s))

    nk = pps * PAGE
    kt_b = jnp.concatenate([r[0, 0].astype(BF16) for r in kp], axis=1)
    vt_b = jnp.concatenate([r[0, 0].astype(BF16) for r in vp], axis=1)
    s = jnp.dot(qe_s[...], kt_b, preferred_element_type=F32)
    qpos = past + (_iota((rows, nk), 0) & (nq - 1))
    kpos = kbase + step * nk + _iota((rows, nk), 1)
    mask = kpos <= qpos
    if mode == "win":
        mask = mask & (kpos > qpos - NSA_WINDOW)
    if mode == "slc":
        mask = mask & (selx_ref[0] > 0.5)
    _online_update(s, mask, vt_b, m_s, l_s, acc_s, v_transposed=True)

    @pl.when(step == pl.num_programs(1) - 1)
    def _():
        s2 = lax.dot_general(qe_s[...], knew_ref[0].astype(BF16), (((1,), (1,)), ((), ())),
                             preferred_element_type=F32)
        qpos2 = past + (_iota((rows, nq), 0) & (nq - 1))
        kpos2 = past + _iota((rows, nq), 1)
        mask2 = kpos2 <= qpos2
        if mode == "win":
            mask2 = mask2 & (kpos2 > qpos2 - NSA_WINDOW)
        if mode == "slc":
            mask2 = mask2 & (selnew_ref[0][:, 0:nq] > 0.5)
        _online_update(s2, mask2, vnew_ref[0].astype(BF16), m_s, l_s, acc_s)
        o = acc_s[...] / jnp.maximum(l_s[...], 1e-30)
        if mode == "diff":
            lam = _diff_lambda(lamv_ref, lam_init)
            head = _iota((nq, width), 1) >> 6
            out = jnp.zeros((nq, width), F32)
            for h in range(4):
                o0 = o[(2 * h) * nq:(2 * h + 1) * nq]
                o1 = o[(2 * h + 1) * nq:(2 * h + 2) * nq]
                out = jnp.where(head == h, o0 - lam * o1, out)
            o_ref[0] = _diff_finish(out, head, 4, nw_ref[...], lam_init)
        else:
            o_ref[0] = _nsa_collect(o, nq)


def _paged_attn(mode, q, k_src, v_src, page_spec, n_pages, table, knew, vnew, kbase, past, extra=(), lam_init=0.0):
    nb, nq = q.shape[0], q.shape[1]
    width = knew.shape[-1]
    pps = min(PAGES_PER_STEP, n_pages)
    rows = 8 * nq if mode == "diff" else 4 * nq
    nk = pps * PAGE
    per_b = lambda r_, w: pl.BlockSpec((1, r_, w), lambda b, s, pt: (b, 0, 0))
    in_specs = [per_b(nq, 256)] + [page_spec(r, pps) for r in range(pps)] * 2 + [per_b(nq, width), per_b(nq, width)]
    args = [q] + [k_src] * pps + [v_src] * pps + [knew, vnew]
    if mode == "slc":
        (selx,) = extra
        new_blk = (past - kbase) // LANES
        in_specs += [pl.BlockSpec((1, rows, nk), lambda b, s, pt: (b, 0, s)),
                     pl.BlockSpec((1, rows, LANES), lambda b, s, pt: (b, 0, new_blk))]
        args += [selx, selx]
    if mode == "diff":
        in_specs += [pl.BlockSpec((4, 32), lambda b, s, pt: (0, 0)), pl.BlockSpec((1, 256), lambda b, s, pt: (0, 0))]
        args += list(extra)
    return pl.pallas_call(
        functools.partial(_paged_kernel, mode=mode, pps=pps, width=width, kbase=kbase, past=past, new_len=nq,
                          lam_init=lam_init),
        grid_spec=pltpu.PrefetchScalarGridSpec(
            num_scalar_prefetch=1, grid=(nb, n_pages // pps), in_specs=in_specs,
            out_specs=per_b(nq, 256),
            scratch_shapes=[pltpu.VMEM((rows, width), BF16), pltpu.VMEM((rows, 1), F32),
                            pltpu.VMEM((rows, 1), F32), pltpu.VMEM((rows, width), F32)]),
        out_shape=jax.ShapeDtypeStruct((nb, nq, 256), F32),
        compiler_params=_params(("parallel", "arbitrary")),
        name="paged_" + mode,
    )(table, *args)


def _page_copy(src_hbm, layer, page, buf, slot, r, sem):
    return pltpu.make_async_copy(src_hbm.at[layer, page], buf.at[slot, r], sem.at[slot])


def _start_pages(srcs, bufs, sems, pt_ref, layer, b, first, slot, n):
    for r in range(n):
        page = pt_ref[b, first + r]
        for src, buf, sem in zip(srcs, bufs, sems):
            _page_copy(src, layer, page, buf, slot, r, sem).start()


def _wait_pages(srcs, bufs, sems, layer, slot, n):
    for r in range(n):
        for src, buf, sem in zip(srcs, bufs, sems):
            _page_copy(src, layer, 0, buf, slot, r, sem).wait()


def _page_ring(pt_ref, srcs, bufs, sems, layer, pps, n_steps):
    b, nb = pl.program_id(0), pl.num_programs(0)
    slot_of = lambda s: (b * n_steps + s) & 1

    @pl.when(b == 0)
    def _():
        _start_pages(srcs, bufs, sems, pt_ref, layer, 0, 0, 0, pps)

    def advance(s):
        nxt = 1 - slot_of(s)
        if s + 1 < n_steps:
            _start_pages(srcs, bufs, sems, pt_ref, layer, b, (s + 1) * pps, nxt, pps)
        else:
            @pl.when(b + 1 < nb)
            def _():
                _start_pages(srcs, bufs, sems, pt_ref, layer, b + 1, 0, nxt, pps)
        _wait_pages(srcs, bufs, sems, layer, slot_of(s), pps)

    return slot_of, advance


def _paged2_kernel(pt_ref, q_ref, k_hbm, v_hbm, knew_ref, vnew_ref, *rest, mode, layer, pps, n_steps, width, past,
                   new_len, lam_init):
    selx_ref = lamv_ref = nw_ref = None
    if mode == "slc":
        selx_ref, rest = rest[0], rest[1:]
    if mode == "diff":
        lamv_ref, nw_ref, rest = rest[0], rest[1], rest[2:]
    o_ref, kbuf, vbuf, ksem, vsem, qe_s, m_s, l_s, acc_s = rest
    nq = new_len
    rows = qe_s.shape[0]
    nk = pps * PAGE
    slot_of, advance = _page_ring(pt_ref, (k_hbm, v_hbm), (kbuf, vbuf), (ksem, vsem), layer, pps, n_steps)

    if mode == "diff":
        q = q_ref[0] * DIFF_QSCALE
        part = _iota((nq, width), 1) >> 5
        qe = jnp.concatenate([jnp.where(part == k, q, 0.0) for k in range(8)], axis=0)
    else:
        qe = _nsa_expand(q_ref[0] * NSA_QSCALE)
    qe_s[...] = qe.astype(BF16)
    _init_stats((m_s, l_s, acc_s))

    for s in range(n_steps):
        advance(s)
        slot = slot_of(s)
        kt_b = jnp.concatenate([kbuf[slot, r].astype(BF16) for r in range(pps)], axis=1)
        vt_b = jnp.concatenate([vbuf[slot, r].astype(BF16) for r in range(pps)], axis=1)
        sc = jnp.dot(qe_s[...], kt_b, preferred_element_type=F32)
        qpos = past + (_iota((rows, nk), 0) & (nq - 1))
        kpos = s * nk + _iota((rows, nk), 1)
        mask = kpos <= qpos
        if mode == "slc":
            mask = mask & (selx_ref[0, :, s * nk:(s + 1) * nk] > 0.5)
        _online_update(sc, mask, vt_b, m_s, l_s, acc_s, v_transposed=True)

    s2 = lax.dot_general(qe_s[...], knew_ref[0].astype(BF16), (((1,), (1,)), ((), ())), preferred_element_type=F32)
    qpos2 = past + (_iota((rows, nq), 0) & (nq - 1))
    kpos2 = past + _iota((rows, nq), 1)
    mask2 = kpos2 <= qpos2
    if mode == "slc":
        mask2 = mask2 & (selx_ref[0, :, past:past + nq] > 0.5)
    _online_update(s2, mask2, vnew_ref[0].astype(BF16), m_s, l_s, acc_s)
    o = acc_s[...] / jnp.maximum(l_s[...], 1e-30)
    if mode == "diff":
        lam = _diff_lambda(lamv_ref, lam_init)
        head = _iota((nq, width), 1) >> 6
        out = jnp.zeros((nq, width), F32)
        for h in range(4):
            out = jnp.where(head == h, o[(2 * h) * nq:(2 * h + 1) * nq] - lam * o[(2 * h + 1) * nq:(2 * h + 2) * nq],
                            out)
        o_ref[0] = _diff_finish(out, head, 4, nw_ref[...], lam_init)
    else:
        o_ref[0] = _nsa_collect(o, nq)


def _paged2_attn(mode, q, k_src, v_src, layer, table, knew, vnew, past, extra=(), lam_init=0.0):
    nb, nq = q.shape[0], q.shape[1]
    n_pages = table.shape[1]
    width = knew.shape[-1]
    pps = min(PAGES_PER_STEP, n_pages)
    n_steps = n_pages // pps
    assert n_pages % pps == 0 and past == n_pages * PAGE
    rows = 8 * nq if mode == "diff" else 4 * nq
    per_b = lambda r_, w: pl.BlockSpec((1, r_, w), lambda b, pt: (b, 0, 0))
    hbm = pl.BlockSpec(memory_space=pl.ANY)
    in_specs = [per_b(nq, 256), hbm, hbm, per_b(nq, width), per_b(nq, width)]
    args = [q, k_src, v_src, knew, vnew]
    if mode == "slc":
        (selx,) = extra
        in_specs.append(per_b(rows, selx.shape[-1]))
        args.append(selx)
    if mode == "diff":
        in_specs += [pl.BlockSpec((4, 32), lambda b, pt: (0, 0)), pl.BlockSpec((1, 256), lambda b, pt: (0, 0))]
        args += list(extra)
    return pl.pallas_call(
        functools.partial(_paged2_kernel, mode=mode, layer=layer, pps=pps, n_steps=n_steps, width=width, past=past,
                          new_len=nq, lam_init=lam_init),
        grid_spec=pltpu.PrefetchScalarGridSpec(
            num_scalar_prefetch=1, grid=(nb,), in_specs=in_specs, out_specs=per_b(nq, 256),
            scratch_shapes=[pltpu.VMEM((2, pps, width, PAGE), F32), pltpu.VMEM((2, pps, width, PAGE), F32),
                            pltpu.SemaphoreType.DMA((2,)), pltpu.SemaphoreType.DMA((2,)),
                            pltpu.VMEM((rows, width), BF16), pltpu.VMEM((rows, 1), F32),
                            pltpu.VMEM((rows, 1), F32), pltpu.VMEM((rows, width), F32)]),
        out_shape=jax.ShapeDtypeStruct((nb, nq, 256), F32),
        compiler_params=_params(("arbitrary",)),
        name="paged2_" + mode,
    )(table, *args)


def _summ2_kernel(pt_ref, k_hbm, v_hbm, mk_ref, mv_ref, ko_ref, vo_ref, kbuf, vbuf, ksem, vsem, *, layer, pps, n_steps):
    slot_of, advance = _page_ring(pt_ref, (k_hbm, v_hbm), (kbuf, vbuf), (ksem, vsem), layer, pps, n_steps)
    n_blk = pps * PAGE // NSA_CMP_BLOCK
    for s in range(n_steps):
        advance(s)
        slot = slot_of(s)
        for buf, m_ref, o_ref in ((kbuf, mk_ref, ko_ref), (vbuf, mv_ref, vo_ref)):
            x = jnp.concatenate([buf[slot, r].astype(BF16) for r in range(pps)], axis=1)
            top = jnp.dot(x[0:64], m_ref[0], preferred_element_type=F32)
            bot = jnp.dot(x[64:128], m_ref[1], preferred_element_type=F32)
            o_ref[0, :, s * n_blk:(s + 1) * n_blk] = jnp.concatenate([top, bot], axis=0)


def _summarize_pages(k_src, v_src, layer, table, w_k, w_v):
    nb, n_pages = table.shape
    pps = min(PAGES_PER_STEP, n_pages)
    n_steps = n_pages // pps
    step_w = pps * PAGE
    n_blk = step_w // NSA_CMP_BLOCK
    tok = jnp.arange(step_w, dtype=jnp.int32)
    in_block = tok[:, None] // NSA_CMP_BLOCK == jnp.arange(n_blk, dtype=jnp.int32)[None, :]
    weights = lambda w: jnp.where(in_block[None], w[:, tok % NSA_CMP_BLOCK][:, :, None], 0.0).astype(BF16)
    hbm = pl.BlockSpec(memory_space=pl.ANY)
    wspec = pl.BlockSpec((2, step_w, n_blk), lambda b, pt: (0, 0, 0))
    ospec = pl.BlockSpec((1, LANES, n_steps * n_blk), lambda b, pt: (b, 0, 0))
    oshape = jax.ShapeDtypeStruct((nb, LANES, n_steps * n_blk), F32)
    return pl.pallas_call(
        functools.partial(_summ2_kernel, layer=layer, pps=pps, n_steps=n_steps),
        grid_spec=pltpu.PrefetchScalarGridSpec(
            num_scalar_prefetch=1, grid=(nb,), in_specs=[hbm, hbm, wspec, wspec], out_specs=[ospec, ospec],
            scratch_shapes=[pltpu.VMEM((2, pps, LANES, PAGE), F32), pltpu.VMEM((2, pps, LANES, PAGE), F32),
                            pltpu.SemaphoreType.DMA((2,)), pltpu.SemaphoreType.DMA((2,))]),
        out_shape=[oshape, oshape],
        compiler_params=_params(("arbitrary",)),
        name="nsa_summarize_pages",
    )(table, k_src, v_src, weights(w_k), weights(w_v))


def _post_kernel(x_ref, ya_ref, misc_ref, oc_ref, os_ref, ow_ref, yc_ref, yd_ref, wo_ref, g1_ref, b1_ref,
                 w1_ref, w2_ref, g2_ref, b2_ref, o_ref, x1_s, x1b_s, acc_s, *, alpha):
    f = pl.program_id(1)

    @pl.when(f == 0)
    def _():
        yb = _nsa_gate(misc_ref[...], oc_ref[...], os_ref[...], ow_ref[...])
        mix = _dot(ya_ref[...], wo_ref[0:256, :])
        mix = mix + _dot(yb, wo_ref[256:512, :])
        mix = mix + _dot(yc_ref[...], wo_ref[512:768, :])
        mix = mix + _dot(yd_ref[...], wo_ref[768:1024, :])
        x1 = _layer_norm(alpha * x_ref[...] + mix, g1_ref[...], b1_ref[...])
        x1_s[...] = x1
        x1b_s[...] = x1.astype(BF16)
        acc_s[...] = jnp.zeros(acc_s.shape, F32)

    h = jnp.dot(x1b_s[...], w1_ref[...], preferred_element_type=F32)
    h = jnp.square(jnp.maximum(h, 0.0))
    acc_s[...] += jnp.dot(h.astype(BF16), w2_ref[...], preferred_element_type=F32)

    @pl.when(f == pl.num_programs(1) - 1)
    def _():
        o_ref[...] = _layer_norm(alpha * x1_s[...] + acc_s[...], g2_ref[...], b2_ref[...])


def _post(x2d, ya, misc, oc, os_, ow, yc, yd, p, tm, alpha):
    t = x2d.shape[0]
    tf = 1024
    row = lambda w: pl.BlockSpec((tm, w), lambda i, f: (i, 0))
    const = lambda r, w: pl.BlockSpec((r, w), lambda i, f: (0, 0))
    return pl.pallas_call(
        functools.partial(_post_kernel, alpha=alpha),
        grid=(t // tm, D_FF // tf),
        in_specs=[row(D_MODEL), row(256), row(LANES), row(256), row(256), row(256), row(256), row(256),
                  const(D_MODEL, D_MODEL), const(1, D_MODEL), const(1, D_MODEL),
                  pl.BlockSpec((D_MODEL, tf), lambda i, f: (0, f)), pl.BlockSpec((tf, D_MODEL), lambda i, f: (f, 0)),
                  const(1, D_MODEL), const(1, D_MODEL)],
        out_specs=row(D_MODEL),
        out_shape=jax.ShapeDtypeStruct((t, D_MODEL), F32),
        scratch_shapes=[pltpu.VMEM((tm, D_MODEL), F32), pltpu.VMEM((tm, D_MODEL), BF16),
                        pltpu.VMEM((tm, D_MODEL), F32)],
        compiler_params=_params(("parallel", "arbitrary")),
        name="post",
    )(x2d, ya, misc, oc.reshape(t, 256), os_.reshape(t, 256), ow.reshape(t, 256), yc, yd,
      p["w_out"], p["ln1_g"], p["ln1_b"], p["w_ff1"], p["w_ff2"], p["ln2_g"], p["ln2_b"])


def _prep_params(w, l):
    w_in = w["w_in"][l]
    misc = jnp.concatenate([w_in[:, a:b] for a, b in MISC_COLS], axis=1)
    misc = jnp.pad(misc, ((0, 0), (0, LANES - misc.shape[1])))
    w_in_p = jnp.concatenate([w_in[:, a:b] for a, b in W_IN_ORDER] + [misc], axis=1).astype(BF16)
    row = lambda v: v.reshape(1, -1).astype(F32)
    pad_lanes = lambda v: jnp.pad(v.reshape(1, -1), ((0, 0), (0, LANES - v.shape[-1])))
    blockdiag = lambda m: jax.scipy.linalg.block_diag(*[m[i] for i in range(m.shape[0])]).astype(BF16)
    return dict(
        w_in=w_in_p,
        ssd_cw=jnp.pad(w["ssd_conv_w"][l], ((0, 4), (0, 0))), ssd_cb=row(w["ssd_conv_b"][l]),
        ssd_dtb=pad_lanes(w["ssd_dt_bias"][l]), ssd_alog=pad_lanes(w["ssd_A_log"][l]),
        ssd_dexp=row(jnp.repeat(w["ssd_D"][l], 64)), ssd_nw=row(w["ssd_norm_w"][l]),
        cmp_wk=w["nsa_w_cmp_k"][l], cmp_wv=w["nsa_w_cmp_v"][l],
        diff_lambda=w["diff_lambda"][l], diff_nw256=row(jnp.tile(w["diff_norm_w"][l], 4)),
        lru_cw=jnp.pad(w["lru_conv_w"][l], ((0, 4), (0, 0))), lru_cb=row(w["lru_conv_b"][l]),
        lru_wa=blockdiag(w["lru_w_a"][l]), lru_ba=row(w["lru_b_a"][l]),
        lru_wx=blockdiag(w["lru_w_x"][l]), lru_bx=row(w["lru_b_x"][l]), lru_lam=row(w["lru_lambda"][l]),
        w_out=w["w_out"][l].astype(BF16), ln1_g=row(w["ln1_g"][l]), ln1_b=row(w["ln1_b"][l]),
        w_ff1=w["w_ff1"][l].astype(BF16), w_ff2=w["w_ff2"][l].astype(BF16),
        ln2_g=row(w["ln2_g"][l]), ln2_b=row(w["ln2_b"][l]),
    )


def _pad_prev(prev):
    return jnp.pad(prev, ((0, 0), (5, 0), (0, 0)))


def _token_minor(a):
    a = jnp.moveaxis(a, -3, -1)
    return a.reshape(*a.shape[:-3], a.shape[-3] * a.shape[-2], a.shape[-1])


def _token_major(a_t, heads):
    nb, _, n = a_t.shape
    return jnp.moveaxis(a_t.reshape(nb, heads, 64, n), -1, 1)


def _layer(x2d, nb, sl, past, hist, p, lam_init, alpha):
    t = nb * sl
    tm = min(256, t)
    pos = past + jnp.arange(sl, dtype=jnp.int32)
    pj = _proj(x2d, p["w_in"], pos, nb, sl, tm)
    y_a, ssd_h = _ssd(pj, nb, sl, _pad_prev(hist["ssd_conv"]), hist["ssd_h"].reshape(nb, 256, 128), p)
    y_d, lru_h = _lru(pj, nb, sl, _pad_prev(hist["lru_conv"]), hist["lru_h"], p)
    r3 = lambda a, w: a.reshape(nb, sl, w)
    dummy = jnp.zeros((1, 1), jnp.int32)
    if past == 0:
        step_w = min(4096, sl)
        spec = pl.BlockSpec((1, LANES, step_w), lambda b, s, pt: (b, 0, s))
        kcmp_t, vcmp_t = _summarize(pj["kcT"], pj["vcT"], [spec], step_w, nb, sl, dummy, p["cmp_wk"], p["cmp_wv"])
        o_cmp, sel = _cmp_select_prompt(pj["qn"], kcmp_t, vcmp_t, nb, sl, min(256, sl))
        o_slc, o_win = _nsa_prompt_attn(pj, sel, nb, sl)
        y_c = _diff_prompt(pj, nb, sl, p, lam_init)
        kw_t, vw_t = pj["kwT"][:, :, sl - NSA_WINDOW:], pj["vwT"][:, :, sl - NSA_WINDOW:]
        new = dict(kc=_token_major(pj["kcT"], 2), vc=_token_major(pj["vcT"], 2), ks=_token_major(pj["ksT"], 2),
                   vs=_token_major(pj["vsT"], 2), dk=_token_major(pj["dkT"], 4), dv=_token_major(pj["dvT"], 4))
    else:
        table, l = hist["table"], hist["layer"]
        n_pages = table.shape[1]
        win_spec = lambda r, per: pl.BlockSpec(
            (1, 1, LANES, PAGE), lambda b, s, pt, r=r, per=per: (l, b, 0, s * per + r))
        kcmp_t, vcmp_t = _summarize_pages(hist["kc"], hist["vc"], l, table, p["cmp_wk"], p["cmp_wv"])
        o_cmp, selx = _cmp_select_sample(pj["qn"], kcmp_t, vcmp_t, nb, sl, past)
        qnr = r3(pj["qnr"], 256)
        o_slc = _paged2_attn("slc", qnr, hist["ks"], hist["vs"], l, table, r3(pj["ks"], 128), r3(pj["vs"], 128),
                             past, extra=(selx,))
        wbuf = hist["kw"].shape[-1]
        o_win = _paged_attn("win", qnr, hist["kw"], hist["vw"], win_spec, wbuf // PAGE, dummy,
                            r3(pj["kw"], 128), r3(pj["vw"], 128), past - wbuf, past)
        y_c = _paged2_attn("diff", r3(pj["dq"], 256), hist["dk"], hist["dv"], l, table, r3(pj["dk"], 256),
                           r3(pj["dv"], 256), past, extra=(p["diff_lambda"], p["diff_nw256"]),
                           lam_init=lam_init).reshape(t, 256)
        kw_t = jnp.concatenate([hist["kw"][l], pj["kwT"]], axis=2)[:, :, -NSA_WINDOW:]
        vw_t = jnp.concatenate([hist["vw"][l], pj["vwT"]], axis=2)[:, :, -NSA_WINDOW:]
        kv = lambda a: r3(a, 128).reshape(nb, sl, 2, 64)
        new = dict(kc=kv(pj["kc"]), vc=kv(pj["vc"]), ks=kv(pj["ks"]), vs=kv(pj["vs"]),
                   dk=r3(pj["dk"], 256).reshape(nb, sl, 4, 64), dv=r3(pj["dv"], 256).reshape(nb, sl, 4, 64))
    x_out = _post(x2d, y_a, pj["misc"], o_cmp, o_slc, o_win, y_c, y_d, p, min(512, t), alpha)
    new.update(kw=_token_major(kw_t, 2), vw=_token_major(vw_t, 2),
               ssd_h=ssd_h.reshape(nb, 4, 64, 128), ssd_conv=r3(pj["xbc"], 768)[:, sl - 3:],
               lru_h=lru_h, lru_conv=r3(pj["lx"], 256)[:, sl - 3:])
    return x_out, new


STATE_ORDER = ("kc", "vc", "ks", "vs", "dk", "dv", "kw", "vw", "ssd_h", "ssd_conv", "lru_h", "lru_conv")


def kernel(x_prompt, x_sample, cache_nsa_k_cmp, cache_nsa_v_cmp, cache_nsa_k_slc, cache_nsa_v_slc, cache_diff_k, cache_diff_v, cache_nsa_k_win, cache_nsa_v_win, state_ssd, state_ssd_conv, state_lru, state_lru_conv, page_table, w_in, ssd_conv_w, ssd_conv_b, ssd_dt_bias, ssd_A_log, ssd_D, ssd_norm_w, nsa_w_cmp_k, nsa_w_cmp_v, diff_lambda, diff_norm_w, lru_conv_w, lru_conv_b, lru_w_a, lru_b_a, lru_w_x, lru_b_x, lru_lambda, w_out, ln1_g, ln1_b, w_ff1, w_ff2, ln2_g, ln2_b):
    weights = dict(w_in=w_in, ssd_conv_w=ssd_conv_w, ssd_conv_b=ssd_conv_b, ssd_dt_bias=ssd_dt_bias,
                   ssd_A_log=ssd_A_log, ssd_D=ssd_D, ssd_norm_w=ssd_norm_w, nsa_w_cmp_k=nsa_w_cmp_k,
                   nsa_w_cmp_v=nsa_w_cmp_v, diff_lambda=diff_lambda, diff_norm_w=diff_norm_w, lru_conv_w=lru_conv_w,
                   lru_conv_b=lru_conv_b, lru_w_a=lru_w_a, lru_b_a=lru_b_a, lru_w_x=lru_w_x, lru_b_x=lru_b_x,
                   lru_lambda=lru_lambda, w_out=w_out, ln1_g=ln1_g, ln1_b=ln1_b, w_ff1=w_ff1, w_ff2=w_ff2,
                   ln2_g=ln2_g, ln2_b=ln2_b)
    depth = w_in.shape[0]
    nbp, slp, _ = x_prompt.shape
    nbs, sls, _ = x_sample.shape
    past = page_table.shape[1] * cache_nsa_k_cmp.shape[2]
    assert past % PAGE == 0 and sls < NSA_CMP_BLOCK and slp % 256 == 0
    alpha = (2 * depth) ** 0.25
    caches = dict(kc=_token_minor(cache_nsa_k_cmp), vc=_token_minor(cache_nsa_v_cmp), ks=_token_minor(cache_nsa_k_slc),
                  vs=_token_minor(cache_nsa_v_slc), dk=_token_minor(cache_diff_k), dv=_token_minor(cache_diff_v),
                  kw=_token_minor(cache_nsa_k_win), vw=_token_minor(cache_nsa_v_win))
    xp = x_prompt.reshape(nbp * slp, D_MODEL)
    xs = x_sample.reshape(nbs * sls, D_MODEL)
    outs_p, outs_s = [], []
    for l in range(depth):
        p = _prep_params(weights, l)
        lam_init = 0.8 - 0.6 * math.exp(-0.3 * l)
        hist_p = dict(ssd_conv=jnp.zeros((nbp, 3, SSD_CONV_CH), F32), ssd_h=jnp.zeros((nbp, 4, 64, 128), F32),
                      lru_conv=jnp.zeros((nbp, 3, GROUP_W), F32), lru_h=jnp.zeros((nbp, GROUP_W), F32))
        hist_s = dict(table=page_table, layer=l, ssd_conv=state_ssd_conv[l], ssd_h=state_ssd[l],
                      lru_conv=state_lru_conv[l], lru_h=state_lru[l], **caches)
        xp, new_p = _layer(xp, nbp, slp, 0, hist_p, p, lam_init, alpha)
        xs, new_s = _layer(xs, nbs, sls, past, hist_s, p, lam_init, alpha)
        outs_p.append(new_p)
        outs_s.append(new_s)
    stack = lambda lst, name: jnp.stack([d[name] for d in lst])
    return ((xp.reshape(nbp, slp, D_MODEL), xs.reshape(nbs, sls, D_MODEL))
            + tuple(stack(outs_p, n) for n in STATE_ORDER) + tuple(stack(outs_s, n) for n in STATE_ORDER))
```

```python
import functools
import math

import jax
import jax.numpy as jnp
from jax import lax
from jax.experimental import pallas as pl
from jax.experimental.pallas import tpu as pltpu

F32 = jnp.float32
BF16 = jnp.bfloat16
NEG_INF = float("-inf")

D_MODEL = 1024
GROUP_W = 256
SSD_HEADS = 4
SSD_STATE = 128
SSD_CONV_CH = 768
NSA_CMP_BLOCK = 32
NSA_SLC_BLOCK = 64
NSA_TOPN = 16
NSA_WINDOW = 512
FORCE_SCORE = 1e4
LRU_C = 8.0
D_FF = 4096
ROPE_THETA = 500000.0
EPS = 1e-5
PAGE = 128
LANES = 128
VMEM_LIMIT = 56 * 1024 * 1024
PAGES_PER_STEP = 32
LAZY_MAX_RISE = 64.0
LOG2E = 1.4426950408889634
NSA_QSCALE = 64.0 ** -0.5 * LOG2E
DIFF_QSCALE = 32.0 ** -0.5 * LOG2E

W_IN_ORDER = ((0, 1024), (1028, 2052), (2064, 3344))
MISC_COLS = ((1024, 1028), (2052, 2064))
SEG = dict(z=(0, 256), xbc=(256, 1024), qn=(1024, 1280), kc=(1280, 1408), vc=(1408, 1536), ks=(1536, 1664),
           vs=(1664, 1792), kw=(1792, 1920), vw=(1920, 2048), dq=(2048, 2304), dk=(2304, 2560), dv=(2560, 2816),
           lx=(2816, 3072), lg=(3072, 3328), misc=(3328, 3456))
D_IN_PAD = 3456
GATE_LANE0 = 4


def _params(sem):
    return pltpu.CompilerParams(dimension_semantics=sem, vmem_limit_bytes=VMEM_LIMIT)


def _iota(shape, dim):
    return lax.broadcasted_iota(jnp.int32, shape, dim)


def _dot(a, b):
    return jnp.dot(a.astype(BF16), b.astype(BF16), preferred_element_type=F32)


def _dot_nt(a, b):
    return lax.dot_general(a.astype(BF16), b.astype(BF16), (((1,), (1,)), ((), ())), preferred_element_type=F32)


def _split3(x):
    h1 = x.astype(BF16)
    r1 = x - h1.astype(F32)
    h2 = r1.astype(BF16)
    h3 = (r1 - h2.astype(F32)).astype(BF16)
    return h1, h2, h3


def _dot_01(m01, x):
    return sum(jnp.dot(m01, part, preferred_element_type=F32) for part in _split3(x))


def _dot_01_r(x, m01):
    return sum(jnp.dot(part, m01, preferred_element_type=F32) for part in _split3(x))


def _dot_nt_01(m01, x):
    return sum(lax.dot_general(m01, part, (((1,), (1,)), ((), ())), preferred_element_type=F32)
               for part in _split3(x))


def _eye(n, m):
    return (_iota((n, m), 0) == _iota((n, m), 1)).astype(BF16)


def _softplus(x):
    return jnp.maximum(x, 0.0) + jnp.log1p(jnp.exp(-jnp.abs(x)))


def _head_bcast(cols, h_of_lane, nheads, lane0=0):
    out = jnp.zeros(h_of_lane.shape, F32)
    for h in range(nheads):
        out = jnp.where(h_of_lane == h, cols[:, lane0 + h:lane0 + h + 1], out)
    return out


def _online_update(s, mask, v_b, m_ref, l_ref, acc_ref, v_transposed=False):
    s = jnp.where(mask, s, NEG_INF)
    m_prev = m_ref[...]
    m_new = jnp.maximum(m_prev, jnp.max(s, axis=-1, keepdims=True))
    m_safe = jnp.where(m_new == NEG_INF, 0.0, m_new)
    alpha = jnp.exp2(m_prev - m_safe)
    p = jnp.exp2(s - m_safe)
    l_ref[...] = alpha * l_ref[...] + jnp.sum(p, axis=-1, keepdims=True)
    if v_transposed:
        pv = lax.dot_general(p.astype(BF16), v_b, (((1,), (1,)), ((), ())), preferred_element_type=F32)
    else:
        pv = jnp.dot(p.astype(BF16), v_b, preferred_element_type=F32)
    acc_ref[...] = alpha * acc_ref[...] + pv
    m_ref[...] = m_new


def _online_update_t(s, mask, vt_b, m_ref, l_ref, acc_ref):
    if mask is not None:
        s = jnp.where(mask, s, NEG_INF)
    m_prev = m_ref[...]
    m_new = jnp.maximum(m_prev, jnp.max(s, axis=0, keepdims=True))
    m_safe = jnp.where(m_new == NEG_INF, 0.0, m_new)
    alpha = jnp.exp2(m_prev - m_safe)
    p = jnp.exp2(s - m_safe)
    l_ref[...] = alpha * l_ref[...] + jnp.sum(p, axis=0, keepdims=True)
    acc_ref[...] = alpha * acc_ref[...] + jnp.dot(vt_b, p.astype(BF16), preferred_element_type=F32)
    m_ref[...] = m_new


def _online_update_t_lazy(scores, mask, vt_b, m_ref, l_ref, acc_ref):
    s = scores()
    if mask is not None:
        s = jnp.where(mask, s, NEG_INF)
    m_prev = m_ref[...]
    tile_max = jnp.max(s, axis=0, keepdims=True)
    p = jnp.exp2(s - m_prev)
    p_sum = jnp.sum(p, axis=0, keepdims=True)
    pv = jnp.dot(vt_b, p.astype(BF16), preferred_element_type=F32)
    safe = jnp.max(tile_max - m_prev) <= LAZY_MAX_RISE

    @pl.when(safe)
    def _():
        m_new = jnp.maximum(m_prev, tile_max)
        alpha = jnp.exp2(m_prev - m_new)
        l_ref[...] = (l_ref[...] + p_sum) * alpha
        acc_ref[...] = (acc_ref[...] + pv) * alpha
        m_ref[...] = m_new

    @pl.when(jnp.logical_not(safe))
    def _():
        _online_update_t(scores(), mask, vt_b, m_ref, l_ref, acc_ref)


def _causal_pairs(n_q, qt, kt):
    qi, kj = [], []
    for i in range(n_q):
        for j in range((i * qt + qt - 1) // kt + 1):
            qi.append(i)
            kj.append(j)
    return jnp.asarray(qi, jnp.int32), jnp.asarray(kj, jnp.int32)


def _init_stats(*triples):
    for m_, l_, a_ in triples:
        m_[...] = jnp.full(m_.shape, NEG_INF, F32)
        l_[...] = jnp.zeros(l_.shape, F32)
        a_[...] = jnp.zeros(a_.shape, F32)


def _nsa_expand(q):
    lo = _iota((q.shape[0], LANES), 1) < 64
    a, b = q[:, :LANES], q[:, LANES:]
    h0 = jnp.where(lo, a, 0.0)
    h1 = jnp.where(lo, pltpu.roll(a, 64, 1), 0.0)
    h2 = jnp.where(lo, 0.0, pltpu.roll(b, 64, 1))
    h3 = jnp.where(lo, 0.0, b)
    return jnp.concatenate([h0, h1, h2, h3], axis=0)


def _nsa_collect(o, n):
    lo = _iota((n, LANES), 1) < 64
    o0, o1, o2, o3 = o[0:n], o[n:2 * n], o[2 * n:3 * n], o[3 * n:4 * n]
    left = jnp.where(lo, o0, pltpu.roll(o1, 64, 1))
    right = jnp.where(lo, pltpu.roll(o2, 64, 1), o3)
    return jnp.concatenate([left, right], axis=1)


def _collect_t(o_t, n):
    return jnp.concatenate([o_t[0:64, 0:n], o_t[0:64, n:2 * n], o_t[64:128, 2 * n:3 * n], o_t[64:128, 3 * n:4 * n]],
                           axis=0)


def _layer_norm(v, g, b):
    mu = jnp.mean(v, axis=-1, keepdims=True)
    d = v - mu
    var = jnp.mean(d * d, axis=-1, keepdims=True)
    return d * lax.rsqrt(var + EPS) * g + b


def _rope128(v, c, sa, sb, half):
    return v * c + pltpu.roll(v, LANES - half, 1) * sa + pltpu.roll(v, half, 1) * sb


PROJ_OUT = ("z", "xbc", "qn", "qnr", "kc", "vc", "ks", "vs", "kw", "vw", "dq", "dk", "dv", "lx", "lg", "misc")
PROJ_OUT_T = ("kcT", "vcT", "ksT", "vsT", "kwT", "vwT", "dkT", "dvT")
PROJ_W = dict(z=256, xbc=768, qn=256, qnr=256, kc=128, vc=128, ks=128, vs=128, kw=128, vw=128, dq=256, dk=256,
              dv=256, lx=256, lg=256, misc=128)


def _proj_kernel(x_ref, w_ref, cn_ref, san_ref, sbn_ref, cd_ref, sad_ref, sbd_ref, *outs, flat_t):
    o = dict(zip(PROJ_OUT + PROJ_OUT_T, outs))
    xb = x_ref[...].astype(BF16)

    def seg(name):
        a, b = SEG[name]
        return jnp.dot(xb, w_ref[:, a:b], preferred_element_type=F32)

    def rope_n(v):
        return _rope128(v, cn_ref[...], san_ref[...], sbn_ref[...], 8)

    def rope_d(v):
        return _rope128(v, cd_ref[...], sad_ref[...], sbd_ref[...], 4)

    def put(name, v):
        o[name][...] = v
        if name + "T" in o:
            if flat_t:
                o[name + "T"][...] = v.T
            else:
                o[name + "T"][0] = v.T

    put("z", seg("z"))
    put("xbc", seg("xbc"))
    qn = seg("qn")
    put("qn", qn)
    put("qnr", jnp.concatenate([rope_n(qn[:, :LANES]), rope_n(qn[:, LANES:])], axis=1))
    put("kc", seg("kc"))
    put("vc", seg("vc"))
    put("ks", rope_n(seg("ks")))
    put("vs", seg("vs"))
    put("kw", rope_n(seg("kw")))
    put("vw", seg("vw"))
    dq = seg("dq")
    put("dq", jnp.concatenate([rope_d(dq[:, :LANES]), rope_d(dq[:, LANES:])], axis=1))
    dk = seg("dk")
    put("dk", jnp.concatenate([rope_d(dk[:, :LANES]), rope_d(dk[:, LANES:])], axis=1))
    put("dv", seg("dv"))
    put("lx", seg("lx"))
    put("lg", seg("lg"))
    put("misc", seg("misc"))


def _rope_tables(pos, dh, rows):
    rd = dh // 4
    half = rd // 2
    inv = ROPE_THETA ** (-jnp.arange(half, dtype=F32) / half)
    ang = pos.astype(F32)[:, None] * inv[None, :]
    cos, sin = jnp.cos(ang), jnp.sin(ang)
    n = pos.shape[0]
    zero_h = jnp.zeros((n, half), F32)
    zero_r = jnp.zeros((n, dh - rd), F32)
    c = jnp.concatenate([cos, cos, jnp.ones((n, dh - rd), F32)], 1)
    sa = jnp.concatenate([-sin, zero_h, zero_r], 1)
    sb = jnp.concatenate([zero_h, sin, zero_r], 1)
    reps = (max(rows // n, 1), LANES // dh)
    return tuple(jnp.tile(t, reps) for t in (c, sa, sb))


def _proj(x2d, w_in_p, pos, nb, sl, tm):
    t = x2d.shape[0]
    n_tab = max(sl // tm, 1)
    flat_t = sl < tm
    tabs = _rope_tables(pos, 64, tm) + _rope_tables(pos, 32, tm)
    tab_spec = pl.BlockSpec((tm, LANES), lambda i: (i % n_tab, 0))
    out_specs = [pl.BlockSpec((tm, PROJ_W[k]), lambda i: (i, 0)) for k in PROJ_OUT]
    out_shape = [jax.ShapeDtypeStruct((t, PROJ_W[k]), F32) for k in PROJ_OUT]
    for k in PROJ_OUT_T:
        c = PROJ_W[k[:-1]]
        if flat_t:
            out_specs.append(pl.BlockSpec((c, tm), lambda i: (0, i)))
            out_shape.append(jax.ShapeDtypeStruct((c, t), F32))
        else:
            out_specs.append(pl.BlockSpec((1, c, tm), lambda i: (i // n_tab, 0, i % n_tab)))
            out_shape.append(jax.ShapeDtypeStruct((nb, c, sl), F32))
    outs = pl.pallas_call(
        functools.partial(_proj_kernel, flat_t=flat_t),
        grid=(t // tm,),
        in_specs=[pl.BlockSpec((tm, D_MODEL), lambda i: (i, 0)),
                  pl.BlockSpec((D_MODEL, D_IN_PAD), lambda i: (0, 0))] + [tab_spec] * 6,
        out_specs=out_specs, out_shape=out_shape,
        compiler_params=_params(("parallel",)),
        name="proj",
    )(x2d, w_in_p, *tabs)
    pj = dict(zip(PROJ_OUT + PROJ_OUT_T, outs))
    if flat_t:
        for k in PROJ_OUT_T:
            pj[k] = pj[k].reshape(-1, nb, sl).transpose(1, 0, 2)
    return pj


def _conv_chunk(x_ref, prev_ref, cw_ref, cb_ref, xbuf, q, first):
    @pl.when(first)
    def _():
        xbuf[0:8, :] = prev_ref[0]

    xbuf[8:8 + q, :] = x_ref[0]
    acc = cb_ref[...] + cw_ref[0:1, :] * xbuf[pl.ds(5, q), :]
    for j in range(1, 4):
        acc = acc + cw_ref[j:j + 1, :] * xbuf[pl.ds(5 + j, q), :]
    xbuf[0:8, :] = xbuf[q:q + 8, :]
    return acc


def _ssd_kernel(xbc_ref, z_ref, misc_ref, prev_ref, h0_ref, cw_ref, cb_ref, dtb_ref, alog_ref, dexp_ref, nw_ref,
                y_ref, hout_ref, xbuf, ht, *, q):
    c = pl.program_id(1)

    @pl.when(c == 0)
    def _():
        ht[...] = _dot_nt_01(_eye(SSD_STATE, SSD_STATE), h0_ref[0])

    acc = _conv_chunk(xbc_ref, prev_ref, cw_ref, cb_ref, xbuf, q, c == 0)
    xc = acc * jax.nn.sigmoid(acc)
    sx = xc[:, 0:256]
    sb = (xc[:, 256:384], xc[:, 384:512])
    sc = (xc[:, 512:640], xc[:, 640:768])
    dt = _softplus(misc_ref[0] + dtb_ref[...])
    da = dt * (-jnp.exp(alog_ref[...]))
    causal = _iota((q, q), 0) >= _iota((q, q), 1)
    acum = _dot_01(causal.astype(BF16), da)
    xsel = _eye(8, LANES)
    acum_t = _dot_nt_01(xsel, acum)
    dt_t = _dot_nt_01(xsel, dt)
    a_last = acum[q - 1:q, :]
    wlast = jnp.exp(a_last - acum) * dt
    ea = jnp.exp(acum)
    head = _iota((q, GROUP_W), 1) >> 6
    ht_old = ht[...]
    y = dexp_ref[...] * sx
    for g in range(2):
        cb = _dot_nt(sc[g], sb[g])
        for h in (2 * g, 2 * g + 1):
            seg = acum[:, h:h + 1] - acum_t[h:h + 1, :]
            decay = jnp.exp(jnp.where(causal, seg, NEG_INF))
            m = cb * decay * dt_t[h:h + 1, :]
            y_h = _dot(m, sx) + _dot(sc[g] * ea[:, h:h + 1], ht_old)
            y = y + jnp.where(head == h, y_h, 0.0)
    xw = sx * _head_bcast(wlast, head, SSD_HEADS)
    lane = _iota((q, GROUP_W), 1)
    bt0 = _dot_nt(_eye(SSD_STATE, SSD_STATE), sb[0])
    bt1 = _dot_nt(_eye(SSD_STATE, SSD_STATE), sb[1])
    head1 = _iota((1, GROUP_W), 1) >> 6
    dch = _head_bcast(jnp.exp(a_last), head1, SSD_HEADS)
    ht_new = dch * ht_old + _dot(bt0, jnp.where(lane < 128, xw, 0.0)) + _dot(bt1, jnp.where(lane < 128, 0.0, xw))
    ht[...] = ht_new
    zz = z_ref[0]
    y = y * (zz * jax.nn.sigmoid(zz))
    y_ref[0] = y * lax.rsqrt(jnp.mean(y * y, axis=-1, keepdims=True) + EPS) * nw_ref[...]

    @pl.when(c == pl.num_programs(1) - 1)
    def _():
        hout_ref[0] = _dot_nt_01(_eye(GROUP_W, GROUP_W), ht_new)


def _ssd(pj, nb, sl, prev8, h0, p):
    q = min(128, sl)
    nc = sl // q
    row = lambda w: pl.BlockSpec((1, q, w), lambda b, c: (b, c, 0))
    per_b = lambda r, w: pl.BlockSpec((1, r, w), lambda b, c: (b, 0, 0))
    const = lambda r, w: pl.BlockSpec((r, w), lambda b, c: (0, 0))
    y, hout = pl.pallas_call(
        functools.partial(_ssd_kernel, q=q),
        grid=(nb, nc),
        in_specs=[row(768), row(256), row(128), per_b(8, 768), per_b(256, 128),
                  const(8, 768), const(1, 768), const(1, 128), const(1, 128), const(1, 256), const(1, 256)],
        out_specs=[row(256), per_b(256, 128)],
        out_shape=[jax.ShapeDtypeStruct((nb, sl, 256), F32), jax.ShapeDtypeStruct((nb, 256, 128), F32)],
        scratch_shapes=[pltpu.VMEM((q + 8, 768), F32), pltpu.VMEM((SSD_STATE, GROUP_W), F32)],
        compiler_params=_params(("parallel", "arbitrary")),
        name="ssd",
    )(pj["xbc"].reshape(nb, sl, 768), pj["z"].reshape(nb, sl, 256), pj["misc"].reshape(nb, sl, 128), prev8, h0,
      p["ssd_cw"], p["ssd_cb"], p["ssd_dtb"], p["ssd_alog"], p["ssd_dexp"], p["ssd_nw"])
    return y.reshape(nb * sl, 256), hout


def _lru_kernel(x_ref, g_ref, prev_ref, h0_ref, cw_ref, cb_ref, wa_ref, ba_ref, wx_ref, bx_ref, lam_ref,
                y_ref, hout_ref, xbuf, hc, *, q):
    c = pl.program_id(1)

    @pl.when(c == 0)
    def _():
        hc[...] = h0_ref[0]

    xc = _conv_chunk(x_ref, prev_ref, cw_ref, cb_ref, xbuf, q, c == 0)
    xcb = xc.astype(BF16)
    r = jax.nn.sigmoid(jnp.dot(xcb, wa_ref[...], preferred_element_type=F32) + ba_ref[...])
    i = jax.nn.sigmoid(jnp.dot(xcb, wx_ref[...], preferred_element_type=F32) + bx_ref[...])
    log_a = -LRU_C * r * _softplus(-lam_ref[...])
    a = jnp.exp(log_a)
    u = jnp.sqrt(1.0 - jnp.exp(2.0 * log_a)) * (i * xc)
    row = _iota((q, GROUP_W), 0)
    s = 1
    while s < q:
        a_sh = pltpu.roll(a, s, 0)
        u_sh = pltpu.roll(u, s, 0)
        keep = row >= s
        u = jnp.where(keep, a * u_sh + u, u)
        a = jnp.where(keep, a * a_sh, a)
        s *= 2
    h = u + a * hc[...]
    hc[...] = h[q - 1:q, :]
    gg = g_ref[0]
    gelu = 0.5 * gg * (1.0 + jnp.tanh(math.sqrt(2.0 / math.pi) * (gg + 0.044715 * (gg * gg * gg))))
    y_ref[0] = h * gelu

    @pl.when(c == pl.num_programs(1) - 1)
    def _():
        hout_ref[0] = h[q - 1:q, :]


def _lru(pj, nb, sl, prev8, h0, p):
    q = min(256, sl)
    nc = sl // q
    row = lambda w: pl.BlockSpec((1, q, w), lambda b, c: (b, c, 0))
    per_b = lambda r, w: pl.BlockSpec((1, r, w), lambda b, c: (b, 0, 0))
    const = lambda r, w: pl.BlockSpec((r, w), lambda b, c: (0, 0))
    y, hout = pl.pallas_call(
        functools.partial(_lru_kernel, q=q),
        grid=(nb, nc),
        in_specs=[row(256), row(256), per_b(8, 256), per_b(1, 256), const(8, 256), const(1, 256),
                  const(256, 256), const(1, 256), const(256, 256), const(1, 256), const(1, 256)],
        out_specs=[row(256), per_b(1, 256)],
        out_shape=[jax.ShapeDtypeStruct((nb, sl, 256), F32), jax.ShapeDtypeStruct((nb, 1, 256), F32)],
        scratch_shapes=[pltpu.VMEM((q + 8, 256), F32), pltpu.VMEM((1, 256), F32)],
        compiler_params=_params(("parallel", "arbitrary")),
        name="lru",
    )(pj["lx"].reshape(nb, sl, 256), pj["lg"].reshape(nb, sl, 256), prev8, h0.reshape(nb, 1, 256),
      p["lru_cw"], p["lru_cb"], p["lru_wa"], p["lru_ba"], p["lru_wx"], p["lru_bx"], p["lru_lam"])
    return y.reshape(nb * sl, 256), hout.reshape(nb, 256)


def _summ_kernel(pt_ref, *refs, n_parts):
    kp, vp = refs[:n_parts], refs[n_parts:2 * n_parts]
    mk_ref, mv_ref, ko_ref, vo_ref = refs[2 * n_parts:]

    def one(parts, m_ref, o_ref):
        chunks = []
        for r in parts:
            x = r[...]
            chunks.append(x.reshape(x.shape[-2], x.shape[-1]).astype(BF16))
        x = jnp.concatenate(chunks, axis=1) if len(chunks) > 1 else chunks[0]
        top = jnp.dot(x[0:64], m_ref[0], preferred_element_type=F32)
        bot = jnp.dot(x[64:128], m_ref[1], preferred_element_type=F32)
        o_ref[0] = jnp.concatenate([top, bot], axis=0)

    one(kp, mk_ref, ko_ref)
    one(vp, mv_ref, vo_ref)


def _summarize(k_src, v_src, part_specs, part_w, nb, total, table, w_k, w_v):
    n_parts = len(part_specs)
    step_w = n_parts * part_w
    n_blk = step_w // NSA_CMP_BLOCK
    tok = jnp.arange(step_w, dtype=jnp.int32)
    in_block = tok[:, None] // NSA_CMP_BLOCK == jnp.arange(n_blk, dtype=jnp.int32)[None, :]
    weights = lambda w: jnp.where(in_block[None], w[:, tok % NSA_CMP_BLOCK][:, :, None], 0.0).astype(BF16)
    wspec = pl.BlockSpec((2, step_w, n_blk), lambda b, s, pt: (0, 0, 0))
    ospec = pl.BlockSpec((1, LANES, n_blk), lambda b, s, pt: (b, 0, s))
    oshape = jax.ShapeDtypeStruct((nb, LANES, total // NSA_CMP_BLOCK), F32)
    return pl.pallas_call(
        functools.partial(_summ_kernel, n_parts=n_parts),
        grid_spec=pltpu.PrefetchScalarGridSpec(
            num_scalar_prefetch=1, grid=(nb, total // step_w),
            in_specs=list(part_specs) * 2 + [wspec, wspec],
            out_specs=[ospec, ospec]),
        out_shape=[oshape, oshape],
        compiler_params=_params(("parallel", "arbitrary")),
        name="nsa_summarize",
    )(table, *([k_src] * n_parts), *([v_src] * n_parts), weights(w_k), weights(w_v))


def _cmpsel_t_kernel(q_ref, kct_ref, vct_ref, ocmp_ref, sel_ref, *, qt, nc, ns):
    qi = pl.program_id(1)
    q4 = _nsa_expand(q_ref[0] * 0.125)
    s = _dot_nt(kct_ref[0].T, q4)
    qpos = qi * qt + (_iota((nc, 4 * qt), 1) & (qt - 1))
    blk_end = (_iota((nc, 4 * qt), 0) + 1) * NSA_CMP_BLOCK - 1
    s = jnp.where(blk_end <= qpos, s, NEG_INF)
    m = jnp.max(s, axis=0, keepdims=True)
    m = jnp.where(m == NEG_INF, 0.0, m)
    e = jnp.exp(s - m)
    p = e / jnp.maximum(jnp.sum(e, axis=0, keepdims=True), 1e-30)
    ocmp_ref[0] = _collect_t(_dot(vct_ref[0], p), qt).T
    impc = jnp.concatenate([p[:, 0:qt] + p[:, qt:2 * qt], p[:, 2 * qt:3 * qt] + p[:, 3 * qt:4 * qt]], axis=1)
    pair = (_iota((ns, nc), 1) >> 1 == _iota((ns, nc), 0)).astype(BF16)
    imp = _dot_01(pair, impc)
    blk = _iota((ns, 2 * qt), 0)
    cur = (qi * qt + (_iota((ns, 2 * qt), 1) & (qt - 1))) >> 6
    forced = (blk == 0) | (blk == cur) | (blk == cur - 1)
    score = jnp.where(forced, FORCE_SCORE, jnp.where(blk > cur, -1.0, imp))
    cnt = jnp.zeros((ns, 2 * qt), F32)
    for j in range(ns):
        row = score[j:j + 1, :]
        cnt = cnt + ((row > score) | ((row == score) & (blk > j))).astype(F32)
    sel = (cnt < float(NSA_TOPN)).astype(F32)
    sel_ref[0, 0] = sel[:, 0:qt]
    sel_ref[0, 1] = sel[:, qt:2 * qt]


def _cmp_select_prompt(qn, kcmp_t, vcmp_t, nb, sl, qt):
    nc = kcmp_t.shape[2]
    ns = sl // NSA_SLC_BLOCK
    return pl.pallas_call(
        functools.partial(_cmpsel_t_kernel, qt=qt, nc=nc, ns=ns),
        grid=(nb, sl // qt),
        in_specs=[pl.BlockSpec((1, qt, 256), lambda b, i: (b, i, 0)),
                  pl.BlockSpec((1, LANES, nc), lambda b, i: (b, 0, 0)),
                  pl.BlockSpec((1, LANES, nc), lambda b, i: (b, 0, 0))],
        out_specs=[pl.BlockSpec((1, qt, 256), lambda b, i: (b, i, 0)),
                   pl.BlockSpec((1, 2, ns, qt), lambda b, i: (b, 0, 0, i))],
        out_shape=[jax.ShapeDtypeStruct((nb, sl, 256), F32), jax.ShapeDtypeStruct((nb, 2, ns, sl), F32)],
        compiler_params=_params(("parallel", "parallel")),
        name="nsa_cmp_select_prompt",
    )(qn.reshape(nb, sl, 256), kcmp_t, vcmp_t)


def _cmpsel_kernel(q_ref, kct_ref, vct_ref, ocmp_ref, selx_ref, *, qt, nc, ns, nsp, past, n_chunks):
    q4 = _nsa_expand(q_ref[0] * 0.125)
    s = _dot(q4, kct_ref[0])
    qpos = past + (_iota((4 * qt, nc), 0) & (qt - 1))
    blk_end = (_iota((4 * qt, nc), 1) + 1) * NSA_CMP_BLOCK - 1
    s = jnp.where(blk_end <= qpos, s, NEG_INF)
    m = jnp.max(s, axis=-1, keepdims=True)
    m = jnp.where(m == NEG_INF, 0.0, m)
    e = jnp.exp(s - m)
    p = e / jnp.maximum(jnp.sum(e, axis=-1, keepdims=True), 1e-30)
    ocmp_ref[0] = _nsa_collect(_dot_nt(p, vct_ref[0]), qt)
    impc = jnp.concatenate([p[0:qt] + p[qt:2 * qt], p[2 * qt:3 * qt] + p[3 * qt:4 * qt]], axis=0)
    pair = (_iota((nc, nsp), 0) >> 1 == _iota((nc, nsp), 1)).astype(BF16)
    imp = _dot_01_r(impc, pair)
    blk = _iota((2 * qt, nsp), 1)
    cur = (past + (_iota((2 * qt, nsp), 0) & (qt - 1))) >> 6
    forced = (blk == 0) | (blk == cur) | (blk == cur - 1)
    score = jnp.where(forced, FORCE_SCORE, jnp.where(blk > cur, -1.0, imp))
    score = jnp.where(blk < ns, score, -3.0)

    cnt = jnp.zeros((2 * qt, nsp), F32)
    for j in range(ns):
        col = score[:, j:j + 1]
        cnt = cnt + ((col > score) | ((col == score) & (blk > j))).astype(F32)
    sel = ((cnt < float(NSA_TOPN)) & (blk < ns)).astype(F32)
    sel4 = jnp.concatenate([sel[0:qt], sel[0:qt], sel[qt:2 * qt], sel[qt:2 * qt]], axis=0).astype(BF16)
    e16 = (_iota((LANES, 1024), 1) >> 6 == _iota((LANES, 1024), 0)).astype(BF16)
    for t in range(n_chunks):
        pick = ((_iota((nsp, LANES), 0) == 16 * t + _iota((nsp, LANES), 1))
                & (_iota((nsp, LANES), 1) < 16)).astype(BF16)
        blocks = jnp.dot(sel4, pick, preferred_element_type=F32).astype(BF16)
        selx_ref[0, :, 1024 * t:1024 * (t + 1)] = jnp.dot(blocks, e16, preferred_element_type=F32).astype(BF16)


def _cmp_select_sample(qn, kcmp_t, vcmp_t, nb, sl, past):
    nc = kcmp_t.shape[2]
    ns = -(-(past + sl) // NSA_SLC_BLOCK)
    nsp = -(-ns // LANES) * LANES
    n_chunks = -(-(past + sl) // 1024)
    return pl.pallas_call(
        functools.partial(_cmpsel_kernel, qt=sl, nc=nc, ns=ns, nsp=nsp, past=past, n_chunks=n_chunks),
        grid=(nb,),
        in_specs=[pl.BlockSpec((1, sl, 256), lambda b: (b, 0, 0)),
                  pl.BlockSpec((1, LANES, nc), lambda b: (b, 0, 0)),
                  pl.BlockSpec((1, LANES, nc), lambda b: (b, 0, 0))],
        out_specs=[pl.BlockSpec((1, sl, 256), lambda b: (b, 0, 0)),
                   pl.BlockSpec((1, 4 * sl, 1024 * n_chunks), lambda b: (b, 0, 0))],
        out_shape=[jax.ShapeDtypeStruct((nb, sl, 256), F32),
                   jax.ShapeDtypeStruct((nb, 4 * sl, 1024 * n_chunks), BF16)],
        compiler_params=_params(("parallel",)),
        name="nsa_cmp_select_sample",
    )(qn.reshape(nb, sl, 256), kcmp_t, vcmp_t)


def _nsap_kernel(qi_ref, kj_ref, q_ref, ks_ref, vst_ref, sel_ref, oslc_ref, q4_s, m1, l1, a1, *, qt, kt):
    step = pl.program_id(1)
    qi, kj = qi_ref[step], kj_ref[step]
    last = (qi * qt + qt - 1) // kt

    @pl.when(kj == 0)
    def _():
        q4_s[...] = _nsa_expand(q_ref[0] * NSA_QSCALE).astype(BF16)
        _init_stats((m1, l1, a1))

    def picked(g):
        rows = []
        for u in range(kt // NSA_SLC_BLOCK):
            r = sel_ref[0, g, pl.ds(kj * (kt // NSA_SLC_BLOCK) + u, 1), :]
            rows.append(jnp.broadcast_to(r, (NSA_SLC_BLOCK, qt)))
        return jnp.concatenate(rows, axis=0) > 0.5

    def slc_mask(with_causal):
        m0, m1_ = picked(0), picked(1)
        if with_causal:
            causal = kj * kt + _iota((kt, qt), 0) <= qi * qt + _iota((kt, qt), 1)
            m0, m1_ = m0 & causal, m1_ & causal
        return jnp.concatenate([m0, m0, m1_, m1_], axis=1)

    def slc_scores():
        return lax.dot_general(ks_ref[0].astype(BF16), q4_s[...], (((1,), (1,)), ((), ())),
                               preferred_element_type=F32)

    @pl.when((kj == 0) & (last > 0))
    def _():
        _online_update_t(slc_scores(), slc_mask(False), vst_ref[0].astype(BF16), m1, l1, a1)

    @pl.when((kj == 0) & (last == 0))
    def _():
        _online_update_t(slc_scores(), slc_mask(True), vst_ref[0].astype(BF16), m1, l1, a1)

    @pl.when((kj > 0) & (kj < last))
    def _():
        _online_update_t_lazy(slc_scores, slc_mask(False), vst_ref[0].astype(BF16), m1, l1, a1)

    @pl.when((kj > 0) & (kj == last))
    def _():
        _online_update_t_lazy(slc_scores, slc_mask(True), vst_ref[0].astype(BF16), m1, l1, a1)

    @pl.when(kj == last)
    def _():
        oslc_ref[0] = _collect_t(a1[...] / jnp.maximum(l1[...], 1e-30), qt).T


def _nsa_prompt_attn(pj, sel, nb, sl):
    qt, kt = min(256, sl), min(512, sl)
    assert kt % qt == 0
    qi_tab, kj_tab = _causal_pairs(sl // qt, qt, kt)
    kspec = pl.BlockSpec((1, kt, LANES), lambda b, s, qi, kj: (b, kj[s], 0))
    vspec = pl.BlockSpec((1, LANES, kt), lambda b, s, qi, kj: (b, 0, kj[s]))
    qspec = pl.BlockSpec((1, qt, 256), lambda b, s, qi, kj: (b, qi[s], 0))
    ns = sel.shape[2]
    r3 = lambda a, w: a.reshape(nb, sl, w)
    return pl.pallas_call(
        functools.partial(_nsap_kernel, qt=qt, kt=kt),
        grid_spec=pltpu.PrefetchScalarGridSpec(
            num_scalar_prefetch=2, grid=(nb, qi_tab.shape[0]),
            in_specs=[qspec, kspec, vspec, pl.BlockSpec((1, 2, ns, qt), lambda b, s, qi, kj: (b, 0, 0, qi[s]))],
            out_specs=qspec,
            scratch_shapes=[pltpu.VMEM((4 * qt, LANES), BF16), pltpu.VMEM((1, 4 * qt), F32),
                            pltpu.VMEM((1, 4 * qt), F32), pltpu.VMEM((LANES, 4 * qt), F32)]),
        out_shape=jax.ShapeDtypeStruct((nb, sl, 256), F32),
        compiler_params=_params(("parallel", "arbitrary")),
        name="nsa_prompt_attn",
    )(qi_tab, kj_tab, r3(pj["qnr"], 256), r3(pj["ks"], 128), pj["vsT"], sel)


def _winp_kernel(q_ref, *refs, qt, n_blk):
    kb, vtb, o_ref = refs[:n_blk], refs[n_blk:2 * n_blk], refs[2 * n_blk]
    qi = pl.program_id(1)
    q4 = _nsa_expand(q_ref[0] * NSA_QSCALE).astype(BF16)
    k_b = jnp.concatenate([r[0].astype(BF16) for r in kb], axis=0)
    vt_b = jnp.concatenate([r[0].astype(BF16) for r in vtb], axis=1)
    s = lax.dot_general(k_b, q4, (((1,), (1,)), ((), ())), preferred_element_type=F32)
    nk = n_blk * qt
    kpos = (qi - (n_blk - 1)) * qt + _iota((nk, qt), 0)
    qpos = qi * qt + _iota((nk, qt), 1)
    band = (kpos >= 0) & (kpos <= qpos) & (kpos > qpos - NSA_WINDOW)
    s = jnp.where(jnp.concatenate([band] * 4, axis=1), s, NEG_INF)
    m = jnp.max(s, axis=0, keepdims=True)
    p = jnp.exp2(s - m)
    l = jnp.sum(p, axis=0, keepdims=True)
    o = jnp.dot(vt_b, p.astype(BF16), preferred_element_type=F32) / jnp.maximum(l, 1e-30)
    o_ref[0] = _collect_t(o, qt).T


def _nsa_prompt_window(pj, nb, sl):
    qt = min(256, sl)
    n_blk = -(-(NSA_WINDOW - 1) // qt) + 1
    blk = lambda j: (lambda b, i: (b, jnp.maximum(i - (n_blk - 1) + j, 0), 0))
    blk_t = lambda j: (lambda b, i: (b, 0, jnp.maximum(i - (n_blk - 1) + j, 0)))
    qspec = pl.BlockSpec((1, qt, 256), lambda b, i: (b, i, 0))
    r3 = lambda a, w: a.reshape(nb, sl, w)
    return pl.pallas_call(
        functools.partial(_winp_kernel, qt=qt, n_blk=n_blk),
        grid=(nb, sl // qt),
        in_specs=[qspec] + [pl.BlockSpec((1, qt, LANES), blk(j)) for j in range(n_blk)]
        + [pl.BlockSpec((1, LANES, qt), blk_t(j)) for j in range(n_blk)],
        out_specs=qspec,
        out_shape=jax.ShapeDtypeStruct((nb, sl, 256), F32),
        compiler_params=_params(("parallel", "parallel")),
        name="nsa_prompt_window",
    )(r3(pj["qnr"], 256), *([r3(pj["kw"], 128)] * n_blk), *([pj["vwT"]] * n_blk))


def _nsa_gate(misc, oc, os_, ow):
    g = jax.nn.sigmoid(misc)
    head = _iota(oc.shape, 1) >> 6
    return (_head_bcast(g, head, 4, GATE_LANE0) * oc + _head_bcast(g, head, 4, GATE_LANE0 + 4) * os_
            + _head_bcast(g, head, 4, GATE_LANE0 + 8) * ow)


def _diff_lambda(lamv_ref, lam_init):
    lv = lamv_ref[...]
    s1 = jnp.sum(lv[0:1] * lv[1:2], axis=1, keepdims=True)
    s2 = jnp.sum(lv[2:3] * lv[3:4], axis=1, keepdims=True)
    return jnp.exp(s1) - jnp.exp(s2) + lam_init


def _diff_finish(o, head, nheads, nw, lam_init):
    inv = jnp.zeros(o.shape, F32)
    for h in range(nheads):
        ms = jnp.sum(jnp.where(head == h, o * o, 0.0), axis=-1, keepdims=True) * (1.0 / 64.0)
        inv = jnp.where(head == h, lax.rsqrt(ms + EPS), inv)
    return o * inv * nw * (1.0 - lam_init)


def _diffp_kernel(qi_ref, kj_ref, q_ref, k_ref, vt_ref, lamv_ref, nwt_ref, o_ref, q4_s, m_s, l_s, acc_s,
                  *, qt, kt, lam_init):
    step = pl.program_id(2)
    qi, kj = qi_ref[step], kj_ref[step]
    last = (qi * qt + qt - 1) // kt

    @pl.when(kj == 0)
    def _():
        q = q_ref[0] * DIFF_QSCALE
        part = _iota((qt, LANES), 1) >> 5
        q4_s[...] = jnp.concatenate([jnp.where(part == k, q, 0.0) for k in range(4)], axis=0).astype(BF16)
        _init_stats((m_s, l_s, acc_s))

    def scores():
        return lax.dot_general(k_ref[0].astype(BF16), q4_s[...], (((1,), (1,)), ((), ())),
                               preferred_element_type=F32)

    def causal():
        mask = kj * kt + _iota((kt, qt), 0) <= qi * qt + _iota((kt, qt), 1)
        return jnp.concatenate([mask] * 4, axis=1)

    @pl.when((kj == 0) & (last > 0))
    def _():
        _online_update_t(scores(), None, vt_ref[0].astype(BF16), m_s, l_s, acc_s)

    @pl.when((kj == 0) & (last == 0))
    def _():
        _online_update_t(scores(), causal(), vt_ref[0].astype(BF16), m_s, l_s, acc_s)

    @pl.when((kj > 0) & (kj < last))
    def _():
        _online_update_t_lazy(scores, None, vt_ref[0].astype(BF16), m_s, l_s, acc_s)

    @pl.when((kj > 0) & (kj == last))
    def _():
        _online_update_t_lazy(scores, causal(), vt_ref[0].astype(BF16), m_s, l_s, acc_s)

    @pl.when(kj == last)
    def _():
        lam = _diff_lambda(lamv_ref, lam_init)
        o = acc_s[...] / jnp.maximum(l_s[...], 1e-30)
        halves = []
        for h in range(2):
            rows = slice(64 * h, 64 * (h + 1))
            oh = o[rows, 2 * h * qt:(2 * h + 1) * qt] - lam * o[rows, (2 * h + 1) * qt:(2 * h + 2) * qt]
            ms = jnp.mean(oh * oh, axis=0, keepdims=True)
            halves.append(oh * lax.rsqrt(ms + EPS))
        o_ref[0] = (jnp.concatenate(halves, axis=0) * nwt_ref[...] * (1.0 - lam_init)).T


def _diff_prompt(pj, nb, sl, p, lam_init):
    qt, kt = min(256, sl), min(512, sl)
    assert kt % qt == 0
    qi_tab, kj_tab = _causal_pairs(sl // qt, qt, kt)
    qspec = pl.BlockSpec((1, qt, LANES), lambda b, h, s, qi, kj: (b, qi[s], h))
    kspec = pl.BlockSpec((1, kt, LANES), lambda b, h, s, qi, kj: (b, kj[s], h))
    vspec = pl.BlockSpec((1, LANES, kt), lambda b, h, s, qi, kj: (b, h, kj[s]))
    r3 = lambda a: a.reshape(nb, sl, 256)
    out = pl.pallas_call(
        functools.partial(_diffp_kernel, qt=qt, kt=kt, lam_init=lam_init),
        grid_spec=pltpu.PrefetchScalarGridSpec(
            num_scalar_prefetch=2, grid=(nb, 2, qi_tab.shape[0]),
            in_specs=[qspec, kspec, vspec, pl.BlockSpec((4, 32), lambda b, h, s, qi, kj: (0, 0)),
                      pl.BlockSpec((LANES, 1), lambda b, h, s, qi, kj: (0, 0))],
            out_specs=qspec,
            scratch_shapes=[pltpu.VMEM((4 * qt, LANES), BF16), pltpu.VMEM((1, 4 * qt), F32),
                            pltpu.VMEM((1, 4 * qt), F32), pltpu.VMEM((LANES, 4 * qt), F32)]),
        out_shape=jax.ShapeDtypeStruct((nb, sl, 256), F32),
        compiler_params=_params(("parallel", "parallel", "arbitrary")),
        name="diff_prompt",
    )(qi_tab, kj_tab, r3(pj["dq"]), r3(pj["dk"]), pj["dvT"], p["diff_lambda"],
      p["diff_nw256"][:, :LANES].reshape(LANES, 1))
    return out.reshape(nb * sl, 256)


def _paged_kernel(pt_ref, q_ref, *refs, mode, pps, width, kbase, past, new_len, lam_init):
    kp, vp = refs[:pps], refs[pps:2 * pps]
    knew_ref, vnew_ref = refs[2 * pps:2 * pps + 2]
    rest = list(refs[2 * pps + 2:])
    selx_ref = selnew_ref = lamv_ref = nw_ref = None
    if mode == "slc":
        selx_ref, selnew_ref = rest[:2]
        rest = rest[2:]
    if mode == "diff":
        lamv_ref, nw_ref = rest[:2]
        rest = rest[2:]
    o_ref, qe_s, m_s, l_s, acc_s = rest
    nq = new_len
    rows = qe_s.shape[0]
    step = pl.program_id(1)

    @pl.when(step == 0)
    def _():
        if mode == "diff":
            q = q_ref[0] * DIFF_QSCALE
            part = _iota((nq, width), 1) >> 5
            qe = jnp.concatenate([jnp.where(part == k, q, 0.0) for k in range(8)], axis=0)
        else:
            qe = _nsa_expand(q_ref[0] * NSA_QSCALE)
        qe_s[...] = qe.astype(BF16)
        _init_stats((m_s, l_s, acc_s))

    nk = pps * PAGE
    kt_b = jnp.concatenate([r[0, 0].astype(BF16) for r in kp], axis=1)
    vt_b = jnp.concatenate([r[0, 0].astype(BF16) for r in vp], axis=1)
    s = jnp.dot(qe_s[...], kt_b, preferred_element_type=F32)
    qpos = past + (_iota((rows, nk), 0) & (nq - 1))
    kpos = kbase + step * nk + _iota((rows, nk), 1)
    mask = kpos <= qpos
    if mode == "win":
        mask = mask & (kpos > qpos - NSA_WINDOW)
    if mode == "slc":
        mask = mask & (selx_ref[0] > 0.5)
    _online_update(s, mask, vt_b, m_s, l_s, acc_s, v_transposed=True)

    @pl.when(step == pl.num_programs(1) - 1)
    def _():
        s2 = lax.dot_general(qe_s[...], knew_ref[0].astype(BF16), (((1,), (1,)), ((), ())),
                             preferred_element_type=F32)
        qpos2 = past + (_iota((rows, nq), 0) & (nq - 1))
        kpos2 = past + _iota((rows, nq), 1)
        mask2 = kpos2 <= qpos2
        if mode == "win":
            mask2 = mask2 & (kpos2 > qpos2 - NSA_WINDOW)
        if mode == "slc":
            mask2 = mask2 & (selnew_ref[0][:, 0:nq] > 0.5)
        _online_update(s2, mask2, vnew_ref[0].astype(BF16), m_s, l_s, acc_s)
        o = acc_s[...] / jnp.maximum(l_s[...], 1e-30)
        if mode == "diff":
            lam = _diff_lambda(lamv_ref, lam_init)
            head = _iota((nq, width), 1) >> 6
            out = jnp.zeros((nq, width), F32)
            for h in range(4):
                o0 = o[(2 * h) * nq:(2 * h + 1) * nq]
                o1 = o[(2 * h + 1) * nq:(2 * h + 2) * nq]
                out = jnp.where(head == h, o0 - lam * o1, out)
            o_ref[0] = _diff_finish(out, head, 4, nw_ref[...], lam_init)
        else:
            o_ref[0] = _nsa_collect(o, nq)


def _paged_attn(mode, q, k_src, v_src, page_spec, n_pages, table, knew, vnew, kbase, past, extra=(), lam_init=0.0):
    nb, nq = q.shape[0], q.shape[1]
    width = knew.shape[-1]
    pps = min(PAGES_PER_STEP, n_pages)
    rows = 8 * nq if mode == "diff" else 4 * nq
    nk = pps * PAGE
    per_b = lambda r_, w: pl.BlockSpec((1, r_, w), lambda b, s, pt: (b, 0, 0))
    in_specs = [per_b(nq, 256)] + [page_spec(r, pps) for r in range(pps)] * 2 + [per_b(nq, width), per_b(nq, width)]
    args = [q] + [k_src] * pps + [v_src] * pps + [knew, vnew]
    if mode == "slc":
        (selx,) = extra
        new_blk = (past - kbase) // LANES
        in_specs += [pl.BlockSpec((1, rows, nk), lambda b, s, pt: (b, 0, s)),
                     pl.BlockSpec((1, rows, LANES), lambda b, s, pt: (b, 0, new_blk))]
        args += [selx, selx]
    if mode == "diff":
        in_specs += [pl.BlockSpec((4, 32), lambda b, s, pt: (0, 0)), pl.BlockSpec((1, 256), lambda b, s, pt: (0, 0))]
        args += list(extra)
    return pl.pallas_call(
        functools.partial(_paged_kernel, mode=mode, pps=pps, width=width, kbase=kbase, past=past, new_len=nq,
                          lam_init=lam_init),
        grid_spec=pltpu.PrefetchScalarGridSpec(
            num_scalar_prefetch=1, grid=(nb, n_pages // pps), in_specs=in_specs,
            out_specs=per_b(nq, 256),
            scratch_shapes=[pltpu.VMEM((rows, width), BF16), pltpu.VMEM((rows, 1), F32),
                            pltpu.VMEM((rows, 1), F32), pltpu.VMEM((rows, width), F32)]),
        out_shape=jax.ShapeDtypeStruct((nb, nq, 256), F32),
        compiler_params=_params(("parallel", "arbitrary")),
        name="paged_" + mode,
    )(table, *args)


def _page_copy(src_hbm, layer, page, buf, slot, r, sem):
    return pltpu.make_async_copy(src_hbm.at[layer, page], buf.at[slot, r], sem.at[slot])


def _start_pages(srcs, bufs, sems, pt_ref, layer, b, first, slot, n):
    for r in range(n):
        page = pt_ref[b, first + r]
        for src, buf, sem in zip(srcs, bufs, sems):
            _page_copy(src, layer, page, buf, slot, r, sem).start()


def _wait_pages(srcs, bufs, sems, layer, slot, n):
    for r in range(n):
        for src, buf, sem in zip(srcs, bufs, sems):
            _page_copy(src, layer, 0, buf, slot, r, sem).wait()


def _page_ring(pt_ref, srcs, bufs, sems, layer, pps, n_steps):
    b, nb = pl.program_id(0), pl.num_programs(0)
    slot_of = lambda s: (b * n_steps + s) & 1

    @pl.when(b == 0)
    def _():
        _start_pages(srcs, bufs, sems, pt_ref, layer, 0, 0, 0, pps)

    def advance(s):
        nxt = 1 - slot_of(s)
        if s + 1 < n_steps:
            _start_pages(srcs, bufs, sems, pt_ref, layer, b, (s + 1) * pps, nxt, pps)
        else:
            @pl.when(b + 1 < nb)
            def _():
                _start_pages(srcs, bufs, sems, pt_ref, layer, b + 1, 0, nxt, pps)
        _wait_pages(srcs, bufs, sems, layer, slot_of(s), pps)

    return slot_of, advance


def _paged2_kernel(pt_ref, q_ref, k_hbm, v_hbm, knew_ref, vnew_ref, *rest, mode, layer, pps, n_steps, width, past,
                   new_len, lam_init):
    selx_ref = lamv_ref = nw_ref = None
    if mode == "slc":
        selx_ref, rest = rest[0], rest[1:]
    if mode == "diff":
        lamv_ref, nw_ref, rest = rest[0], rest[1], rest[2:]
    o_ref, kbuf, vbuf, ksem, vsem, qe_s, m_s, l_s, acc_s = rest
    nq = new_len
    rows = qe_s.shape[0]
    nk = pps * PAGE
    slot_of, advance = _page_ring(pt_ref, (k_hbm, v_hbm), (kbuf, vbuf), (ksem, vsem), layer, pps, n_steps)

    if mode == "diff":
        q = q_ref[0] * DIFF_QSCALE
        part = _iota((nq, width), 1) >> 5
        qe = jnp.concatenate([jnp.where(part == k, q, 0.0) for k in range(8)], axis=0)
    else:
        qe = _nsa_expand(q_ref[0] * NSA_QSCALE)
    qe_s[...] = qe.astype(BF16)
    _init_stats((m_s, l_s, acc_s))

    for s in range(n_steps):
        advance(s)
        slot = slot_of(s)
        kt_b = jnp.concatenate([kbuf[slot, r].astype(BF16) for r in range(pps)], axis=1)
        vt_b = jnp.concatenate([vbuf[slot, r].astype(BF16) for r in range(pps)], axis=1)
        sc = jnp.dot(qe_s[...], kt_b, preferred_element_type=F32)
        qpos = past + (_iota((rows, nk), 0) & (nq - 1))
        kpos = s * nk + _iota((rows, nk), 1)
        mask = kpos <= qpos
        if mode == "slc":
            mask = mask & (selx_ref[0, :, s * nk:(s + 1) * nk].astype(F32) > 0.5)
        _online_update(sc, mask, vt_b, m_s, l_s, acc_s, v_transposed=True)

    s2 = lax.dot_general(qe_s[...], knew_ref[0].astype(BF16), (((1,), (1,)), ((), ())), preferred_element_type=F32)
    qpos2 = past + (_iota((rows, nq), 0) & (nq - 1))
    kpos2 = past + _iota((rows, nq), 1)
    mask2 = kpos2 <= qpos2
    if mode == "slc":
        mask2 = mask2 & (selx_ref[0, :, past:past + nq].astype(F32) > 0.5)
    _online_update(s2, mask2, vnew_ref[0].astype(BF16), m_s, l_s, acc_s)
    o = acc_s[...] / jnp.maximum(l_s[...], 1e-30)
    if mode == "diff":
        lam = _diff_lambda(lamv_ref, lam_init)
        head = _iota((nq, width), 1) >> 6
        out = jnp.zeros((nq, width), F32)
        for h in range(4):
            out = jnp.where(head == h, o[(2 * h) * nq:(2 * h + 1) * nq] - lam * o[(2 * h + 1) * nq:(2 * h + 2) * nq],
                            out)
        o_ref[0] = _diff_finish(out, head, 4, nw_ref[...], lam_init)
    else:
        o_ref[0] = _nsa_collect(o, nq)


def _paged2_attn(mode, q, k_src, v_src, layer, table, knew, vnew, past, extra=(), lam_init=0.0):
    nb, nq = q.shape[0], q.shape[1]
    n_pages = table.shape[1]
    width = knew.shape[-1]
    pps = min(PAGES_PER_STEP, n_pages)
    n_steps = n_pages // pps
    assert n_pages % pps == 0 and past == n_pages * PAGE
    rows = 8 * nq if mode == "diff" else 4 * nq
    per_b = lambda r_, w: pl.BlockSpec((1, r_, w), lambda b, pt: (b, 0, 0))
    hbm = pl.BlockSpec(memory_space=pl.ANY)
    in_specs = [per_b(nq, 256), hbm, hbm, per_b(nq, width), per_b(nq, width)]
    args = [q, k_src, v_src, knew, vnew]
    if mode == "slc":
        (selx,) = extra
        in_specs.append(per_b(rows, selx.shape[-1]))
        args.append(selx)
    if mode == "diff":
        in_specs += [pl.BlockSpec((4, 32), lambda b, pt: (0, 0)), pl.BlockSpec((1, 256), lambda b, pt: (0, 0))]
        args += list(extra)
    return pl.pallas_call(
        functools.partial(_paged2_kernel, mode=mode, layer=layer, pps=pps, n_steps=n_steps, width=width, past=past,
                          new_len=nq, lam_init=lam_init),
        grid_spec=pltpu.PrefetchScalarGridSpec(
            num_scalar_prefetch=1, grid=(nb,), in_specs=in_specs, out_specs=per_b(nq, 256),
            scratch_shapes=[pltpu.VMEM((2, pps, width, PAGE), F32), pltpu.VMEM((2, pps, width, PAGE), F32),
                            pltpu.SemaphoreType.DMA((2,)), pltpu.SemaphoreType.DMA((2,)),
                            pltpu.VMEM((rows, width), BF16), pltpu.VMEM((rows, 1), F32),
                            pltpu.VMEM((rows, 1), F32), pltpu.VMEM((rows, width), F32)]),
        out_shape=jax.ShapeDtypeStruct((nb, nq, 256), F32),
        compiler_params=_params(("arbitrary",)),
        name="paged2_" + mode,
    )(table, *args)


def _summ2_kernel(pt_ref, k_hbm, v_hbm, mk_ref, mv_ref, ko_ref, vo_ref, kbuf, vbuf, ksem, vsem, *, layer, pps, n_steps):
    slot_of, advance = _page_ring(pt_ref, (k_hbm, v_hbm), (kbuf, vbuf), (ksem, vsem), layer, pps, n_steps)
    n_blk = pps * PAGE // NSA_CMP_BLOCK
    for s in range(n_steps):
        advance(s)
        slot = slot_of(s)
        for buf, m_ref, o_ref in ((kbuf, mk_ref, ko_ref), (vbuf, mv_ref, vo_ref)):
            x = jnp.concatenate([buf[slot, r].astype(BF16) for r in range(pps)], axis=1)
            top = jnp.dot(x[0:64], m_ref[0], preferred_element_type=F32)
            bot = jnp.dot(x[64:128], m_ref[1], preferred_element_type=F32)
            o_ref[0, :, s * n_blk:(s + 1) * n_blk] = jnp.concatenate([top, bot], axis=0)


def _summarize_pages(k_src, v_src, layer, table, w_k, w_v):
    nb, n_pages = table.shape
    pps = min(PAGES_PER_STEP, n_pages)
    n_steps = n_pages // pps
    step_w = pps * PAGE
    n_blk = step_w // NSA_CMP_BLOCK
    tok = jnp.arange(step_w, dtype=jnp.int32)
    in_block = tok[:, None] // NSA_CMP_BLOCK == jnp.arange(n_blk, dtype=jnp.int32)[None, :]
    weights = lambda w: jnp.where(in_block[None], w[:, tok % NSA_CMP_BLOCK][:, :, None], 0.0).astype(BF16)
    hbm = pl.BlockSpec(memory_space=pl.ANY)
    wspec = pl.BlockSpec((2, step_w, n_blk), lambda b, pt: (0, 0, 0))
    ospec = pl.BlockSpec((1, LANES, n_steps * n_blk), lambda b, pt: (b, 0, 0))
    oshape = jax.ShapeDtypeStruct((nb, LANES, n_steps * n_blk), F32)
    return pl.pallas_call(
        functools.partial(_summ2_kernel, layer=layer, pps=pps, n_steps=n_steps),
        grid_spec=pltpu.PrefetchScalarGridSpec(
            num_scalar_prefetch=1, grid=(nb,), in_specs=[hbm, hbm, wspec, wspec], out_specs=[ospec, ospec],
            scratch_shapes=[pltpu.VMEM((2, pps, LANES, PAGE), F32), pltpu.VMEM((2, pps, LANES, PAGE), F32),
                            pltpu.SemaphoreType.DMA((2,)), pltpu.SemaphoreType.DMA((2,))]),
        out_shape=[oshape, oshape],
        compiler_params=_params(("arbitrary",)),
        name="nsa_summarize_pages",
    )(table, k_src, v_src, weights(w_k), weights(w_v))


def _post_kernel(x_ref, ya_ref, misc_ref, oc_ref, os_ref, ow_ref, yc_ref, yd_ref, wo_ref, g1_ref, b1_ref,
                 w1_ref, w2_ref, g2_ref, b2_ref, o_ref, x1_s, x1b_s, acc_s, *, alpha):
    f = pl.program_id(1)

    @pl.when(f == 0)
    def _():
        yb = _nsa_gate(misc_ref[...], oc_ref[...], os_ref[...], ow_ref[...])
        mix = _dot(ya_ref[...], wo_ref[0:256, :])
        mix = mix + _dot(yb, wo_ref[256:512, :])
        mix = mix + _dot(yc_ref[...], wo_ref[512:768, :])
        mix = mix + _dot(yd_ref[...], wo_ref[768:1024, :])
        x1 = _layer_norm(alpha * x_ref[...] + mix, g1_ref[...], b1_ref[...])
        x1_s[...] = x1
        x1b_s[...] = x1.astype(BF16)
        acc_s[...] = jnp.zeros(acc_s.shape, F32)

    h = jnp.dot(x1b_s[...], w1_ref[...], preferred_element_type=F32)
    h = jnp.square(jnp.maximum(h, 0.0))
    acc_s[...] += jnp.dot(h.astype(BF16), w2_ref[...], preferred_element_type=F32)

    @pl.when(f == pl.num_programs(1) - 1)
    def _():
        o_ref[...] = _layer_norm(alpha * x1_s[...] + acc_s[...], g2_ref[...], b2_ref[...])


def _post(x2d, ya, misc, oc, os_, ow, yc, yd, p, tm, alpha):
    t = x2d.shape[0]
    tf = 1024
    row = lambda w: pl.BlockSpec((tm, w), lambda i, f: (i, 0))
    const = lambda r, w: pl.BlockSpec((r, w), lambda i, f: (0, 0))
    return pl.pallas_call(
        functools.partial(_post_kernel, alpha=alpha),
        grid=(t // tm, D_FF // tf),
        in_specs=[row(D_MODEL), row(256), row(LANES), row(256), row(256), row(256), row(256), row(256),
                  const(D_MODEL, D_MODEL), const(1, D_MODEL), const(1, D_MODEL),
                  pl.BlockSpec((D_MODEL, tf), lambda i, f: (0, f)), pl.BlockSpec((tf, D_MODEL), lambda i, f: (f, 0)),
                  const(1, D_MODEL), const(1, D_MODEL)],
        out_specs=row(D_MODEL),
        out_shape=jax.ShapeDtypeStruct((t, D_MODEL), F32),
        scratch_shapes=[pltpu.VMEM((tm, D_MODEL), F32), pltpu.VMEM((tm, D_MODEL), BF16),
                        pltpu.VMEM((tm, D_MODEL), F32)],
        compiler_params=_params(("parallel", "arbitrary")),
        name="post",
    )(x2d, ya, misc, oc.reshape(t, 256), os_.reshape(t, 256), ow.reshape(t, 256), yc, yd,
      p["w_out"], p["ln1_g"], p["ln1_b"], p["w_ff1"], p["w_ff2"], p["ln2_g"], p["ln2_b"])


def _prep_params(w, l):
    w_in = w["w_in"][l]
    misc = jnp.concatenate([w_in[:, a:b] for a, b in MISC_COLS], axis=1)
    misc = jnp.pad(misc, ((0, 0), (0, LANES - misc.shape[1])))
    w_in_p = jnp.concatenate([w_in[:, a:b] for a, b in W_IN_ORDER] + [misc], axis=1).astype(BF16)
    row = lambda v: v.reshape(1, -1).astype(F32)
    pad_lanes = lambda v: jnp.pad(v.reshape(1, -1), ((0, 0), (0, LANES - v.shape[-1])))
    blockdiag = lambda m: jax.scipy.linalg.block_diag(*[m[i] for i in range(m.shape[0])]).astype(BF16)
    return dict(
        w_in=w_in_p,
        ssd_cw=jnp.pad(w["ssd_conv_w"][l], ((0, 4), (0, 0))), ssd_cb=row(w["ssd_conv_b"][l]),
        ssd_dtb=pad_lanes(w["ssd_dt_bias"][l]), ssd_alog=pad_lanes(w["ssd_A_log"][l]),
        ssd_dexp=row(jnp.repeat(w["ssd_D"][l], 64)), ssd_nw=row(w["ssd_norm_w"][l]),
        cmp_wk=w["nsa_w_cmp_k"][l], cmp_wv=w["nsa_w_cmp_v"][l],
        diff_lambda=w["diff_lambda"][l], diff_nw256=row(jnp.tile(w["diff_norm_w"][l], 4)),
        lru_cw=jnp.pad(w["lru_conv_w"][l], ((0, 4), (0, 0))), lru_cb=row(w["lru_conv_b"][l]),
        lru_wa=blockdiag(w["lru_w_a"][l]), lru_ba=row(w["lru_b_a"][l]),
        lru_wx=blockdiag(w["lru_w_x"][l]), lru_bx=row(w["lru_b_x"][l]), lru_lam=row(w["lru_lambda"][l]),
        w_out=w["w_out"][l].astype(BF16), ln1_g=row(w["ln1_g"][l]), ln1_b=row(w["ln1_b"][l]),
        w_ff1=w["w_ff1"][l].astype(BF16), w_ff2=w["w_ff2"][l].astype(BF16),
        ln2_g=row(w["ln2_g"][l]), ln2_b=row(w["ln2_b"][l]),
    )


def _pad_prev(prev):
    return jnp.pad(prev, ((0, 0), (5, 0), (0, 0)))


def _token_minor(a):
    a = jnp.moveaxis(a, -3, -1)
    return a.reshape(*a.shape[:-3], a.shape[-3] * a.shape[-2], a.shape[-1])


def _token_major(a_t, heads):
    nb, _, n = a_t.shape
    return jnp.moveaxis(a_t.reshape(nb, heads, 64, n), -1, 1)


def _layer(x2d, nb, sl, past, hist, p, lam_init, alpha):
    t = nb * sl
    tm = min(256, t)
    pos = past + jnp.arange(sl, dtype=jnp.int32)
    pj = _proj(x2d, p["w_in"], pos, nb, sl, tm)
    y_a, ssd_h = _ssd(pj, nb, sl, _pad_prev(hist["ssd_conv"]), hist["ssd_h"].reshape(nb, 256, 128), p)
    y_d, lru_h = _lru(pj, nb, sl, _pad_prev(hist["lru_conv"]), hist["lru_h"], p)
    r3 = lambda a, w: a.reshape(nb, sl, w)
    dummy = jnp.zeros((1, 1), jnp.int32)
    if past == 0:
        step_w = min(4096, sl)
        spec = pl.BlockSpec((1, LANES, step_w), lambda b, s, pt: (b, 0, s))
        kcmp_t, vcmp_t = _summarize(pj["kcT"], pj["vcT"], [spec], step_w, nb, sl, dummy, p["cmp_wk"], p["cmp_wv"])
        o_cmp, sel = _cmp_select_prompt(pj["qn"], kcmp_t, vcmp_t, nb, sl, min(256, sl))
        o_slc = _nsa_prompt_attn(pj, sel, nb, sl)
        o_win = _nsa_prompt_window(pj, nb, sl)
        y_c = _diff_prompt(pj, nb, sl, p, lam_init)
        kw_t, vw_t = pj["kwT"][:, :, sl - NSA_WINDOW:], pj["vwT"][:, :, sl - NSA_WINDOW:]
        new = dict(kc=_token_major(pj["kcT"], 2), vc=_token_major(pj["vcT"], 2), ks=_token_major(pj["ksT"], 2),
                   vs=_token_major(pj["vsT"], 2), dk=_token_major(pj["dkT"], 4), dv=_token_major(pj["dvT"], 4))
    else:
        table, l = hist["table"], hist["layer"]
        n_pages = table.shape[1]
        win_spec = lambda r, per: pl.BlockSpec(
            (1, 1, LANES, PAGE), lambda b, s, pt, r=r, per=per: (l, b, 0, s * per + r))
        kcmp_t, vcmp_t = _summarize_pages(hist["kc"], hist["vc"], l, table, p["cmp_wk"], p["cmp_wv"])
        o_cmp, selx = _cmp_select_sample(pj["qn"], kcmp_t, vcmp_t, nb, sl, past)
        qnr = r3(pj["qnr"], 256)
        o_slc = _paged2_attn("slc", qnr, hist["ks"], hist["vs"], l, table, r3(pj["ks"], 128), r3(pj["vs"], 128),
                             past, extra=(selx,))
        wbuf = hist["kw"].shape[-1]
        o_win = _paged_attn("win", qnr, hist["kw"], hist["vw"], win_spec, wbuf // PAGE, dummy,
                            r3(pj["kw"], 128), r3(pj["vw"], 128), past - wbuf, past)
        y_c = _paged2_attn("diff", r3(pj["dq"], 256), hist["dk"], hist["dv"], l, table, r3(pj["dk"], 256),
                           r3(pj["dv"], 256), past, extra=(p["diff_lambda"], p["diff_nw256"]),
                           lam_init=lam_init).reshape(t, 256)
        kw_t = jnp.concatenate([hist["kw"][l], pj["kwT"]], axis=2)[:, :, -NSA_WINDOW:]
        vw_t = jnp.concatenate([hist["vw"][l], pj["vwT"]], axis=2)[:, :, -NSA_WINDOW:]
        kv = lambda a: r3(a, 128).reshape(nb, sl, 2, 64)
        new = dict(kc=kv(pj["kc"]), vc=kv(pj["vc"]), ks=kv(pj["ks"]), vs=kv(pj["vs"]),
                   dk=r3(pj["dk"], 256).reshape(nb, sl, 4, 64), dv=r3(pj["dv"], 256).reshape(nb, sl, 4, 64))
    x_out = _post(x2d, y_a, pj["misc"], o_cmp, o_slc, o_win, y_c, y_d, p, min(512, t), alpha)
    new.update(kw=_token_major(kw_t, 2), vw=_token_major(vw_t, 2),
               ssd_h=ssd_h.reshape(nb, 4, 64, 128), ssd_conv=r3(pj["xbc"], 768)[:, sl - 3:],
               lru_h=lru_h, lru_conv=r3(pj["lx"], 256)[:, sl - 3:])
    return x_out, new


STATE_ORDER = ("kc", "vc", "ks", "vs", "dk", "dv", "kw", "vw", "ssd_h", "ssd_conv", "lru_h", "lru_conv")


def kernel(x_prompt, x_sample, cache_nsa_k_cmp, cache_nsa_v_cmp, cache_nsa_k_slc, cache_nsa_v_slc, cache_diff_k, cache_diff_v, cache_nsa_k_win, cache_nsa_v_win, state_ssd, state_ssd_conv, state_lru, state_lru_conv, page_table, w_in, ssd_conv_w, ssd_conv_b, ssd_dt_bias, ssd_A_log, ssd_D, ssd_norm_w, nsa_w_cmp_k, nsa_w_cmp_v, diff_lambda, diff_norm_w, lru_conv_w, lru_conv_b, lru_w_a, lru_b_a, lru_w_x, lru_b_x, lru_lambda, w_out, ln1_g, ln1_b, w_ff1, w_ff2, ln2_g, ln2_b):
    weights = dict(w_in=w_in, ssd_conv_w=ssd_conv_w, ssd_conv_b=ssd_conv_b, ssd_dt_bias=ssd_dt_bias,
                   ssd_A_log=ssd_A_log, ssd_D=ssd_D, ssd_norm_w=ssd_norm_w, nsa_w_cmp_k=nsa_w_cmp_k,
                   nsa_w_cmp_v=nsa_w_cmp_v, diff_lambda=diff_lambda, diff_norm_w=diff_norm_w, lru_conv_w=lru_conv_w,
                   lru_conv_b=lru_conv_b, lru_w_a=lru_w_a, lru_b_a=lru_b_a, lru_w_x=lru_w_x, lru_b_x=lru_b_x,
                   lru_lambda=lru_lambda, w_out=w_out, ln1_g=ln1_g, ln1_b=ln1_b, w_ff1=w_ff1, w_ff2=w_ff2,
                   ln2_g=ln2_g, ln2_b=ln2_b)
    depth = w_in.shape[0]
    nbp, slp, _ = x_prompt.shape
    nbs, sls, _ = x_sample.shape
    past = page_table.shape[1] * cache_nsa_k_cmp.shape[2]
    assert past % PAGE == 0 and sls < NSA_CMP_BLOCK and slp % 256 == 0
    alpha = (2 * depth) ** 0.25
    caches = dict(kc=_token_minor(cache_nsa_k_cmp), vc=_token_minor(cache_nsa_v_cmp), ks=_token_minor(cache_nsa_k_slc),
                  vs=_token_minor(cache_nsa_v_slc), dk=_token_minor(cache_diff_k), dv=_token_minor(cache_diff_v),
                  kw=_token_minor(cache_nsa_k_win), vw=_token_minor(cache_nsa_v_win))
    xp = x_prompt.reshape(nbp * slp, D_MODEL)
    xs = x_sample.reshape(nbs * sls, D_MODEL)
    outs_p, outs_s = [], []
    for l in range(depth):
        p = _prep_params(weights, l)
        lam_init = 0.8 - 0.6 * math.exp(-0.3 * l)
        hist_p = dict(ssd_conv=jnp.zeros((nbp, 3, SSD_CONV_CH), F32), ssd_h=jnp.zeros((nbp, 4, 64, 128), F32),
                      lru_conv=jnp.zeros((nbp, 3, GROUP_W), F32), lru_h=jnp.zeros((nbp, GROUP_W), F32))
        hist_s = dict(table=page_table, layer=l, ssd_conv=state_ssd_conv[l], ssd_h=state_ssd[l],
                      lru_conv=state_lru_conv[l], lru_h=state_lru[l], **caches)
        xp, new_p = _layer(xp, nbp, slp, 0, hist_p, p, lam_init, alpha)
        xs, new_s = _layer(xs, nbs, sls, past, hist_s, p, lam_init, alpha)
        outs_p.append(new_p)
        outs_s.append(new_s)
    stack = lambda lst, name: jnp.stack([d[name] for d in lst])
    return ((xp.reshape(nbp, slp, D_MODEL), xs.reshape(nbs, sls, D_MODEL))
            + tuple(stack(outs_p, n) for n in STATE_ORDER) + tuple(stack(outs_s, n) for n in STATE_ORDER))
```

```python
import functools
import math

import jax
import jax.numpy as jnp
from jax import lax
from jax.experimental import pallas as pl
from jax.experimental.pallas import tpu as pltpu

F32 = jnp.float32
BF16 = jnp.bfloat16
NEG_INF = float("-inf")

D_MODEL = 1024
GROUP_W = 256
SSD_HEADS = 4
SSD_STATE = 128
SSD_CONV_CH = 768
NSA_CMP_BLOCK = 32
NSA_SLC_BLOCK = 64
NSA_TOPN = 16
NSA_WINDOW = 512
FORCE_SCORE = 1e4
LRU_C = 8.0
D_FF = 4096
ROPE_THETA = 500000.0
EPS = 1e-5
PAGE = 128
LANES = 128
VMEM_LIMIT = 56 * 1024 * 1024
PAGES_PER_STEP = 32
LAZY_MAX_RISE = 64.0
LOG2E = 1.4426950408889634
NSA_QSCALE = 64.0 ** -0.5 * LOG2E
DIFF_QSCALE = 32.0 ** -0.5 * LOG2E

W_IN_ORDER = ((0, 1024), (1028, 2052), (2064, 3344))
MISC_COLS = ((1024, 1028), (2052, 2064))
SEG = dict(z=(0, 256), xbc=(256, 1024), qn=(1024, 1280), kc=(1280, 1408), vc=(1408, 1536), ks=(1536, 1664),
           vs=(1664, 1792), kw=(1792, 1920), vw=(1920, 2048), dq=(2048, 2304), dk=(2304, 2560), dv=(2560, 2816),
           lx=(2816, 3072), lg=(3072, 3328), misc=(3328, 3456))
D_IN_PAD = 3456
GATE_LANE0 = 4


def _params(sem):
    return pltpu.CompilerParams(dimension_semantics=sem, vmem_limit_bytes=VMEM_LIMIT)


def _iota(shape, dim):
    return lax.broadcasted_iota(jnp.int32, shape, dim)


def _dot(a, b):
    return jnp.dot(a.astype(BF16), b.astype(BF16), preferred_element_type=F32)


def _dot_nt(a, b):
    return lax.dot_general(a.astype(BF16), b.astype(BF16), (((1,), (1,)), ((), ())), preferred_element_type=F32)


def _split3(x):
    h1 = x.astype(BF16)
    r1 = x - h1.astype(F32)
    h2 = r1.astype(BF16)
    h3 = (r1 - h2.astype(F32)).astype(BF16)
    return h1, h2, h3


def _dot_01(m01, x):
    return sum(jnp.dot(m01, part, preferred_element_type=F32) for part in _split3(x))


def _dot_01_r(x, m01):
    return sum(jnp.dot(part, m01, preferred_element_type=F32) for part in _split3(x))


def _dot_nt_01(m01, x):
    return sum(lax.dot_general(m01, part, (((1,), (1,)), ((), ())), preferred_element_type=F32)
               for part in _split3(x))


def _eye(n, m):
    return (_iota((n, m), 0) == _iota((n, m), 1)).astype(BF16)


def _softplus(x):
    return jnp.maximum(x, 0.0) + jnp.log1p(jnp.exp(-jnp.abs(x)))


def _head_bcast(cols, h_of_lane, nheads, lane0=0):
    out = jnp.zeros(h_of_lane.shape, F32)
    for h in range(nheads):
        out = jnp.where(h_of_lane == h, cols[:, lane0 + h:lane0 + h + 1], out)
    return out


def _online_update(s, mask, v_b, m_ref, l_ref, acc_ref, v_transposed=False):
    s = jnp.where(mask, s, NEG_INF)
    m_prev = m_ref[...]
    m_new = jnp.maximum(m_prev, jnp.max(s, axis=-1, keepdims=True))
    m_safe = jnp.where(m_new == NEG_INF, 0.0, m_new)
    alpha = jnp.exp2(m_prev - m_safe)
    p = jnp.exp2(s - m_safe)
    l_ref[...] = alpha * l_ref[...] + jnp.sum(p, axis=-1, keepdims=True)
    if v_transposed:
        pv = lax.dot_general(p.astype(BF16), v_b, (((1,), (1,)), ((), ())), preferred_element_type=F32)
    else:
        pv = jnp.dot(p.astype(BF16), v_b, preferred_element_type=F32)
    acc_ref[...] = alpha * acc_ref[...] + pv
    m_ref[...] = m_new


def _online_update_t(s, mask, vt_b, m_ref, l_ref, acc_ref):
    if mask is not None:
        s = jnp.where(mask, s, NEG_INF)
    m_prev = m_ref[...]
    m_new = jnp.maximum(m_prev, jnp.max(s, axis=0, keepdims=True))
    m_safe = jnp.where(m_new == NEG_INF, 0.0, m_new)
    alpha = jnp.exp2(m_prev - m_safe)
    p = jnp.exp2(s - m_safe)
    l_ref[...] = alpha * l_ref[...] + jnp.sum(p, axis=0, keepdims=True)
    acc_ref[...] = alpha * acc_ref[...] + jnp.dot(vt_b, p.astype(BF16), preferred_element_type=F32)
    m_ref[...] = m_new


def _online_update_t_lazy(scores, mask, vt_b, m_ref, l_ref, acc_ref):
    s = scores()
    if mask is not None:
        s = jnp.where(mask, s, NEG_INF)
    m_prev = m_ref[...]
    tile_max = jnp.max(s, axis=0, keepdims=True)
    p = jnp.exp2(s - m_prev)
    p_sum = jnp.sum(p, axis=0, keepdims=True)
    pv = jnp.dot(vt_b, p.astype(BF16), preferred_element_type=F32)
    safe = jnp.max(tile_max - m_prev) <= LAZY_MAX_RISE

    @pl.when(safe)
    def _():
        m_new = jnp.maximum(m_prev, tile_max)
        alpha = jnp.exp2(m_prev - m_new)
        l_ref[...] = (l_ref[...] + p_sum) * alpha
        acc_ref[...] = (acc_ref[...] + pv) * alpha
        m_ref[...] = m_new

    @pl.when(jnp.logical_not(safe))
    def _():
        _online_update_t(scores(), mask, vt_b, m_ref, l_ref, acc_ref)


def _causal_pairs(n_q, qt, kt):
    qi, kj = [], []
    for i in range(n_q):
        for j in range((i * qt + qt - 1) // kt + 1):
            qi.append(i)
            kj.append(j)
    return jnp.asarray(qi, jnp.int32), jnp.asarray(kj, jnp.int32)


def _init_stats(*triples):
    for m_, l_, a_ in triples:
        m_[...] = jnp.full(m_.shape, NEG_INF, F32)
        l_[...] = jnp.zeros(l_.shape, F32)
        a_[...] = jnp.zeros(a_.shape, F32)


def _nsa_expand(q):
    lo = _iota((q.shape[0], LANES), 1) < 64
    a, b = q[:, :LANES], q[:, LANES:]
    h0 = jnp.where(lo, a, 0.0)
    h1 = jnp.where(lo, pltpu.roll(a, 64, 1), 0.0)
    h2 = jnp.where(lo, 0.0, pltpu.roll(b, 64, 1))
    h3 = jnp.where(lo, 0.0, b)
    return jnp.concatenate([h0, h1, h2, h3], axis=0)


def _nsa_collect(o, n):
    lo = _iota((n, LANES), 1) < 64
    o0, o1, o2, o3 = o[0:n], o[n:2 * n], o[2 * n:3 * n], o[3 * n:4 * n]
    left = jnp.where(lo, o0, pltpu.roll(o1, 64, 1))
    right = jnp.where(lo, pltpu.roll(o2, 64, 1), o3)
    return jnp.concatenate([left, right], axis=1)


def _collect_t(o_t, n):
    return jnp.concatenate([o_t[0:64, 0:n], o_t[0:64, n:2 * n], o_t[64:128, 2 * n:3 * n], o_t[64:128, 3 * n:4 * n]],
                           axis=0)


def _layer_norm(v, g, b):
    mu = jnp.mean(v, axis=-1, keepdims=True)
    d = v - mu
    var = jnp.mean(d * d, axis=-1, keepdims=True)
    return d * lax.rsqrt(var + EPS) * g + b


def _rope128(v, c, sa, sb, half):
    return v * c + pltpu.roll(v, LANES - half, 1) * sa + pltpu.roll(v, half, 1) * sb


PROJ_OUT = ("z", "xbc", "qn", "qnr", "kc", "vc", "ks", "vs", "kw", "vw", "dq", "dk", "dv", "lx", "lg", "misc")
PROJ_OUT_T = ("kcT", "vcT", "ksT", "vsT", "kwT", "vwT", "dkT", "dvT")
PROJ_W = dict(z=256, xbc=768, qn=256, qnr=256, kc=128, vc=128, ks=128, vs=128, kw=128, vw=128, dq=256, dk=256,
              dv=256, lx=256, lg=256, misc=128)


def _proj_kernel(x_ref, w_ref, cn_ref, san_ref, sbn_ref, cd_ref, sad_ref, sbd_ref, *outs, flat_t):
    o = dict(zip(PROJ_OUT + PROJ_OUT_T, outs))
    xb = x_ref[...].astype(BF16)

    def seg(name):
        a, b = SEG[name]
        return jnp.dot(xb, w_ref[:, a:b], preferred_element_type=F32)

    def rope_n(v):
        return _rope128(v, cn_ref[...], san_ref[...], sbn_ref[...], 8)

    def rope_d(v):
        return _rope128(v, cd_ref[...], sad_ref[...], sbd_ref[...], 4)

    def put(name, v):
        o[name][...] = v
        if name + "T" in o:
            if flat_t:
                o[name + "T"][...] = v.T
            else:
                o[name + "T"][0] = v.T

    put("z", seg("z"))
    put("xbc", seg("xbc"))
    qn = seg("qn")
    put("qn", qn)
    put("qnr", jnp.concatenate([rope_n(qn[:, :LANES]), rope_n(qn[:, LANES:])], axis=1))
    put("kc", seg("kc"))
    put("vc", seg("vc"))
    put("ks", rope_n(seg("ks")))
    put("vs", seg("vs"))
    put("kw", rope_n(seg("kw")))
    put("vw", seg("vw"))
    dq = seg("dq")
    put("dq", jnp.concatenate([rope_d(dq[:, :LANES]), rope_d(dq[:, LANES:])], axis=1))
    dk = seg("dk")
    put("dk", jnp.concatenate([rope_d(dk[:, :LANES]), rope_d(dk[:, LANES:])], axis=1))
    put("dv", seg("dv"))
    put("lx", seg("lx"))
    put("lg", seg("lg"))
    put("misc", seg("misc"))


def _rope_tables(pos, dh, rows):
    rd = dh // 4
    half = rd // 2
    inv = ROPE_THETA ** (-jnp.arange(half, dtype=F32) / half)
    ang = pos.astype(F32)[:, None] * inv[None, :]
    cos, sin = jnp.cos(ang), jnp.sin(ang)
    n = pos.shape[0]
    zero_h = jnp.zeros((n, half), F32)
    zero_r = jnp.zeros((n, dh - rd), F32)
    c = jnp.concatenate([cos, cos, jnp.ones((n, dh - rd), F32)], 1)
    sa = jnp.concatenate([-sin, zero_h, zero_r], 1)
    sb = jnp.concatenate([zero_h, sin, zero_r], 1)
    reps = (max(rows // n, 1), LANES // dh)
    return tuple(jnp.tile(t, reps) for t in (c, sa, sb))


def _proj(x2d, w_in_p, pos, nb, sl, tm):
    t = x2d.shape[0]
    n_tab = max(sl // tm, 1)
    flat_t = sl < tm
    tabs = _rope_tables(pos, 64, tm) + _rope_tables(pos, 32, tm)
    tab_spec = pl.BlockSpec((tm, LANES), lambda i: (i % n_tab, 0))
    out_specs = [pl.BlockSpec((tm, PROJ_W[k]), lambda i: (i, 0)) for k in PROJ_OUT]
    out_shape = [jax.ShapeDtypeStruct((t, PROJ_W[k]), F32) for k in PROJ_OUT]
    for k in PROJ_OUT_T:
        c = PROJ_W[k[:-1]]
        if flat_t:
            out_specs.append(pl.BlockSpec((c, tm), lambda i: (0, i)))
            out_shape.append(jax.ShapeDtypeStruct((c, t), F32))
        else:
            out_specs.append(pl.BlockSpec((1, c, tm), lambda i: (i // n_tab, 0, i % n_tab)))
            out_shape.append(jax.ShapeDtypeStruct((nb, c, sl), F32))
    outs = pl.pallas_call(
        functools.partial(_proj_kernel, flat_t=flat_t),
        grid=(t // tm,),
        in_specs=[pl.BlockSpec((tm, D_MODEL), lambda i: (i, 0)),
                  pl.BlockSpec((D_MODEL, D_IN_PAD), lambda i: (0, 0))] + [tab_spec] * 6,
        out_specs=out_specs, out_shape=out_shape,
        compiler_params=_params(("parallel",)),
        name="proj",
    )(x2d, w_in_p, *tabs)
    pj = dict(zip(PROJ_OUT + PROJ_OUT_T, outs))
    if flat_t:
        for k in PROJ_OUT_T:
            pj[k] = pj[k].reshape(-1, nb, sl).transpose(1, 0, 2)
    return pj


def _conv_chunk(x_ref, prev_ref, cw_ref, cb_ref, xbuf, q, first):
    @pl.when(first)
    def _():
        xbuf[0:8, :] = prev_ref[0]

    xbuf[8:8 + q, :] = x_ref[0]
    acc = cb_ref[...] + cw_ref[0:1, :] * xbuf[pl.ds(5, q), :]
    for j in range(1, 4):
        acc = acc + cw_ref[j:j + 1, :] * xbuf[pl.ds(5 + j, q), :]
    xbuf[0:8, :] = xbuf[q:q + 8, :]
    return acc


def _ssd_kernel(xbc_ref, z_ref, misc_ref, prev_ref, h0_ref, cw_ref, cb_ref, dtb_ref, alog_ref, dexp_ref, nw_ref,
                y_ref, hout_ref, xbuf, ht, *, q):
    c = pl.program_id(1)

    @pl.when(c == 0)
    def _():
        ht[...] = _dot_nt_01(_eye(SSD_STATE, SSD_STATE), h0_ref[0])

    acc = _conv_chunk(xbc_ref, prev_ref, cw_ref, cb_ref, xbuf, q, c == 0)
    xc = acc * jax.nn.sigmoid(acc)
    sx = xc[:, 0:256]
    sb = (xc[:, 256:384], xc[:, 384:512])
    sc = (xc[:, 512:640], xc[:, 640:768])
    dt = _softplus(misc_ref[0] + dtb_ref[...])
    da = dt * (-jnp.exp(alog_ref[...]))
    causal = _iota((q, q), 0) >= _iota((q, q), 1)
    acum = _dot_01(causal.astype(BF16), da)
    xsel = _eye(8, LANES)
    acum_t = _dot_nt_01(xsel, acum)
    dt_t = _dot_nt_01(xsel, dt)
    a_last = acum[q - 1:q, :]
    wlast = jnp.exp(a_last - acum) * dt
    ea = jnp.exp(acum)
    head = _iota((q, GROUP_W), 1) >> 6
    ht_old = ht[...]
    y = dexp_ref[...] * sx
    for g in range(2):
        cb = _dot_nt(sc[g], sb[g])
        for h in (2 * g, 2 * g + 1):
            seg = acum[:, h:h + 1] - acum_t[h:h + 1, :]
            decay = jnp.exp(jnp.where(causal, seg, NEG_INF))
            m = cb * decay * dt_t[h:h + 1, :]
            y_h = _dot(m, sx) + _dot(sc[g] * ea[:, h:h + 1], ht_old)
            y = y + jnp.where(head == h, y_h, 0.0)
    xw = sx * _head_bcast(wlast, head, SSD_HEADS)
    lane = _iota((q, GROUP_W), 1)
    bt0 = _dot_nt(_eye(SSD_STATE, SSD_STATE), sb[0])
    bt1 = _dot_nt(_eye(SSD_STATE, SSD_STATE), sb[1])
    head1 = _iota((1, GROUP_W), 1) >> 6
    dch = _head_bcast(jnp.exp(a_last), head1, SSD_HEADS)
    ht_new = dch * ht_old + _dot(bt0, jnp.where(lane < 128, xw, 0.0)) + _dot(bt1, jnp.where(lane < 128, 0.0, xw))
    ht[...] = ht_new
    zz = z_ref[0]
    y = y * (zz * jax.nn.sigmoid(zz))
    y_ref[0] = y * lax.rsqrt(jnp.mean(y * y, axis=-1, keepdims=True) + EPS) * nw_ref[...]

    @pl.when(c == pl.num_programs(1) - 1)
    def _():
        hout_ref[0] = _dot_nt_01(_eye(GROUP_W, GROUP_W), ht_new)


def _ssd(pj, nb, sl, prev8, h0, p):
    q = min(128, sl)
    nc = sl // q
    row = lambda w: pl.BlockSpec((1, q, w), lambda b, c: (b, c, 0))
    per_b = lambda r, w: pl.BlockSpec((1, r, w), lambda b, c: (b, 0, 0))
    const = lambda r, w: pl.BlockSpec((r, w), lambda b, c: (0, 0))
    y, hout = pl.pallas_call(
        functools.partial(_ssd_kernel, q=q),
        grid=(nb, nc),
        in_specs=[row(768), row(256), row(128), per_b(8, 768), per_b(256, 128),
                  const(8, 768), const(1, 768), const(1, 128), const(1, 128), const(1, 256), const(1, 256)],
        out_specs=[row(256), per_b(256, 128)],
        out_shape=[jax.ShapeDtypeStruct((nb, sl, 256), F32), jax.ShapeDtypeStruct((nb, 256, 128), F32)],
        scratch_shapes=[pltpu.VMEM((q + 8, 768), F32), pltpu.VMEM((SSD_STATE, GROUP_W), F32)],
        compiler_params=_params(("parallel", "arbitrary")),
        name="ssd",
    )(pj["xbc"].reshape(nb, sl, 768), pj["z"].reshape(nb, sl, 256), pj["misc"].reshape(nb, sl, 128), prev8, h0,
      p["ssd_cw"], p["ssd_cb"], p["ssd_dtb"], p["ssd_alog"], p["ssd_dexp"], p["ssd_nw"])
    return y.reshape(nb * sl, 256), hout


def _lru_kernel(x_ref, g_ref, prev_ref, h0_ref, cw_ref, cb_ref, wa_ref, ba_ref, wx_ref, bx_ref, lam_ref,
                y_ref, hout_ref, xbuf, hc, *, q):
    c = pl.program_id(1)

    @pl.when(c == 0)
    def _():
        hc[...] = h0_ref[0]

    xc = _conv_chunk(x_ref, prev_ref, cw_ref, cb_ref, xbuf, q, c == 0)
    xcb = xc.astype(BF16)
    r = jax.nn.sigmoid(jnp.dot(xcb, wa_ref[...], preferred_element_type=F32) + ba_ref[...])
    i = jax.nn.sigmoid(jnp.dot(xcb, wx_ref[...], preferred_element_type=F32) + bx_ref[...])
    log_a = -LRU_C * r * _softplus(-lam_ref[...])
    a = jnp.exp(log_a)
    u = jnp.sqrt(1.0 - jnp.exp(2.0 * log_a)) * (i * xc)
    row = _iota((q, GROUP_W), 0)
    s = 1
    while s < q:
        a_sh = pltpu.roll(a, s, 0)
        u_sh = pltpu.roll(u, s, 0)
        keep = row >= s
        u = jnp.where(keep, a * u_sh + u, u)
        a = jnp.where(keep, a * a_sh, a)
        s *= 2
    h = u + a * hc[...]
    hc[...] = h[q - 1:q, :]
    gg = g_ref[0]
    gelu = 0.5 * gg * (1.0 + jnp.tanh(math.sqrt(2.0 / math.pi) * (gg + 0.044715 * (gg * gg * gg))))
    y_ref[0] = h * gelu

    @pl.when(c == pl.num_programs(1) - 1)
    def _():
        hout_ref[0] = h[q - 1:q, :]


def _lru(pj, nb, sl, prev8, h0, p):
    q = min(256, sl)
    nc = sl // q
    row = lambda w: pl.BlockSpec((1, q, w), lambda b, c: (b, c, 0))
    per_b = lambda r, w: pl.BlockSpec((1, r, w), lambda b, c: (b, 0, 0))
    const = lambda r, w: pl.BlockSpec((r, w), lambda b, c: (0, 0))
    y, hout = pl.pallas_call(
        functools.partial(_lru_kernel, q=q),
        grid=(nb, nc),
        in_specs=[row(256), row(256), per_b(8, 256), per_b(1, 256), const(8, 256), const(1, 256),
                  const(256, 256), const(1, 256), const(256, 256), const(1, 256), const(1, 256)],
        out_specs=[row(256), per_b(1, 256)],
        out_shape=[jax.ShapeDtypeStruct((nb, sl, 256), F32), jax.ShapeDtypeStruct((nb, 1, 256), F32)],
        scratch_shapes=[pltpu.VMEM((q + 8, 256), F32), pltpu.VMEM((1, 256), F32)],
        compiler_params=_params(("parallel", "arbitrary")),
        name="lru",
    )(pj["lx"].reshape(nb, sl, 256), pj["lg"].reshape(nb, sl, 256), prev8, h0.reshape(nb, 1, 256),
      p["lru_cw"], p["lru_cb"], p["lru_wa"], p["lru_ba"], p["lru_wx"], p["lru_bx"], p["lru_lam"])
    return y.reshape(nb * sl, 256), hout.reshape(nb, 256)


def _summ_kernel(pt_ref, *refs, n_parts):
    kp, vp = refs[:n_parts], refs[n_parts:2 * n_parts]
    mk_ref, mv_ref, ko_ref, vo_ref = refs[2 * n_parts:]

    def one(parts, m_ref, o_ref):
        chunks = []
        for r in parts:
            x = r[...]
            chunks.append(x.reshape(x.shape[-2], x.shape[-1]).astype(BF16))
        x = jnp.concatenate(chunks, axis=1) if len(chunks) > 1 else chunks[0]
        top = jnp.dot(x[0:64], m_ref[0], preferred_element_type=F32)
        bot = jnp.dot(x[64:128], m_ref[1], preferred_element_type=F32)
        o_ref[0] = jnp.concatenate([top, bot], axis=0)

    one(kp, mk_ref, ko_ref)
    one(vp, mv_ref, vo_ref)


def _summarize(k_src, v_src, part_specs, part_w, nb, total, table, w_k, w_v):
    n_parts = len(part_specs)
    step_w = n_parts * part_w
    n_blk = step_w // NSA_CMP_BLOCK
    tok = jnp.arange(step_w, dtype=jnp.int32)
    in_block = tok[:, None] // NSA_CMP_BLOCK == jnp.arange(n_blk, dtype=jnp.int32)[None, :]
    weights = lambda w: jnp.where(in_block[None], w[:, tok % NSA_CMP_BLOCK][:, :, None], 0.0).astype(BF16)
    wspec = pl.BlockSpec((2, step_w, n_blk), lambda b, s, pt: (0, 0, 0))
    ospec = pl.BlockSpec((1, LANES, n_blk), lambda b, s, pt: (b, 0, s))
    oshape = jax.ShapeDtypeStruct((nb, LANES, total // NSA_CMP_BLOCK), F32)
    return pl.pallas_call(
        functools.partial(_summ_kernel, n_parts=n_parts),
        grid_spec=pltpu.PrefetchScalarGridSpec(
            num_scalar_prefetch=1, grid=(nb, total // step_w),
            in_specs=list(part_specs) * 2 + [wspec, wspec],
            out_specs=[ospec, ospec]),
        out_shape=[oshape, oshape],
        compiler_params=_params(("parallel", "arbitrary")),
        name="nsa_summarize",
    )(table, *([k_src] * n_parts), *([v_src] * n_parts), weights(w_k), weights(w_v))


def _cmpsel_t_kernel(q_ref, kct_ref, vct_ref, ocmp_ref, sel_ref, *, qt, nc, ns):
    qi = pl.program_id(1)
    q4 = _nsa_expand(q_ref[0] * 0.125)
    s = _dot_nt(kct_ref[0].T, q4)
    qpos = qi * qt + (_iota((nc, 4 * qt), 1) & (qt - 1))
    blk_end = (_iota((nc, 4 * qt), 0) + 1) * NSA_CMP_BLOCK - 1
    s = jnp.where(blk_end <= qpos, s, NEG_INF)
    m = jnp.max(s, axis=0, keepdims=True)
    m = jnp.where(m == NEG_INF, 0.0, m)
    e = jnp.exp(s - m)
    p = e / jnp.maximum(jnp.sum(e, axis=0, keepdims=True), 1e-30)
    ocmp_ref[0] = _collect_t(_dot(vct_ref[0], p), qt).T
    impc = jnp.concatenate([p[:, 0:qt] + p[:, qt:2 * qt], p[:, 2 * qt:3 * qt] + p[:, 3 * qt:4 * qt]], axis=1)
    pair = (_iota((ns, nc), 1) >> 1 == _iota((ns, nc), 0)).astype(BF16)
    imp = _dot_01(pair, impc)
    blk = _iota((ns, 2 * qt), 0)
    cur = (qi * qt + (_iota((ns, 2 * qt), 1) & (qt - 1))) >> 6
    forced = (blk == 0) | (blk == cur) | (blk == cur - 1)
    score = jnp.where(forced, FORCE_SCORE, jnp.where(blk > cur, -1.0, imp))
    cnt = jnp.zeros((ns, 2 * qt), F32)
    for j in range(ns):
        row = score[j:j + 1, :]
        cnt = cnt + ((row > score) | ((row == score) & (blk > j))).astype(F32)
    sel = (cnt < float(NSA_TOPN)).astype(F32)
    sel_ref[0, 0] = sel[:, 0:qt]
    sel_ref[0, 1] = sel[:, qt:2 * qt]


def _cmp_select_prompt(qn, kcmp_t, vcmp_t, nb, sl, qt):
    nc = kcmp_t.shape[2]
    ns = sl // NSA_SLC_BLOCK
    return pl.pallas_call(
        functools.partial(_cmpsel_t_kernel, qt=qt, nc=nc, ns=ns),
        grid=(nb, sl // qt),
        in_specs=[pl.BlockSpec((1, qt, 256), lambda b, i: (b, i, 0)),
                  pl.BlockSpec((1, LANES, nc), lambda b, i: (b, 0, 0)),
                  pl.BlockSpec((1, LANES, nc), lambda b, i: (b, 0, 0))],
        out_specs=[pl.BlockSpec((1, qt, 256), lambda b, i: (b, i, 0)),
                   pl.BlockSpec((1, 2, ns, qt), lambda b, i: (b, 0, 0, i))],
        out_shape=[jax.ShapeDtypeStruct((nb, sl, 256), F32), jax.ShapeDtypeStruct((nb, 2, ns, sl), F32)],
        compiler_params=_params(("parallel", "parallel")),
        name="nsa_cmp_select_prompt",
    )(qn.reshape(nb, sl, 256), kcmp_t, vcmp_t)


def _cmpsel_kernel(q_ref, kct_ref, vct_ref, ocmp_ref, selx_ref, *, qt, nc, ns, nsp, past, n_chunks):
    q4 = _nsa_expand(q_ref[0] * 0.125)
    s = _dot(q4, kct_ref[0])
    qpos = past + (_iota((4 * qt, nc), 0) & (qt - 1))
    blk_end = (_iota((4 * qt, nc), 1) + 1) * NSA_CMP_BLOCK - 1
    s = jnp.where(blk_end <= qpos, s, NEG_INF)
    m = jnp.max(s, axis=-1, keepdims=True)
    m = jnp.where(m == NEG_INF, 0.0, m)
    e = jnp.exp(s - m)
    p = e / jnp.maximum(jnp.sum(e, axis=-1, keepdims=True), 1e-30)
    ocmp_ref[0] = _nsa_collect(_dot_nt(p, vct_ref[0]), qt)
    impc = jnp.concatenate([p[0:qt] + p[qt:2 * qt], p[2 * qt:3 * qt] + p[3 * qt:4 * qt]], axis=0)
    pair = (_iota((nc, nsp), 0) >> 1 == _iota((nc, nsp), 1)).astype(BF16)
    imp = _dot_01_r(impc, pair)
    blk = _iota((2 * qt, nsp), 1)
    cur = (past + (_iota((2 * qt, nsp), 0) & (qt - 1))) >> 6
    forced = (blk == 0) | (blk == cur) | (blk == cur - 1)
    score = jnp.where(forced, FORCE_SCORE, jnp.where(blk > cur, -1.0, imp))
    score = jnp.where(blk < ns, score, -3.0)

    cnt = jnp.zeros((2 * qt, nsp), F32)
    for j in range(ns):
        col = score[:, j:j + 1]
        cnt = cnt + ((col > score) | ((col == score) & (blk > j))).astype(F32)
    sel = ((cnt < float(NSA_TOPN)) & (blk < ns)).astype(F32)
    sel4 = jnp.concatenate([sel[0:qt], sel[0:qt], sel[qt:2 * qt], sel[qt:2 * qt]], axis=0).astype(BF16)
    e16 = (_iota((LANES, 1024), 1) >> 6 == _iota((LANES, 1024), 0)).astype(BF16)
    for t in range(n_chunks):
        pick = ((_iota((nsp, LANES), 0) == 16 * t + _iota((nsp, LANES), 1))
                & (_iota((nsp, LANES), 1) < 16)).astype(BF16)
        blocks = jnp.dot(sel4, pick, preferred_element_type=F32).astype(BF16)
        selx_ref[0, :, 1024 * t:1024 * (t + 1)] = jnp.dot(blocks, e16, preferred_element_type=F32).astype(BF16)


def _cmp_select_sample(qn, kcmp_t, vcmp_t, nb, sl, past):
    nc = kcmp_t.shape[2]
    ns = -(-(past + sl) // NSA_SLC_BLOCK)
    nsp = -(-ns // LANES) * LANES
    n_chunks = -(-(past + sl) // 1024)
    return pl.pallas_call(
        functools.partial(_cmpsel_kernel, qt=sl, nc=nc, ns=ns, nsp=nsp, past=past, n_chunks=n_chunks),
        grid=(nb,),
        in_specs=[pl.BlockSpec((1, sl, 256), lambda b: (b, 0, 0)),
                  pl.BlockSpec((1, LANES, nc), lambda b: (b, 0, 0)),
                  pl.BlockSpec((1, LANES, nc), lambda b: (b, 0, 0))],
        out_specs=[pl.BlockSpec((1, sl, 256), lambda b: (b, 0, 0)),
                   pl.BlockSpec((1, 4 * sl, 1024 * n_chunks), lambda b: (b, 0, 0))],
        out_shape=[jax.ShapeDtypeStruct((nb, sl, 256), F32),
                   jax.ShapeDtypeStruct((nb, 4 * sl, 1024 * n_chunks), BF16)],
        compiler_params=_params(("parallel",)),
        name="nsa_cmp_select_sample",
    )(qn.reshape(nb, sl, 256), kcmp_t, vcmp_t)


def _nsap_kernel(qi_ref, kj_ref, q_ref, ks_ref, vst_ref, sel_ref, oslc_ref, q4_s, m1, l1, a1, *, qt, kt):
    step = pl.program_id(1)
    qi, kj = qi_ref[step], kj_ref[step]
    last = (qi * qt + qt - 1) // kt

    @pl.when(kj == 0)
    def _():
        q4_s[...] = _nsa_expand(q_ref[0] * NSA_QSCALE).astype(BF16)
        _init_stats((m1, l1, a1))

    def picked(g):
        rows = []
        for u in range(kt // NSA_SLC_BLOCK):
            r = sel_ref[0, g, pl.ds(kj * (kt // NSA_SLC_BLOCK) + u, 1), :]
            rows.append(jnp.broadcast_to(r, (NSA_SLC_BLOCK, qt)))
        return jnp.concatenate(rows, axis=0) > 0.5

    def slc_mask(with_causal):
        m0, m1_ = picked(0), picked(1)
        if with_causal:
            causal = kj * kt + _iota((kt, qt), 0) <= qi * qt + _iota((kt, qt), 1)
            m0, m1_ = m0 & causal, m1_ & causal
        return jnp.concatenate([m0, m0, m1_, m1_], axis=1)

    def slc_scores():
        return lax.dot_general(ks_ref[0].astype(BF16), q4_s[...], (((1,), (1,)), ((), ())),
                               preferred_element_type=F32)

    @pl.when((kj == 0) & (last > 0))
    def _():
        _online_update_t(slc_scores(), slc_mask(False), vst_ref[0].astype(BF16), m1, l1, a1)

    @pl.when((kj == 0) & (last == 0))
    def _():
        _online_update_t(slc_scores(), slc_mask(True), vst_ref[0].astype(BF16), m1, l1, a1)

    @pl.when((kj > 0) & (kj < last))
    def _():
        _online_update_t_lazy(slc_scores, slc_mask(False), vst_ref[0].astype(BF16), m1, l1, a1)

    @pl.when((kj > 0) & (kj == last))
    def _():
        _online_update_t_lazy(slc_scores, slc_mask(True), vst_ref[0].astype(BF16), m1, l1, a1)

    @pl.when(kj == last)
    def _():
        oslc_ref[0] = _collect_t(a1[...] / jnp.maximum(l1[...], 1e-30), qt).T


def _nsa_prompt_attn(pj, sel, nb, sl):
    qt, kt = min(512, sl), min(512, sl)
    assert kt % qt == 0
    qi_tab, kj_tab = _causal_pairs(sl // qt, qt, kt)
    kspec = pl.BlockSpec((1, kt, LANES), lambda b, s, qi, kj: (b, kj[s], 0))
    vspec = pl.BlockSpec((1, LANES, kt), lambda b, s, qi, kj: (b, 0, kj[s]))
    qspec = pl.BlockSpec((1, qt, 256), lambda b, s, qi, kj: (b, qi[s], 0))
    ns = sel.shape[2]
    r3 = lambda a, w: a.reshape(nb, sl, w)
    return pl.pallas_call(
        functools.partial(_nsap_kernel, qt=qt, kt=kt),
        grid_spec=pltpu.PrefetchScalarGridSpec(
            num_scalar_prefetch=2, grid=(nb, qi_tab.shape[0]),
            in_specs=[qspec, kspec, vspec, pl.BlockSpec((1, 2, ns, qt), lambda b, s, qi, kj: (b, 0, 0, qi[s]))],
            out_specs=qspec,
            scratch_shapes=[pltpu.VMEM((4 * qt, LANES), BF16), pltpu.VMEM((1, 4 * qt), F32),
                            pltpu.VMEM((1, 4 * qt), F32), pltpu.VMEM((LANES, 4 * qt), F32)]),
        out_shape=jax.ShapeDtypeStruct((nb, sl, 256), F32),
        compiler_params=_params(("parallel", "arbitrary")),
        name="nsa_prompt_attn",
    )(qi_tab, kj_tab, r3(pj["qnr"], 256), r3(pj["ks"], 128), pj["vsT"], sel)


def _winp_kernel(q_ref, *refs, qt, n_blk):
    kb, vtb, o_ref = refs[:n_blk], refs[n_blk:2 * n_blk], refs[2 * n_blk]
    qi = pl.program_id(1)
    q4 = _nsa_expand(q_ref[0] * NSA_QSCALE).astype(BF16)
    k_b = jnp.concatenate([r[0].astype(BF16) for r in kb], axis=0)
    vt_b = jnp.concatenate([r[0].astype(BF16) for r in vtb], axis=1)
    s = lax.dot_general(k_b, q4, (((1,), (1,)), ((), ())), preferred_element_type=F32)
    nk = n_blk * qt
    kpos = (qi - (n_blk - 1)) * qt + _iota((nk, qt), 0)
    qpos = qi * qt + _iota((nk, qt), 1)
    band = (kpos >= 0) & (kpos <= qpos) & (kpos > qpos - NSA_WINDOW)
    s = jnp.where(jnp.concatenate([band] * 4, axis=1), s, NEG_INF)
    m = jnp.max(s, axis=0, keepdims=True)
    p = jnp.exp2(s - m)
    l = jnp.sum(p, axis=0, keepdims=True)
    o = jnp.dot(vt_b, p.astype(BF16), preferred_element_type=F32) / jnp.maximum(l, 1e-30)
    o_ref[0] = _collect_t(o, qt).T


def _nsa_prompt_window(pj, nb, sl):
    qt = min(256, sl)
    n_blk = -(-(NSA_WINDOW - 1) // qt) + 1
    blk = lambda j: (lambda b, i: (b, jnp.maximum(i - (n_blk - 1) + j, 0), 0))
    blk_t = lambda j: (lambda b, i: (b, 0, jnp.maximum(i - (n_blk - 1) + j, 0)))
    qspec = pl.BlockSpec((1, qt, 256), lambda b, i: (b, i, 0))
    r3 = lambda a, w: a.reshape(nb, sl, w)
    return pl.pallas_call(
        functools.partial(_winp_kernel, qt=qt, n_blk=n_blk),
        grid=(nb, sl // qt),
        in_specs=[qspec] + [pl.BlockSpec((1, qt, LANES), blk(j)) for j in range(n_blk)]
        + [pl.BlockSpec((1, LANES, qt), blk_t(j)) for j in range(n_blk)],
        out_specs=qspec,
        out_shape=jax.ShapeDtypeStruct((nb, sl, 256), F32),
        compiler_params=_params(("parallel", "parallel")),
        name="nsa_prompt_window",
    )(r3(pj["qnr"], 256), *([r3(pj["kw"], 128)] * n_blk), *([pj["vwT"]] * n_blk))


def _nsa_gate(misc, oc, os_, ow):
    g = jax.nn.sigmoid(misc)
    head = _iota(oc.shape, 1) >> 6
    return (_head_bcast(g, head, 4, GATE_LANE0) * oc + _head_bcast(g, head, 4, GATE_LANE0 + 4) * os_
            + _head_bcast(g, head, 4, GATE_LANE0 + 8) * ow)


def _diff_lambda(lamv_ref, lam_init):
    lv = lamv_ref[...]
    s1 = jnp.sum(lv[0:1] * lv[1:2], axis=1, keepdims=True)
    s2 = jnp.sum(lv[2:3] * lv[3:4], axis=1, keepdims=True)
    return jnp.exp(s1) - jnp.exp(s2) + lam_init


def _diff_finish(o, head, nheads, nw, lam_init):
    inv = jnp.zeros(o.shape, F32)
    for h in range(nheads):
        ms = jnp.sum(jnp.where(head == h, o * o, 0.0), axis=-1, keepdims=True) * (1.0 / 64.0)
        inv = jnp.where(head == h, lax.rsqrt(ms + EPS), inv)
    return o * inv * nw * (1.0 - lam_init)


def _diffp_kernel(qi_ref, kj_ref, q_ref, k_ref, vt_ref, lamv_ref, nwt_ref, o_ref, q4_s, m_s, l_s, acc_s,
                  *, qt, kt, lam_init):
    step = pl.program_id(2)
    qi, kj = qi_ref[step], kj_ref[step]
    last = (qi * qt + qt - 1) // kt

    @pl.when(kj == 0)
    def _():
        q = q_ref[0] * DIFF_QSCALE
        part = _iota((qt, LANES), 1) >> 5
        q4_s[...] = jnp.concatenate([jnp.where(part == k, q, 0.0) for k in range(4)], axis=0).astype(BF16)
        _init_stats((m_s, l_s, acc_s))

    def scores():
        return lax.dot_general(k_ref[0].astype(BF16), q4_s[...], (((1,), (1,)), ((), ())),
                               preferred_element_type=F32)

    def causal():
        mask = kj * kt + _iota((kt, qt), 0) <= qi * qt + _iota((kt, qt), 1)
        return jnp.concatenate([mask] * 4, axis=1)

    @pl.when((kj == 0) & (last > 0))
    def _():
        _online_update_t(scores(), None, vt_ref[0].astype(BF16), m_s, l_s, acc_s)

    @pl.when((kj == 0) & (last == 0))
    def _():
        _online_update_t(scores(), causal(), vt_ref[0].astype(BF16), m_s, l_s, acc_s)

    @pl.when((kj > 0) & (kj < last))
    def _():
        _online_update_t_lazy(scores, None, vt_ref[0].astype(BF16), m_s, l_s, acc_s)

    @pl.when((kj > 0) & (kj == last))
    def _():
        _online_update_t_lazy(scores, causal(), vt_ref[0].astype(BF16), m_s, l_s, acc_s)

    @pl.when(kj == last)
    def _():
        lam = _diff_lambda(lamv_ref, lam_init)
        o = acc_s[...] / jnp.maximum(l_s[...], 1e-30)
        halves = []
        for h in range(2):
            rows = slice(64 * h, 64 * (h + 1))
            oh = o[rows, 2 * h * qt:(2 * h + 1) * qt] - lam * o[rows, (2 * h + 1) * qt:(2 * h + 2) * qt]
            ms = jnp.mean(oh * oh, axis=0, keepdims=True)
            halves.append(oh * lax.rsqrt(ms + EPS))
        o_ref[0] = (jnp.concatenate(halves, axis=0) * nwt_ref[...] * (1.0 - lam_init)).T


def _diff_prompt(pj, nb, sl, p, lam_init):
    qt, kt = min(512, sl), min(512, sl)
    assert kt % qt == 0
    qi_tab, kj_tab = _causal_pairs(sl // qt, qt, kt)
    qspec = pl.BlockSpec((1, qt, LANES), lambda b, h, s, qi, kj: (b, qi[s], h))
    kspec = pl.BlockSpec((1, kt, LANES), lambda b, h, s, qi, kj: (b, kj[s], h))
    vspec = pl.BlockSpec((1, LANES, kt), lambda b, h, s, qi, kj: (b, h, kj[s]))
    r3 = lambda a: a.reshape(nb, sl, 256)
    out = pl.pallas_call(
        functools.partial(_diffp_kernel, qt=qt, kt=kt, lam_init=lam_init),
        grid_spec=pltpu.PrefetchScalarGridSpec(
            num_scalar_prefetch=2, grid=(nb, 2, qi_tab.shape[0]),
            in_specs=[qspec, kspec, vspec, pl.BlockSpec((4, 32), lambda b, h, s, qi, kj: (0, 0)),
                      pl.BlockSpec((LANES, 1), lambda b, h, s, qi, kj: (0, 0))],
            out_specs=qspec,
            scratch_shapes=[pltpu.VMEM((4 * qt, LANES), BF16), pltpu.VMEM((1, 4 * qt), F32),
                            pltpu.VMEM((1, 4 * qt), F32), pltpu.VMEM((LANES, 4 * qt), F32)]),
        out_shape=jax.ShapeDtypeStruct((nb, sl, 256), F32),
        compiler_params=_params(("parallel", "parallel", "arbitrary")),
        name="diff_prompt",
    )(qi_tab, kj_tab, r3(pj["dq"]), r3(pj["dk"]), pj["dvT"], p["diff_lambda"],
      p["diff_nw256"][:, :LANES].reshape(LANES, 1))
    return out.reshape(nb * sl, 256)


def _paged_kernel(pt_ref, q_ref, *refs, mode, pps, width, kbase, past, new_len, lam_init):
    kp, vp = refs[:pps], refs[pps:2 * pps]
    knew_ref, vnew_ref = refs[2 * pps:2 * pps + 2]
    rest = list(refs[2 * pps + 2:])
    selx_ref = selnew_ref = lamv_ref = nw_ref = None
    if mode == "slc":
        selx_ref, selnew_ref = rest[:2]
        rest = rest[2:]
    if mode == "diff":
        lamv_ref, nw_ref = rest[:2]
        rest = rest[2:]
    o_ref, qe_s, m_s, l_s, acc_s = rest
    nq = new_len
    rows = qe_s.shape[0]
    step = pl.program_id(1)

    @pl.when(step == 0)
    def _():
        if mode == "diff":
            q = q_ref[0] * DIFF_QSCALE
            part = _iota((nq, width), 1) >> 5
            qe = jnp.concatenate([jnp.where(part == k, q, 0.0) for k in range(8)], axis=0)
        else:
            qe = _nsa_expand(q_ref[0] * NSA_QSCALE)
        qe_s[...] = qe.astype(BF16)
        _init_stats((m_s, l_s, acc_s))

    nk = pps * PAGE
    kt_b = jnp.concatenate([r[0, 0].astype(BF16) for r in kp], axis=1)
    vt_b = jnp.concatenate([r[0, 0].astype(BF16) for r in vp], axis=1)
    s = jnp.dot(qe_s[...], kt_b, preferred_element_type=F32)
    qpos = past + (_iota((rows, nk), 0) & (nq - 1))
    kpos = kbase + step * nk + _iota((rows, nk), 1)
    mask = kpos <= qpos
    if mode == "win":
        mask = mask & (kpos > qpos - NSA_WINDOW)
    if mode == "slc":
        mask = mask & (selx_ref[0] > 0.5)
    _online_update(s, mask, vt_b, m_s, l_s, acc_s, v_transposed=True)

    @pl.when(step == pl.num_programs(1) - 1)
    def _():
        s2 = lax.dot_general(qe_s[...], knew_ref[0].astype(BF16), (((1,), (1,)), ((), ())),
                             preferred_element_type=F32)
        qpos2 = past + (_iota((rows, nq), 0) & (nq - 1))
        kpos2 = past + _iota((rows, nq), 1)
        mask2 = kpos2 <= qpos2
        if mode == "win":
            mask2 = mask2 & (kpos2 > qpos2 - NSA_WINDOW)
        if mode == "slc":
            mask2 = mask2 & (selnew_ref[0][:, 0:nq] > 0.5)
        _online_update(s2, mask2, vnew_ref[0].astype(BF16), m_s, l_s, acc_s)
        o = acc_s[...] / jnp.maximum(l_s[...], 1e-30)
        if mode == "diff":
            lam = _diff_lambda(lamv_ref, lam_init)
            head = _iota((nq, width), 1) >> 6
            out = jnp.zeros((nq, width), F32)
            for h in range(4):
                o0 = o[(2 * h) * nq:(2 * h + 1) * nq]
                o1 = o[(2 * h + 1) * nq:(2 * h + 2) * nq]
                out = jnp.where(head == h, o0 - lam * o1, out)
            o_ref[0] = _diff_finish(out, head, 4, nw_ref[...], lam_init)
        else:
            o_ref[0] = _nsa_collect(o, nq)


def _paged_attn(mode, q, k_src, v_src, page_spec, n_pages, table, knew, vnew, kbase, past, extra=(), lam_init=0.0):
    nb, nq = q.shape[0], q.shape[1]
    width = knew.shape[-1]
    pps = min(PAGES_PER_STEP, n_pages)
    rows = 8 * nq if mode == "diff" else 4 * nq
    nk = pps * PAGE
    per_b = lambda r_, w: pl.BlockSpec((1, r_, w), lambda b, s, pt: (b, 0, 0))
    in_specs = [per_b(nq, 256)] + [page_spec(r, pps) for r in range(pps)] * 2 + [per_b(nq, width), per_b(nq, width)]
    args = [q] + [k_src] * pps + [v_src] * pps + [knew, vnew]
    if mode == "slc":
        (selx,) = extra
        new_blk = (past - kbase) // LANES
        in_specs += [pl.BlockSpec((1, rows, nk), lambda b, s, pt: (b, 0, s)),
                     pl.BlockSpec((1, rows, LANES), lambda b, s, pt: (b, 0, new_blk))]
        args += [selx, selx]
    if mode == "diff":
        in_specs += [pl.BlockSpec((4, 32), lambda b, s, pt: (0, 0)), pl.BlockSpec((1, 256), lambda b, s, pt: (0, 0))]
        args += list(extra)
    return pl.pallas_call(
        functools.partial(_paged_kernel, mode=mode, pps=pps, width=width, kbase=kbase, past=past, new_len=nq,
                          lam_init=lam_init),
        grid_spec=pltpu.PrefetchScalarGridSpec(
            num_scalar_prefetch=1, grid=(nb, n_pages // pps), in_specs=in_specs,
            out_specs=per_b(nq, 256),
            scratch_shapes=[pltpu.VMEM((rows, width), BF16), pltpu.VMEM((rows, 1), F32),
                            pltpu.VMEM((rows, 1), F32), pltpu.VMEM((rows, width), F32)]),
        out_shape=jax.ShapeDtypeStruct((nb, nq, 256), F32),
        compiler_params=_params(("parallel", "arbitrary")),
        name="paged_" + mode,
    )(table, *args)


def _page_copy(src_hbm, layer, page, buf, slot, r, sem):
    return pltpu.make_async_copy(src_hbm.at[layer, page], buf.at[slot, r], sem.at[slot])


def _start_pages(srcs, bufs, sems, pt_ref, layer, b, first, slot, n):
    for r in range(n):
        page = pt_ref[b, first + r]
        for src, buf, sem in zip(srcs, bufs, sems):
            _page_copy(src, layer, page, buf, slot, r, sem).start()


def _wait_pages(srcs, bufs, sems, layer, slot, n):
    for r in range(n):
        for src, buf, sem in zip(srcs, bufs, sems):
            _page_copy(src, layer, 0, buf, slot, r, sem).wait()


def _page_ring(pt_ref, srcs, bufs, sems, layer, pps, n_steps):
    b, nb = pl.program_id(0), pl.num_programs(0)
    slot_of = lambda s: (b * n_steps + s) & 1

    @pl.when(b == 0)
    def _():
        _start_pages(srcs, bufs, sems, pt_ref, layer, 0, 0, 0, pps)

    def advance(s):
        nxt = 1 - slot_of(s)
        if s + 1 < n_steps:
            _start_pages(srcs, bufs, sems, pt_ref, layer, b, (s + 1) * pps, nxt, pps)
        else:
            @pl.when(b + 1 < nb)
            def _():
                _start_pages(srcs, bufs, sems, pt_ref, layer, b + 1, 0, nxt, pps)
        _wait_pages(srcs, bufs, sems, layer, slot_of(s), pps)

    return slot_of, advance


def _paged2_kernel(pt_ref, q_ref, k_hbm, v_hbm, knew_ref, vnew_ref, *rest, mode, layer, pps, n_steps, width, past,
                   new_len, lam_init):
    selx_ref = lamv_ref = nw_ref = None
    if mode == "slc":
        selx_ref, rest = rest[0], rest[1:]
    if mode == "diff":
        lamv_ref, nw_ref, rest = rest[0], rest[1], rest[2:]
    o_ref, kbuf, vbuf, ksem, vsem, qe_s, m_s, l_s, acc_s = rest
    nq = new_len
    rows = qe_s.shape[0]
    nk = pps * PAGE
    slot_of, advance = _page_ring(pt_ref, (k_hbm, v_hbm), (kbuf, vbuf), (ksem, vsem), layer, pps, n_steps)

    if mode == "diff":
        q = q_ref[0] * DIFF_QSCALE
        part = _iota((nq, width), 1) >> 5
        qe = jnp.concatenate([jnp.where(part == k, q, 0.0) for k in range(8)], axis=0)
    else:
        qe = _nsa_expand(q_ref[0] * NSA_QSCALE)
    qe_s[...] = qe.astype(BF16)
    _init_stats((m_s, l_s, acc_s))

    for s in range(n_steps):
        advance(s)
        slot = slot_of(s)
        kt_b = jnp.concatenate([kbuf[slot, r].astype(BF16) for r in range(pps)], axis=1)
        vt_b = jnp.concatenate([vbuf[slot, r].astype(BF16) for r in range(pps)], axis=1)
        sc = jnp.dot(qe_s[...], kt_b, preferred_element_type=F32)
        qpos = past + (_iota((rows, nk), 0) & (nq - 1))
        kpos = s * nk + _iota((rows, nk), 1)
        mask = kpos <= qpos
        if mode == "slc":
            mask = mask & (selx_ref[0, :, s * nk:(s + 1) * nk].astype(F32) > 0.5)
        _online_update(sc, mask, vt_b, m_s, l_s, acc_s, v_transposed=True)

    s2 = lax.dot_general(qe_s[...], knew_ref[0].astype(BF16), (((1,), (1,)), ((), ())), preferred_element_type=F32)
    qpos2 = past + (_iota((rows, nq), 0) & (nq - 1))
    kpos2 = past + _iota((rows, nq), 1)
    mask2 = kpos2 <= qpos2
    if mode == "slc":
        mask2 = mask2 & (selx_ref[0, :, past:past + nq].astype(F32) > 0.5)
    _online_update(s2, mask2, vnew_ref[0].astype(BF16), m_s, l_s, acc_s)
    o = acc_s[...] / jnp.maximum(l_s[...], 1e-30)
    if mode == "diff":
        lam = _diff_lambda(lamv_ref, lam_init)
        head = _iota((nq, width), 1) >> 6
        out = jnp.zeros((nq, width), F32)
        for h in range(4):
            out = jnp.where(head == h, o[(2 * h) * nq:(2 * h + 1) * nq] - lam * o[(2 * h + 1) * nq:(2 * h + 2) * nq],
                            out)
        o_ref[0] = _diff_finish(out, head, 4, nw_ref[...], lam_init)
    else:
        o_ref[0] = _nsa_collect(o, nq)


def _paged2_attn(mode, q, k_src, v_src, layer, table, knew, vnew, past, extra=(), lam_init=0.0):
    nb, nq = q.shape[0], q.shape[1]
    n_pages = table.shape[1]
    width = knew.shape[-1]
    pps = min(PAGES_PER_STEP, n_pages)
    n_steps = n_pages // pps
    assert n_pages % pps == 0 and past == n_pages * PAGE
    rows = 8 * nq if mode == "diff" else 4 * nq
    per_b = lambda r_, w: pl.BlockSpec((1, r_, w), lambda b, pt: (b, 0, 0))
    hbm = pl.BlockSpec(memory_space=pl.ANY)
    in_specs = [per_b(nq, 256), hbm, hbm, per_b(nq, width), per_b(nq, width)]
    args = [q, k_src, v_src, knew, vnew]
    if mode == "slc":
        (selx,) = extra
        in_specs.append(per_b(rows, selx.shape[-1]))
        args.append(selx)
    if mode == "diff":
        in_specs += [pl.BlockSpec((4, 32), lambda b, pt: (0, 0)), pl.BlockSpec((1, 256), lambda b, pt: (0, 0))]
        args += list(extra)
    return pl.pallas_call(
        functools.partial(_paged2_kernel, mode=mode, layer=layer, pps=pps, n_steps=n_steps, width=width, past=past,
                          new_len=nq, lam_init=lam_init),
        grid_spec=pltpu.PrefetchScalarGridSpec(
            num_scalar_prefetch=1, grid=(nb,), in_specs=in_specs, out_specs=per_b(nq, 256),
            scratch_shapes=[pltpu.VMEM((2, pps, width, PAGE), F32), pltpu.VMEM((2, pps, width, PAGE), F32),
                            pltpu.SemaphoreType.DMA((2,)), pltpu.SemaphoreType.DMA((2,)),
                            pltpu.VMEM((rows, width), BF16), pltpu.VMEM((rows, 1), F32),
                            pltpu.VMEM((rows, 1), F32), pltpu.VMEM((rows, width), F32)]),
        out_shape=jax.ShapeDtypeStruct((nb, nq, 256), F32),
        compiler_params=_params(("arbitrary",)),
        name="paged2_" + mode,
    )(table, *args)


def _summ2_kernel(pt_ref, k_hbm, v_hbm, mk_ref, mv_ref, ko_ref, vo_ref, kbuf, vbuf, ksem, vsem, *, layer, pps, n_steps):
    slot_of, advance = _page_ring(pt_ref, (k_hbm, v_hbm), (kbuf, vbuf), (ksem, vsem), layer, pps, n_steps)
    n_blk = pps * PAGE // NSA_CMP_BLOCK
    for s in range(n_steps):
        advance(s)
        slot = slot_of(s)
        for buf, m_ref, o_ref in ((kbuf, mk_ref, ko_ref), (vbuf, mv_ref, vo_ref)):
            x = jnp.concatenate([buf[slot, r].astype(BF16) for r in range(pps)], axis=1)
            top = jnp.dot(x[0:64], m_ref[0], preferred_element_type=F32)
            bot = jnp.dot(x[64:128], m_ref[1], preferred_element_type=F32)
            o_ref[0, :, s * n_blk:(s + 1) * n_blk] = jnp.concatenate([top, bot], axis=0)


def _summarize_pages(k_src, v_src, layer, table, w_k, w_v):
    nb, n_pages = table.shape
    pps = min(PAGES_PER_STEP, n_pages)
    n_steps = n_pages // pps
    step_w = pps * PAGE
    n_blk = step_w // NSA_CMP_BLOCK
    tok = jnp.arange(step_w, dtype=jnp.int32)
    in_block = tok[:, None] // NSA_CMP_BLOCK == jnp.arange(n_blk, dtype=jnp.int32)[None, :]
    weights = lambda w: jnp.where(in_block[None], w[:, tok % NSA_CMP_BLOCK][:, :, None], 0.0).astype(BF16)
    hbm = pl.BlockSpec(memory_space=pl.ANY)
    wspec = pl.BlockSpec((2, step_w, n_blk), lambda b, pt: (0, 0, 0))
    ospec = pl.BlockSpec((1, LANES, n_steps * n_blk), lambda b, pt: (b, 0, 0))
    oshape = jax.ShapeDtypeStruct((nb, LANES, n_steps * n_blk), F32)
    return pl.pallas_call(
        functools.partial(_summ2_kernel, layer=layer, pps=pps, n_steps=n_steps),
        grid_spec=pltpu.PrefetchScalarGridSpec(
            num_scalar_prefetch=1, grid=(nb,), in_specs=[hbm, hbm, wspec, wspec], out_specs=[ospec, ospec],
            scratch_shapes=[pltpu.VMEM((2, pps, LANES, PAGE), F32), pltpu.VMEM((2, pps, LANES, PAGE), F32),
                            pltpu.SemaphoreType.DMA((2,)), pltpu.SemaphoreType.DMA((2,))]),
        out_shape=[oshape, oshape],
        compiler_params=_params(("arbitrary",)),
        name="nsa_summarize_pages",
    )(table, k_src, v_src, weights(w_k), weights(w_v))


def _post_kernel(x_ref, ya_ref, misc_ref, oc_ref, os_ref, ow_ref, yc_ref, yd_ref, wo_ref, g1_ref, b1_ref,
                 w1_ref, w2_ref, g2_ref, b2_ref, o_ref, x1_s, x1b_s, acc_s, *, alpha):
    f = pl.program_id(1)

    @pl.when(f == 0)
    def _():
        yb = _nsa_gate(misc_ref[...], oc_ref[...], os_ref[...], ow_ref[...])
        mix = _dot(ya_ref[...], wo_ref[0:256, :])
        mix = mix + _dot(yb, wo_ref[256:512, :])
        mix = mix + _dot(yc_ref[...], wo_ref[512:768, :])
        mix = mix + _dot(yd_ref[...], wo_ref[768:1024, :])
        x1 = _layer_norm(alpha * x_ref[...] + mix, g1_ref[...], b1_ref[...])
        x1_s[...] = x1
        x1b_s[...] = x1.astype(BF16)
        acc_s[...] = jnp.zeros(acc_s.shape, F32)

    h = jnp.dot(x1b_s[...], w1_ref[...], preferred_element_type=F32)
    h = jnp.square(jnp.maximum(h, 0.0))
    acc_s[...] += jnp.dot(h.astype(BF16), w2_ref[...], preferred_element_type=F32)

    @pl.when(f == pl.num_programs(1) - 1)
    def _():
        o_ref[...] = _layer_norm(alpha * x1_s[...] + acc_s[...], g2_ref[...], b2_ref[...])


def _post(x2d, ya, misc, oc, os_, ow, yc, yd, p, tm, alpha):
    t = x2d.shape[0]
    tf = 1024
    row = lambda w: pl.BlockSpec((tm, w), lambda i, f: (i, 0))
    const = lambda r, w: pl.BlockSpec((r, w), lambda i, f: (0, 0))
    return pl.pallas_call(
        functools.partial(_post_kernel, alpha=alpha),
        grid=(t // tm, D_FF // tf),
        in_specs=[row(D_MODEL), row(256), row(LANES), row(256), row(256), row(256), row(256), row(256),
                  const(D_MODEL, D_MODEL), const(1, D_MODEL), const(1, D_MODEL),
                  pl.BlockSpec((D_MODEL, tf), lambda i, f: (0, f)), pl.BlockSpec((tf, D_MODEL), lambda i, f: (f, 0)),
                  const(1, D_MODEL), const(1, D_MODEL)],
        out_specs=row(D_MODEL),
        out_shape=jax.ShapeDtypeStruct((t, D_MODEL), F32),
        scratch_shapes=[pltpu.VMEM((tm, D_MODEL), F32), pltpu.VMEM((tm, D_MODEL), BF16),
                        pltpu.VMEM((tm, D_MODEL), F32)],
        compiler_params=_params(("parallel", "arbitrary")),
        name="post",
    )(x2d, ya, misc, oc.reshape(t, 256), os_.reshape(t, 256), ow.reshape(t, 256), yc, yd,
      p["w_out"], p["ln1_g"], p["ln1_b"], p["w_ff1"], p["w_ff2"], p["ln2_g"], p["ln2_b"])


def _prep_params(w, l):
    w_in = w["w_in"][l]
    misc = jnp.concatenate([w_in[:, a:b] for a, b in MISC_COLS], axis=1)
    misc = jnp.pad(misc, ((0, 0), (0, LANES - misc.shape[1])))
    w_in_p = jnp.concatenate([w_in[:, a:b] for a, b in W_IN_ORDER] + [misc], axis=1).astype(BF16)
    row = lambda v: v.reshape(1, -1).astype(F32)
    pad_lanes = lambda v: jnp.pad(v.reshape(1, -1), ((0, 0), (0, LANES - v.shape[-1])))
    blockdiag = lambda m: jax.scipy.linalg.block_diag(*[m[i] for i in range(m.shape[0])]).astype(BF16)
    return dict(
        w_in=w_in_p,
        ssd_cw=jnp.pad(w["ssd_conv_w"][l], ((0, 4), (0, 0))), ssd_cb=row(w["ssd_conv_b"][l]),
        ssd_dtb=pad_lanes(w["ssd_dt_bias"][l]), ssd_alog=pad_lanes(w["ssd_A_log"][l]),
        ssd_dexp=row(jnp.repeat(w["ssd_D"][l], 64)), ssd_nw=row(w["ssd_norm_w"][l]),
        cmp_wk=w["nsa_w_cmp_k"][l], cmp_wv=w["nsa_w_cmp_v"][l],
        diff_lambda=w["diff_lambda"][l], diff_nw256=row(jnp.tile(w["diff_norm_w"][l], 4)),
        lru_cw=jnp.pad(w["lru_conv_w"][l], ((0, 4), (0, 0))), lru_cb=row(w["lru_conv_b"][l]),
        lru_wa=blockdiag(w["lru_w_a"][l]), lru_ba=row(w["lru_b_a"][l]),
        lru_wx=blockdiag(w["lru_w_x"][l]), lru_bx=row(w["lru_b_x"][l]), lru_lam=row(w["lru_lambda"][l]),
        w_out=w["w_out"][l].astype(BF16), ln1_g=row(w["ln1_g"][l]), ln1_b=row(w["ln1_b"][l]),
        w_ff1=w["w_ff1"][l].astype(BF16), w_ff2=w["w_ff2"][l].astype(BF16),
        ln2_g=row(w["ln2_g"][l]), ln2_b=row(w["ln2_b"][l]),
    )


def _pad_prev(prev):
    return jnp.pad(prev, ((0, 0), (5, 0), (0, 0)))


def _token_minor(a):
    a = jnp.moveaxis(a, -3, -1)
    return a.reshape(*a.shape[:-3], a.shape[-3] * a.shape[-2], a.shape[-1])


def _token_major(a_t, heads):
    nb, _, n = a_t.shape
    return jnp.moveaxis(a_t.reshape(nb, heads, 64, n), -1, 1)


def _layer(x2d, nb, sl, past, hist, p, lam_init, alpha):
    t = nb * sl
    tm = min(256, t)
    pos = past + jnp.arange(sl, dtype=jnp.int32)
    pj = _proj(x2d, p["w_in"], pos, nb, sl, tm)
    y_a, ssd_h = _ssd(pj, nb, sl, _pad_prev(hist["ssd_conv"]), hist["ssd_h"].reshape(nb, 256, 128), p)
    y_d, lru_h = _lru(pj, nb, sl, _pad_prev(hist["lru_conv"]), hist["lru_h"], p)
    r3 = lambda a, w: a.reshape(nb, sl, w)
    dummy = jnp.zeros((1, 1), jnp.int32)
    if past == 0:
        step_w = min(4096, sl)
        spec = pl.BlockSpec((1, LANES, step_w), lambda b, s, pt: (b, 0, s))
        kcmp_t, vcmp_t = _summarize(pj["kcT"], pj["vcT"], [spec], step_w, nb, sl, dummy, p["cmp_wk"], p["cmp_wv"])
        o_cmp, sel = _cmp_select_prompt(pj["qn"], kcmp_t, vcmp_t, nb, sl, min(256, sl))
        o_slc = _nsa_prompt_attn(pj, sel, nb, sl)
        o_win = _nsa_prompt_window(pj, nb, sl)
        y_c = _diff_prompt(pj, nb, sl, p, lam_init)
        kw_t, vw_t = pj["kwT"][:, :, sl - NSA_WINDOW:], pj["vwT"][:, :, sl - NSA_WINDOW:]
        new = dict(kc=_token_major(pj["kcT"], 2), vc=_token_major(pj["vcT"], 2), ks=_token_major(pj["ksT"], 2),
                   vs=_token_major(pj["vsT"], 2), dk=_token_major(pj["dkT"], 4), dv=_token_major(pj["dvT"], 4))
    else:
        table, l = hist["table"], hist["layer"]
        n_pages = table.shape[1]
        win_spec = lambda r, per: pl.BlockSpec(
            (1, 1, LANES, PAGE), lambda b, s, pt, r=r, per=per: (l, b, 0, s * per + r))
        kcmp_t, vcmp_t = _summarize_pages(hist["kc"], hist["vc"], l, table, p["cmp_wk"], p["cmp_wv"])
        o_cmp, selx = _cmp_select_sample(pj["qn"], kcmp_t, vcmp_t, nb, sl, past)
        qnr = r3(pj["qnr"], 256)
        o_slc = _paged2_attn("slc", qnr, hist["ks"], hist["vs"], l, table, r3(pj["ks"], 128), r3(pj["vs"], 128),
                             past, extra=(selx,))
        wbuf = hist["kw"].shape[-1]
        o_win = _paged_attn("win", qnr, hist["kw"], hist["vw"], win_spec, wbuf // PAGE, dummy,
                            r3(pj["kw"], 128), r3(pj["vw"], 128), past - wbuf, past)
        y_c = _paged2_attn("diff", r3(pj["dq"], 256), hist["dk"], hist["dv"], l, table, r3(pj["dk"], 256),
                           r3(pj["dv"], 256), past, extra=(p["diff_lambda"], p["diff_nw256"]),
                           lam_init=lam_init).reshape(t, 256)
        kw_t = jnp.concatenate([hist["kw"][l], pj["kwT"]], axis=2)[:, :, -NSA_WINDOW:]
        vw_t = jnp.concatenate([hist["vw"][l], pj["vwT"]], axis=2)[:, :, -NSA_WINDOW:]
        kv = lambda a: r3(a, 128).reshape(nb, sl, 2, 64)
        new = dict(kc=kv(pj["kc"]), vc=kv(pj["vc"]), ks=kv(pj["ks"]), vs=kv(pj["vs"]),
                   dk=r3(pj["dk"], 256).reshape(nb, sl, 4, 64), dv=r3(pj["dv"], 256).reshape(nb, sl, 4, 64))
    x_out = _post(x2d, y_a, pj["misc"], o_cmp, o_slc, o_win, y_c, y_d, p, min(512, t), alpha)
    new.update(kw=_token_major(kw_t, 2), vw=_token_major(vw_t, 2),
               ssd_h=ssd_h.reshape(nb, 4, 64, 128), ssd_conv=r3(pj["xbc"], 768)[:, sl - 3:],
               lru_h=lru_h, lru_conv=r3(pj["lx"], 256)[:, sl - 3:])
    return x_out, new


STATE_ORDER = ("kc", "vc", "ks", "vs", "dk", "dv", "kw", "vw", "ssd_h", "ssd_conv", "lru_h", "lru_conv")


def kernel(x_prompt, x_sample, cache_nsa_k_cmp, cache_nsa_v_cmp, cache_nsa_k_slc, cache_nsa_v_slc, cache_diff_k, cache_diff_v, cache_nsa_k_win, cache_nsa_v_win, state_ssd, state_ssd_conv, state_lru, state_lru_conv, page_table, w_in, ssd_conv_w, ssd_conv_b, ssd_dt_bias, ssd_A_log, ssd_D, ssd_norm_w, nsa_w_cmp_k, nsa_w_cmp_v, diff_lambda, diff_norm_w, lru_conv_w, lru_conv_b, lru_w_a, lru_b_a, lru_w_x, lru_b_x, lru_lambda, w_out, ln1_g, ln1_b, w_ff1, w_ff2, ln2_g, ln2_b):
    weights = dict(w_in=w_in, ssd_conv_w=ssd_conv_w, ssd_conv_b=ssd_conv_b, ssd_dt_bias=ssd_dt_bias,
                   ssd_A_log=ssd_A_log, ssd_D=ssd_D, ssd_norm_w=ssd_norm_w, nsa_w_cmp_k=nsa_w_cmp_k,
                   nsa_w_cmp_v=nsa_w_cmp_v, diff_lambda=diff_lambda, diff_norm_w=diff_norm_w, lru_conv_w=lru_conv_w,
                   lru_conv_b=lru_conv_b, lru_w_a=lru_w_a, lru_b_a=lru_b_a, lru_w_x=lru_w_x, lru_b_x=lru_b_x,
                   lru_lambda=lru_lambda, w_out=w_out, ln1_g=ln1_g, ln1_b=ln1_b, w_ff1=w_ff1, w_ff2=w_ff2,
                   ln2_g=ln2_g, ln2_b=ln2_b)
    depth = w_in.shape[0]
    nbp, slp, _ = x_prompt.shape
    nbs, sls, _ = x_sample.shape
    past = page_table.shape[1] * cache_nsa_k_cmp.shape[2]
    assert past % PAGE == 0 and sls < NSA_CMP_BLOCK and slp % 256 == 0
    alpha = (2 * depth) ** 0.25
    caches = dict(kc=_token_minor(cache_nsa_k_cmp), vc=_token_minor(cache_nsa_v_cmp), ks=_token_minor(cache_nsa_k_slc),
                  vs=_token_minor(cache_nsa_v_slc), dk=_token_minor(cache_diff_k), dv=_token_minor(cache_diff_v),
                  kw=_token_minor(cache_nsa_k_win), vw=_token_minor(cache_nsa_v_win))
    xp = x_prompt.reshape(nbp * slp, D_MODEL)
    xs = x_sample.reshape(nbs * sls, D_MODEL)
    outs_p, outs_s = [], []
    for l in range(depth):
        p = _prep_params(weights, l)
        lam_init = 0.8 - 0.6 * math.exp(-0.3 * l)
        hist_p = dict(ssd_conv=jnp.zeros((nbp, 3, SSD_CONV_CH), F32), ssd_h=jnp.zeros((nbp, 4, 64, 128), F32),
                      lru_conv=jnp.zeros((nbp, 3, GROUP_W), F32), lru_h=jnp.zeros((nbp, GROUP_W), F32))
        hist_s = dict(table=page_table, layer=l, ssd_conv=state_ssd_conv[l], ssd_h=state_ssd[l],
                      lru_conv=state_lru_conv[l], lru_h=state_lru[l], **caches)
        xp, new_p = _layer(xp, nbp, slp, 0, hist_p, p, lam_init, alpha)
        xs, new_s = _layer(xs, nbs, sls, past, hist_s, p, lam_init, alpha)
        outs_p.append(new_p)
        outs_s.append(new_s)
    stack = lambda lst, name: jnp.stack([d[name] for d in lst])
    return ((xp.reshape(nbp, slp, D_MODEL), xs.reshape(nbs, sls, D_MODEL))
            + tuple(stack(outs_p, n) for n in STATE_ORDER) + tuple(stack(outs_s, n) for n in STATE_ORDER))
```

```python
import functools
import math

import jax
import jax.numpy as jnp
from jax import lax
from jax.experimental import pallas as pl
from jax.experimental.pallas import tpu as pltpu

F32 = jnp.float32
BF16 = jnp.bfloat16
NEG_INF = float("-inf")

D_MODEL = 1024
GROUP_W = 256
SSD_HEADS = 4
SSD_STATE = 128
SSD_CONV_CH = 768
NSA_CMP_BLOCK = 32
NSA_SLC_BLOCK = 64
NSA_TOPN = 16
NSA_WINDOW = 512
FORCE_SCORE = 1e4
LRU_C = 8.0
D_FF = 4096
ROPE_THETA = 500000.0
EPS = 1e-5
PAGE = 128
LANES = 128
VMEM_LIMIT = 56 * 1024 * 1024
PAGES_PER_STEP = 32
LAZY_MAX_RISE = 64.0
LOG2E = 1.4426950408889634
NSA_QSCALE = 64.0 ** -0.5 * LOG2E
DIFF_QSCALE = 32.0 ** -0.5 * LOG2E

W_IN_ORDER = ((0, 1024), (1028, 2052), (2064, 3344))
MISC_COLS = ((1024, 1028), (2052, 2064))
SEG = dict(z=(0, 256), xbc=(256, 1024), qn=(1024, 1280), kc=(1280, 1408), vc=(1408, 1536), ks=(1536, 1664),
           vs=(1664, 1792), kw=(1792, 1920), vw=(1920, 2048), dq=(2048, 2304), dk=(2304, 2560), dv=(2560, 2816),
           lx=(2816, 3072), lg=(3072, 3328), misc=(3328, 3456))
D_IN_PAD = 3456
GATE_LANE0 = 4


def _params(sem):
    return pltpu.CompilerParams(dimension_semantics=sem, vmem_limit_bytes=VMEM_LIMIT)


def _iota(shape, dim):
    return lax.broadcasted_iota(jnp.int32, shape, dim)


def _dot(a, b):
    return jnp.dot(a.astype(BF16), b.astype(BF16), preferred_element_type=F32)


def _dot_nt(a, b):
    return lax.dot_general(a.astype(BF16), b.astype(BF16), (((1,), (1,)), ((), ())), preferred_element_type=F32)


def _split3(x):
    h1 = x.astype(BF16)
    r1 = x - h1.astype(F32)
    h2 = r1.astype(BF16)
    h3 = (r1 - h2.astype(F32)).astype(BF16)
    return h1, h2, h3


def _dot_01(m01, x):
    return sum(jnp.dot(m01, part, preferred_element_type=F32) for part in _split3(x))


def _dot_01_r(x, m01):
    return sum(jnp.dot(part, m01, preferred_element_type=F32) for part in _split3(x))


def _dot_nt_01(m01, x):
    return sum(lax.dot_general(m01, part, (((1,), (1,)), ((), ())), preferred_element_type=F32)
               for part in _split3(x))


def _eye(n, m):
    return (_iota((n, m), 0) == _iota((n, m), 1)).astype(BF16)


def _softplus(x):
    return jnp.maximum(x, 0.0) + jnp.log1p(jnp.exp(-jnp.abs(x)))


def _head_bcast(cols, h_of_lane, nheads, lane0=0):
    out = jnp.zeros(h_of_lane.shape, F32)
    for h in range(nheads):
        out = jnp.where(h_of_lane == h, cols[:, lane0 + h:lane0 + h + 1], out)
    return out


def _online_update(s, mask, v_b, m_ref, l_ref, acc_ref, v_transposed=False):
    s = jnp.where(mask, s, NEG_INF)
    m_prev = m_ref[...]
    m_new = jnp.maximum(m_prev, jnp.max(s, axis=-1, keepdims=True))
    m_safe = jnp.where(m_new == NEG_INF, 0.0, m_new)
    alpha = jnp.exp2(m_prev - m_safe)
    p = jnp.exp2(s - m_safe)
    l_ref[...] = alpha * l_ref[...] + jnp.sum(p, axis=-1, keepdims=True)
    if v_transposed:
        pv = lax.dot_general(p.astype(BF16), v_b, (((1,), (1,)), ((), ())), preferred_element_type=F32)
    else:
        pv = jnp.dot(p.astype(BF16), v_b, preferred_element_type=F32)
    acc_ref[...] = alpha * acc_ref[...] + pv
    m_ref[...] = m_new


def _online_update_t(s, mask, vt_b, m_ref, l_ref, acc_ref):
    if mask is not None:
        s = jnp.where(mask, s, NEG_INF)
    m_prev = m_ref[...]
    m_new = jnp.maximum(m_prev, jnp.max(s, axis=0, keepdims=True))
    m_safe = jnp.where(m_new == NEG_INF, 0.0, m_new)
    alpha = jnp.exp2(m_prev - m_safe)
    p = jnp.exp2(s - m_safe)
    l_ref[...] = alpha * l_ref[...] + jnp.sum(p, axis=0, keepdims=True)
    acc_ref[...] = alpha * acc_ref[...] + jnp.dot(vt_b, p.astype(BF16), preferred_element_type=F32)
    m_ref[...] = m_new


def _online_update_t_lazy(scores, mask, vt_b, m_ref, l_ref, acc_ref, first=False):
    s = scores()
    if mask is not None:
        s = jnp.where(mask, s, NEG_INF)
    m_prev = s[0:1, :] if first else m_ref[...]
    tile_max = jnp.max(s, axis=0, keepdims=True)
    p = jnp.exp2(s - m_prev)
    p_sum = jnp.sum(p, axis=0, keepdims=True)
    pv = jnp.dot(vt_b, p.astype(BF16), preferred_element_type=F32)
    safe = jnp.max(tile_max - m_prev) <= LAZY_MAX_RISE

    @pl.when(safe)
    def _():
        m_new = jnp.maximum(m_prev, tile_max)
        alpha = jnp.exp2(m_prev - m_new)
        l_ref[...] = (l_ref[...] + p_sum) * alpha
        acc_ref[...] = (acc_ref[...] + pv) * alpha
        m_ref[...] = m_new

    @pl.when(jnp.logical_not(safe))
    def _():
        _online_update_t(scores(), mask, vt_b, m_ref, l_ref, acc_ref)


def _causal_pairs(n_q, qt, kt):
    qi, kj = [], []
    for i in range(n_q):
        for j in range((i * qt + qt - 1) // kt + 1):
            qi.append(i)
            kj.append(j)
    return jnp.asarray(qi, jnp.int32), jnp.asarray(kj, jnp.int32)


def _init_stats(*triples):
    for m_, l_, a_ in triples:
        m_[...] = jnp.full(m_.shape, NEG_INF, F32)
        l_[...] = jnp.zeros(l_.shape, F32)
        a_[...] = jnp.zeros(a_.shape, F32)


def _nsa_expand(q):
    lo = _iota((q.shape[0], LANES), 1) < 64
    a, b = q[:, :LANES], q[:, LANES:]
    h0 = jnp.where(lo, a, 0.0)
    h1 = jnp.where(lo, pltpu.roll(a, 64, 1), 0.0)
    h2 = jnp.where(lo, 0.0, pltpu.roll(b, 64, 1))
    h3 = jnp.where(lo, 0.0, b)
    return jnp.concatenate([h0, h1, h2, h3], axis=0)


def _nsa_collect(o, n):
    lo = _iota((n, LANES), 1) < 64
    o0, o1, o2, o3 = o[0:n], o[n:2 * n], o[2 * n:3 * n], o[3 * n:4 * n]
    left = jnp.where(lo, o0, pltpu.roll(o1, 64, 1))
    right = jnp.where(lo, pltpu.roll(o2, 64, 1), o3)
    return jnp.concatenate([left, right], axis=1)


def _collect_t(o_t, n):
    return jnp.concatenate([o_t[0:64, 0:n], o_t[0:64, n:2 * n], o_t[64:128, 2 * n:3 * n], o_t[64:128, 3 * n:4 * n]],
                           axis=0)


def _layer_norm(v, g, b):
    mu = jnp.mean(v, axis=-1, keepdims=True)
    d = v - mu
    var = jnp.mean(d * d, axis=-1, keepdims=True)
    return d * lax.rsqrt(var + EPS) * g + b


def _rope128(v, c, sa, sb, half):
    return v * c + pltpu.roll(v, LANES - half, 1) * sa + pltpu.roll(v, half, 1) * sb


PROJ_OUT = ("z", "xbc", "qn", "qnr", "kc", "vc", "ks", "vs", "kw", "vw", "dq", "dk", "dv", "lx", "lg", "misc")
PROJ_OUT_T = ("kcT", "vcT", "ksT", "vsT", "kwT", "vwT", "dkT", "dvT")
PROJ_W = dict(z=256, xbc=768, qn=256, qnr=256, kc=128, vc=128, ks=128, vs=128, kw=128, vw=128, dq=256, dk=256,
              dv=256, lx=256, lg=256, misc=128)


def _proj_kernel(x_ref, w_ref, cn_ref, san_ref, sbn_ref, cd_ref, sad_ref, sbd_ref, *outs, flat_t):
    o = dict(zip(PROJ_OUT + PROJ_OUT_T, outs))
    xb = x_ref[...].astype(BF16)

    def seg(name):
        a, b = SEG[name]
        return jnp.dot(xb, w_ref[:, a:b], preferred_element_type=F32)

    def rope_n(v):
        return _rope128(v, cn_ref[...], san_ref[...], sbn_ref[...], 8)

    def rope_d(v):
        return _rope128(v, cd_ref[...], sad_ref[...], sbd_ref[...], 4)

    def put(name, v):
        o[name][...] = v
        if name + "T" in o:
            if flat_t:
                o[name + "T"][...] = v.T
            else:
                o[name + "T"][0] = v.T

    put("z", seg("z"))
    put("xbc", seg("xbc"))
    qn = seg("qn")
    put("qn", qn)
    put("qnr", jnp.concatenate([rope_n(qn[:, :LANES]), rope_n(qn[:, LANES:])], axis=1))
    put("kc", seg("kc"))
    put("vc", seg("vc"))
    put("ks", rope_n(seg("ks")))
    put("vs", seg("vs"))
    put("kw", rope_n(seg("kw")))
    put("vw", seg("vw"))
    dq = seg("dq")
    put("dq", jnp.concatenate([rope_d(dq[:, :LANES]), rope_d(dq[:, LANES:])], axis=1))
    dk = seg("dk")
    put("dk", jnp.concatenate([rope_d(dk[:, :LANES]), rope_d(dk[:, LANES:])], axis=1))
    put("dv", seg("dv"))
    put("lx", seg("lx"))
    put("lg", seg("lg"))
    put("misc", seg("misc"))


def _rope_tables(pos, dh, rows):
    rd = dh // 4
    half = rd // 2
    inv = ROPE_THETA ** (-jnp.arange(half, dtype=F32) / half)
    ang = pos.astype(F32)[:, None] * inv[None, :]
    cos, sin = jnp.cos(ang), jnp.sin(ang)
    n = pos.shape[0]
    zero_h = jnp.zeros((n, half), F32)
    zero_r = jnp.zeros((n, dh - rd), F32)
    c = jnp.concatenate([cos, cos, jnp.ones((n, dh - rd), F32)], 1)
    sa = jnp.concatenate([-sin, zero_h, zero_r], 1)
    sb = jnp.concatenate([zero_h, sin, zero_r], 1)
    reps = (max(rows // n, 1), LANES // dh)
    return tuple(jnp.tile(t, reps) for t in (c, sa, sb))


def _proj(x2d, w_in_p, pos, nb, sl, tm):
    t = x2d.shape[0]
    n_tab = max(sl // tm, 1)
    flat_t = sl < tm
    tabs = _rope_tables(pos, 64, tm) + _rope_tables(pos, 32, tm)
    tab_spec = pl.BlockSpec((tm, LANES), lambda i: (i % n_tab, 0))
    out_specs = [pl.BlockSpec((tm, PROJ_W[k]), lambda i: (i, 0)) for k in PROJ_OUT]
    out_shape = [jax.ShapeDtypeStruct((t, PROJ_W[k]), F32) for k in PROJ_OUT]
    for k in PROJ_OUT_T:
        c = PROJ_W[k[:-1]]
        if flat_t:
            out_specs.append(pl.BlockSpec((c, tm), lambda i: (0, i)))
            out_shape.append(jax.ShapeDtypeStruct((c, t), F32))
        else:
            out_specs.append(pl.BlockSpec((1, c, tm), lambda i: (i // n_tab, 0, i % n_tab)))
            out_shape.append(jax.ShapeDtypeStruct((nb, c, sl), F32))
    outs = pl.pallas_call(
        functools.partial(_proj_kernel, flat_t=flat_t),
        grid=(t // tm,),
        in_specs=[pl.BlockSpec((tm, D_MODEL), lambda i: (i, 0)),
                  pl.BlockSpec((D_MODEL, D_IN_PAD), lambda i: (0, 0))] + [tab_spec] * 6,
        out_specs=out_specs, out_shape=out_shape,
        compiler_params=_params(("parallel",)),
        name="proj",
    )(x2d, w_in_p, *tabs)
    pj = dict(zip(PROJ_OUT + PROJ_OUT_T, outs))
    if flat_t:
        for k in PROJ_OUT_T:
            pj[k] = pj[k].reshape(-1, nb, sl).transpose(1, 0, 2)
    return pj


def _conv_chunk(x_ref, prev_ref, cw_ref, cb_ref, xbuf, q, first):
    @pl.when(first)
    def _():
        xbuf[0:8, :] = prev_ref[0]

    xbuf[8:8 + q, :] = x_ref[0]
    acc = cb_ref[...] + cw_ref[0:1, :] * xbuf[pl.ds(5, q), :]
    for j in range(1, 4):
        acc = acc + cw_ref[j:j + 1, :] * xbuf[pl.ds(5 + j, q), :]
    xbuf[0:8, :] = xbuf[q:q + 8, :]
    return acc


def _ssd_kernel(xbc_ref, z_ref, misc_ref, prev_ref, h0_ref, cw_ref, cb_ref, dtb_ref, alog_ref, dexp_ref, nw_ref,
                y_ref, hout_ref, xbuf, ht, *, q):
    c = pl.program_id(1)

    @pl.when(c == 0)
    def _():
        ht[...] = _dot_nt_01(_eye(SSD_STATE, SSD_STATE), h0_ref[0])

    acc = _conv_chunk(xbc_ref, prev_ref, cw_ref, cb_ref, xbuf, q, c == 0)
    xc = acc * jax.nn.sigmoid(acc)
    sx = xc[:, 0:256]
    sb = (xc[:, 256:384], xc[:, 384:512])
    sc = (xc[:, 512:640], xc[:, 640:768])
    dt = _softplus(misc_ref[0] + dtb_ref[...])
    da = dt * (-jnp.exp(alog_ref[...]))
    causal = _iota((q, q), 0) >= _iota((q, q), 1)
    acum = _dot_01(causal.astype(BF16), da)
    xsel = _eye(8, LANES)
    acum_t = _dot_nt_01(xsel, acum)
    dt_t = _dot_nt_01(xsel, dt)
    a_last = acum[q - 1:q, :]
    wlast = jnp.exp(a_last - acum) * dt
    ea = jnp.exp(acum)
    head = _iota((q, GROUP_W), 1) >> 6
    ht_old = ht[...]
    y = dexp_ref[...] * sx
    for g in range(2):
        cb = _dot_nt(sc[g], sb[g])
        for h in (2 * g, 2 * g + 1):
            seg = acum[:, h:h + 1] - acum_t[h:h + 1, :]
            decay = jnp.exp(jnp.where(causal, seg, NEG_INF))
            m = cb * decay * dt_t[h:h + 1, :]
            y_h = _dot(m, sx) + _dot(sc[g] * ea[:, h:h + 1], ht_old)
            y = y + jnp.where(head == h, y_h, 0.0)
    xw = sx * _head_bcast(wlast, head, SSD_HEADS)
    lane = _iota((q, GROUP_W), 1)
    bt0 = _dot_nt(_eye(SSD_STATE, SSD_STATE), sb[0])
    bt1 = _dot_nt(_eye(SSD_STATE, SSD_STATE), sb[1])
    head1 = _iota((1, GROUP_W), 1) >> 6
    dch = _head_bcast(jnp.exp(a_last), head1, SSD_HEADS)
    ht_new = dch * ht_old + _dot(bt0, jnp.where(lane < 128, xw, 0.0)) + _dot(bt1, jnp.where(lane < 128, 0.0, xw))
    ht[...] = ht_new
    zz = z_ref[0]
    y = y * (zz * jax.nn.sigmoid(zz))
    y_ref[0] = y * lax.rsqrt(jnp.mean(y * y, axis=-1, keepdims=True) + EPS) * nw_ref[...]

    @pl.when(c == pl.num_programs(1) - 1)
    def _():
        hout_ref[0] = _dot_nt_01(_eye(GROUP_W, GROUP_W), ht_new)


def _ssd(pj, nb, sl, prev8, h0, p):
    q = min(128, sl)
    nc = sl // q
    row = lambda w: pl.BlockSpec((1, q, w), lambda b, c: (b, c, 0))
    per_b = lambda r, w: pl.BlockSpec((1, r, w), lambda b, c: (b, 0, 0))
    const = lambda r, w: pl.BlockSpec((r, w), lambda b, c: (0, 0))
    y, hout = pl.pallas_call(
        functools.partial(_ssd_kernel, q=q),
        grid=(nb, nc),
        in_specs=[row(768), row(256), row(128), per_b(8, 768), per_b(256, 128),
                  const(8, 768), const(1, 768), const(1, 128), const(1, 128), const(1, 256), const(1, 256)],
        out_specs=[row(256), per_b(256, 128)],
        out_shape=[jax.ShapeDtypeStruct((nb, sl, 256), F32), jax.ShapeDtypeStruct((nb, 256, 128), F32)],
        scratch_shapes=[pltpu.VMEM((q + 8, 768), F32), pltpu.VMEM((SSD_STATE, GROUP_W), F32)],
        compiler_params=_params(("parallel", "arbitrary")),
        name="ssd",
    )(pj["xbc"].reshape(nb, sl, 768), pj["z"].reshape(nb, sl, 256), pj["misc"].reshape(nb, sl, 128), prev8, h0,
      p["ssd_cw"], p["ssd_cb"], p["ssd_dtb"], p["ssd_alog"], p["ssd_dexp"], p["ssd_nw"])
    return y.reshape(nb * sl, 256), hout


def _lru_kernel(x_ref, g_ref, prev_ref, h0_ref, cw_ref, cb_ref, wa_ref, ba_ref, wx_ref, bx_ref, lam_ref,
                y_ref, hout_ref, xbuf, hc, *, q):
    c = pl.program_id(1)

    @pl.when(c == 0)
    def _():
        hc[...] = h0_ref[0]

    xc = _conv_chunk(x_ref, prev_ref, cw_ref, cb_ref, xbuf, q, c == 0)
    xcb = xc.astype(BF16)
    r = jax.nn.sigmoid(jnp.dot(xcb, wa_ref[...], preferred_element_type=F32) + ba_ref[...])
    i = jax.nn.sigmoid(jnp.dot(xcb, wx_ref[...], preferred_element_type=F32) + bx_ref[...])
    log_a = -LRU_C * r * _softplus(-lam_ref[...])
    a = jnp.exp(log_a)
    u = jnp.sqrt(1.0 - jnp.exp(2.0 * log_a)) * (i * xc)
    row = _iota((q, GROUP_W), 0)
    s = 1
    while s < q:
        a_sh = pltpu.roll(a, s, 0)
        u_sh = pltpu.roll(u, s, 0)
        keep = row >= s
        u = jnp.where(keep, a * u_sh + u, u)
        a = jnp.where(keep, a * a_sh, a)
        s *= 2
    h = u + a * hc[...]
    hc[...] = h[q - 1:q, :]
    gg = g_ref[0]
    gelu = 0.5 * gg * (1.0 + jnp.tanh(math.sqrt(2.0 / math.pi) * (gg + 0.044715 * (gg * gg * gg))))
    y_ref[0] = h * gelu

    @pl.when(c == pl.num_programs(1) - 1)
    def _():
        hout_ref[0] = h[q - 1:q, :]


def _lru(pj, nb, sl, prev8, h0, p):
    q = min(256, sl)
    nc = sl // q
    row = lambda w: pl.BlockSpec((1, q, w), lambda b, c: (b, c, 0))
    per_b = lambda r, w: pl.BlockSpec((1, r, w), lambda b, c: (b, 0, 0))
    const = lambda r, w: pl.BlockSpec((r, w), lambda b, c: (0, 0))
    y, hout = pl.pallas_call(
        functools.partial(_lru_kernel, q=q),
        grid=(nb, nc),
        in_specs=[row(256), row(256), per_b(8, 256), per_b(1, 256), const(8, 256), const(1, 256),
                  const(256, 256), const(1, 256), const(256, 256), const(1, 256), const(1, 256)],
        out_specs=[row(256), per_b(1, 256)],
        out_shape=[jax.ShapeDtypeStruct((nb, sl, 256), F32), jax.ShapeDtypeStruct((nb, 1, 256), F32)],
        scratch_shapes=[pltpu.VMEM((q + 8, 256), F32), pltpu.VMEM((1, 256), F32)],
        compiler_params=_params(("parallel", "arbitrary")),
        name="lru",
    )(pj["lx"].reshape(nb, sl, 256), pj["lg"].reshape(nb, sl, 256), prev8, h0.reshape(nb, 1, 256),
      p["lru_cw"], p["lru_cb"], p["lru_wa"], p["lru_ba"], p["lru_wx"], p["lru_bx"], p["lru_lam"])
    return y.reshape(nb * sl, 256), hout.reshape(nb, 256)


def _summ_kernel(pt_ref, *refs, n_parts):
    kp, vp = refs[:n_parts], refs[n_parts:2 * n_parts]
    mk_ref, mv_ref, ko_ref, vo_ref = refs[2 * n_parts:]

    def one(parts, m_ref, o_ref):
        chunks = []
        for r in parts:
            x = r[...]
            chunks.append(x.reshape(x.shape[-2], x.shape[-1]).astype(BF16))
        x = jnp.concatenate(chunks, axis=1) if len(chunks) > 1 else chunks[0]
        top = jnp.dot(x[0:64], m_ref[0], preferred_element_type=F32)
        bot = jnp.dot(x[64:128], m_ref[1], preferred_element_type=F32)
        o_ref[0] = jnp.concatenate([top, bot], axis=0)

    one(kp, mk_ref, ko_ref)
    one(vp, mv_ref, vo_ref)


def _summarize(k_src, v_src, part_specs, part_w, nb, total, table, w_k, w_v):
    n_parts = len(part_specs)
    step_w = n_parts * part_w
    n_blk = step_w // NSA_CMP_BLOCK
    tok = jnp.arange(step_w, dtype=jnp.int32)
    in_block = tok[:, None] // NSA_CMP_BLOCK == jnp.arange(n_blk, dtype=jnp.int32)[None, :]
    weights = lambda w: jnp.where(in_block[None], w[:, tok % NSA_CMP_BLOCK][:, :, None], 0.0).astype(BF16)
    wspec = pl.BlockSpec((2, step_w, n_blk), lambda b, s, pt: (0, 0, 0))
    ospec = pl.BlockSpec((1, LANES, n_blk), lambda b, s, pt: (b, 0, s))
    oshape = jax.ShapeDtypeStruct((nb, LANES, total // NSA_CMP_BLOCK), F32)
    return pl.pallas_call(
        functools.partial(_summ_kernel, n_parts=n_parts),
        grid_spec=pltpu.PrefetchScalarGridSpec(
            num_scalar_prefetch=1, grid=(nb, total // step_w),
            in_specs=list(part_specs) * 2 + [wspec, wspec],
            out_specs=[ospec, ospec]),
        out_shape=[oshape, oshape],
        compiler_params=_params(("parallel", "arbitrary")),
        name="nsa_summarize",
    )(table, *([k_src] * n_parts), *([v_src] * n_parts), weights(w_k), weights(w_v))


def _cmpsel_t_kernel(q_ref, kct_ref, vct_ref, ocmp_ref, sel_ref, *, qt, nc, ns):
    qi = pl.program_id(1)
    q4 = _nsa_expand(q_ref[0] * 0.125)
    s = _dot_nt(kct_ref[0].T, q4)
    qpos = qi * qt + (_iota((nc, 4 * qt), 1) & (qt - 1))
    blk_end = (_iota((nc, 4 * qt), 0) + 1) * NSA_CMP_BLOCK - 1
    s = jnp.where(blk_end <= qpos, s, NEG_INF)
    m = jnp.max(s, axis=0, keepdims=True)
    m = jnp.where(m == NEG_INF, 0.0, m)
    e = jnp.exp(s - m)
    p = e / jnp.maximum(jnp.sum(e, axis=0, keepdims=True), 1e-30)
    ocmp_ref[0] = _collect_t(_dot(vct_ref[0], p), qt).T
    impc = jnp.concatenate([p[:, 0:qt] + p[:, qt:2 * qt], p[:, 2 * qt:3 * qt] + p[:, 3 * qt:4 * qt]], axis=1)
    pair = (_iota((ns, nc), 1) >> 1 == _iota((ns, nc), 0)).astype(BF16)
    imp = _dot_01(pair, impc)
    blk = _iota((ns, 2 * qt), 0)
    cur = (qi * qt + (_iota((ns, 2 * qt), 1) & (qt - 1))) >> 6
    forced = (blk == 0) | (blk == cur) | (blk == cur - 1)
    score = jnp.where(forced, FORCE_SCORE, jnp.where(blk > cur, -1.0, imp))
    cnt = jnp.zeros((ns, 2 * qt), F32)
    for j in range(ns):
        row = score[j:j + 1, :]
        cnt = cnt + ((row > score) | ((row == score) & (blk > j))).astype(F32)
    sel = (cnt < float(NSA_TOPN)).astype(F32)
    sel_ref[0, 0] = sel[:, 0:qt]
    sel_ref[0, 1] = sel[:, qt:2 * qt]


def _cmp_select_prompt(qn, kcmp_t, vcmp_t, nb, sl, qt):
    nc = kcmp_t.shape[2]
    ns = sl // NSA_SLC_BLOCK
    return pl.pallas_call(
        functools.partial(_cmpsel_t_kernel, qt=qt, nc=nc, ns=ns),
        grid=(nb, sl // qt),
        in_specs=[pl.BlockSpec((1, qt, 256), lambda b, i: (b, i, 0)),
                  pl.BlockSpec((1, LANES, nc), lambda b, i: (b, 0, 0)),
                  pl.BlockSpec((1, LANES, nc), lambda b, i: (b, 0, 0))],
        out_specs=[pl.BlockSpec((1, qt, 256), lambda b, i: (b, i, 0)),
                   pl.BlockSpec((1, 2, ns, qt), lambda b, i: (b, 0, 0, i))],
        out_shape=[jax.ShapeDtypeStruct((nb, sl, 256), F32), jax.ShapeDtypeStruct((nb, 2, ns, sl), F32)],
        compiler_params=_params(("parallel", "parallel")),
        name="nsa_cmp_select_prompt",
    )(qn.reshape(nb, sl, 256), kcmp_t, vcmp_t)


def _cmpsel_kernel(q_ref, kct_ref, vct_ref, ocmp_ref, selx_ref, *, qt, nc, ns, nsp, past, n_chunks):
    q4 = _nsa_expand(q_ref[0] * 0.125)
    s = _dot(q4, kct_ref[0])
    qpos = past + (_iota((4 * qt, nc), 0) & (qt - 1))
    blk_end = (_iota((4 * qt, nc), 1) + 1) * NSA_CMP_BLOCK - 1
    s = jnp.where(blk_end <= qpos, s, NEG_INF)
    m = jnp.max(s, axis=-1, keepdims=True)
    m = jnp.where(m == NEG_INF, 0.0, m)
    e = jnp.exp(s - m)
    p = e / jnp.maximum(jnp.sum(e, axis=-1, keepdims=True), 1e-30)
    ocmp_ref[0] = _nsa_collect(_dot_nt(p, vct_ref[0]), qt)
    impc = jnp.concatenate([p[0:qt] + p[qt:2 * qt], p[2 * qt:3 * qt] + p[3 * qt:4 * qt]], axis=0)
    pair = (_iota((nc, nsp), 0) >> 1 == _iota((nc, nsp), 1)).astype(BF16)
    imp = _dot_01_r(impc, pair)
    blk = _iota((2 * qt, nsp), 1)
    cur = (past + (_iota((2 * qt, nsp), 0) & (qt - 1))) >> 6
    forced = (blk == 0) | (blk == cur) | (blk == cur - 1)
    score = jnp.where(forced, FORCE_SCORE, jnp.where(blk > cur, -1.0, imp))
    score = jnp.where(blk < ns, score, -3.0)

    cnt = jnp.zeros((2 * qt, nsp), F32)
    for j in range(ns):
        col = score[:, j:j + 1]
        cnt = cnt + ((col > score) | ((col == score) & (blk > j))).astype(F32)
    sel = ((cnt < float(NSA_TOPN)) & (blk < ns)).astype(F32)
    sel4 = jnp.concatenate([sel[0:qt], sel[0:qt], sel[qt:2 * qt], sel[qt:2 * qt]], axis=0).astype(BF16)
    e16 = (_iota((LANES, 1024), 1) >> 6 == _iota((LANES, 1024), 0)).astype(BF16)
    for t in range(n_chunks):
        pick = ((_iota((nsp, LANES), 0) == 16 * t + _iota((nsp, LANES), 1))
                & (_iota((nsp, LANES), 1) < 16)).astype(BF16)
        blocks = jnp.dot(sel4, pick, preferred_element_type=F32).astype(BF16)
        selx_ref[0, :, 1024 * t:1024 * (t + 1)] = jnp.dot(blocks, e16, preferred_element_type=F32).astype(BF16)


def _cmp_select_sample(qn, kcmp_t, vcmp_t, nb, sl, past):
    nc = kcmp_t.shape[2]
    ns = -(-(past + sl) // NSA_SLC_BLOCK)
    nsp = -(-ns // LANES) * LANES
    n_chunks = -(-(past + sl) // 1024)
    return pl.pallas_call(
        functools.partial(_cmpsel_kernel, qt=sl, nc=nc, ns=ns, nsp=nsp, past=past, n_chunks=n_chunks),
        grid=(nb,),
        in_specs=[pl.BlockSpec((1, sl, 256), lambda b: (b, 0, 0)),
                  pl.BlockSpec((1, LANES, nc), lambda b: (b, 0, 0)),
                  pl.BlockSpec((1, LANES, nc), lambda b: (b, 0, 0))],
        out_specs=[pl.BlockSpec((1, sl, 256), lambda b: (b, 0, 0)),
                   pl.BlockSpec((1, 4 * sl, 1024 * n_chunks), lambda b: (b, 0, 0))],
        out_shape=[jax.ShapeDtypeStruct((nb, sl, 256), F32),
                   jax.ShapeDtypeStruct((nb, 4 * sl, 1024 * n_chunks), BF16)],
        compiler_params=_params(("parallel",)),
        name="nsa_cmp_select_sample",
    )(qn.reshape(nb, sl, 256), kcmp_t, vcmp_t)


def _nsap_kernel(qi_ref, kj_ref, q_ref, ks_ref, vst_ref, sel_ref, oslc_ref, q4_s, m1, l1, a1, *, qt, kt):
    step = pl.program_id(1)
    qi, kj = qi_ref[step], kj_ref[step]
    last = (qi * qt + qt - 1) // kt

    @pl.when(kj == 0)
    def _():
        q4_s[...] = _nsa_expand(q_ref[0] * NSA_QSCALE).astype(BF16)
        _init_stats((m1, l1, a1))

    def picked(g):
        rows = []
        for u in range(kt // NSA_SLC_BLOCK):
            r = sel_ref[0, g, pl.ds(kj * (kt // NSA_SLC_BLOCK) + u, 1), :]
            rows.append(jnp.broadcast_to(r, (NSA_SLC_BLOCK, qt)))
        return jnp.concatenate(rows, axis=0) > 0.5

    def slc_mask(with_causal):
        m0, m1_ = picked(0), picked(1)
        if with_causal:
            causal = kj * kt + _iota((kt, qt), 0) <= qi * qt + _iota((kt, qt), 1)
            m0, m1_ = m0 & causal, m1_ & causal
        return jnp.concatenate([m0, m0, m1_, m1_], axis=1)

    def slc_scores():
        return lax.dot_general(ks_ref[0].astype(BF16), q4_s[...], (((1,), (1,)), ((), ())),
                               preferred_element_type=F32)

    @pl.when((kj == 0) & (last > 0))
    def _():
        _online_update_t_lazy(slc_scores, slc_mask(False), vst_ref[0].astype(BF16), m1, l1, a1, first=True)

    @pl.when((kj == 0) & (last == 0))
    def _():
        _online_update_t_lazy(slc_scores, slc_mask(True), vst_ref[0].astype(BF16), m1, l1, a1, first=True)

    @pl.when((kj > 0) & (kj < last))
    def _():
        _online_update_t_lazy(slc_scores, slc_mask(False), vst_ref[0].astype(BF16), m1, l1, a1)

    @pl.when((kj > 0) & (kj == last))
    def _():
        _online_update_t_lazy(slc_scores, slc_mask(True), vst_ref[0].astype(BF16), m1, l1, a1)

    @pl.when(kj == last)
    def _():
        oslc_ref[0] = _collect_t(a1[...] / jnp.maximum(l1[...], 1e-30), qt).T


def _nsa_prompt_attn(pj, sel, nb, sl):
    qt, kt = min(512, sl), min(512, sl)
    assert kt % qt == 0
    qi_tab, kj_tab = _causal_pairs(sl // qt, qt, kt)
    kspec = pl.BlockSpec((1, kt, LANES), lambda b, s, qi, kj: (b, kj[s], 0))
    vspec = pl.BlockSpec((1, LANES, kt), lambda b, s, qi, kj: (b, 0, kj[s]))
    qspec = pl.BlockSpec((1, qt, 256), lambda b, s, qi, kj: (b, qi[s], 0))
    ns = sel.shape[2]
    r3 = lambda a, w: a.reshape(nb, sl, w)
    return pl.pallas_call(
        functools.partial(_nsap_kernel, qt=qt, kt=kt),
        grid_spec=pltpu.PrefetchScalarGridSpec(
            num_scalar_prefetch=2, grid=(nb, qi_tab.shape[0]),
            in_specs=[qspec, kspec, vspec, pl.BlockSpec((1, 2, ns, qt), lambda b, s, qi, kj: (b, 0, 0, qi[s]))],
            out_specs=qspec,
            scratch_shapes=[pltpu.VMEM((4 * qt, LANES), BF16), pltpu.VMEM((1, 4 * qt), F32),
                            pltpu.VMEM((1, 4 * qt), F32), pltpu.VMEM((LANES, 4 * qt), F32)]),
        out_shape=jax.ShapeDtypeStruct((nb, sl, 256), F32),
        compiler_params=_params(("parallel", "arbitrary")),
        name="nsa_prompt_attn",
    )(qi_tab, kj_tab, r3(pj["qnr"], 256), r3(pj["ks"], 128), pj["vsT"], sel)


def _winp_kernel(q_ref, *refs, qt, n_blk):
    kb, vtb, o_ref = refs[:n_blk], refs[n_blk:2 * n_blk], refs[2 * n_blk]
    qi = pl.program_id(1)
    q4 = _nsa_expand(q_ref[0] * NSA_QSCALE).astype(BF16)
    k_b = jnp.concatenate([r[0].astype(BF16) for r in kb], axis=0)
    vt_b = jnp.concatenate([r[0].astype(BF16) for r in vtb], axis=1)
    s = lax.dot_general(k_b, q4, (((1,), (1,)), ((), ())), preferred_element_type=F32)
    nk = n_blk * qt
    kpos = (qi - (n_blk - 1)) * qt + _iota((nk, qt), 0)
    qpos = qi * qt + _iota((nk, qt), 1)
    band = (kpos >= 0) & (kpos <= qpos) & (kpos > qpos - NSA_WINDOW)
    s = jnp.where(jnp.concatenate([band] * 4, axis=1), s, NEG_INF)
    m = jnp.max(s, axis=0, keepdims=True)
    p = jnp.exp2(s - m)
    l = jnp.sum(p, axis=0, keepdims=True)
    o = jnp.dot(vt_b, p.astype(BF16), preferred_element_type=F32) / jnp.maximum(l, 1e-30)
    o_ref[0] = _collect_t(o, qt).T


def _nsa_prompt_window(pj, nb, sl):
    qt = min(256, sl)
    n_blk = -(-(NSA_WINDOW - 1) // qt) + 1
    blk = lambda j: (lambda b, i: (b, jnp.maximum(i - (n_blk - 1) + j, 0), 0))
    blk_t = lambda j: (lambda b, i: (b, 0, jnp.maximum(i - (n_blk - 1) + j, 0)))
    qspec = pl.BlockSpec((1, qt, 256), lambda b, i: (b, i, 0))
    r3 = lambda a, w: a.reshape(nb, sl, w)
    return pl.pallas_call(
        functools.partial(_winp_kernel, qt=qt, n_blk=n_blk),
        grid=(nb, sl // qt),
        in_specs=[qspec] + [pl.BlockSpec((1, qt, LANES), blk(j)) for j in range(n_blk)]
        + [pl.BlockSpec((1, LANES, qt), blk_t(j)) for j in range(n_blk)],
        out_specs=qspec,
        out_shape=jax.ShapeDtypeStruct((nb, sl, 256), F32),
        compiler_params=_params(("parallel", "parallel")),
        name="nsa_prompt_window",
    )(r3(pj["qnr"], 256), *([r3(pj["kw"], 128)] * n_blk), *([pj["vwT"]] * n_blk))


def _nsa_gate(misc, oc, os_, ow):
    g = jax.nn.sigmoid(misc)
    head = _iota(oc.shape, 1) >> 6
    return (_head_bcast(g, head, 4, GATE_LANE0) * oc + _head_bcast(g, head, 4, GATE_LANE0 + 4) * os_
            + _head_bcast(g, head, 4, GATE_LANE0 + 8) * ow)


def _diff_lambda(lamv_ref, lam_init):
    lv = lamv_ref[...]
    s1 = jnp.sum(lv[0:1] * lv[1:2], axis=1, keepdims=True)
    s2 = jnp.sum(lv[2:3] * lv[3:4], axis=1, keepdims=True)
    return jnp.exp(s1) - jnp.exp(s2) + lam_init


def _diff_finish(o, head, nheads, nw, lam_init):
    inv = jnp.zeros(o.shape, F32)
    for h in range(nheads):
        ms = jnp.sum(jnp.where(head == h, o * o, 0.0), axis=-1, keepdims=True) * (1.0 / 64.0)
        inv = jnp.where(head == h, lax.rsqrt(ms + EPS), inv)
    return o * inv * nw * (1.0 - lam_init)


def _diffp_kernel(qi_ref, kj_ref, q_ref, k_ref, vt_ref, lamv_ref, nwt_ref, o_ref, q4_s, m_s, l_s, acc_s,
                  *, qt, kt, lam_init):
    step = pl.program_id(2)
    qi, kj = qi_ref[step], kj_ref[step]
    last = (qi * qt + qt - 1) // kt

    @pl.when(kj == 0)
    def _():
        q = q_ref[0] * DIFF_QSCALE
        part = _iota((qt, LANES), 1) >> 5
        q4_s[...] = jnp.concatenate([jnp.where(part == k, q, 0.0) for k in range(4)], axis=0).astype(BF16)
        _init_stats((m_s, l_s, acc_s))

    def scores():
        return lax.dot_general(k_ref[0].astype(BF16), q4_s[...], (((1,), (1,)), ((), ())),
                               preferred_element_type=F32)

    def causal():
        mask = kj * kt + _iota((kt, qt), 0) <= qi * qt + _iota((kt, qt), 1)
        return jnp.concatenate([mask] * 4, axis=1)

    @pl.when((kj == 0) & (last > 0))
    def _():
        _online_update_t_lazy(scores, None, vt_ref[0].astype(BF16), m_s, l_s, acc_s, first=True)

    @pl.when((kj == 0) & (last == 0))
    def _():
        _online_update_t_lazy(scores, causal(), vt_ref[0].astype(BF16), m_s, l_s, acc_s, first=True)

    @pl.when((kj > 0) & (kj < last))
    def _():
        _online_update_t_lazy(scores, None, vt_ref[0].astype(BF16), m_s, l_s, acc_s)

    @pl.when((kj > 0) & (kj == last))
    def _():
        _online_update_t_lazy(scores, causal(), vt_ref[0].astype(BF16), m_s, l_s, acc_s)

    @pl.when(kj == last)
    def _():
        lam = _diff_lambda(lamv_ref, lam_init)
        o = acc_s[...] / jnp.maximum(l_s[...], 1e-30)
        halves = []
        for h in range(2):
            rows = slice(64 * h, 64 * (h + 1))
            oh = o[rows, 2 * h * qt:(2 * h + 1) * qt] - lam * o[rows, (2 * h + 1) * qt:(2 * h + 2) * qt]
            ms = jnp.mean(oh * oh, axis=0, keepdims=True)
            halves.append(oh * lax.rsqrt(ms + EPS))
        o_ref[0] = (jnp.concatenate(halves, axis=0) * nwt_ref[...] * (1.0 - lam_init)).T


def _diff_prompt(pj, nb, sl, p, lam_init):
    qt, kt = min(512, sl), min(512, sl)
    assert kt % qt == 0
    qi_tab, kj_tab = _causal_pairs(sl // qt, qt, kt)
    qspec = pl.BlockSpec((1, qt, LANES), lambda b, h, s, qi, kj: (b, qi[s], h))
    kspec = pl.BlockSpec((1, kt, LANES), lambda b, h, s, qi, kj: (b, kj[s], h))
    vspec = pl.BlockSpec((1, LANES, kt), lambda b, h, s, qi, kj: (b, h, kj[s]))
    r3 = lambda a: a.reshape(nb, sl, 256)
    out = pl.pallas_call(
        functools.partial(_diffp_kernel, qt=qt, kt=kt, lam_init=lam_init),
        grid_spec=pltpu.PrefetchScalarGridSpec(
            num_scalar_prefetch=2, grid=(nb, 2, qi_tab.shape[0]),
            in_specs=[qspec, kspec, vspec, pl.BlockSpec((4, 32), lambda b, h, s, qi, kj: (0, 0)),
                      pl.BlockSpec((LANES, 1), lambda b, h, s, qi, kj: (0, 0))],
            out_specs=qspec,
            scratch_shapes=[pltpu.VMEM((4 * qt, LANES), BF16), pltpu.VMEM((1, 4 * qt), F32),
                            pltpu.VMEM((1, 4 * qt), F32), pltpu.VMEM((LANES, 4 * qt), F32)]),
        out_shape=jax.ShapeDtypeStruct((nb, sl, 256), F32),
        compiler_params=_params(("parallel", "parallel", "arbitrary")),
        name="diff_prompt",
    )(qi_tab, kj_tab, r3(pj["dq"]), r3(pj["dk"]), pj["dvT"], p["diff_lambda"],
      p["diff_nw256"][:, :LANES].reshape(LANES, 1))
    return out.reshape(nb * sl, 256)


def _wins_kernel(q_ref, kwt_ref, vwt_ref, knew_ref, vnew_ref, o_ref, m_s, l_s, acc_s, *, kbase, past):
    nq = q_ref.shape[1]
    rows = 4 * nq
    nk = kwt_ref.shape[-1]
    qe = _nsa_expand(q_ref[0] * NSA_QSCALE).astype(BF16)
    _init_stats((m_s, l_s, acc_s))
    s = jnp.dot(qe, kwt_ref[0, 0].astype(BF16), preferred_element_type=F32)
    qpos = past + (_iota((rows, nk), 0) & (nq - 1))
    kpos = kbase + _iota((rows, nk), 1)
    _online_update(s, (kpos <= qpos) & (kpos > qpos - NSA_WINDOW), vwt_ref[0, 0].astype(BF16), m_s, l_s, acc_s,
                   v_transposed=True)
    s2 = lax.dot_general(qe, knew_ref[0].astype(BF16), (((1,), (1,)), ((), ())), preferred_element_type=F32)
    qpos2 = past + (_iota((rows, nq), 0) & (nq - 1))
    kpos2 = past + _iota((rows, nq), 1)
    _online_update(s2, (kpos2 <= qpos2) & (kpos2 > qpos2 - NSA_WINDOW), vnew_ref[0].astype(BF16), m_s, l_s, acc_s)
    o_ref[0] = _nsa_collect(acc_s[...] / jnp.maximum(l_s[...], 1e-30), nq)


def _window_sample(q, kw_t, vw_t, layer, knew, vnew, past):
    nb, nq = q.shape[0], q.shape[1]
    wbuf = kw_t.shape[-1]
    rows = 4 * nq
    per_b = lambda r_, w: pl.BlockSpec((1, r_, w), lambda b: (b, 0, 0))
    wspec = pl.BlockSpec((1, 1, LANES, wbuf), lambda b: (layer, b, 0, 0))
    return pl.pallas_call(
        functools.partial(_wins_kernel, kbase=past - wbuf, past=past),
        grid=(nb,),
        in_specs=[per_b(nq, 256), wspec, wspec, per_b(nq, LANES), per_b(nq, LANES)],
        out_specs=per_b(nq, 256),
        out_shape=jax.ShapeDtypeStruct((nb, nq, 256), F32),
        scratch_shapes=[pltpu.VMEM((rows, 1), F32), pltpu.VMEM((rows, 1), F32), pltpu.VMEM((rows, LANES), F32)],
        compiler_params=_params(("parallel",)),
        name="window_sample",
    )(q, kw_t, vw_t, knew, vnew)


def _page_copy(src_hbm, layer, page, buf, slot, r, sem):
    return pltpu.make_async_copy(src_hbm.at[layer, page], buf.at[slot, r], sem.at[slot])


def _start_pages(srcs, bufs, sems, pt_ref, layer, b, first, slot, n):
    for r in range(n):
        page = pt_ref[b, first + r]
        for src, buf, sem in zip(srcs, bufs, sems):
            _page_copy(src, layer, page, buf, slot, r, sem).start()


def _wait_pages(srcs, bufs, sems, layer, slot, n):
    for r in range(n):
        for src, buf, sem in zip(srcs, bufs, sems):
            _page_copy(src, layer, 0, buf, slot, r, sem).wait()


def _page_ring(pt_ref, srcs, bufs, sems, layer, pps, n_steps):
    b, nb = pl.program_id(0), pl.num_programs(0)
    slot_of = lambda s: (b * n_steps + s) & 1

    @pl.when(b == 0)
    def _():
        _start_pages(srcs, bufs, sems, pt_ref, layer, 0, 0, 0, pps)

    def advance(s):
        nxt = 1 - slot_of(s)
        if s + 1 < n_steps:
            _start_pages(srcs, bufs, sems, pt_ref, layer, b, (s + 1) * pps, nxt, pps)
        else:
            @pl.when(b + 1 < nb)
            def _():
                _start_pages(srcs, bufs, sems, pt_ref, layer, b + 1, 0, nxt, pps)
        _wait_pages(srcs, bufs, sems, layer, slot_of(s), pps)

    return slot_of, advance


def _paged2_kernel(pt_ref, q_ref, k_hbm, v_hbm, knew_ref, vnew_ref, *rest, mode, layer, pps, n_steps, width, past,
                   new_len, lam_init):
    selx_ref = lamv_ref = nw_ref = None
    if mode == "slc":
        selx_ref, rest = rest[0], rest[1:]
    if mode == "diff":
        lamv_ref, nw_ref, rest = rest[0], rest[1], rest[2:]
    o_ref, kbuf, vbuf, ksem, vsem, qe_s, m_s, l_s, acc_s = rest
    nq = new_len
    rows = qe_s.shape[0]
    nk = pps * PAGE
    slot_of, advance = _page_ring(pt_ref, (k_hbm, v_hbm), (kbuf, vbuf), (ksem, vsem), layer, pps, n_steps)

    if mode == "diff":
        q = q_ref[0] * DIFF_QSCALE
        part = _iota((nq, width), 1) >> 5
        qe = jnp.concatenate([jnp.where(part == k, q, 0.0) for k in range(8)], axis=0)
    else:
        qe = _nsa_expand(q_ref[0] * NSA_QSCALE)
    qe_s[...] = qe.astype(BF16)
    _init_stats((m_s, l_s, acc_s))

    for s in range(n_steps):
        advance(s)
        slot = slot_of(s)
        kt_b = jnp.concatenate([kbuf[slot, r].astype(BF16) for r in range(pps)], axis=1)
        vt_b = jnp.concatenate([vbuf[slot, r].astype(BF16) for r in range(pps)], axis=1)
        sc = jnp.dot(qe_s[...], kt_b, preferred_element_type=F32)
        qpos = past + (_iota((rows, nk), 0) & (nq - 1))
        kpos = s * nk + _iota((rows, nk), 1)
        mask = kpos <= qpos
        if mode == "slc":
            mask = mask & (selx_ref[0, :, s * nk:(s + 1) * nk].astype(F32) > 0.5)
        _online_update(sc, mask, vt_b, m_s, l_s, acc_s, v_transposed=True)

    s2 = lax.dot_general(qe_s[...], knew_ref[0].astype(BF16), (((1,), (1,)), ((), ())), preferred_element_type=F32)
    qpos2 = past + (_iota((rows, nq), 0) & (nq - 1))
    kpos2 = past + _iota((rows, nq), 1)
    mask2 = kpos2 <= qpos2
    if mode == "slc":
        mask2 = mask2 & (selx_ref[0, :, past:past + nq].astype(F32) > 0.5)
    _online_update(s2, mask2, vnew_ref[0].astype(BF16), m_s, l_s, acc_s)
    o = acc_s[...] / jnp.maximum(l_s[...], 1e-30)
    if mode == "diff":
        lam = _diff_lambda(lamv_ref, lam_init)
        head = _iota((nq, width), 1) >> 6
        out = jnp.zeros((nq, width), F32)
        for h in range(4):
            out = jnp.where(head == h, o[(2 * h) * nq:(2 * h + 1) * nq] - lam * o[(2 * h + 1) * nq:(2 * h + 2) * nq],
                            out)
        o_ref[0] = _diff_finish(out, head, 4, nw_ref[...], lam_init)
    else:
        o_ref[0] = _nsa_collect(o, nq)


def _paged2_attn(mode, q, k_src, v_src, layer, table, knew, vnew, past, extra=(), lam_init=0.0):
    nb, nq = q.shape[0], q.shape[1]
    n_pages = table.shape[1]
    width = knew.shape[-1]
    pps = min(PAGES_PER_STEP, n_pages)
    n_steps = n_pages // pps
    assert n_pages % pps == 0 and past == n_pages * PAGE
    rows = 8 * nq if mode == "diff" else 4 * nq
    per_b = lambda r_, w: pl.BlockSpec((1, r_, w), lambda b, pt: (b, 0, 0))
    hbm = pl.BlockSpec(memory_space=pl.ANY)
    in_specs = [per_b(nq, 256), hbm, hbm, per_b(nq, width), per_b(nq, width)]
    args = [q, k_src, v_src, knew, vnew]
    if mode == "slc":
        (selx,) = extra
        in_specs.append(per_b(rows, selx.shape[-1]))
        args.append(selx)
    if mode == "diff":
        in_specs += [pl.BlockSpec((4, 32), lambda b, pt: (0, 0)), pl.BlockSpec((1, 256), lambda b, pt: (0, 0))]
        args += list(extra)
    return pl.pallas_call(
        functools.partial(_paged2_kernel, mode=mode, layer=layer, pps=pps, n_steps=n_steps, width=width, past=past,
                          new_len=nq, lam_init=lam_init),
        grid_spec=pltpu.PrefetchScalarGridSpec(
            num_scalar_prefetch=1, grid=(nb,), in_specs=in_specs, out_specs=per_b(nq, 256),
            scratch_shapes=[pltpu.VMEM((2, pps, width, PAGE), F32), pltpu.VMEM((2, pps, width, PAGE), F32),
                            pltpu.SemaphoreType.DMA((2,)), pltpu.SemaphoreType.DMA((2,)),
                            pltpu.VMEM((rows, width), BF16), pltpu.VMEM((rows, 1), F32),
                            pltpu.VMEM((rows, 1), F32), pltpu.VMEM((rows, width), F32)]),
        out_shape=jax.ShapeDtypeStruct((nb, nq, 256), F32),
        compiler_params=_params(("arbitrary",)),
        name="paged2_" + mode,
    )(table, *args)


def _summ2_kernel(pt_ref, k_hbm, v_hbm, mk_ref, mv_ref, ko_ref, vo_ref, kbuf, vbuf, ksem, vsem, *, layer, pps, n_steps):
    slot_of, advance = _page_ring(pt_ref, (k_hbm, v_hbm), (kbuf, vbuf), (ksem, vsem), layer, pps, n_steps)
    n_blk = pps * PAGE // NSA_CMP_BLOCK
    for s in range(n_steps):
        advance(s)
        slot = slot_of(s)
        for buf, m_ref, o_ref in ((kbuf, mk_ref, ko_ref), (vbuf, mv_ref, vo_ref)):
            x = jnp.concatenate([buf[slot, r].astype(BF16) for r in range(pps)], axis=1)
            top = jnp.dot(x[0:64], m_ref[0], preferred_element_type=F32)
            bot = jnp.dot(x[64:128], m_ref[1], preferred_element_type=F32)
            o_ref[0, :, s * n_blk:(s + 1) * n_blk] = jnp.concatenate([top, bot], axis=0)


def _summarize_pages(k_src, v_src, layer, table, w_k, w_v):
    nb, n_pages = table.shape
    pps = min(PAGES_PER_STEP, n_pages)
    n_steps = n_pages // pps
    step_w = pps * PAGE
    n_blk = step_w // NSA_CMP_BLOCK
    tok = jnp.arange(step_w, dtype=jnp.int32)
    in_block = tok[:, None] // NSA_CMP_BLOCK == jnp.arange(n_blk, dtype=jnp.int32)[None, :]
    weights = lambda w: jnp.where(in_block[None], w[:, tok % NSA_CMP_BLOCK][:, :, None], 0.0).astype(BF16)
    hbm = pl.BlockSpec(memory_space=pl.ANY)
    wspec = pl.BlockSpec((2, step_w, n_blk), lambda b, pt: (0, 0, 0))
    ospec = pl.BlockSpec((1, LANES, n_steps * n_blk), lambda b, pt: (b, 0, 0))
    oshape = jax.ShapeDtypeStruct((nb, LANES, n_steps * n_blk), F32)
    return pl.pallas_call(
        functools.partial(_summ2_kernel, layer=layer, pps=pps, n_steps=n_steps),
        grid_spec=pltpu.PrefetchScalarGridSpec(
            num_scalar_prefetch=1, grid=(nb,), in_specs=[hbm, hbm, wspec, wspec], out_specs=[ospec, ospec],
            scratch_shapes=[pltpu.VMEM((2, pps, LANES, PAGE), F32), pltpu.VMEM((2, pps, LANES, PAGE), F32),
                            pltpu.SemaphoreType.DMA((2,)), pltpu.SemaphoreType.DMA((2,))]),
        out_shape=[oshape, oshape],
        compiler_params=_params(("arbitrary",)),
        name="nsa_summarize_pages",
    )(table, k_src, v_src, weights(w_k), weights(w_v))


def _post_kernel(x_ref, ya_ref, misc_ref, oc_ref, os_ref, ow_ref, yc_ref, yd_ref, wo_ref, g1_ref, b1_ref,
                 w1_ref, w2_ref, g2_ref, b2_ref, o_ref, x1_s, x1b_s, acc_s, *, alpha):
    f = pl.program_id(1)

    @pl.when(f == 0)
    def _():
        yb = _nsa_gate(misc_ref[...], oc_ref[...], os_ref[...], ow_ref[...])
        mix = _dot(ya_ref[...], wo_ref[0:256, :])
        mix = mix + _dot(yb, wo_ref[256:512, :])
        mix = mix + _dot(yc_ref[...], wo_ref[512:768, :])
        mix = mix + _dot(yd_ref[...], wo_ref[768:1024, :])
        x1 = _layer_norm(alpha * x_ref[...] + mix, g1_ref[...], b1_ref[...])
        x1_s[...] = x1
        x1b_s[...] = x1.astype(BF16)
        acc_s[...] = jnp.zeros(acc_s.shape, F32)

    h = jnp.dot(x1b_s[...], w1_ref[...], preferred_element_type=F32)
    h = jnp.square(jnp.maximum(h, 0.0))
    acc_s[...] += jnp.dot(h.astype(BF16), w2_ref[...], preferred_element_type=F32)

    @pl.when(f == pl.num_programs(1) - 1)
    def _():
        o_ref[...] = _layer_norm(alpha * x1_s[...] + acc_s[...], g2_ref[...], b2_ref[...])


def _post(x2d, ya, misc, oc, os_, ow, yc, yd, p, tm, alpha):
    t = x2d.shape[0]
    tf = 1024
    row = lambda w: pl.BlockSpec((tm, w), lambda i, f: (i, 0))
    const = lambda r, w: pl.BlockSpec((r, w), lambda i, f: (0, 0))
    return pl.pallas_call(
        functools.partial(_post_kernel, alpha=alpha),
        grid=(t // tm, D_FF // tf),
        in_specs=[row(D_MODEL), row(256), row(LANES), row(256), row(256), row(256), row(256), row(256),
                  const(D_MODEL, D_MODEL), const(1, D_MODEL), const(1, D_MODEL),
                  pl.BlockSpec((D_MODEL, tf), lambda i, f: (0, f)), pl.BlockSpec((tf, D_MODEL), lambda i, f: (f, 0)),
                  const(1, D_MODEL), const(1, D_MODEL)],
        out_specs=row(D_MODEL),
        out_shape=jax.ShapeDtypeStruct((t, D_MODEL), F32),
        scratch_shapes=[pltpu.VMEM((tm, D_MODEL), F32), pltpu.VMEM((tm, D_MODEL), BF16),
                        pltpu.VMEM((tm, D_MODEL), F32)],
        compiler_params=_params(("parallel", "arbitrary")),
        name="post",
    )(x2d, ya, misc, oc.reshape(t, 256), os_.reshape(t, 256), ow.reshape(t, 256), yc, yd,
      p["w_out"], p["ln1_g"], p["ln1_b"], p["w_ff1"], p["w_ff2"], p["ln2_g"], p["ln2_b"])


def _prep_params(w, l):
    w_in = w["w_in"][l]
    misc = jnp.concatenate([w_in[:, a:b] for a, b in MISC_COLS], axis=1)
    misc = jnp.pad(misc, ((0, 0), (0, LANES - misc.shape[1])))
    w_in_p = jnp.concatenate([w_in[:, a:b] for a, b in W_IN_ORDER] + [misc], axis=1).astype(BF16)
    row = lambda v: v.reshape(1, -1).astype(F32)
    pad_lanes = lambda v: jnp.pad(v.reshape(1, -1), ((0, 0), (0, LANES - v.shape[-1])))
    blockdiag = lambda m: jax.scipy.linalg.block_diag(*[m[i] for i in range(m.shape[0])]).astype(BF16)
    return dict(
        w_in=w_in_p,
        ssd_cw=jnp.pad(w["ssd_conv_w"][l], ((0, 4), (0, 0))), ssd_cb=row(w["ssd_conv_b"][l]),
        ssd_dtb=pad_lanes(w["ssd_dt_bias"][l]), ssd_alog=pad_lanes(w["ssd_A_log"][l]),
        ssd_dexp=row(jnp.repeat(w["ssd_D"][l], 64)), ssd_nw=row(w["ssd_norm_w"][l]),
        cmp_wk=w["nsa_w_cmp_k"][l], cmp_wv=w["nsa_w_cmp_v"][l],
        diff_lambda=w["diff_lambda"][l], diff_nw256=row(jnp.tile(w["diff_norm_w"][l], 4)),
        lru_cw=jnp.pad(w["lru_conv_w"][l], ((0, 4), (0, 0))), lru_cb=row(w["lru_conv_b"][l]),
        lru_wa=blockdiag(w["lru_w_a"][l]), lru_ba=row(w["lru_b_a"][l]),
        lru_wx=blockdiag(w["lru_w_x"][l]), lru_bx=row(w["lru_b_x"][l]), lru_lam=row(w["lru_lambda"][l]),
        w_out=w["w_out"][l].astype(BF16), ln1_g=row(w["ln1_g"][l]), ln1_b=row(w["ln1_b"][l]),
        w_ff1=w["w_ff1"][l].astype(BF16), w_ff2=w["w_ff2"][l].astype(BF16),
        ln2_g=row(w["ln2_g"][l]), ln2_b=row(w["ln2_b"][l]),
    )


def _pad_prev(prev):
    return jnp.pad(prev, ((0, 0), (5, 0), (0, 0)))


def _token_minor(a):
    a = jnp.moveaxis(a, -3, -1)
    return a.reshape(*a.shape[:-3], a.shape[-3] * a.shape[-2], a.shape[-1])


def _token_major(a_t, heads):
    nb, _, n = a_t.shape
    return jnp.moveaxis(a_t.reshape(nb, heads, 64, n), -1, 1)


def _layer(x2d, nb, sl, past, hist, p, lam_init, alpha):
    t = nb * sl
    tm = min(256, t)
    pos = past + jnp.arange(sl, dtype=jnp.int32)
    pj = _proj(x2d, p["w_in"], pos, nb, sl, tm)
    y_a, ssd_h = _ssd(pj, nb, sl, _pad_prev(hist["ssd_conv"]), hist["ssd_h"].reshape(nb, 256, 128), p)
    y_d, lru_h = _lru(pj, nb, sl, _pad_prev(hist["lru_conv"]), hist["lru_h"], p)
    r3 = lambda a, w: a.reshape(nb, sl, w)
    dummy = jnp.zeros((1, 1), jnp.int32)
    if past == 0:
        step_w = min(4096, sl)
        spec = pl.BlockSpec((1, LANES, step_w), lambda b, s, pt: (b, 0, s))
        kcmp_t, vcmp_t = _summarize(pj["kcT"], pj["vcT"], [spec], step_w, nb, sl, dummy, p["cmp_wk"], p["cmp_wv"])
        o_cmp, sel = _cmp_select_prompt(pj["qn"], kcmp_t, vcmp_t, nb, sl, min(256, sl))
        o_slc = _nsa_prompt_attn(pj, sel, nb, sl)
        o_win = _nsa_prompt_window(pj, nb, sl)
        y_c = _diff_prompt(pj, nb, sl, p, lam_init)
        kw_t, vw_t = pj["kwT"][:, :, sl - NSA_WINDOW:], pj["vwT"][:, :, sl - NSA_WINDOW:]
        new = dict(kc=_token_major(pj["kcT"], 2), vc=_token_major(pj["vcT"], 2), ks=_token_major(pj["ksT"], 2),
                   vs=_token_major(pj["vsT"], 2), dk=_token_major(pj["dkT"], 4), dv=_token_major(pj["dvT"], 4))
    else:
        table, l = hist["table"], hist["layer"]
        n_pages = table.shape[1]
        kcmp_t, vcmp_t = _summarize_pages(hist["kc"], hist["vc"], l, table, p["cmp_wk"], p["cmp_wv"])
        o_cmp, selx = _cmp_select_sample(pj["qn"], kcmp_t, vcmp_t, nb, sl, past)
        qnr = r3(pj["qnr"], 256)
        o_slc = _paged2_attn("slc", qnr, hist["ks"], hist["vs"], l, table, r3(pj["ks"], 128), r3(pj["vs"], 128),
                             past, extra=(selx,))
        o_win = _window_sample(qnr, hist["kw"], hist["vw"], l, r3(pj["kw"], 128), r3(pj["vw"], 128), past)
        y_c = _paged2_attn("diff", r3(pj["dq"], 256), hist["dk"], hist["dv"], l, table, r3(pj["dk"], 256),
                           r3(pj["dv"], 256), past, extra=(p["diff_lambda"], p["diff_nw256"]),
                           lam_init=lam_init).reshape(t, 256)
        kw_t = jnp.concatenate([hist["kw"][l], pj["kwT"]], axis=2)[:, :, -NSA_WINDOW:]
        vw_t = jnp.concatenate([hist["vw"][l], pj["vwT"]], axis=2)[:, :, -NSA_WINDOW:]
        kv = lambda a: r3(a, 128).reshape(nb, sl, 2, 64)
        new = dict(kc=kv(pj["kc"]), vc=kv(pj["vc"]), ks=kv(pj["ks"]), vs=kv(pj["vs"]),
                   dk=r3(pj["dk"], 256).reshape(nb, sl, 4, 64), dv=r3(pj["dv"], 256).reshape(nb, sl, 4, 64))
    x_out = _post(x2d, y_a, pj["misc"], o_cmp, o_slc, o_win, y_c, y_d, p, min(512, t), alpha)
    new.update(kw=_token_major(kw_t, 2), vw=_token_major(vw_t, 2),
               ssd_h=ssd_h.reshape(nb, 4, 64, 128), ssd_conv=r3(pj["xbc"], 768)[:, sl - 3:],
               lru_h=lru_h, lru_conv=r3(pj["lx"], 256)[:, sl - 3:])
    return x_out, new


STATE_ORDER = ("kc", "vc", "ks", "vs", "dk", "dv", "kw", "vw", "ssd_h", "ssd_conv", "lru_h", "lru_conv")


def kernel(x_prompt, x_sample, cache_nsa_k_cmp, cache_nsa_v_cmp, cache_nsa_k_slc, cache_nsa_v_slc, cache_diff_k, cache_diff_v, cache_nsa_k_win, cache_nsa_v_win, state_ssd, state_ssd_conv, state_lru, state_lru_conv, page_table, w_in, ssd_conv_w, ssd_conv_b, ssd_dt_bias, ssd_A_log, ssd_D, ssd_norm_w, nsa_w_cmp_k, nsa_w_cmp_v, diff_lambda, diff_norm_w, lru_conv_w, lru_conv_b, lru_w_a, lru_b_a, lru_w_x, lru_b_x, lru_lambda, w_out, ln1_g, ln1_b, w_ff1, w_ff2, ln2_g, ln2_b):
    weights = dict(w_in=w_in, ssd_conv_w=ssd_conv_w, ssd_conv_b=ssd_conv_b, ssd_dt_bias=ssd_dt_bias,
                   ssd_A_log=ssd_A_log, ssd_D=ssd_D, ssd_norm_w=ssd_norm_w, nsa_w_cmp_k=nsa_w_cmp_k,
                   nsa_w_cmp_v=nsa_w_cmp_v, diff_lambda=diff_lambda, diff_norm_w=diff_norm_w, lru_conv_w=lru_conv_w,
                   lru_conv_b=lru_conv_b, lru_w_a=lru_w_a, lru_b_a=lru_b_a, lru_w_x=lru_w_x, lru_b_x=lru_b_x,
                   lru_lambda=lru_lambda, w_out=w_out, ln1_g=ln1_g, ln1_b=ln1_b, w_ff1=w_ff1, w_ff2=w_ff2,
                   ln2_g=ln2_g, ln2_b=ln2_b)
    depth = w_in.shape[0]
    nbp, slp, _ = x_prompt.shape
    nbs, sls, _ = x_sample.shape
    past = page_table.shape[1] * cache_nsa_k_cmp.shape[2]
    assert past % PAGE == 0 and sls < NSA_CMP_BLOCK and slp % 256 == 0
    alpha = (2 * depth) ** 0.25
    caches = dict(kc=_token_minor(cache_nsa_k_cmp), vc=_token_minor(cache_nsa_v_cmp), ks=_token_minor(cache_nsa_k_slc),
                  vs=_token_minor(cache_nsa_v_slc), dk=_token_minor(cache_diff_k), dv=_token_minor(cache_diff_v),
                  kw=_token_minor(cache_nsa_k_win), vw=_token_minor(cache_nsa_v_win))
    xp = x_prompt.reshape(nbp * slp, D_MODEL)
    xs = x_sample.reshape(nbs * sls, D_MODEL)
    outs_p, outs_s = [], []
    for l in range(depth):
        p = _prep_params(weights, l)
        lam_init = 0.8 - 0.6 * math.exp(-0.3 * l)
        hist_p = dict(ssd_conv=jnp.zeros((nbp, 3, SSD_CONV_CH), F32), ssd_h=jnp.zeros((nbp, 4, 64, 128), F32),
                      lru_conv=jnp.zeros((nbp, 3, GROUP_W), F32), lru_h=jnp.zeros((nbp, GROUP_W), F32))
        hist_s = dict(table=page_table, layer=l, ssd_conv=state_ssd_conv[l], ssd_h=state_ssd[l],
                      lru_conv=state_lru_conv[l], lru_h=state_lru[l], **caches)
        xp, new_p = _layer(xp, nbp, slp, 0, hist_p, p, lam_init, alpha)
        xs, new_s = _layer(xs, nbs, sls, past, hist_s, p, lam_init, alpha)
        outs_p.append(new_p)
        outs_s.append(new_s)
    stack = lambda lst, name: jnp.stack([d[name] for d in lst])
    return ((xp.reshape(nbp, slp, D_MODEL), xs.reshape(nbs, sls, D_MODEL))
            + tuple(stack(outs_p, n) for n in STATE_ORDER) + tuple(stack(outs_s, n) for n in STATE_ORDER))
```

```python
import functools
import math

import jax
import jax.numpy as jnp
from jax import lax
from jax.experimental import pallas as pl
from jax.experimental.pallas import tpu as pltpu

F32 = jnp.float32
BF16 = jnp.bfloat16
NEG_INF = float("-inf")

D_MODEL = 1024
GROUP_W = 256
SSD_HEADS = 4
SSD_STATE = 128
SSD_CONV_CH = 768
NSA_CMP_BLOCK = 32
NSA_SLC_BLOCK = 64
NSA_TOPN = 16
NSA_WINDOW = 512
FORCE_SCORE = 1e4
LRU_C = 8.0
D_FF = 4096
ROPE_THETA = 500000.0
EPS = 1e-5
PAGE = 128
LANES = 128
VMEM_LIMIT = 56 * 1024 * 1024
PAGES_PER_STEP = 32
LAZY_MAX_RISE = 64.0
LOG2E = 1.4426950408889634
NSA_QSCALE = 64.0 ** -0.5 * LOG2E
DIFF_QSCALE = 32.0 ** -0.5 * LOG2E

W_IN_ORDER = ((0, 1024), (1028, 2052), (2064, 3344))
MISC_COLS = ((1024, 1028), (2052, 2064))
SEG = dict(z=(0, 256), xbc=(256, 1024), qn=(1024, 1280), kc=(1280, 1408), vc=(1408, 1536), ks=(1536, 1664),
           vs=(1664, 1792), kw=(1792, 1920), vw=(1920, 2048), dq=(2048, 2304), dk=(2304, 2560), dv=(2560, 2816),
           lx=(2816, 3072), lg=(3072, 3328), misc=(3328, 3456))
D_IN_PAD = 3456
GATE_LANE0 = 4


def _params(sem):
    return pltpu.CompilerParams(dimension_semantics=sem, vmem_limit_bytes=VMEM_LIMIT)


def _iota(shape, dim):
    return lax.broadcasted_iota(jnp.int32, shape, dim)


def _dot(a, b):
    return jnp.dot(a.astype(BF16), b.astype(BF16), preferred_element_type=F32)


def _dot_nt(a, b):
    return lax.dot_general(a.astype(BF16), b.astype(BF16), (((1,), (1,)), ((), ())), preferred_element_type=F32)


def _split3(x):
    h1 = x.astype(BF16)
    r1 = x - h1.astype(F32)
    h2 = r1.astype(BF16)
    h3 = (r1 - h2.astype(F32)).astype(BF16)
    return h1, h2, h3


def _dot_01(m01, x):
    return sum(jnp.dot(m01, part, preferred_element_type=F32) for part in _split3(x))


def _dot_01_r(x, m01):
    return sum(jnp.dot(part, m01, preferred_element_type=F32) for part in _split3(x))


def _dot_nt_01(m01, x):
    return sum(lax.dot_general(m01, part, (((1,), (1,)), ((), ())), preferred_element_type=F32)
               for part in _split3(x))


def _eye(n, m):
    return (_iota((n, m), 0) == _iota((n, m), 1)).astype(BF16)


def _softplus(x):
    return jnp.maximum(x, 0.0) + jnp.log1p(jnp.exp(-jnp.abs(x)))


def _head_bcast(cols, h_of_lane, nheads, lane0=0):
    out = jnp.zeros(h_of_lane.shape, F32)
    for h in range(nheads):
        out = jnp.where(h_of_lane == h, cols[:, lane0 + h:lane0 + h + 1], out)
    return out


def _online_update(s, mask, v_b, m_ref, l_ref, acc_ref, v_transposed=False):
    s = jnp.where(mask, s, NEG_INF)
    m_prev = m_ref[...]
    m_new = jnp.maximum(m_prev, jnp.max(s, axis=-1, keepdims=True))
    m_safe = jnp.where(m_new == NEG_INF, 0.0, m_new)
    alpha = jnp.exp2(m_prev - m_safe)
    p = jnp.exp2(s - m_safe)
    l_ref[...] = alpha * l_ref[...] + jnp.sum(p, axis=-1, keepdims=True)
    if v_transposed:
        pv = lax.dot_general(p.astype(BF16), v_b, (((1,), (1,)), ((), ())), preferred_element_type=F32)
    else:
        pv = jnp.dot(p.astype(BF16), v_b, preferred_element_type=F32)
    acc_ref[...] = alpha * acc_ref[...] + pv
    m_ref[...] = m_new


def _online_update_t(s, mask, vt_b, m_ref, l_ref, acc_ref):
    if mask is not None:
        s = jnp.where(mask, s, NEG_INF)
    m_prev = m_ref[...]
    m_new = jnp.maximum(m_prev, jnp.max(s, axis=0, keepdims=True))
    m_safe = jnp.where(m_new == NEG_INF, 0.0, m_new)
    alpha = jnp.exp2(m_prev - m_safe)
    p = jnp.exp2(s - m_safe)
    l_ref[...] = alpha * l_ref[...] + jnp.sum(p, axis=0, keepdims=True)
    acc_ref[...] = alpha * acc_ref[...] + jnp.dot(vt_b, p.astype(BF16), preferred_element_type=F32)
    m_ref[...] = m_new


def _online_update_t_lazy(scores, mask, vt_b, m_ref, l_ref, acc_ref, first=False):
    s = scores()
    if mask is not None:
        s = jnp.where(mask, s, NEG_INF)
    m_prev = s[0:1, :] if first else m_ref[...]
    tile_max = jnp.max(s, axis=0, keepdims=True)
    p = jnp.exp2(s - m_prev)
    p_sum = jnp.sum(p, axis=0, keepdims=True)
    pv = jnp.dot(vt_b, p.astype(BF16), preferred_element_type=F32)
    safe = jnp.max(tile_max - m_prev) <= LAZY_MAX_RISE

    @pl.when(safe)
    def _():
        m_new = jnp.maximum(m_prev, tile_max)
        alpha = jnp.exp2(m_prev - m_new)
        l_ref[...] = (l_ref[...] + p_sum) * alpha
        acc_ref[...] = (acc_ref[...] + pv) * alpha
        m_ref[...] = m_new

    @pl.when(jnp.logical_not(safe))
    def _():
        _online_update_t(scores(), mask, vt_b, m_ref, l_ref, acc_ref)


def _causal_pairs(n_q, qt, kt):
    qi, kj = [], []
    for i in range(n_q):
        for j in range((i * qt + qt - 1) // kt + 1):
            qi.append(i)
            kj.append(j)
    return jnp.asarray(qi, jnp.int32), jnp.asarray(kj, jnp.int32)


def _init_stats(*triples):
    for m_, l_, a_ in triples:
        m_[...] = jnp.full(m_.shape, NEG_INF, F32)
        l_[...] = jnp.zeros(l_.shape, F32)
        a_[...] = jnp.zeros(a_.shape, F32)


def _nsa_expand(q):
    lo = _iota((q.shape[0], LANES), 1) < 64
    a, b = q[:, :LANES], q[:, LANES:]
    h0 = jnp.where(lo, a, 0.0)
    h1 = jnp.where(lo, pltpu.roll(a, 64, 1), 0.0)
    h2 = jnp.where(lo, 0.0, pltpu.roll(b, 64, 1))
    h3 = jnp.where(lo, 0.0, b)
    return jnp.concatenate([h0, h1, h2, h3], axis=0)


def _nsa_collect(o, n):
    lo = _iota((n, LANES), 1) < 64
    o0, o1, o2, o3 = o[0:n], o[n:2 * n], o[2 * n:3 * n], o[3 * n:4 * n]
    left = jnp.where(lo, o0, pltpu.roll(o1, 64, 1))
    right = jnp.where(lo, pltpu.roll(o2, 64, 1), o3)
    return jnp.concatenate([left, right], axis=1)


def _collect_t(o_t, n):
    return jnp.concatenate([o_t[0:64, 0:n], o_t[0:64, n:2 * n], o_t[64:128, 2 * n:3 * n], o_t[64:128, 3 * n:4 * n]],
                           axis=0)


def _layer_norm(v, g, b):
    mu = jnp.mean(v, axis=-1, keepdims=True)
    d = v - mu
    var = jnp.mean(d * d, axis=-1, keepdims=True)
    return d * lax.rsqrt(var + EPS) * g + b


def _rope128(v, c, sa, sb, half):
    return v * c + pltpu.roll(v, LANES - half, 1) * sa + pltpu.roll(v, half, 1) * sb


PROJ_OUT = ("z", "xbc", "qn", "qnr", "kc", "vc", "ks", "vs", "kw", "vw", "dq", "dk", "dv", "lx", "lg", "misc")
PROJ_OUT_T = ("kcT", "vcT", "ksT", "vsT", "kwT", "vwT", "dkT", "dvT")
PROJ_W = dict(z=256, xbc=768, qn=256, qnr=256, kc=128, vc=128, ks=128, vs=128, kw=128, vw=128, dq=256, dk=256,
              dv=256, lx=256, lg=256, misc=128)


def _proj_kernel(x_ref, w_ref, cn_ref, san_ref, sbn_ref, cd_ref, sad_ref, sbd_ref, *outs, flat_t):
    o = dict(zip(PROJ_OUT + PROJ_OUT_T, outs))
    xb = x_ref[...].astype(BF16)

    def seg(name):
        a, b = SEG[name]
        return jnp.dot(xb, w_ref[:, a:b], preferred_element_type=F32)

    def rope_n(v):
        return _rope128(v, cn_ref[...], san_ref[...], sbn_ref[...], 8)

    def rope_d(v):
        return _rope128(v, cd_ref[...], sad_ref[...], sbd_ref[...], 4)

    def put(name, v):
        o[name][...] = v
        if name + "T" in o:
            if flat_t:
                o[name + "T"][...] = v.T
            else:
                o[name + "T"][0] = v.T

    put("z", seg("z"))
    put("xbc", seg("xbc"))
    qn = seg("qn")
    put("qn", qn)
    put("qnr", jnp.concatenate([rope_n(qn[:, :LANES]), rope_n(qn[:, LANES:])], axis=1))
    put("kc", seg("kc"))
    put("vc", seg("vc"))
    put("ks", rope_n(seg("ks")))
    put("vs", seg("vs"))
    put("kw", rope_n(seg("kw")))
    put("vw", seg("vw"))
    dq = seg("dq")
    put("dq", jnp.concatenate([rope_d(dq[:, :LANES]), rope_d(dq[:, LANES:])], axis=1))
    dk = seg("dk")
    put("dk", jnp.concatenate([rope_d(dk[:, :LANES]), rope_d(dk[:, LANES:])], axis=1))
    put("dv", seg("dv"))
    put("lx", seg("lx"))
    put("lg", seg("lg"))
    put("misc", seg("misc"))


def _rope_tables(pos, dh, rows):
    rd = dh // 4
    half = rd // 2
    inv = ROPE_THETA ** (-jnp.arange(half, dtype=F32) / half)
    ang = pos.astype(F32)[:, None] * inv[None, :]
    cos, sin = jnp.cos(ang), jnp.sin(ang)
    n = pos.shape[0]
    zero_h = jnp.zeros((n, half), F32)
    zero_r = jnp.zeros((n, dh - rd), F32)
    c = jnp.concatenate([cos, cos, jnp.ones((n, dh - rd), F32)], 1)
    sa = jnp.concatenate([-sin, zero_h, zero_r], 1)
    sb = jnp.concatenate([zero_h, sin, zero_r], 1)
    reps = (max(rows // n, 1), LANES // dh)
    return tuple(jnp.tile(t, reps) for t in (c, sa, sb))


def _proj(x2d, w_in_p, pos, nb, sl, tm):
    t = x2d.shape[0]
    n_tab = max(sl // tm, 1)
    flat_t = sl < tm
    tabs = _rope_tables(pos, 64, tm) + _rope_tables(pos, 32, tm)
    tab_spec = pl.BlockSpec((tm, LANES), lambda i: (i % n_tab, 0))
    out_specs = [pl.BlockSpec((tm, PROJ_W[k]), lambda i: (i, 0)) for k in PROJ_OUT]
    out_shape = [jax.ShapeDtypeStruct((t, PROJ_W[k]), F32) for k in PROJ_OUT]
    for k in PROJ_OUT_T:
        c = PROJ_W[k[:-1]]
        if flat_t:
            out_specs.append(pl.BlockSpec((c, tm), lambda i: (0, i)))
            out_shape.append(jax.ShapeDtypeStruct((c, t), F32))
        else:
            out_specs.append(pl.BlockSpec((1, c, tm), lambda i: (i // n_tab, 0, i % n_tab)))
            out_shape.append(jax.ShapeDtypeStruct((nb, c, sl), F32))
    outs = pl.pallas_call(
        functools.partial(_proj_kernel, flat_t=flat_t),
        grid=(t // tm,),
        in_specs=[pl.BlockSpec((tm, D_MODEL), lambda i: (i, 0)),
                  pl.BlockSpec((D_MODEL, D_IN_PAD), lambda i: (0, 0))] + [tab_spec] * 6,
        out_specs=out_specs, out_shape=out_shape,
        compiler_params=_params(("parallel",)),
        name="proj",
    )(x2d, w_in_p, *tabs)
    pj = dict(zip(PROJ_OUT + PROJ_OUT_T, outs))
    if flat_t:
        for k in PROJ_OUT_T:
            pj[k] = pj[k].reshape(-1, nb, sl).transpose(1, 0, 2)
    return pj


def _conv_chunk(x_ref, prev_ref, cw_ref, cb_ref, xbuf, q, first):
    @pl.when(first)
    def _():
        xbuf[0:8, :] = prev_ref[0]

    xbuf[8:8 + q, :] = x_ref[0]
    acc = cb_ref[...] + cw_ref[0:1, :] * xbuf[pl.ds(5, q), :]
    for j in range(1, 4):
        acc = acc + cw_ref[j:j + 1, :] * xbuf[pl.ds(5 + j, q), :]
    xbuf[0:8, :] = xbuf[q:q + 8, :]
    return acc


def _ssd_kernel(xbc_ref, z_ref, misc_ref, prev_ref, h0_ref, cw_ref, cb_ref, dtb_ref, alog_ref, dexp_ref, nw_ref,
                y_ref, hout_ref, xbuf, ht, *, q):
    c = pl.program_id(1)

    @pl.when(c == 0)
    def _():
        ht[...] = _dot_nt_01(_eye(SSD_STATE, SSD_STATE), h0_ref[0])

    acc = _conv_chunk(xbc_ref, prev_ref, cw_ref, cb_ref, xbuf, q, c == 0)
    xc = acc * jax.nn.sigmoid(acc)
    sx = xc[:, 0:256]
    sb = (xc[:, 256:384], xc[:, 384:512])
    sc = (xc[:, 512:640], xc[:, 640:768])
    dt = _softplus(misc_ref[0] + dtb_ref[...])
    da = dt * (-jnp.exp(alog_ref[...]))
    causal = _iota((q, q), 0) >= _iota((q, q), 1)
    acum = _dot_01(causal.astype(BF16), da)
    xsel = _eye(8, LANES)
    acum_t = _dot_nt_01(xsel, acum)
    dt_t = _dot_nt_01(xsel, dt)
    a_last = acum[q - 1:q, :]
    wlast = jnp.exp(a_last - acum) * dt
    ea = jnp.exp(acum)
    head = _iota((q, GROUP_W), 1) >> 6
    ht_old = ht[...]
    y = dexp_ref[...] * sx
    for g in range(2):
        cb = _dot_nt(sc[g], sb[g])
        for h in (2 * g, 2 * g + 1):
            seg = acum[:, h:h + 1] - acum_t[h:h + 1, :]
            decay = jnp.exp(jnp.where(causal, seg, NEG_INF))
            m = cb * decay * dt_t[h:h + 1, :]
            y_h = _dot(m, sx) + _dot(sc[g] * ea[:, h:h + 1], ht_old)
            y = y + jnp.where(head == h, y_h, 0.0)
    xw = sx * _head_bcast(wlast, head, SSD_HEADS)
    lane = _iota((q, GROUP_W), 1)
    bt0 = _dot_nt(_eye(SSD_STATE, SSD_STATE), sb[0])
    bt1 = _dot_nt(_eye(SSD_STATE, SSD_STATE), sb[1])
    head1 = _iota((1, GROUP_W), 1) >> 6
    dch = _head_bcast(jnp.exp(a_last), head1, SSD_HEADS)
    ht_new = dch * ht_old + _dot(bt0, jnp.where(lane < 128, xw, 0.0)) + _dot(bt1, jnp.where(lane < 128, 0.0, xw))
    ht[...] = ht_new
    zz = z_ref[0]
    y = y * (zz * jax.nn.sigmoid(zz))
    y_ref[0] = y * lax.rsqrt(jnp.mean(y * y, axis=-1, keepdims=True) + EPS) * nw_ref[...]

    @pl.when(c == pl.num_programs(1) - 1)
    def _():
        hout_ref[0] = _dot_nt_01(_eye(GROUP_W, GROUP_W), ht_new)


def _ssd(pj, nb, sl, prev8, h0, p):
    q = min(128, sl)
    nc = sl // q
    row = lambda w: pl.BlockSpec((1, q, w), lambda b, c: (b, c, 0))
    per_b = lambda r, w: pl.BlockSpec((1, r, w), lambda b, c: (b, 0, 0))
    const = lambda r, w: pl.BlockSpec((r, w), lambda b, c: (0, 0))
    y, hout = pl.pallas_call(
        functools.partial(_ssd_kernel, q=q),
        grid=(nb, nc),
        in_specs=[row(768), row(256), row(128), per_b(8, 768), per_b(256, 128),
                  const(8, 768), const(1, 768), const(1, 128), const(1, 128), const(1, 256), const(1, 256)],
        out_specs=[row(256), per_b(256, 128)],
        out_shape=[jax.ShapeDtypeStruct((nb, sl, 256), F32), jax.ShapeDtypeStruct((nb, 256, 128), F32)],
        scratch_shapes=[pltpu.VMEM((q + 8, 768), F32), pltpu.VMEM((SSD_STATE, GROUP_W), F32)],
        compiler_params=_params(("parallel", "arbitrary")),
        name="ssd",
    )(pj["xbc"].reshape(nb, sl, 768), pj["z"].reshape(nb, sl, 256), pj["misc"].reshape(nb, sl, 128), prev8, h0,
      p["ssd_cw"], p["ssd_cb"], p["ssd_dtb"], p["ssd_alog"], p["ssd_dexp"], p["ssd_nw"])
    return y.reshape(nb * sl, 256), hout


def _lru_kernel(x_ref, g_ref, prev_ref, h0_ref, cw_ref, cb_ref, wa_ref, ba_ref, wx_ref, bx_ref, lam_ref,
                y_ref, hout_ref, xbuf, hc, *, q):
    c = pl.program_id(1)

    @pl.when(c == 0)
    def _():
        hc[...] = h0_ref[0]

    xc = _conv_chunk(x_ref, prev_ref, cw_ref, cb_ref, xbuf, q, c == 0)
    xcb = xc.astype(BF16)
    r = jax.nn.sigmoid(jnp.dot(xcb, wa_ref[...], preferred_element_type=F32) + ba_ref[...])
    i = jax.nn.sigmoid(jnp.dot(xcb, wx_ref[...], preferred_element_type=F32) + bx_ref[...])
    log_a = -LRU_C * r * _softplus(-lam_ref[...])
    a = jnp.exp(log_a)
    u = jnp.sqrt(1.0 - jnp.exp(2.0 * log_a)) * (i * xc)
    row = _iota((q, GROUP_W), 0)
    s = 1
    while s < q:
        a_sh = pltpu.roll(a, s, 0)
        u_sh = pltpu.roll(u, s, 0)
        keep = row >= s
        u = jnp.where(keep, a * u_sh + u, u)
        a = jnp.where(keep, a * a_sh, a)
        s *= 2
    h = u + a * hc[...]
    hc[...] = h[q - 1:q, :]
    gg = g_ref[0]
    gelu = 0.5 * gg * (1.0 + jnp.tanh(math.sqrt(2.0 / math.pi) * (gg + 0.044715 * (gg * gg * gg))))
    y_ref[0] = h * gelu

    @pl.when(c == pl.num_programs(1) - 1)
    def _():
        hout_ref[0] = h[q - 1:q, :]


def _lru(pj, nb, sl, prev8, h0, p):
    q = min(256, sl)
    nc = sl // q
    row = lambda w: pl.BlockSpec((1, q, w), lambda b, c: (b, c, 0))
    per_b = lambda r, w: pl.BlockSpec((1, r, w), lambda b, c: (b, 0, 0))
    const = lambda r, w: pl.BlockSpec((r, w), lambda b, c: (0, 0))
    y, hout = pl.pallas_call(
        functools.partial(_lru_kernel, q=q),
        grid=(nb, nc),
        in_specs=[row(256), row(256), per_b(8, 256), per_b(1, 256), const(8, 256), const(1, 256),
                  const(256, 256), const(1, 256), const(256, 256), const(1, 256), const(1, 256)],
        out_specs=[row(256), per_b(1, 256)],
        out_shape=[jax.ShapeDtypeStruct((nb, sl, 256), F32), jax.ShapeDtypeStruct((nb, 1, 256), F32)],
        scratch_shapes=[pltpu.VMEM((q + 8, 256), F32), pltpu.VMEM((1, 256), F32)],
        compiler_params=_params(("parallel", "arbitrary")),
        name="lru",
    )(pj["lx"].reshape(nb, sl, 256), pj["lg"].reshape(nb, sl, 256), prev8, h0.reshape(nb, 1, 256),
      p["lru_cw"], p["lru_cb"], p["lru_wa"], p["lru_ba"], p["lru_wx"], p["lru_bx"], p["lru_lam"])
    return y.reshape(nb * sl, 256), hout.reshape(nb, 256)


def _summ_kernel(pt_ref, *refs, n_parts):
    kp, vp = refs[:n_parts], refs[n_parts:2 * n_parts]
    mk_ref, mv_ref, ko_ref, vo_ref = refs[2 * n_parts:]

    def one(parts, m_ref, o_ref):
        chunks = []
        for r in parts:
            x = r[...]
            chunks.append(x.reshape(x.shape[-2], x.shape[-1]).astype(BF16))
        x = jnp.concatenate(chunks, axis=1) if len(chunks) > 1 else chunks[0]
        top = jnp.dot(x[0:64], m_ref[0], preferred_element_type=F32)
        bot = jnp.dot(x[64:128], m_ref[1], preferred_element_type=F32)
        o_ref[0] = jnp.concatenate([top, bot], axis=0)

    one(kp, mk_ref, ko_ref)
    one(vp, mv_ref, vo_ref)


def _summarize(k_src, v_src, part_specs, part_w, nb, total, table, w_k, w_v):
    n_parts = len(part_specs)
    step_w = n_parts * part_w
    n_blk = step_w // NSA_CMP_BLOCK
    tok = jnp.arange(step_w, dtype=jnp.int32)
    in_block = tok[:, None] // NSA_CMP_BLOCK == jnp.arange(n_blk, dtype=jnp.int32)[None, :]
    weights = lambda w: jnp.where(in_block[None], w[:, tok % NSA_CMP_BLOCK][:, :, None], 0.0).astype(BF16)
    wspec = pl.BlockSpec((2, step_w, n_blk), lambda b, s, pt: (0, 0, 0))
    ospec = pl.BlockSpec((1, LANES, n_blk), lambda b, s, pt: (b, 0, s))
    oshape = jax.ShapeDtypeStruct((nb, LANES, total // NSA_CMP_BLOCK), F32)
    return pl.pallas_call(
        functools.partial(_summ_kernel, n_parts=n_parts),
        grid_spec=pltpu.PrefetchScalarGridSpec(
            num_scalar_prefetch=1, grid=(nb, total // step_w),
            in_specs=list(part_specs) * 2 + [wspec, wspec],
            out_specs=[ospec, ospec]),
        out_shape=[oshape, oshape],
        compiler_params=_params(("parallel", "arbitrary")),
        name="nsa_summarize",
    )(table, *([k_src] * n_parts), *([v_src] * n_parts), weights(w_k), weights(w_v))


def _cmpsel_t_kernel(q_ref, kct_ref, vct_ref, ocmp_ref, sel_ref, *, qt, nc, ns):
    qi = pl.program_id(1)
    q4 = _nsa_expand(q_ref[0] * 0.125)
    s = _dot_nt(kct_ref[0].T, q4)
    qpos = qi * qt + (_iota((nc, 4 * qt), 1) & (qt - 1))
    blk_end = (_iota((nc, 4 * qt), 0) + 1) * NSA_CMP_BLOCK - 1
    s = jnp.where(blk_end <= qpos, s, NEG_INF)
    m = jnp.max(s, axis=0, keepdims=True)
    m = jnp.where(m == NEG_INF, 0.0, m)
    e = jnp.exp(s - m)
    p = e / jnp.maximum(jnp.sum(e, axis=0, keepdims=True), 1e-30)
    ocmp_ref[0] = _collect_t(_dot(vct_ref[0], p), qt).T
    impc = jnp.concatenate([p[:, 0:qt] + p[:, qt:2 * qt], p[:, 2 * qt:3 * qt] + p[:, 3 * qt:4 * qt]], axis=1)
    pair = (_iota((ns, nc), 1) >> 1 == _iota((ns, nc), 0)).astype(BF16)
    imp = _dot_01(pair, impc)
    blk = _iota((ns, 2 * qt), 0)
    cur = (qi * qt + (_iota((ns, 2 * qt), 1) & (qt - 1))) >> 6
    forced = (blk == 0) | (blk == cur) | (blk == cur - 1)
    score = jnp.where(forced, FORCE_SCORE, jnp.where(blk > cur, -1.0, imp))
    cnt = jnp.zeros((ns, 2 * qt), F32)
    for j in range(ns):
        row = score[j:j + 1, :]
        cnt = cnt + ((row > score) | ((row == score) & (blk > j))).astype(F32)
    sel = (cnt < float(NSA_TOPN)).astype(F32)
    sel_ref[0, 0] = sel[:, 0:qt]
    sel_ref[0, 1] = sel[:, qt:2 * qt]


def _cmp_select_prompt(qn, kcmp_t, vcmp_t, nb, sl, qt):
    nc = kcmp_t.shape[2]
    ns = sl // NSA_SLC_BLOCK
    return pl.pallas_call(
        functools.partial(_cmpsel_t_kernel, qt=qt, nc=nc, ns=ns),
        grid=(nb, sl // qt),
        in_specs=[pl.BlockSpec((1, qt, 256), lambda b, i: (b, i, 0)),
                  pl.BlockSpec((1, LANES, nc), lambda b, i: (b, 0, 0)),
                  pl.BlockSpec((1, LANES, nc), lambda b, i: (b, 0, 0))],
        out_specs=[pl.BlockSpec((1, qt, 256), lambda b, i: (b, i, 0)),
                   pl.BlockSpec((1, 2, ns, qt), lambda b, i: (b, 0, 0, i))],
        out_shape=[jax.ShapeDtypeStruct((nb, sl, 256), F32), jax.ShapeDtypeStruct((nb, 2, ns, sl), F32)],
        compiler_params=_params(("parallel", "parallel")),
        name="nsa_cmp_select_prompt",
    )(qn.reshape(nb, sl, 256), kcmp_t, vcmp_t)


def _cmpsel_kernel(q_ref, kct_ref, vct_ref, ocmp_ref, selx_ref, *, qt, nc, ns, nsp, past, n_chunks):
    q4 = _nsa_expand(q_ref[0] * 0.125)
    s = _dot(q4, kct_ref[0])
    qpos = past + (_iota((4 * qt, nc), 0) & (qt - 1))
    blk_end = (_iota((4 * qt, nc), 1) + 1) * NSA_CMP_BLOCK - 1
    s = jnp.where(blk_end <= qpos, s, NEG_INF)
    m = jnp.max(s, axis=-1, keepdims=True)
    m = jnp.where(m == NEG_INF, 0.0, m)
    e = jnp.exp(s - m)
    p = e / jnp.maximum(jnp.sum(e, axis=-1, keepdims=True), 1e-30)
    ocmp_ref[0] = _nsa_collect(_dot_nt(p, vct_ref[0]), qt)
    impc = jnp.concatenate([p[0:qt] + p[qt:2 * qt], p[2 * qt:3 * qt] + p[3 * qt:4 * qt]], axis=0)
    pair = (_iota((nc, nsp), 0) >> 1 == _iota((nc, nsp), 1)).astype(BF16)
    imp = _dot_01_r(impc, pair)
    blk = _iota((2 * qt, nsp), 1)
    cur = (past + (_iota((2 * qt, nsp), 0) & (qt - 1))) >> 6
    forced = (blk == 0) | (blk == cur) | (blk == cur - 1)
    score = jnp.where(forced, FORCE_SCORE, jnp.where(blk > cur, -1.0, imp))
    score = jnp.where(blk < ns, score, -3.0)

    cnt = jnp.zeros((2 * qt, nsp), F32)
    for j in range(ns):
        col = score[:, j:j + 1]
        cnt = cnt + ((col > score) | ((col == score) & (blk > j))).astype(F32)
    sel = ((cnt < float(NSA_TOPN)) & (blk < ns)).astype(F32)
    sel4 = jnp.concatenate([sel[0:qt], sel[0:qt], sel[qt:2 * qt], sel[qt:2 * qt]], axis=0).astype(BF16)
    e16 = (_iota((LANES, 1024), 1) >> 6 == _iota((LANES, 1024), 0)).astype(BF16)
    for t in range(n_chunks):
        pick = ((_iota((nsp, LANES), 0) == 16 * t + _iota((nsp, LANES), 1))
                & (_iota((nsp, LANES), 1) < 16)).astype(BF16)
        blocks = jnp.dot(sel4, pick, preferred_element_type=F32).astype(BF16)
        selx_ref[0, :, 1024 * t:1024 * (t + 1)] = jnp.dot(blocks, e16, preferred_element_type=F32).astype(BF16)


def _cmp_select_sample(qn, kcmp_t, vcmp_t, nb, sl, past):
    nc = kcmp_t.shape[2]
    ns = -(-(past + sl) // NSA_SLC_BLOCK)
    nsp = -(-ns // LANES) * LANES
    n_chunks = -(-(past + sl) // 1024)
    return pl.pallas_call(
        functools.partial(_cmpsel_kernel, qt=sl, nc=nc, ns=ns, nsp=nsp, past=past, n_chunks=n_chunks),
        grid=(nb,),
        in_specs=[pl.BlockSpec((1, sl, 256), lambda b: (b, 0, 0)),
                  pl.BlockSpec((1, LANES, nc), lambda b: (b, 0, 0)),
                  pl.BlockSpec((1, LANES, nc), lambda b: (b, 0, 0))],
        out_specs=[pl.BlockSpec((1, sl, 256), lambda b: (b, 0, 0)),
                   pl.BlockSpec((1, 4 * sl, 1024 * n_chunks), lambda b: (b, 0, 0))],
        out_shape=[jax.ShapeDtypeStruct((nb, sl, 256), F32),
                   jax.ShapeDtypeStruct((nb, 4 * sl, 1024 * n_chunks), BF16)],
        compiler_params=_params(("parallel",)),
        name="nsa_cmp_select_sample",
    )(qn.reshape(nb, sl, 256), kcmp_t, vcmp_t)


def _nsap_kernel(qi_ref, kj_ref, q_ref, ks_ref, vst_ref, sel_ref, oslc_ref, q4_s, m1, l1, a1, *, qt, kt):
    step = pl.program_id(1)
    qi, kj = qi_ref[step], kj_ref[step]
    last = (qi * qt + qt - 1) // kt

    @pl.when(kj == 0)
    def _():
        q4_s[...] = _nsa_expand(q_ref[0] * NSA_QSCALE).astype(BF16)
        _init_stats((m1, l1, a1))

    def picked(g):
        rows = []
        for u in range(kt // NSA_SLC_BLOCK):
            r = sel_ref[0, g, pl.ds(kj * (kt // NSA_SLC_BLOCK) + u, 1), :]
            rows.append(jnp.broadcast_to(r, (NSA_SLC_BLOCK, qt)))
        return jnp.concatenate(rows, axis=0) > 0.5

    def slc_mask(with_causal):
        m0, m1_ = picked(0), picked(1)
        if with_causal:
            causal = kj * kt + _iota((kt, qt), 0) <= qi * qt + _iota((kt, qt), 1)
            m0, m1_ = m0 & causal, m1_ & causal
        return jnp.concatenate([m0, m0, m1_, m1_], axis=1)

    def slc_scores():
        return lax.dot_general(ks_ref[0].astype(BF16), q4_s[...], (((1,), (1,)), ((), ())),
                               preferred_element_type=F32)

    @pl.when((kj == 0) & (last > 0))
    def _():
        _online_update_t_lazy(slc_scores, slc_mask(False), vst_ref[0].astype(BF16), m1, l1, a1, first=True)

    @pl.when((kj == 0) & (last == 0))
    def _():
        _online_update_t_lazy(slc_scores, slc_mask(True), vst_ref[0].astype(BF16), m1, l1, a1, first=True)

    @pl.when((kj > 0) & (kj < last))
    def _():
        _online_update_t_lazy(slc_scores, slc_mask(False), vst_ref[0].astype(BF16), m1, l1, a1)

    @pl.when((kj > 0) & (kj == last))
    def _():
        _online_update_t_lazy(slc_scores, slc_mask(True), vst_ref[0].astype(BF16), m1, l1, a1)

    @pl.when(kj == last)
    def _():
        oslc_ref[0] = _collect_t(a1[...] / jnp.maximum(l1[...], 1e-30), qt).T


def _nsa_prompt_attn(pj, sel, nb, sl):
    qt, kt = min(512, sl), min(512, sl)
    assert kt % qt == 0
    qi_tab, kj_tab = _causal_pairs(sl // qt, qt, kt)
    kspec = pl.BlockSpec((1, kt, LANES), lambda b, s, qi, kj: (b, kj[s], 0))
    vspec = pl.BlockSpec((1, LANES, kt), lambda b, s, qi, kj: (b, 0, kj[s]))
    qspec = pl.BlockSpec((1, qt, 256), lambda b, s, qi, kj: (b, qi[s], 0))
    ns = sel.shape[2]
    r3 = lambda a, w: a.reshape(nb, sl, w)
    return pl.pallas_call(
        functools.partial(_nsap_kernel, qt=qt, kt=kt),
        grid_spec=pltpu.PrefetchScalarGridSpec(
            num_scalar_prefetch=2, grid=(nb, qi_tab.shape[0]),
            in_specs=[qspec, kspec, vspec, pl.BlockSpec((1, 2, ns, qt), lambda b, s, qi, kj: (b, 0, 0, qi[s]))],
            out_specs=qspec,
            scratch_shapes=[pltpu.VMEM((4 * qt, LANES), BF16), pltpu.VMEM((1, 4 * qt), F32),
                            pltpu.VMEM((1, 4 * qt), F32), pltpu.VMEM((LANES, 4 * qt), F32)]),
        out_shape=jax.ShapeDtypeStruct((nb, sl, 256), F32),
        compiler_params=_params(("parallel", "arbitrary")),
        name="nsa_prompt_attn",
    )(qi_tab, kj_tab, r3(pj["qnr"], 256), r3(pj["ks"], 128), pj["vsT"], sel)


def _winp_kernel(q_ref, *refs, qt, n_blk):
    kb, vtb, o_ref = refs[:n_blk], refs[n_blk:2 * n_blk], refs[2 * n_blk]
    qi = pl.program_id(1)
    q4 = _nsa_expand(q_ref[0] * NSA_QSCALE).astype(BF16)
    k_b = jnp.concatenate([r[0].astype(BF16) for r in kb], axis=0)
    vt_b = jnp.concatenate([r[0].astype(BF16) for r in vtb], axis=1)
    s = lax.dot_general(k_b, q4, (((1,), (1,)), ((), ())), preferred_element_type=F32)
    nk = n_blk * qt
    kpos = (qi - (n_blk - 1)) * qt + _iota((nk, qt), 0)
    qpos = qi * qt + _iota((nk, qt), 1)
    band = (kpos >= 0) & (kpos <= qpos) & (kpos > qpos - NSA_WINDOW)
    s = jnp.where(jnp.concatenate([band] * 4, axis=1), s, NEG_INF)
    m = jnp.max(s, axis=0, keepdims=True)
    p = jnp.exp2(s - m)
    l = jnp.sum(p, axis=0, keepdims=True)
    o = jnp.dot(vt_b, p.astype(BF16), preferred_element_type=F32) / jnp.maximum(l, 1e-30)
    o_ref[0] = _collect_t(o, qt).T


def _nsa_prompt_window(pj, nb, sl):
    qt = min(256, sl)
    n_blk = -(-(NSA_WINDOW - 1) // qt) + 1
    blk = lambda j: (lambda b, i: (b, jnp.maximum(i - (n_blk - 1) + j, 0), 0))
    blk_t = lambda j: (lambda b, i: (b, 0, jnp.maximum(i - (n_blk - 1) + j, 0)))
    qspec = pl.BlockSpec((1, qt, 256), lambda b, i: (b, i, 0))
    r3 = lambda a, w: a.reshape(nb, sl, w)
    return pl.pallas_call(
        functools.partial(_winp_kernel, qt=qt, n_blk=n_blk),
        grid=(nb, sl // qt),
        in_specs=[qspec] + [pl.BlockSpec((1, qt, LANES), blk(j)) for j in range(n_blk)]
        + [pl.BlockSpec((1, LANES, qt), blk_t(j)) for j in range(n_blk)],
        out_specs=qspec,
        out_shape=jax.ShapeDtypeStruct((nb, sl, 256), F32),
        compiler_params=_params(("parallel", "parallel")),
        name="nsa_prompt_window",
    )(r3(pj["qnr"], 256), *([r3(pj["kw"], 128)] * n_blk), *([pj["vwT"]] * n_blk))


def _nsa_gate(misc, oc, os_, ow):
    g = jax.nn.sigmoid(misc)
    head = _iota(oc.shape, 1) >> 6
    return (_head_bcast(g, head, 4, GATE_LANE0) * oc + _head_bcast(g, head, 4, GATE_LANE0 + 4) * os_
            + _head_bcast(g, head, 4, GATE_LANE0 + 8) * ow)


def _diff_lambda(lamv_ref, lam_init):
    lv = lamv_ref[...]
    s1 = jnp.sum(lv[0:1] * lv[1:2], axis=1, keepdims=True)
    s2 = jnp.sum(lv[2:3] * lv[3:4], axis=1, keepdims=True)
    return jnp.exp(s1) - jnp.exp(s2) + lam_init


def _diff_finish(o, head, nheads, nw, lam_init):
    inv = jnp.zeros(o.shape, F32)
    for h in range(nheads):
        ms = jnp.sum(jnp.where(head == h, o * o, 0.0), axis=-1, keepdims=True) * (1.0 / 64.0)
        inv = jnp.where(head == h, lax.rsqrt(ms + EPS), inv)
    return o * inv * nw * (1.0 - lam_init)


def _diffp_kernel(qi_ref, kj_ref, q_ref, k_ref, vt_ref, lamv_ref, nwt_ref, o_ref, q4_s, m_s, l_s, acc_s,
                  *, qt, kt, lam_init):
    step = pl.program_id(2)
    qi, kj = qi_ref[step], kj_ref[step]
    last = (qi * qt + qt - 1) // kt

    @pl.when(kj == 0)
    def _():
        q = q_ref[0] * DIFF_QSCALE
        part = _iota((qt, LANES), 1) >> 5
        q4_s[...] = jnp.concatenate([jnp.where(part == k, q, 0.0) for k in range(4)], axis=0).astype(BF16)
        _init_stats((m_s, l_s, acc_s))

    def scores():
        return lax.dot_general(k_ref[0].astype(BF16), q4_s[...], (((1,), (1,)), ((), ())),
                               preferred_element_type=F32)

    def causal():
        mask = kj * kt + _iota((kt, qt), 0) <= qi * qt + _iota((kt, qt), 1)
        return jnp.concatenate([mask] * 4, axis=1)

    @pl.when((kj == 0) & (last > 0))
    def _():
        _online_update_t_lazy(scores, None, vt_ref[0].astype(BF16), m_s, l_s, acc_s, first=True)

    @pl.when((kj == 0) & (last == 0))
    def _():
        _online_update_t_lazy(scores, causal(), vt_ref[0].astype(BF16), m_s, l_s, acc_s, first=True)

    @pl.when((kj > 0) & (kj < last))
    def _():
        _online_update_t_lazy(scores, None, vt_ref[0].astype(BF16), m_s, l_s, acc_s)

    @pl.when((kj > 0) & (kj == last))
    def _():
        _online_update_t_lazy(scores, causal(), vt_ref[0].astype(BF16), m_s, l_s, acc_s)

    @pl.when(kj == last)
    def _():
        lam = _diff_lambda(lamv_ref, lam_init)
        o = acc_s[...] / jnp.maximum(l_s[...], 1e-30)
        halves = []
        for h in range(2):
            rows = slice(64 * h, 64 * (h + 1))
            oh = o[rows, 2 * h * qt:(2 * h + 1) * qt] - lam * o[rows, (2 * h + 1) * qt:(2 * h + 2) * qt]
            ms = jnp.mean(oh * oh, axis=0, keepdims=True)
            halves.append(oh * lax.rsqrt(ms + EPS))
        o_ref[0] = (jnp.concatenate(halves, axis=0) * nwt_ref[...] * (1.0 - lam_init)).T


def _diff_prompt(pj, nb, sl, p, lam_init):
    qt, kt = min(512, sl), min(512, sl)
    assert kt % qt == 0
    qi_tab, kj_tab = _causal_pairs(sl // qt, qt, kt)
    qspec = pl.BlockSpec((1, qt, LANES), lambda b, h, s, qi, kj: (b, qi[s], h))
    kspec = pl.BlockSpec((1, kt, LANES), lambda b, h, s, qi, kj: (b, kj[s], h))
    vspec = pl.BlockSpec((1, LANES, kt), lambda b, h, s, qi, kj: (b, h, kj[s]))
    r3 = lambda a: a.reshape(nb, sl, 256)
    out = pl.pallas_call(
        functools.partial(_diffp_kernel, qt=qt, kt=kt, lam_init=lam_init),
        grid_spec=pltpu.PrefetchScalarGridSpec(
            num_scalar_prefetch=2, grid=(nb, 2, qi_tab.shape[0]),
            in_specs=[qspec, kspec, vspec, pl.BlockSpec((4, 32), lambda b, h, s, qi, kj: (0, 0)),
                      pl.BlockSpec((LANES, 1), lambda b, h, s, qi, kj: (0, 0))],
            out_specs=qspec,
            scratch_shapes=[pltpu.VMEM((4 * qt, LANES), BF16), pltpu.VMEM((1, 4 * qt), F32),
                            pltpu.VMEM((1, 4 * qt), F32), pltpu.VMEM((LANES, 4 * qt), F32)]),
        out_shape=jax.ShapeDtypeStruct((nb, sl, 256), F32),
        compiler_params=_params(("parallel", "parallel", "arbitrary")),
        name="diff_prompt",
    )(qi_tab, kj_tab, r3(pj["dq"]), r3(pj["dk"]), pj["dvT"], p["diff_lambda"],
      p["diff_nw256"][:, :LANES].reshape(LANES, 1))
    return out.reshape(nb * sl, 256)


def _wins_kernel(q_ref, kwt_ref, vwt_ref, knew_ref, vnew_ref, o_ref, m_s, l_s, acc_s, *, kbase, past):
    nq = q_ref.shape[1]
    rows = 4 * nq
    nk = kwt_ref.shape[-1]
    qe = _nsa_expand(q_ref[0] * NSA_QSCALE).astype(BF16)
    _init_stats((m_s, l_s, acc_s))
    s = jnp.dot(qe, kwt_ref[0, 0].astype(BF16), preferred_element_type=F32)
    qpos = past + (_iota((rows, nk), 0) & (nq - 1))
    kpos = kbase + _iota((rows, nk), 1)
    _online_update(s, (kpos <= qpos) & (kpos > qpos - NSA_WINDOW), vwt_ref[0, 0].astype(BF16), m_s, l_s, acc_s,
                   v_transposed=True)
    s2 = lax.dot_general(qe, knew_ref[0].astype(BF16), (((1,), (1,)), ((), ())), preferred_element_type=F32)
    qpos2 = past + (_iota((rows, nq), 0) & (nq - 1))
    kpos2 = past + _iota((rows, nq), 1)
    _online_update(s2, (kpos2 <= qpos2) & (kpos2 > qpos2 - NSA_WINDOW), vnew_ref[0].astype(BF16), m_s, l_s, acc_s)
    o_ref[0] = _nsa_collect(acc_s[...] / jnp.maximum(l_s[...], 1e-30), nq)


def _window_sample(q, kw_t, vw_t, layer, knew, vnew, past):
    nb, nq = q.shape[0], q.shape[1]
    wbuf = kw_t.shape[-1]
    rows = 4 * nq
    per_b = lambda r_, w: pl.BlockSpec((1, r_, w), lambda b: (b, 0, 0))
    wspec = pl.BlockSpec((1, 1, LANES, wbuf), lambda b: (layer, b, 0, 0))
    return pl.pallas_call(
        functools.partial(_wins_kernel, kbase=past - wbuf, past=past),
        grid=(nb,),
        in_specs=[per_b(nq, 256), wspec, wspec, per_b(nq, LANES), per_b(nq, LANES)],
        out_specs=per_b(nq, 256),
        out_shape=jax.ShapeDtypeStruct((nb, nq, 256), F32),
        scratch_shapes=[pltpu.VMEM((rows, 1), F32), pltpu.VMEM((rows, 1), F32), pltpu.VMEM((rows, LANES), F32)],
        compiler_params=_params(("parallel",)),
        name="window_sample",
    )(q, kw_t, vw_t, knew, vnew)


def _page_copy(src_hbm, layer, page, buf, slot, r, sem):
    return pltpu.make_async_copy(src_hbm.at[layer, page], buf.at[slot, r], sem.at[slot])


def _start_pages(srcs, bufs, sems, pt_ref, layer, b, first, slot, n):
    for r in range(n):
        page = pt_ref[b, first + r]
        for src, buf, sem in zip(srcs, bufs, sems):
            _page_copy(src, layer, page, buf, slot, r, sem).start()


def _wait_pages(srcs, bufs, sems, layer, slot, n):
    for r in range(n):
        for src, buf, sem in zip(srcs, bufs, sems):
            _page_copy(src, layer, 0, buf, slot, r, sem).wait()


def _page_ring(pt_ref, srcs, bufs, sems, layer, pps, n_steps):
    b, nb = pl.program_id(0), pl.num_programs(0)
    slot_of = lambda s: (b * n_steps + s) & 1

    @pl.when(b == 0)
    def _():
        _start_pages(srcs, bufs, sems, pt_ref, layer, 0, 0, 0, pps)

    def advance(s):
        nxt = 1 - slot_of(s)
        if s + 1 < n_steps:
            _start_pages(srcs, bufs, sems, pt_ref, layer, b, (s + 1) * pps, nxt, pps)
        else:
            @pl.when(b + 1 < nb)
            def _():
                _start_pages(srcs, bufs, sems, pt_ref, layer, b + 1, 0, nxt, pps)
        _wait_pages(srcs, bufs, sems, layer, slot_of(s), pps)

    return slot_of, advance


def _paged2_kernel(pt_ref, q_ref, k_hbm, v_hbm, knew_ref, vnew_ref, *rest, mode, layer, pps, n_steps, width, past,
                   new_len, lam_init):
    selx_ref = lamv_ref = nw_ref = None
    if mode == "slc":
        selx_ref, rest = rest[0], rest[1:]
    if mode == "diff":
        lamv_ref, nw_ref, rest = rest[0], rest[1], rest[2:]
    o_ref, kbuf, vbuf, ksem, vsem, qe_s, m_s, l_s, acc_s = rest
    nq = new_len
    rows = qe_s.shape[0]
    nk = pps * PAGE
    slot_of, advance = _page_ring(pt_ref, (k_hbm, v_hbm), (kbuf, vbuf), (ksem, vsem), layer, pps, n_steps)

    if mode == "diff":
        q = q_ref[0] * DIFF_QSCALE
        part = _iota((nq, width), 1) >> 5
        qe = jnp.concatenate([jnp.where(part == k, q, 0.0) for k in range(8)], axis=0)
    else:
        qe = _nsa_expand(q_ref[0] * NSA_QSCALE)
    qe_s[...] = qe.astype(BF16)
    _init_stats((m_s, l_s, acc_s))

    for s in range(n_steps):
        advance(s)
        slot = slot_of(s)
        kt_b = jnp.concatenate([kbuf[slot, r].astype(BF16) for r in range(pps)], axis=1)
        vt_b = jnp.concatenate([vbuf[slot, r].astype(BF16) for r in range(pps)], axis=1)
        sc = jnp.dot(qe_s[...], kt_b, preferred_element_type=F32)
        qpos = past + (_iota((rows, nk), 0) & (nq - 1))
        kpos = s * nk + _iota((rows, nk), 1)
        mask = kpos <= qpos
        if mode == "slc":
            mask = mask & (selx_ref[0, :, s * nk:(s + 1) * nk].astype(F32) > 0.5)
        _online_update(sc, mask, vt_b, m_s, l_s, acc_s, v_transposed=True)

    s2 = lax.dot_general(qe_s[...], knew_ref[0].astype(BF16), (((1,), (1,)), ((), ())), preferred_element_type=F32)
    qpos2 = past + (_iota((rows, nq), 0) & (nq - 1))
    kpos2 = past + _iota((rows, nq), 1)
    mask2 = kpos2 <= qpos2
    if mode == "slc":
        mask2 = mask2 & (selx_ref[0, :, past:past + nq].astype(F32) > 0.5)
    _online_update(s2, mask2, vnew_ref[0].astype(BF16), m_s, l_s, acc_s)
    o = acc_s[...] / jnp.maximum(l_s[...], 1e-30)
    if mode == "diff":
        lam = _diff_lambda(lamv_ref, lam_init)
        head = _iota((nq, width), 1) >> 6
        out = jnp.zeros((nq, width), F32)
        for h in range(4):
            out = jnp.where(head == h, o[(2 * h) * nq:(2 * h + 1) * nq] - lam * o[(2 * h + 1) * nq:(2 * h + 2) * nq],
                            out)
        o_ref[0] = _diff_finish(out, head, 4, nw_ref[...], lam_init)
    else:
        o_ref[0] = _nsa_collect(o, nq)


def _paged2_attn(mode, q, k_src, v_src, layer, table, knew, vnew, past, extra=(), lam_init=0.0):
    nb, nq = q.shape[0], q.shape[1]
    n_pages = table.shape[1]
    width = knew.shape[-1]
    pps = min(PAGES_PER_STEP, n_pages)
    n_steps = n_pages // pps
    assert n_pages % pps == 0 and past == n_pages * PAGE
    rows = 8 * nq if mode == "diff" else 4 * nq
    per_b = lambda r_, w: pl.BlockSpec((1, r_, w), lambda b, pt: (b, 0, 0))
    hbm = pl.BlockSpec(memory_space=pl.ANY)
    in_specs = [per_b(nq, 256), hbm, hbm, per_b(nq, width), per_b(nq, width)]
    args = [q, k_src, v_src, knew, vnew]
    if mode == "slc":
        (selx,) = extra
        in_specs.append(per_b(rows, selx.shape[-1]))
        args.append(selx)
    if mode == "diff":
        in_specs += [pl.BlockSpec((4, 32), lambda b, pt: (0, 0)), pl.BlockSpec((1, 256), lambda b, pt: (0, 0))]
        args += list(extra)
    return pl.pallas_call(
        functools.partial(_paged2_kernel, mode=mode, layer=layer, pps=pps, n_steps=n_steps, width=width, past=past,
                          new_len=nq, lam_init=lam_init),
        grid_spec=pltpu.PrefetchScalarGridSpec(
            num_scalar_prefetch=1, grid=(nb,), in_specs=in_specs, out_specs=per_b(nq, 256),
            scratch_shapes=[pltpu.VMEM((2, pps, width, PAGE), F32), pltpu.VMEM((2, pps, width, PAGE), F32),
                            pltpu.SemaphoreType.DMA((2,)), pltpu.SemaphoreType.DMA((2,)),
                            pltpu.VMEM((rows, width), BF16), pltpu.VMEM((rows, 1), F32),
                            pltpu.VMEM((rows, 1), F32), pltpu.VMEM((rows, width), F32)]),
        out_shape=jax.ShapeDtypeStruct((nb, nq, 256), F32),
        compiler_params=_params(("arbitrary",)),
        name="paged2_" + mode,
    )(table, *args)


def _summ2_kernel(pt_ref, k_hbm, v_hbm, mk_ref, mv_ref, ko_ref, vo_ref, kbuf, vbuf, ksem, vsem, *, layer, pps, n_steps):
    slot_of, advance = _page_ring(pt_ref, (k_hbm, v_hbm), (kbuf, vbuf), (ksem, vsem), layer, pps, n_steps)
    n_blk = pps * PAGE // NSA_CMP_BLOCK
    for s in range(n_steps):
        advance(s)
        slot = slot_of(s)
        for buf, m_ref, o_ref in ((kbuf, mk_ref, ko_ref), (vbuf, mv_ref, vo_ref)):
            x = jnp.concatenate([buf[slot, r].astype(BF16) for r in range(pps)], axis=1)
            top = jnp.dot(x[0:64], m_ref[0], preferred_element_type=F32)
            bot = jnp.dot(x[64:128], m_ref[1], preferred_element_type=F32)
            o_ref[0, :, s * n_blk:(s + 1) * n_blk] = jnp.concatenate([top, bot], axis=0)


def _summarize_pages(k_src, v_src, layer, table, w_k, w_v):
    nb, n_pages = table.shape
    pps = min(PAGES_PER_STEP, n_pages)
    n_steps = n_pages // pps
    step_w = pps * PAGE
    n_blk = step_w // NSA_CMP_BLOCK
    tok = jnp.arange(step_w, dtype=jnp.int32)
    in_block = tok[:, None] // NSA_CMP_BLOCK == jnp.arange(n_blk, dtype=jnp.int32)[None, :]
    weights = lambda w: jnp.where(in_block[None], w[:, tok % NSA_CMP_BLOCK][:, :, None], 0.0).astype(BF16)
    hbm = pl.BlockSpec(memory_space=pl.ANY)
    wspec = pl.BlockSpec((2, step_w, n_blk), lambda b, pt: (0, 0, 0))
    ospec = pl.BlockSpec((1, LANES, n_steps * n_blk), lambda b, pt: (b, 0, 0))
    oshape = jax.ShapeDtypeStruct((nb, LANES, n_steps * n_blk), F32)
    return pl.pallas_call(
        functools.partial(_summ2_kernel, layer=layer, pps=pps, n_steps=n_steps),
        grid_spec=pltpu.PrefetchScalarGridSpec(
            num_scalar_prefetch=1, grid=(nb,), in_specs=[hbm, hbm, wspec, wspec], out_specs=[ospec, ospec],
            scratch_shapes=[pltpu.VMEM((2, pps, LANES, PAGE), F32), pltpu.VMEM((2, pps, LANES, PAGE), F32),
                            pltpu.SemaphoreType.DMA((2,)), pltpu.SemaphoreType.DMA((2,))]),
        out_shape=[oshape, oshape],
        compiler_params=_params(("arbitrary",)),
        name="nsa_summarize_pages",
    )(table, k_src, v_src, weights(w_k), weights(w_v))


def _post_kernel(x_ref, ya_ref, misc_ref, oc_ref, os_ref, ow_ref, yc_ref, yd_ref, wo_ref, g1_ref, b1_ref,
                 w1_ref, w2_ref, g2_ref, b2_ref, o_ref, x1_s, x1b_s, acc_s, *, alpha):
    f = pl.program_id(1)

    @pl.when(f == 0)
    def _():
        yb = _nsa_gate(misc_ref[...], oc_ref[...], os_ref[...], ow_ref[...])
        mix = _dot(ya_ref[...], wo_ref[0:256, :])
        mix = mix + _dot(yb, wo_ref[256:512, :])
        mix = mix + _dot(yc_ref[...], wo_ref[512:768, :])
        mix = mix + _dot(yd_ref[...], wo_ref[768:1024, :])
        x1 = _layer_norm(alpha * x_ref[...] + mix, g1_ref[...], b1_ref[...])
        x1_s[...] = x1
        x1b_s[...] = x1.astype(BF16)
        acc_s[...] = jnp.zeros(acc_s.shape, F32)

    h = jnp.dot(x1b_s[...], w1_ref[...], preferred_element_type=F32)
    h = jnp.square(jnp.maximum(h, 0.0))
    acc_s[...] += jnp.dot(h.astype(BF16), w2_ref[...], preferred_element_type=F32)

    @pl.when(f == pl.num_programs(1) - 1)
    def _():
        o_ref[...] = _layer_norm(alpha * x1_s[...] + acc_s[...], g2_ref[...], b2_ref[...])


def _post(x2d, ya, misc, oc, os_, ow, yc, yd, p, tm, alpha):
    t = x2d.shape[0]
    tf = 2048
    row = lambda w: pl.BlockSpec((tm, w), lambda i, f: (i, 0))
    const = lambda r, w: pl.BlockSpec((r, w), lambda i, f: (0, 0))
    return pl.pallas_call(
        functools.partial(_post_kernel, alpha=alpha),
        grid=(t // tm, D_FF // tf),
        in_specs=[row(D_MODEL), row(256), row(LANES), row(256), row(256), row(256), row(256), row(256),
                  const(D_MODEL, D_MODEL), const(1, D_MODEL), const(1, D_MODEL),
                  pl.BlockSpec((D_MODEL, tf), lambda i, f: (0, f)), pl.BlockSpec((tf, D_MODEL), lambda i, f: (f, 0)),
                  const(1, D_MODEL), const(1, D_MODEL)],
        out_specs=row(D_MODEL),
        out_shape=jax.ShapeDtypeStruct((t, D_MODEL), F32),
        scratch_shapes=[pltpu.VMEM((tm, D_MODEL), F32), pltpu.VMEM((tm, D_MODEL), BF16),
                        pltpu.VMEM((tm, D_MODEL), F32)],
        compiler_params=_params(("parallel", "arbitrary")),
        name="post",
    )(x2d, ya, misc, oc.reshape(t, 256), os_.reshape(t, 256), ow.reshape(t, 256), yc, yd,
      p["w_out"], p["ln1_g"], p["ln1_b"], p["w_ff1"], p["w_ff2"], p["ln2_g"], p["ln2_b"])


def _prep_params(w, l):
    w_in = w["w_in"][l]
    misc = jnp.concatenate([w_in[:, a:b] for a, b in MISC_COLS], axis=1)
    misc = jnp.pad(misc, ((0, 0), (0, LANES - misc.shape[1])))
    w_in_p = jnp.concatenate([w_in[:, a:b] for a, b in W_IN_ORDER] + [misc], axis=1).astype(BF16)
    row = lambda v: v.reshape(1, -1).astype(F32)
    pad_lanes = lambda v: jnp.pad(v.reshape(1, -1), ((0, 0), (0, LANES - v.shape[-1])))
    blockdiag = lambda m: jax.scipy.linalg.block_diag(*[m[i] for i in range(m.shape[0])]).astype(BF16)
    return dict(
        w_in=w_in_p,
        ssd_cw=jnp.pad(w["ssd_conv_w"][l], ((0, 4), (0, 0))), ssd_cb=row(w["ssd_conv_b"][l]),
        ssd_dtb=pad_lanes(w["ssd_dt_bias"][l]), ssd_alog=pad_lanes(w["ssd_A_log"][l]),
        ssd_dexp=row(jnp.repeat(w["ssd_D"][l], 64)), ssd_nw=row(w["ssd_norm_w"][l]),
        cmp_wk=w["nsa_w_cmp_k"][l], cmp_wv=w["nsa_w_cmp_v"][l],
        diff_lambda=w["diff_lambda"][l], diff_nw256=row(jnp.tile(w["diff_norm_w"][l], 4)),
        lru_cw=jnp.pad(w["lru_conv_w"][l], ((0, 4), (0, 0))), lru_cb=row(w["lru_conv_b"][l]),
        lru_wa=blockdiag(w["lru_w_a"][l]), lru_ba=row(w["lru_b_a"][l]),
        lru_wx=blockdiag(w["lru_w_x"][l]), lru_bx=row(w["lru_b_x"][l]), lru_lam=row(w["lru_lambda"][l]),
        w_out=w["w_out"][l].astype(BF16), ln1_g=row(w["ln1_g"][l]), ln1_b=row(w["ln1_b"][l]),
        w_ff1=w["w_ff1"][l].astype(BF16), w_ff2=w["w_ff2"][l].astype(BF16),
        ln2_g=row(w["ln2_g"][l]), ln2_b=row(w["ln2_b"][l]),
    )


def _pad_prev(prev):
    return jnp.pad(prev, ((0, 0), (5, 0), (0, 0)))


def _token_minor(a):
    a = jnp.moveaxis(a, -3, -1)
    return a.reshape(*a.shape[:-3], a.shape[-3] * a.shape[-2], a.shape[-1])


def _token_major(a_t, heads):
    nb, _, n = a_t.shape
    return jnp.moveaxis(a_t.reshape(nb, heads, 64, n), -1, 1)


def _layer(x2d, nb, sl, past, hist, p, lam_init, alpha):
    t = nb * sl
    tm = min(256, t)
    pos = past + jnp.arange(sl, dtype=jnp.int32)
    pj = _proj(x2d, p["w_in"], pos, nb, sl, tm)
    y_a, ssd_h = _ssd(pj, nb, sl, _pad_prev(hist["ssd_conv"]), hist["ssd_h"].reshape(nb, 256, 128), p)
    y_d, lru_h = _lru(pj, nb, sl, _pad_prev(hist["lru_conv"]), hist["lru_h"], p)
    r3 = lambda a, w: a.reshape(nb, sl, w)
    dummy = jnp.zeros((1, 1), jnp.int32)
    if past == 0:
        step_w = min(4096, sl)
        spec = pl.BlockSpec((1, LANES, step_w), lambda b, s, pt: (b, 0, s))
        kcmp_t, vcmp_t = _summarize(pj["kcT"], pj["vcT"], [spec], step_w, nb, sl, dummy, p["cmp_wk"], p["cmp_wv"])
        o_cmp, sel = _cmp_select_prompt(pj["qn"], kcmp_t, vcmp_t, nb, sl, min(256, sl))
        o_slc = _nsa_prompt_attn(pj, sel, nb, sl)
        o_win = _nsa_prompt_window(pj, nb, sl)
        y_c = _diff_prompt(pj, nb, sl, p, lam_init)
        kw_t, vw_t = pj["kwT"][:, :, sl - NSA_WINDOW:], pj["vwT"][:, :, sl - NSA_WINDOW:]
        new = dict(kc=_token_major(pj["kcT"], 2), vc=_token_major(pj["vcT"], 2), ks=_token_major(pj["ksT"], 2),
                   vs=_token_major(pj["vsT"], 2), dk=_token_major(pj["dkT"], 4), dv=_token_major(pj["dvT"], 4))
    else:
        table, l = hist["table"], hist["layer"]
        n_pages = table.shape[1]
        kcmp_t, vcmp_t = _summarize_pages(hist["kc"], hist["vc"], l, table, p["cmp_wk"], p["cmp_wv"])
        o_cmp, selx = _cmp_select_sample(pj["qn"], kcmp_t, vcmp_t, nb, sl, past)
        qnr = r3(pj["qnr"], 256)
        o_slc = _paged2_attn("slc", qnr, hist["ks"], hist["vs"], l, table, r3(pj["ks"], 128), r3(pj["vs"], 128),
                             past, extra=(selx,))
        o_win = _window_sample(qnr, hist["kw"], hist["vw"], l, r3(pj["kw"], 128), r3(pj["vw"], 128), past)
        y_c = _paged2_attn("diff", r3(pj["dq"], 256), hist["dk"], hist["dv"], l, table, r3(pj["dk"], 256),
                           r3(pj["dv"], 256), past, extra=(p["diff_lambda"], p["diff_nw256"]),
                           lam_init=lam_init).reshape(t, 256)
        kw_t = jnp.concatenate([hist["kw"][l], pj["kwT"]], axis=2)[:, :, -NSA_WINDOW:]
        vw_t = jnp.concatenate([hist["vw"][l], pj["vwT"]], axis=2)[:, :, -NSA_WINDOW:]
        kv = lambda a: r3(a, 128).reshape(nb, sl, 2, 64)
        new = dict(kc=kv(pj["kc"]), vc=kv(pj["vc"]), ks=kv(pj["ks"]), vs=kv(pj["vs"]),
                   dk=r3(pj["dk"], 256).reshape(nb, sl, 4, 64), dv=r3(pj["dv"], 256).reshape(nb, sl, 4, 64))
    x_out = _post(x2d, y_a, pj["misc"], o_cmp, o_slc, o_win, y_c, y_d, p, min(512, t), alpha)
    new.update(kw=_token_major(kw_t, 2), vw=_token_major(vw_t, 2),
               ssd_h=ssd_h.reshape(nb, 4, 64, 128), ssd_conv=r3(pj["xbc"], 768)[:, sl - 3:],
               lru_h=lru_h, lru_conv=r3(pj["lx"], 256)[:, sl - 3:])
    return x_out, new


STATE_ORDER = ("kc", "vc", "ks", "vs", "dk", "dv", "kw", "vw", "ssd_h", "ssd_conv", "lru_h", "lru_conv")


def kernel(x_prompt, x_sample, cache_nsa_k_cmp, cache_nsa_v_cmp, cache_nsa_k_slc, cache_nsa_v_slc, cache_diff_k, cache_diff_v, cache_nsa_k_win, cache_nsa_v_win, state_ssd, state_ssd_conv, state_lru, state_lru_conv, page_table, w_in, ssd_conv_w, ssd_conv_b, ssd_dt_bias, ssd_A_log, ssd_D, ssd_norm_w, nsa_w_cmp_k, nsa_w_cmp_v, diff_lambda, diff_norm_w, lru_conv_w, lru_conv_b, lru_w_a, lru_b_a, lru_w_x, lru_b_x, lru_lambda, w_out, ln1_g, ln1_b, w_ff1, w_ff2, ln2_g, ln2_b):
    weights = dict(w_in=w_in, ssd_conv_w=ssd_conv_w, ssd_conv_b=ssd_conv_b, ssd_dt_bias=ssd_dt_bias,
                   ssd_A_log=ssd_A_log, ssd_D=ssd_D, ssd_norm_w=ssd_norm_w, nsa_w_cmp_k=nsa_w_cmp_k,
                   nsa_w_cmp_v=nsa_w_cmp_v, diff_lambda=diff_lambda, diff_norm_w=diff_norm_w, lru_conv_w=lru_conv_w,
                   lru_conv_b=lru_conv_b, lru_w_a=lru_w_a, lru_b_a=lru_b_a, lru_w_x=lru_w_x, lru_b_x=lru_b_x,
                   lru_lambda=lru_lambda, w_out=w_out, ln1_g=ln1_g, ln1_b=ln1_b, w_ff1=w_ff1, w_ff2=w_ff2,
                   ln2_g=ln2_g, ln2_b=ln2_b)
    depth = w_in.shape[0]
    nbp, slp, _ = x_prompt.shape
    nbs, sls, _ = x_sample.shape
    past = page_table.shape[1] * cache_nsa_k_cmp.shape[2]
    assert past % PAGE == 0 and sls < NSA_CMP_BLOCK and slp % 256 == 0
    alpha = (2 * depth) ** 0.25
    caches = dict(kc=_token_minor(cache_nsa_k_cmp), vc=_token_minor(cache_nsa_v_cmp), ks=_token_minor(cache_nsa_k_slc),
                  vs=_token_minor(cache_nsa_v_slc), dk=_token_minor(cache_diff_k), dv=_token_minor(cache_diff_v),
                  kw=_token_minor(cache_nsa_k_win), vw=_token_minor(cache_nsa_v_win))
    xp = x_prompt.reshape(nbp * slp, D_MODEL)
    xs = x_sample.reshape(nbs * sls, D_MODEL)
    outs_p, outs_s = [], []
    for l in range(depth):
        p = _prep_params(weights, l)
        lam_init = 0.8 - 0.6 * math.exp(-0.3 * l)
        hist_p = dict(ssd_conv=jnp.zeros((nbp, 3, SSD_CONV_CH), F32), ssd_h=jnp.zeros((nbp, 4, 64, 128), F32),
                      lru_conv=jnp.zeros((nbp, 3, GROUP_W), F32), lru_h=jnp.zeros((nbp, GROUP_W), F32))
        hist_s = dict(table=page_table, layer=l, ssd_conv=state_ssd_conv[l], ssd_h=state_ssd[l],
                      lru_conv=state_lru_conv[l], lru_h=state_lru[l], **caches)
        xp, new_p = _layer(xp, nbp, slp, 0, hist_p, p, lam_init, alpha)
        xs, new_s = _layer(xs, nbs, sls, past, hist_s, p, lam_init, alpha)
        outs_p.append(new_p)
        outs_s.append(new_s)
    stack = lambda lst, name: jnp.stack([d[name] for d in lst])
    return ((xp.reshape(nbp, slp, D_MODEL), xs.reshape(nbs, sls, D_MODEL))
            + tuple(stack(outs_p, n) for n in STATE_ORDER) + tuple(stack(outs_s, n) for n in STATE_ORDER))
```
